```python
import math
import jax
import jax.numpy as jnp
from jax import lax
import numpy as np

D_MODEL = 1024
BATCH = 16
SEQ = 2048
DEPTH = 4

GRID_W = 64
CTX_LEN = 256
N_MIXERS = 4
N_A = (DEPTH + 3) // N_MIXERS
N_B = (DEPTH + 2) // N_MIXERS
N_C = (DEPTH + 1) // N_MIXERS
N_D = DEPTH // N_MIXERS
EPS = 1e-6
ROPE_THETA = 10000.0
Q_BLOCK = 128

D_RNN = 1280
RG_BLOCKS = 8
RG_BW = D_RNN // RG_BLOCKS
RG_CONV = 4
RG_CONV_LEFT = 2
RG_C = 8.0

NA_HEADS = 16
NA_DH = D_MODEL // NA_HEADS
NA_KR = 8
NA_KC = 16

GQA_HEADS = 16
GQA_KV = 4
GQA_GROUP = GQA_HEADS // GQA_KV
GQA_DH = D_MODEL // GQA_HEADS

DIFF_HEADS = 8
DIFF_DH = D_MODEL // (2 * DIFF_HEADS)

D_FF = 2816
FFN_CONV = 3

kernel_name = 'hybrid_interleaved_diffusion_trunk'


def rmsnorm(x, g):
    xf = x.astype(jnp.float32)
    y = xf * lax.rsqrt(jnp.mean(xf * xf, axis=-1, keepdims=True) + EPS)
    return (y * g).astype(x.dtype)


def modulate(z, shift, scale):
    return z * (1.0 + scale) + shift


def dwconv(x, w, b, left):
    K, L = w.shape[0], x.shape[1]
    xp = jnp.pad(x, ((0, 0), (left, K - 1 - left), (0, 0)))
    y = b + w[0] * xp[:, 0:L]
    for k in range(1, K):
        y = y + w[k] * xp[:, k:k + L]
    return y


def axial_rope(L, dh):
    t = jnp.arange(L)
    row = (t // GRID_W).astype(jnp.float32)
    col = (t % GRID_W).astype(jnp.float32)
    n = dh // 4
    inv = ROPE_THETA ** (-jnp.arange(n, dtype=jnp.float32) / n)
    ang = jnp.concatenate([row[:, None] * inv, col[:, None] * inv], axis=-1)
    return jnp.cos(ang), jnp.sin(ang)


def apply_rope(x, cos, sin):
    x1, x2 = jnp.split(x, 2, axis=-1)
    return jnp.concatenate([x1 * cos - x2 * sin, x1 * sin + x2 * cos], axis=-1).astype(x.dtype)


def softmax_attend(q, k, v, scale):
    s = jnp.einsum('bkgqd,bksd->bkgqs', q, k).astype(jnp.float32) * scale
    p = jax.nn.softmax(s, axis=-1).astype(v.dtype)
    return jnp.einsum('bkgqs,bksd->bkgqd', p, v)


def over_query_blocks(fn, q, q_axis, out_axis):
    L = q.shape[q_axis]
    nb = L // Q_BLOCK
    qb = q.reshape(q.shape[:q_axis] + (nb, Q_BLOCK) + q.shape[q_axis + 1:])
    o = lax.map(fn, jnp.moveaxis(qb, q_axis, 0))
    o = jnp.moveaxis(o, 0, out_axis)
    return o.reshape(o.shape[:out_axis] + (L,) + o.shape[out_axis + 2:])


def block_diag_linear(x, w, b):
    xb = x.reshape(x.shape[:-1] + (RG_BLOCKS, RG_BW))
    return jnp.einsum('blnj,njk->blnk', xb, w).reshape(x.shape) + b


def rglru_coeffs(xr, wa, ba, wx, bx, lam):
    xf = xr.astype(jnp.float32)
    r = jax.nn.sigmoid(block_diag_linear(xf, wa, ba))
    i = jax.nn.sigmoid(block_diag_linear(xf, wx, bx))
    log_a = -RG_C * r * jax.nn.softplus(-lam.astype(jnp.float32))
    return jnp.exp(log_a), jnp.sqrt(-jnp.expm1(2.0 * log_a)) * (i * xf)


def linear_scan(a, b, h0, reverse):
    def combine(e1, e2):
        a1, b1 = e1
        a2, b2 = e2
        return a1 * a2, a2 * b1 + b2
    a_cum, h = lax.associative_scan(combine, (a, b), reverse=reverse, axis=1)
    return h if h0 is None else h + a_cum * h0[:, None]


def rglru_mixer(zl, zc, w_in, conv_w, conv_b, wa, ba, wx, bx, lam, w_out, need_ctx):
    def branches(z):
        g, xr = jnp.split(z @ w_in, 2, axis=-1)
        return jax.nn.gelu(g), dwconv(xr, conv_w, conv_b, RG_CONV_LEFT)
    gl, xl = branches(zl)
    gc, xc = branches(zc)
    lat_states, ctx_states = [], []
    for d in range(2):
        rev = d == 1
        a_c, b_c = rglru_coeffs(xc, wa[d], ba[d], wx[d], bx[d], lam[d])
        s_c = linear_scan(a_c, b_c, None, rev)
        h0 = s_c[:, 0] if rev else s_c[:, -1]
        a_l, b_l = rglru_coeffs(xl, wa[d], ba[d], wx[d], bx[d], lam[d])
        lat_states.append(linear_scan(a_l, b_l, h0, rev))
        ctx_states.append(s_c)
    y_lat = (gl * (lat_states[0] + lat_states[1])) @ w_out
    y_ctx = (gc * (ctx_states[0] + ctx_states[1])) @ w_out if need_ctx else None
    return y_lat, y_ctx


def na_mixer(zl, zc, w_in, rpb, w_out, need_ctx):
    B, S, _ = zl.shape
    rows = S // GRID_W
    kr = min(NA_KR, rows)
    scale = NA_DH ** -0.5

    def heads(z):
        q, k, v = jnp.split(z @ w_in, 3, axis=-1)
        sh = lambda t: t.reshape(z.shape[0], z.shape[1], NA_HEADS, NA_DH).transpose(0, 2, 1, 3)
        return sh(q), sh(k), sh(v)

    ql, kl, vl = heads(zl)
    qc, kc, vc = heads(zc)
    grid = lambda t: t.reshape(B, NA_HEADS, rows, GRID_W, NA_DH)
    qg, kg, vg = grid(ql), grid(kl), grid(vl)

    cols = jnp.arange(GRID_W)
    col_start = jnp.clip(cols - NA_KC // 2, 0, GRID_W - NA_KC)
    col_in = (cols[None, :] >= col_start[:, None]) & (cols[None, :] < col_start[:, None] + NA_KC)
    col_idx = jnp.clip(cols[None, :] - cols[:, None] + NA_KC - 1, 0, 2 * NA_KC - 2)
    row_start = jnp.clip(jnp.arange(rows) - kr // 2, 0, rows - kr)

    def row_block(r):
        rs = row_start[r]
        qb = lax.dynamic_index_in_dim(qg, r, axis=2, keepdims=False)
        kb = lax.dynamic_slice_in_dim(kg, rs, kr, axis=2)
        vb = lax.dynamic_slice_in_dim(vg, rs, kr, axis=2)
        dr = rs + jnp.arange(kr) - r + NA_KR - 1
        bias = rpb[:, dr][:, :, col_idx].transpose(0, 2, 1, 3)
        s = jnp.einsum('bhqd,bhrkd->bhqrk', qb, kb).astype(jnp.float32) * scale + bias
        s = jnp.where(col_in[:, None, :], s, -jnp.inf)
        s_ctx = jnp.einsum('bhqd,bhcd->bhqc', qb, kc).astype(jnp.float32) * scale
        p = jax.nn.softmax(jnp.concatenate([s.reshape(B, NA_HEADS, GRID_W, kr * GRID_W), s_ctx], axis=-1), axis=-1).astype(vb.dtype)
        p_lat = p[..., :kr * GRID_W].reshape(B, NA_HEADS, GRID_W, kr, GRID_W)
        return jnp.einsum('bhqrk,bhrkd->bhqd', p_lat, vb) + jnp.einsum('bhqc,bhcd->bhqd', p[..., kr * GRID_W:], vc)

    o = lax.map(row_block, jnp.arange(rows))
    y_lat = o.transpose(1, 0, 3, 2, 4).reshape(B, S, D_MODEL) @ w_out
    y_ctx = None
    if need_ctx:
        oc = softmax_attend(qc[:, :, None], kc, vc, scale)[:, :, 0]
        y_ctx = oc.transpose(0, 2, 1, 3).reshape(B, zc.shape[1], D_MODEL) @ w_out
    return y_lat, y_ctx


def gqa_mixer(zl, zc, w_in, q_norm, k_norm, w_out, need_ctx):
    scale = GQA_DH ** -0.5
    cos, sin = axial_rope(zl.shape[1], GQA_DH)

    def heads(z):
        B, L, _ = z.shape
        q, k, v = jnp.split(z @ w_in, [GQA_HEADS * GQA_DH, (GQA_HEADS + GQA_KV) * GQA_DH], axis=-1)
        q = rmsnorm(q.reshape(B, L, GQA_KV, GQA_GROUP, GQA_DH), q_norm).transpose(0, 2, 3, 1, 4)
        k = rmsnorm(k.reshape(B, L, GQA_KV, GQA_DH), k_norm).transpose(0, 2, 1, 3)
        v = v.reshape(B, L, GQA_KV, GQA_DH).transpose(0, 2, 1, 3)
        return q, k, v

    def merge(o):
        return o.transpose(0, 3, 1, 2, 4).reshape(o.shape[0], o.shape[3], D_MODEL) @ w_out

    ql, kl, vl = heads(zl)
    qc, kc, vc = heads(zc)
    ql, kl = apply_rope(ql, cos, sin), apply_rope(kl, cos, sin)
    k_all = jnp.concatenate([kl, kc], axis=2)
    v_all = jnp.concatenate([vl, vc], axis=2)
    ol = over_query_blocks(lambda qb: softmax_attend(qb, k_all, v_all, scale), ql, 3, 3)
    y_ctx = merge(softmax_attend(qc, kc, vc, scale)) if need_ctx else None
    return merge(ol), y_ctx


def diff_mixer(zl, zc, w_in, lq1, lk1, lq2, lk2, subln_g, w_out, lambda_init, need_ctx):
    scale = DIFF_DH ** -0.5
    cos, sin = axial_rope(zl.shape[1], DIFF_DH)
    f32 = jnp.float32
    lam = (jnp.exp(jnp.sum(lq1.astype(f32) * lk1.astype(f32)))
           - jnp.exp(jnp.sum(lq2.astype(f32) * lk2.astype(f32))) + lambda_init)

    def heads(z):
        B, L, _ = z.shape
        q, k, v = jnp.split(z @ w_in, 3, axis=-1)
        q = q.reshape(B, L, DIFF_HEADS, 2, DIFF_DH).transpose(0, 2, 3, 1, 4)
        k = k.reshape(B, L, DIFF_HEADS, 2, DIFF_DH).transpose(0, 2, 3, 1, 4)
        v = v.reshape(B, L, DIFF_HEADS, 2 * DIFF_DH).transpose(0, 2, 1, 3)
        return q, k, v

    def attend(q, k, v):
        s = jnp.einsum('bhiqd,bhikd->bhiqk', q, k).astype(f32) * scale
        p = jax.nn.softmax(s, axis=-1)
        a = (p[:, :, 0] - lam * p[:, :, 1]).astype(v.dtype)
        return jnp.einsum('bhqk,bhkd->bhqd', a, v)

    def merge(o):
        o = rmsnorm(o, subln_g) * (1.0 - lambda_init)
        return o.transpose(0, 2, 1, 3).reshape(o.shape[0], o.shape[2], D_MODEL) @ w_out

    ql, kl, vl = heads(zl)
    qc, kc, vc = heads(zc)
    ql, kl = apply_rope(ql, cos, sin), apply_rope(kl, cos, sin)
    k_all = jnp.concatenate([kl, kc], axis=3)
    v_all = jnp.concatenate([vl, vc], axis=2)
    ol = over_query_blocks(lambda qb: attend(qb, k_all, v_all), ql, 3, 2)
    y_ctx = merge(attend(qc, kc, vc)) if need_ctx else None
    return merge(ol), y_ctx


def conv_ffn(z, w_up, conv_w, conv_b, w_down):
    u = dwconv(z @ w_up, conv_w, conv_b, FFN_CONV // 2)
    g, v = jnp.split(u, 2, axis=-1)
    return (jax.nn.silu(g) * v) @ w_down


def setup_inputs(seed: int = 0) -> dict:
    key = jax.random.key(seed)
    ks = iter(jax.random.split(key, 40))
    f32 = jnp.float32
    D = D_MODEL

    def nrm(shape, scale):
        return jax.random.normal(next(ks), shape, f32) * scale

    def gain(shape):
        return 1.0 + nrm(shape, 0.05)

    u = jax.random.uniform(next(ks), (N_A, 2, D_RNN), f32, 0.9, 0.999)
    return {
        'x': nrm((BATCH, SEQ, D), 1.0),
        'c': nrm((BATCH, D), 1.0),
        'ctx': nrm((BATCH, CTX_LEN, D), 1.0),
        'c_ctx': nrm((D,), 1.0),
        'mod_w': nrm((DEPTH, D, 6 * D), 0.5 * D ** -0.5),
        'mod_b': nrm((DEPTH, 6 * D), 0.02),
        'norm1_g': gain((DEPTH, D)),
        'norm2_g': gain((DEPTH, D)),
        'rg_w_in': nrm((N_A, D, 2 * D_RNN), D ** -0.5),
        'rg_conv_w': nrm((N_A, RG_CONV, D_RNN), RG_CONV ** -0.5),
        'rg_conv_b': nrm((N_A, D_RNN), 0.02),
        'rg_wa': nrm((N_A, 2, RG_BLOCKS, RG_BW, RG_BW), RG_BW ** -0.5),
        'rg_ba': nrm((N_A, 2, D_RNN), 0.02),
        'rg_wx': nrm((N_A, 2, RG_BLOCKS, RG_BW, RG_BW), RG_BW ** -0.5),
        'rg_bx': nrm((N_A, 2, D_RNN), 0.02),
        'rg_lam': jnp.log(u) - jnp.log1p(-u),
        'rg_w_out': nrm((N_A, D_RNN, D), D_RNN ** -0.5),
        'na_w_in': nrm((N_B, D, 3 * D), D ** -0.5),
        'na_rpb': nrm((N_B, NA_HEADS, 2 * NA_KR - 1, 2 * NA_KC - 1), 0.1),
        'na_w_out': nrm((N_B, D, D), D ** -0.5),
        'gqa_w_in': nrm((N_C, D, (GQA_HEADS + 2 * GQA_KV) * GQA_DH), D ** -0.5),
        'gqa_q_norm': gain((N_C, GQA_DH)),
        'gqa_k_norm': gain((N_C, GQA_DH)),
        'gqa_w_out': nrm((N_C, GQA_HEADS * GQA_DH, D), (GQA_HEADS * GQA_DH) ** -0.5),
        'diff_w_in': nrm((N_D, D, 3 * D), D ** -0.5),
        'diff_lq1': nrm((N_D, DIFF_DH), 0.1),
        'diff_lk1': nrm((N_D, DIFF_DH), 0.1),
        'diff_lq2': nrm((N_D, DIFF_DH), 0.1),
        'diff_lk2': nrm((N_D, DIFF_DH), 0.1),
        'diff_subln_g': gain((N_D, 2 * DIFF_DH)),
        'diff_w_out': nrm((N_D, D, D), D ** -0.5),
        'ffn_w_up': nrm((DEPTH, D, 2 * D_FF), D ** -0.5),
        'ffn_conv_w': nrm((DEPTH, FFN_CONV, 2 * D_FF), FFN_CONV ** -0.5),
        'ffn_conv_b': nrm((DEPTH, 2 * D_FF), 0.02),
        'ffn_w_down': nrm((DEPTH, D_FF, D), D_FF ** -0.5),
        'final_g': gain((D,)),
    }


def reference(x, c, ctx, c_ctx, mod_w, mod_b, norm1_g, norm2_g,
              rg_w_in, rg_conv_w, rg_conv_b, rg_wa, rg_ba, rg_wx, rg_bx, rg_lam, rg_w_out,
              na_w_in, na_rpb, na_w_out,
              gqa_w_in, gqa_q_norm, gqa_k_norm, gqa_w_out,
              diff_w_in, diff_lq1, diff_lk1, diff_lq2, diff_lk2, diff_subln_g, diff_w_out,
              ffn_w_up, ffn_conv_w, ffn_conv_b, ffn_w_down, final_g):
    xl, xc = x, ctx
    silu_c, silu_cc = jax.nn.silu(c), jax.nn.silu(c_ctx)
    for l in range(DEPTH):
        m, j = l % N_MIXERS, l // N_MIXERS
        need_ctx = l < DEPTH - 1
        mods = jnp.split((silu_c @ mod_w[l] + mod_b[l])[:, None, :], 6, axis=-1)
        cmods = jnp.split((silu_cc @ mod_w[l] + mod_b[l])[None, None, :], 6, axis=-1)
        zl = modulate(rmsnorm(xl, norm1_g[l]), mods[0], mods[1])
        zc = modulate(rmsnorm(xc, norm1_g[l]), cmods[0], cmods[1])
        if m == 0:
            yl, yc = rglru_mixer(zl, zc, rg_w_in[j], rg_conv_w[j], rg_conv_b[j], rg_wa[j], rg_ba[j],
                                 rg_wx[j], rg_bx[j], rg_lam[j], rg_w_out[j], need_ctx)
        elif m == 1:
            yl, yc = na_mixer(zl, zc, na_w_in[j], na_rpb[j], na_w_out[j], need_ctx)
        elif m == 2:
            yl, yc = gqa_mixer(zl, zc, gqa_w_in[j], gqa_q_norm[j], gqa_k_norm[j], gqa_w_out[j], need_ctx)
        else:
            lambda_init = 0.8 - 0.6 * math.exp(-0.3 * l)
            yl, yc = diff_mixer(zl, zc, diff_w_in[j], diff_lq1[j], diff_lk1[j], diff_lq2[j], diff_lk2[j],
                                diff_subln_g[j], diff_w_out[j], lambda_init, need_ctx)
        xl = xl + mods[2] * yl
        zl = modulate(rmsnorm(xl, norm2_g[l]), mods[3], mods[4])
        xl = xl + mods[5] * conv_ffn(zl, ffn_w_up[l], ffn_conv_w[l], ffn_conv_b[l], ffn_w_down[l])
        if need_ctx:
            xc = xc + cmods[2] * yc
            zc = modulate(rmsnorm(xc, norm2_g[l]), cmods[3], cmods[4])
            xc = xc + cmods[5] * conv_ffn(zc, ffn_w_up[l], ffn_conv_w[l], ffn_conv_b[l], ffn_w_down[l])
    return rmsnorm(xl, final_g)
```

```python
import functools
import math

import jax
import jax.numpy as jnp
from jax import lax
from jax.experimental import pallas as pl
from jax.experimental.pallas import tpu as pltpu

F32 = jnp.float32
BF16 = jnp.bfloat16
EPS = 1e-6
ROPE_THETA = 10000.0
GRID_W = 64
RG_C = 8.0
RG_CONV_LEFT = 2
N_MIXERS = 4
HEAD_DIM = 64
LANES = 128
SUBLANES = 8
NEG = -1e30
VMEM_LIMIT_BYTES = 56 * 1024 * 1024

_NT = (((1,), (1,)), ((), ()))


def _cparams(sem):
    return pltpu.CompilerParams(dimension_semantics=sem, vmem_limit_bytes=VMEM_LIMIT_BYTES)


def _dot(a, b):
    return jnp.dot(a, b, preferred_element_type=F32)


def _dot_nt(a, b):
    return lax.dot_general(a, b, _NT, preferred_element_type=F32)


def _silu(x):
    return x / (1.0 + jnp.exp(-x))


def _gelu_tanh(x):
    cdf = 0.5 * (1.0 + jnp.tanh(math.sqrt(2.0 / math.pi) * (x + 0.044715 * (x * x * x))))
    return x * cdf


def _rms(x, g):
    y = x * lax.rsqrt(jnp.mean(x * x, axis=-1, keepdims=True) + EPS)
    return y * g


def _normmod(x, g, shift, scale):
    return _rms(x, g) * (1.0 + scale) + shift


def _mods_kernel(cc_ref, w_ref, b_ref, o_ref):
    a = _silu(cc_ref[...]).astype(BF16)
    o_ref[0] = _dot(a, w_ref[0].astype(BF16)) + b_ref[0]


def _mods_call(cc, mod_w, mod_b):
    depth, d, n = mod_w.shape
    r = cc.shape[0]
    tn = n // 4
    return pl.pallas_call(
        _mods_kernel,
        out_shape=jax.ShapeDtypeStruct((depth, r, n), F32),
        grid=(depth, n // tn),
        in_specs=[
            pl.BlockSpec((r, d), lambda l, j: (0, 0)),
            pl.BlockSpec((1, d, tn), lambda l, j: (l, 0, j)),
            pl.BlockSpec((1, 1, tn), lambda l, j: (l, 0, j)),
        ],
        out_specs=pl.BlockSpec((1, r, tn), lambda l, j: (l, 0, j)),
        compiler_params=_cparams(("arbitrary", "arbitrary")),
        name="mods",
    )(cc, mod_w, mod_b.reshape(depth, 1, n))


def _mod_spec(d, row):
    if row is None:
        return pl.BlockSpec((1, 6, d), lambda b, *_: (b, 0, 0))
    return pl.BlockSpec((1, 6, d), lambda b, *_: (row, 0, 0))


def _row_tile(n, pref):
    t = min(n, pref)
    assert n % t == 0
    return t


def _rope_tables(n_tok):
    t = jnp.arange(n_tok)
    row = (t // GRID_W).astype(F32)
    col = (t % GRID_W).astype(F32)
    n = HEAD_DIM // 4
    inv = ROPE_THETA ** (-jnp.arange(n, dtype=F32) / n)
    ang = jnp.concatenate([row[:, None] * inv, col[:, None] * inv], axis=-1)
    cos, sin = jnp.cos(ang), jnp.sin(ang)
    reps = LANES // HEAD_DIM
    cos_t = jnp.tile(jnp.concatenate([cos, cos], axis=-1), (1, reps))
    sin_t = jnp.tile(jnp.concatenate([-sin, sin], axis=-1), (1, reps))
    return cos_t, sin_t


def _swap_halves(x):
    lane = lax.broadcasted_iota(jnp.int32, x.shape, 1)
    lo = (lane & (HEAD_DIM // 2)) == 0
    return jnp.where(lo, pltpu.roll(x, LANES - HEAD_DIM // 2, 1), pltpu.roll(x, HEAD_DIM // 2, 1))


def _rope_cols(u, cos_t, sin_t):
    outs = []
    for c in range(u.shape[1] // LANES):
        xc = u[:, c * LANES:(c + 1) * LANES]
        outs.append(xc * cos_t + _swap_halves(xc) * sin_t)
    return jnp.concatenate(outs, axis=1)


def _rg_proj_kernel(x_ref, mod_ref, g_ref, w_ref, go_ref, xo_ref, *, dr):
    z = _normmod(x_ref[0], g_ref[...], mod_ref[0, 0:1], mod_ref[0, 1:2]).astype(BF16)
    u = _dot(z, w_ref[...])
    go_ref[0] = _gelu_tanh(u[:, :dr])
    xo_ref[0] = u[:, dr:]


def _rg_proj_call(x, mods_l, mod_row, g, w):
    b, n, d = x.shape
    dr = w.shape[1] // 2
    tm = _row_tile(n, 512)
    return pl.pallas_call(
        functools.partial(_rg_proj_kernel, dr=dr),
        out_shape=(jax.ShapeDtypeStruct((b, n, dr), F32), jax.ShapeDtypeStruct((b, n, dr), F32)),
        grid=(b, n // tm),
        in_specs=[
            pl.BlockSpec((1, tm, d), lambda bi, i: (bi, i, 0)),
            _mod_spec(d, mod_row),
            pl.BlockSpec((1, d), lambda bi, i: (0, 0)),
            pl.BlockSpec((d, 2 * dr), lambda bi, i: (0, 0)),
        ],
        out_specs=(pl.BlockSpec((1, tm, dr), lambda bi, i: (bi, i, 0)),
                   pl.BlockSpec((1, tm, dr), lambda bi, i: (bi, i, 0))),
        compiler_params=_cparams(("arbitrary", "arbitrary")),
        name="rg_proj",
    )(x, mods_l, g, w)


def _qkv_proj_kernel(x_ref, mod_ref, g_ref, w_ref, *rest, d, rope, qknorm):
    rest = list(rest)
    if rope:
        cos_ref, sin_ref = rest.pop(0), rest.pop(0)
    if qknorm:
        gain_ref, ind_ref, indt_ref = rest.pop(0), rest.pop(0), rest.pop(0)
    q_ref, k_ref, v_ref = rest
    z = _normmod(x_ref[0], g_ref[...], mod_ref[0, 0:1], mod_ref[0, 1:2]).astype(BF16)
    u = _dot(z, w_ref[...])
    nk = k_ref.shape[-1]
    qk = u[:, :d + nk]
    if qknorm:
        x2 = qk * qk
        hi = x2.astype(BF16)
        lo = (x2 - hi.astype(F32)).astype(BF16)
        ssq = _dot(hi, ind_ref[...]) + _dot(lo, ind_ref[...])
        r = lax.rsqrt(ssq * (1.0 / HEAD_DIM) + EPS)
        rhi = r.astype(BF16)
        rlo = (r - rhi.astype(F32)).astype(BF16)
        rb = _dot(rhi, indt_ref[...]) + _dot(rlo, indt_ref[...])
        qk = (qk * rb) * gain_ref[...]
    if rope:
        qk = _rope_cols(qk, cos_ref[...], sin_ref[...])
    q_ref[0] = (qk[:, :d] * (HEAD_DIM ** -0.5)).astype(q_ref.dtype)
    k_ref[0] = qk[:, d:].astype(k_ref.dtype)
    v_ref[0] = u[:, d + nk:].astype(v_ref.dtype)


def _qkv_proj_call(x, mods_l, mod_row, g, w, nk, *, rope=None, qknorm=None):
    b, n, d = x.shape
    nv = w.shape[1] - d - nk
    tm = _row_tile(n, 512)
    const = lambda bi, i: (0, 0)
    in_specs = [
        pl.BlockSpec((1, tm, d), lambda bi, i: (bi, i, 0)),
        _mod_spec(d, mod_row),
        pl.BlockSpec((1, d), const),
        pl.BlockSpec(w.shape, const),
    ]
    args = [x, mods_l, g, w]
    if rope is not None:
        in_specs += [pl.BlockSpec((tm, LANES), lambda bi, i: (i, 0))] * 2
        args += list(rope)
    if qknorm is not None:
        in_specs += [pl.BlockSpec(a.shape, const) for a in qknorm]
        args += list(qknorm)
    out_spec = lambda w_: pl.BlockSpec((1, tm, w_), lambda bi, i: (bi, i, 0))
    return pl.pallas_call(
        functools.partial(_qkv_proj_kernel, d=d, rope=rope is not None, qknorm=qknorm is not None),
        out_shape=(jax.ShapeDtypeStruct((b, n, d), BF16), jax.ShapeDtypeStruct((b, n, nk), BF16),
                   jax.ShapeDtypeStruct((b, n, nv), BF16)),
        grid=(b, n // tm),
        in_specs=in_specs,
        out_specs=(out_spec(d), out_spec(nk), out_spec(nv)),
        compiler_params=_cparams(("arbitrary", "arbitrary")),
        name="qkv_proj",
    )(*args)


def _shift_rows(big, k, t):
    n = big.shape[0]
    return pltpu.roll(big, k % n, 0)[SUBLANES:SUBLANES + t]


def _rg_conv(x, prev, nxt, cw):
    t = x.shape[0]
    big = jnp.concatenate([prev, x, nxt], axis=0)
    y = cw[4:5] + cw[0:1] * _shift_rows(big, 2, t)
    y = y + cw[1:2] * _shift_rows(big, 1, t)
    y = y + cw[2:3] * x
    y = y + cw[3:4] * _shift_rows(big, -1, t)
    return y


def _rg_gates(xc, w, ba, bx, lam):
    gw = xc.shape[1]
    y = _dot(xc.astype(BF16), w)
    r = jax.nn.sigmoid(y[:, :gw] + ba)
    i = jax.nn.sigmoid(y[:, gw:] + bx)
    nl = -lam
    softplus = jnp.maximum(nl, 0.0) + jnp.log1p(jnp.exp(-jnp.abs(nl)))
    log_a = -RG_C * r * softplus
    th = jnp.tanh(log_a)
    one_minus_a2 = -2.0 * th / (1.0 - th)
    return jnp.exp(log_a), jnp.sqrt(one_minus_a2) * (i * xc)


def _rg_scan_tile(a, b, carry, reverse):
    t = a.shape[0]
    rmod = lax.broadcasted_iota(jnp.int32, (t, 1), 0) & (SUBLANES - 1)
    for s in (1, 2, 4):
        if reverse:
            ok = rmod < SUBLANES - s
            ash, bsh = pltpu.roll(a, t - s, 0), pltpu.roll(b, t - s, 0)
        else:
            ok = rmod >= s
            ash, bsh = pltpu.roll(a, s, 0), pltpu.roll(b, s, 0)
        b = a * jnp.where(ok, bsh, 0.0) + b
        a = a * jnp.where(ok, ash, 1.0)
    nck = t // SUBLANES
    hs = [None] * nck
    for ck in (range(nck - 1, -1, -1) if reverse else range(nck)):
        sl = slice(ck * SUBLANES, (ck + 1) * SUBLANES)
        hc = b[sl] + a[sl] * carry
        carry = hc[0:1] if reverse else hc[SUBLANES - 1:SUBLANES]
        hs[ck] = hc
    return jnp.concatenate(hs, axis=0), carry


def _rg_scan_kernel(xc_ref, gc_ref, xl_ref, xp_ref, xn_ref, gl_ref, cw_ref, w_ref, gp_ref,
                    mc_ref, ml_ref, hf_scr, car_scr, *, nt):
    s = pl.program_id(2)
    gw = xl_ref.shape[-1]
    tl = xl_ref.shape[1]
    cw = cw_ref[0]
    gp = gp_ref[0]

    def gates(xconv, d):
        w = w_ref[0, :, d * 2 * gw:(d + 1) * 2 * gw]
        return _rg_gates(xconv, w, gp[3 * d:3 * d + 1], gp[3 * d + 1:3 * d + 2], gp[3 * d + 2:3 * d + 3])

    @pl.when(s == 0)
    def _ctx():
        zeros8 = jnp.zeros((SUBLANES, gw), F32)
        xconv = _rg_conv(xc_ref[0], zeros8, zeros8, cw)
        zero = jnp.zeros((1, gw), F32)
        a0, b0 = gates(xconv, 0)
        hf, cf = _rg_scan_tile(a0, b0, zero, False)
        a1, b1 = gates(xconv, 1)
        hb, cb = _rg_scan_tile(a1, b1, zero, True)
        car_scr[0:1] = cf
        car_scr[1:2] = cb
        mc_ref[0] = (gc_ref[0] * (hf + hb)).astype(mc_ref.dtype)

    def lat_conv(t):
        prev = xp_ref[0] * (t > 0).astype(F32)
        nxt = xn_ref[0] * (t < nt - 1).astype(F32)
        return _rg_conv(xl_ref[0], prev, nxt, cw)

    @pl.when((s >= 1) & (s <= nt))
    def _fwd():
        t = s - 1
        a0, b0 = gates(lat_conv(t), 0)
        h, c = _rg_scan_tile(a0, b0, car_scr[0:1], False)
        car_scr[0:1] = c
        hf_scr[pl.ds(pl.multiple_of(t * tl, tl), tl), :] = h

    @pl.when(s > nt)
    def _bwd():
        t = 2 * nt - s
        a1, b1 = gates(lat_conv(t), 1)
        h, c = _rg_scan_tile(a1, b1, car_scr[1:2], True)
        car_scr[1:2] = c
        hf = hf_scr[pl.ds(pl.multiple_of(t * tl, tl), tl), :]
        ml_ref[0] = (gl_ref[0] * (hf + h)).astype(ml_ref.dtype)


def _rg_scan_call(xr_c, g_c, xr_l, g_l, cw, wbd, gp):
    b, s_len, dr = xr_l.shape
    c_len = xr_c.shape[1]
    ng, gw = wbd.shape[0], wbd.shape[1]
    tl = _row_tile(s_len, 512)
    nt = s_len // tl
    nb8 = tl // SUBLANES

    def tile(s):
        return jnp.where(s <= nt, jnp.maximum(s - 1, 0), 2 * nt - s)

    def tile_bwd(s):
        return jnp.where(s <= nt, nt - 1, 2 * nt - s)

    return pl.pallas_call(
        functools.partial(_rg_scan_kernel, nt=nt),
        out_shape=(jax.ShapeDtypeStruct((b, c_len, dr), BF16), jax.ShapeDtypeStruct((b, s_len, dr), BF16)),
        grid=(b, ng, 2 * nt + 1),
        in_specs=[
            pl.BlockSpec((1, c_len, gw), lambda bi, h, s: (bi, 0, h)),
            pl.BlockSpec((1, c_len, gw), lambda bi, h, s: (bi, 0, h)),
            pl.BlockSpec((1, tl, gw), lambda bi, h, s: (bi, tile(s), h)),
            pl.BlockSpec((1, SUBLANES, gw), lambda bi, h, s: (bi, jnp.maximum(tile(s) * nb8 - 1, 0), h)),
            pl.BlockSpec((1, SUBLANES, gw),
                         lambda bi, h, s: (bi, jnp.minimum((tile(s) + 1) * nb8, s_len // SUBLANES - 1), h)),
            pl.BlockSpec((1, tl, gw), lambda bi, h, s: (bi, tile_bwd(s), h)),
            pl.BlockSpec((1, SUBLANES, gw), lambda bi, h, s: (h, 0, 0)),
            pl.BlockSpec((1, gw, 4 * gw), lambda bi, h, s: (h, 0, 0)),
            pl.BlockSpec((1, SUBLANES, gw), lambda bi, h, s: (h, 0, 0)),
        ],
        out_specs=(pl.BlockSpec((1, c_len, gw), lambda bi, h, s: (bi, 0, h)),
                   pl.BlockSpec((1, tl, gw), lambda bi, h, s: (bi, tile_bwd(s), h))),
        scratch_shapes=[pltpu.VMEM((s_len, gw), F32), pltpu.VMEM((SUBLANES, gw), F32)],
        compiler_params=_cparams(("arbitrary", "arbitrary", "arbitrary")),
        name="rg_scan",
    )(xr_c, g_c, xr_l, xr_l, xr_l, g_l, cw, wbd, gp)


def _rg_pack_params(conv_w, conv_b, wa, ba, wx, bx, lam):
    nblk, bw = wa.shape[1], wa.shape[2]
    dr = nblk * bw
    per = next(p for p in range(1, nblk + 1) if nblk % p == 0 and (p * bw) % LANES == 0)
    ng, gw = nblk // per, per * bw

    def dense(w):
        w = w.reshape(ng, per, bw, bw)
        eye = jnp.eye(per, dtype=w.dtype)
        return jnp.einsum('gpjk,pq->gpjqk', w, eye).reshape(ng, gw, gw)

    wbd = jnp.concatenate([dense(wa[0]), dense(wx[0]), dense(wa[1]), dense(wx[1])], axis=-1).astype(BF16)
    cw = jnp.concatenate([conv_w, conv_b[None], jnp.zeros((SUBLANES - 1 - conv_w.shape[0], dr), F32)], axis=0)
    cw = cw.reshape(SUBLANES, ng, gw).transpose(1, 0, 2)
    gp = jnp.stack([ba[0], bx[0], lam[0], ba[1], bx[1], lam[1], jnp.zeros_like(lam[0]), jnp.zeros_like(lam[0])])
    gp = gp.reshape(SUBLANES, ng, gw).transpose(1, 0, 2)
    return cw, wbd, gp


def _pair_masks(shape):
    lane = lax.broadcasted_iota(jnp.int32, shape, 1)
    return lane < HEAD_DIM


def _softmax_weights(scores):
    m = scores[0].max(axis=-1, keepdims=True)
    for s in scores[1:]:
        m = jnp.maximum(m, s.max(axis=-1, keepdims=True))
    es = [jnp.exp(s - m) for s in scores]
    l = es[0].sum(axis=-1, keepdims=True)
    for e in es[1:]:
        l = l + e.sum(axis=-1, keepdims=True)
    return es, l


def _plain_attn_kernel(q_ref, *refs, nsrc):
    o_ref = refs[-1]
    ks = [refs[2 * i][0] for i in range(nsrc)]
    vs = [refs[2 * i + 1][0] for i in range(nsrc)]
    q = q_ref[0]
    first = _pair_masks(q.shape)
    outs = []
    for hh in range(2):
        qh = jnp.where(first if hh == 0 else jnp.logical_not(first), q, jnp.zeros_like(q))
        es, l = _softmax_weights([_dot_nt(qh, k) for k in ks])
        o = _dot(es[0].astype(BF16), vs[0])
        for e, v in zip(es[1:], vs[1:]):
            o = o + _dot(e.astype(BF16), v)
        outs.append(o / l)
    o_ref[0] = jnp.where(first, outs[0], outs[1]).astype(o_ref.dtype)


def _plain_attn_call(q, kvs, kv_block):
    b, lq, d = q.shape
    tq = _row_tile(lq, 256)
    in_specs = [pl.BlockSpec((1, tq, LANES), lambda bi, j, i: (bi, i, j))]
    args = [q]
    for k, v in kvs:
        spec = pl.BlockSpec((1, k.shape[1], LANES), lambda bi, j, i: (bi, 0, kv_block(j)))
        in_specs += [spec, spec]
        args += [k, v]
    return pl.pallas_call(
        functools.partial(_plain_attn_kernel, nsrc=len(kvs)),
        out_shape=jax.ShapeDtypeStruct((b, lq, d), BF16),
        grid=(b, d // LANES, lq // tq),
        in_specs=in_specs,
        out_specs=pl.BlockSpec((1, tq, LANES), lambda bi, j, i: (bi, i, j)),
        compiler_params=_cparams(("arbitrary", "arbitrary", "arbitrary")),
        name="plain_attn",
    )(*args)


def _diff_attn_kernel(q_ref, lam_ref, g_ref, *refs, nsrc, lambda_init):
    o_ref = refs[-1]
    ks = [refs[2 * i][0] for i in range(nsrc)]
    vs = [refs[2 * i + 1][0] for i in range(nsrc)]
    lp = lam_ref[...]
    lam = (jnp.exp(jnp.sum(lp[0:1] * lp[1:2], axis=-1, keepdims=True))
           - jnp.exp(jnp.sum(lp[2:3] * lp[3:4], axis=-1, keepdims=True)) + lambda_init)
    q = q_ref[0]
    first = _pair_masks(q.shape)
    ps = []
    for hh in range(2):
        qh = jnp.where(first if hh == 0 else jnp.logical_not(first), q, jnp.zeros_like(q))
        es, l = _softmax_weights([_dot_nt(qh, k) for k in ks])
        ps.append((es, 1.0 / l))
    o = None
    for i in range(nsrc):
        a = (ps[0][0][i] * ps[0][1] - lam * (ps[1][0][i] * ps[1][1])).astype(BF16)
        o = _dot(a, vs[i]) if o is None else o + _dot(a, vs[i])
    o_ref[0] = (_rms(o, g_ref[...]) * (1.0 - lambda_init)).astype(o_ref.dtype)


def _diff_attn_call(q, kvs, lam_params, subln_g, lambda_init):
    b, lq, d = q.shape
    tq = _row_tile(lq, 256)
    const = lambda bi, j, i: (0, 0)
    in_specs = [pl.BlockSpec((1, tq, LANES), lambda bi, j, i: (bi, i, j)),
                pl.BlockSpec(lam_params.shape, const), pl.BlockSpec(subln_g.shape, const)]
    args = [q, lam_params, subln_g]
    for k, v in kvs:
        spec = pl.BlockSpec((1, k.shape[1], LANES), lambda bi, j, i: (bi, 0, j))
        in_specs += [spec, spec]
        args += [k, v]
    return pl.pallas_call(
        functools.partial(_diff_attn_kernel, nsrc=len(kvs), lambda_init=lambda_init),
        out_shape=jax.ShapeDtypeStruct((b, lq, d), BF16),
        grid=(b, d // LANES, lq // tq),
        in_specs=in_specs,
        out_specs=pl.BlockSpec((1, tq, LANES), lambda bi, j, i: (bi, i, j)),
        compiler_params=_cparams(("arbitrary", "arbitrary", "arbitrary")),
        name="diff_attn",
    )(*args)


def _na_attn_kernel(q_ref, k_ref, v_ref, kc_ref, vc_ref, rpb_ref, o_ref, t2_scr, *, rows, kr, kcw, nq):
    w = GRID_W
    nwin = nq + kr
    ntab = 2 * kr
    i = pl.program_id(2)

    @pl.when(i == 0)
    def _build_bias():
        c_io = lax.broadcasted_iota(jnp.int32, (w, LANES), 0)
        cp_io = lax.broadcasted_iota(jnp.int32, (w, LANES), 1)
        cs = jnp.clip(c_io - kcw // 2, 0, w - kcw)
        colok = (cp_io >= cs) & (cp_io < cs + kcw)
        neg = jnp.full((w, LANES), NEG, F32)
        for hh in range(2):
            r = pltpu.roll(rpb_ref[hh], LANES - (kcw - 1), 1)
            ts = []
            for e in range(ntab - 1):
                t = jnp.broadcast_to(r[e:e + 1], (w, LANES))
                t = pltpu.roll(t, 0, 1, stride=1, stride_axis=0)
                ts.append(jnp.where(colok, t, NEG))
            for e in range(ntab):
                lo = ts[e - 1] if e >= 1 else neg
                hi = ts[e] if e < ntab - 1 else neg
                t2_scr[hh, e] = jnp.where(cp_io < w, lo, pltpu.roll(hi, w, 1))

    r0 = i * nq
    ks = jnp.clip(r0 - kr // 2, 0, rows - nwin)
    start = pl.multiple_of(ks * w, w)
    kw = k_ref[0, pl.ds(start, nwin * w), :]
    vw = v_ref[0, pl.ds(start, nwin * w), :]
    kc = kc_ref[0]
    vc = vc_ref[0]
    q = q_ref[0]
    first = _pair_masks(q.shape)
    key_row = ks + jnp.right_shift(lax.broadcasted_iota(jnp.int32, (1, nwin * w), 1), int(math.log2(w)))
    outs = []
    for hh in range(2):
        qh = jnp.where(first if hh == 0 else jnp.logical_not(first), q, jnp.zeros_like(q))
        s = _dot_nt(qh, kw)
        sc = _dot_nt(qh, kc)
        parts = []
        for qi in range(nq):
            r = r0 + qi
            rs = jnp.clip(r - kr // 2, 0, rows - kr)
            rowok = (key_row >= rs) & (key_row < rs + kr)
            blocks = [t2_scr[hh, jnp.clip(ks + 2 * p - r + kr, 0, ntab - 1)] for p in range(nwin // 2)]
            sq = s[qi * w:(qi + 1) * w] + jnp.concatenate(blocks, axis=1)
            parts.append(jnp.where(rowok, sq, NEG))
        (e, ec), l = _softmax_weights([jnp.concatenate(parts, axis=0), sc])
        o = _dot(e.astype(BF16), vw) + _dot(ec.astype(BF16), vc)
        outs.append(o / l)
    o_ref[0] = jnp.where(first, outs[0], outs[1]).astype(o_ref.dtype)


def _na_attn_call(q, k, v, kc, vc, rpb):
    b, s_len, d = q.shape
    c_len = kc.shape[1]
    nh, ndr, ndc = rpb.shape
    kr, kcw = (ndr + 1) // 2, (ndc + 1) // 2
    rows = s_len // GRID_W
    nq = 4
    assert GRID_W == HEAD_DIM and rows % nq == 0 and rows >= nq + kr and (nq + kr) % 2 == 0
    assert ndc <= LANES and 2 * kr <= 2 * SUBLANES
    rpb_p = jnp.full((nh, 2 * SUBLANES, LANES), NEG, F32).at[:, :ndr, :ndc].set(rpb)
    tq = nq * GRID_W
    full = lambda n: pl.BlockSpec((1, n, LANES), lambda bi, j, i: (bi, 0, j))
    return pl.pallas_call(
        functools.partial(_na_attn_kernel, rows=rows, kr=kr, kcw=kcw, nq=nq),
        out_shape=jax.ShapeDtypeStruct((b, s_len, d), BF16),
        grid=(b, d // LANES, s_len // tq),
        in_specs=[pl.BlockSpec((1, tq, LANES), lambda bi, j, i: (bi, i, j)),
                  full(s_len), full(s_len), full(c_len), full(c_len),
                  pl.BlockSpec((2, 2 * SUBLANES, LANES), lambda bi, j, i: (j, 0, 0))],
        out_specs=pl.BlockSpec((1, tq, LANES), lambda bi, j, i: (bi, i, j)),
        scratch_shapes=[pltpu.VMEM((2, 2 * kr, GRID_W, LANES), F32)],
        compiler_params=_cparams(("arbitrary", "arbitrary", "arbitrary")),
        name="na_attn",
    )(q, k, v, kc, vc, rpb_p)


def _outproj_kernel(a_ref, x_ref, mod_ref, w_ref, o_ref):
    o_ref[0] = x_ref[0] + mod_ref[0, 2:3] * _dot(a_ref[0], w_ref[...])


def _outproj_call(a, x, mods_l, mod_row, w):
    b, n, d = x.shape
    k = a.shape[-1]
    tm = _row_tile(n, 512)
    return pl.pallas_call(
        _outproj_kernel,
        out_shape=jax.ShapeDtypeStruct((b, n, d), F32),
        grid=(b, n // tm),
        in_specs=[pl.BlockSpec((1, tm, k), lambda bi, i: (bi, i, 0)),
                  pl.BlockSpec((1, tm, d), lambda bi, i: (bi, i, 0)),
                  _mod_spec(d, mod_row),
                  pl.BlockSpec((k, d), lambda bi, i: (0, 0))],
        out_specs=pl.BlockSpec((1, tm, d), lambda bi, i: (bi, i, 0)),
        compiler_params=_cparams(("arbitrary", "arbitrary")),
        name="outproj",
    )(a, x, mods_l, w)


def _ffn_kernel(x_ref, xp_ref, xn_ref, mod_ref, g_ref, wu_ref, cw_ref, wd_ref, *rest, fc, final):
    if final:
        fg_ref, o_ref = rest
    else:
        (o_ref,) = rest
    i = pl.program_id(1)
    nt = pl.num_programs(1)
    x = x_ref[0]
    tm, d = x.shape
    g, shift, scale, gate = g_ref[...], mod_ref[0, 3:4], mod_ref[0, 4:5], mod_ref[0, 5:6]
    z = _normmod(x, g, shift, scale).astype(BF16)
    zh = _normmod(jnp.concatenate([xp_ref[0], xn_ref[0]], axis=0), g, shift, scale).astype(BF16)
    has_prev = (i > 0).astype(F32)
    has_next = (i < nt - 1).astype(F32)
    row = lax.broadcasted_iota(jnp.int32, (tm, 1), 0)

    def body(j, acc):
        w = wu_ref[j]
        u = _dot(z, w)
        uh = _dot(zh, w)
        prev = uh[SUBLANES - 1:SUBLANES] * has_prev
        nxt = uh[SUBLANES:SUBLANES + 1] * has_next
        dn = jnp.where(row == 0, prev, pltpu.roll(u, 1, 0))
        up = jnp.where(row == tm - 1, nxt, pltpu.roll(u, tm - 1, 0))
        cw = cw_ref[j]
        cv = cw[3:4] + cw[0:1] * dn
        cv = cv + cw[1:2] * u
        cv = cv + cw[2:3] * up
        h = _silu(cv[:, :fc]) * cv[:, fc:]
        return acc + _dot(h.astype(BF16), wd_ref[j])

    acc = lax.fori_loop(0, wu_ref.shape[0], body, jnp.zeros((tm, d), F32))
    out = x + gate * acc
    if final:
        out = _rms(out, fg_ref[...])
    o_ref[0] = out


def _ffn_pack_params(w_up, conv_w, conv_b, w_down, fc):
    d, f2 = w_up.shape
    f = f2 // 2
    nch = f // fc
    wu = jnp.concatenate([w_up[:, :f].reshape(d, nch, fc), w_up[:, f:].reshape(d, nch, fc)], axis=-1)
    wu = wu.transpose(1, 0, 2).astype(BF16)
    cw = jnp.concatenate([conv_w, conv_b[None], jnp.zeros((SUBLANES - 1 - conv_w.shape[0], f2), F32)], axis=0)
    cw = jnp.concatenate([cw[:, :f].reshape(SUBLANES, nch, fc), cw[:, f:].reshape(SUBLANES, nch, fc)], axis=-1)
    cw = cw.transpose(1, 0, 2)
    wd = w_down.reshape(nch, fc, d).astype(BF16)
    return wu, cw, wd


def _ffn_call(x, mods_l, mod_row, g, packed, final_g=None):
    b, n, d = x.shape
    wu, cw, wd = packed
    nch, _, fc2 = wu.shape
    tm = _row_tile(n, 512)
    nb8 = tm // SUBLANES
    const3 = lambda bi, i: (0, 0, 0)
    in_specs = [
        pl.BlockSpec((1, tm, d), lambda bi, i: (bi, i, 0)),
        pl.BlockSpec((1, SUBLANES, d), lambda bi, i: (bi, jnp.maximum(i * nb8 - 1, 0), 0)),
        pl.BlockSpec((1, SUBLANES, d), lambda bi, i: (bi, jnp.minimum((i + 1) * nb8, n // SUBLANES - 1), 0)),
        _mod_spec(d, mod_row),
        pl.BlockSpec((1, d), lambda bi, i: (0, 0)),
        pl.BlockSpec(wu.shape, const3),
        pl.BlockSpec(cw.shape, const3),
        pl.BlockSpec(wd.shape, const3),
    ]
    args = [x, x, x, mods_l, g, wu, cw, wd]
    if final_g is not None:
        in_specs.append(pl.BlockSpec((1, d), lambda bi, i: (0, 0)))
        args.append(final_g)
    return pl.pallas_call(
        functools.partial(_ffn_kernel, fc=fc2 // 2, final=final_g is not None),
        out_shape=jax.ShapeDtypeStruct((b, n, d), F32),
        grid=(b, n // tm),
        in_specs=in_specs,
        out_specs=pl.BlockSpec((1, tm, d), lambda bi, i: (bi, i, 0)),
        compiler_params=_cparams(("arbitrary", "arbitrary")),
        name="ffn",
    )(*args)


def _rg_mixer(xl, xc, mods_l, ctx_row, g, w_in, conv_w, conv_b, wa, ba, wx, bx, lam, need_ctx):
    w = w_in.astype(BF16)
    g_l, xr_l = _rg_proj_call(xl, mods_l, None, g, w)
    g_c, xr_c = _rg_proj_call(xc, mods_l, ctx_row, g, w)
    cw, wbd, gp = _rg_pack_params(conv_w, conv_b, wa, ba, wx, bx, lam)
    m_c, m_l = _rg_scan_call(xr_c, g_c, xr_l, g_l, cw, wbd, gp)
    return m_l, (m_c if need_ctx else None)


def _na_mixer(xl, xc, mods_l, ctx_row, g, w_in, rpb, need_ctx):
    d = xl.shape[-1]
    w = w_in.astype(BF16)
    ql, kl, vl = _qkv_proj_call(xl, mods_l, None, g, w, d)
    qc, kc, vc = _qkv_proj_call(xc, mods_l, ctx_row, g, w, d)
    o_l = _na_attn_call(ql, kl, vl, kc, vc, rpb)
    o_c = _plain_attn_call(qc, [(kc, vc)], lambda j: j) if need_ctx else None
    return o_l, o_c


def _gqa_mixer(xl, xc, mods_l, ctx_row, g, w_in, q_norm, k_norm, need_ctx):
    d = xl.shape[-1]
    nh = d // HEAD_DIM
    nkv = (w_in.shape[1] // HEAD_DIM - nh) // 2
    group = nh // nkv
    assert group % 2 == 0 and q_norm.shape[0] == HEAD_DIM
    dup = lambda wk: jnp.concatenate([wk.reshape(d, nkv, 1, HEAD_DIM)] * 2, axis=2).reshape(d, 2 * nkv * HEAD_DIM)
    nk = 2 * nkv * HEAD_DIM
    w = jnp.concatenate([w_in[:, :d], dup(w_in[:, d:d + nkv * HEAD_DIM]), dup(w_in[:, d + nkv * HEAD_DIM:])],
                        axis=1).astype(BF16)
    nseg = (d + nk) // HEAD_DIM
    assert nseg <= LANES
    seg = jnp.arange(d + nk) // HEAD_DIM
    ind = (seg[:, None] == jnp.arange(LANES)[None, :]).astype(BF16)
    gain = jnp.concatenate([jnp.tile(q_norm, nh), jnp.tile(k_norm, 2 * nkv)])[None]
    qkn = (gain, ind, ind.T)
    ql, kl, vl = _qkv_proj_call(xl, mods_l, None, g, w, nk, rope=_rope_tables(xl.shape[1]), qknorm=qkn)
    qc, kc, vc = _qkv_proj_call(xc, mods_l, ctx_row, g, w, nk, qknorm=qkn)
    kv_block = lambda j: j // (group // 2)
    o_l = _plain_attn_call(ql, [(kl, vl), (kc, vc)], kv_block)
    o_c = _plain_attn_call(qc, [(kc, vc)], kv_block) if need_ctx else None
    return o_l, o_c


def _diff_mixer(xl, xc, mods_l, ctx_row, g, w_in, lq1, lk1, lq2, lk2, subln_g, lambda_init, need_ctx):
    d = xl.shape[-1]
    assert lq1.shape[0] == HEAD_DIM and subln_g.shape[0] == LANES
    w = w_in.astype(BF16)
    ql, kl, vl = _qkv_proj_call(xl, mods_l, None, g, w, d, rope=_rope_tables(xl.shape[1]))
    qc, kc, vc = _qkv_proj_call(xc, mods_l, ctx_row, g, w, d)
    lam_params = jnp.concatenate([jnp.stack([lq1, lk1, lq2, lk2]), jnp.zeros((SUBLANES - 4, HEAD_DIM), F32)])
    o_l = _diff_attn_call(ql, [(kl, vl), (kc, vc)], lam_params, subln_g[None], lambda_init)
    o_c = _diff_attn_call(qc, [(kc, vc)], lam_params, subln_g[None], lambda_init) if need_ctx else None
    return o_l, o_c


def kernel(x, c, ctx, c_ctx, mod_w, mod_b, norm1_g, norm2_g, rg_w_in, rg_conv_w, rg_conv_b, rg_wa, rg_ba, rg_wx, rg_bx, rg_lam, rg_w_out, na_w_in, na_rpb, na_w_out, gqa_w_in, gqa_q_norm, gqa_k_norm, gqa_w_out, diff_w_in, diff_lq1, diff_lk1, diff_lq2, diff_lk2, diff_subln_g, diff_w_out, ffn_w_up, ffn_conv_w, ffn_conv_b, ffn_w_down, final_g):
    bsz, _, d = x.shape
    depth = mod_w.shape[0]
    assert d % LANES == 0
    ctx_row = bsz
    nrow = -(-(bsz + 1) // SUBLANES) * SUBLANES
    cc = jnp.concatenate([c, c_ctx[None], jnp.zeros((nrow - bsz - 1, d), F32)], axis=0)
    mods = _mods_call(cc, mod_w, mod_b).reshape(depth, nrow, 6, d)
    f = ffn_w_down.shape[1]
    fc = next(t for t in (256, 128) if f % t == 0)
    xl, xc = x, ctx
    for l in range(depth):
        m, j = l % N_MIXERS, l // N_MIXERS
        need_ctx = l < depth - 1
        mods_l = mods[l]
        g1, g2 = norm1_g[l][None], norm2_g[l][None]
        if m == 0:
            a_l, a_c = _rg_mixer(xl, xc, mods_l, ctx_row, g1, rg_w_in[j], rg_conv_w[j], rg_conv_b[j], rg_wa[j],
                                 rg_ba[j], rg_wx[j], rg_bx[j], rg_lam[j], need_ctx)
            w_out = rg_w_out[j]
        elif m == 1:
            a_l, a_c = _na_mixer(xl, xc, mods_l, ctx_row, g1, na_w_in[j], na_rpb[j], need_ctx)
            w_out = na_w_out[j]
        elif m == 2:
            a_l, a_c = _gqa_mixer(xl, xc, mods_l, ctx_row, g1, gqa_w_in[j], gqa_q_norm[j], gqa_k_norm[j], need_ctx)
            w_out = gqa_w_out[j]
        else:
            lambda_init = 0.8 - 0.6 * math.exp(-0.3 * l)
            a_l, a_c = _diff_mixer(xl, xc, mods_l, ctx_row, g1, diff_w_in[j], diff_lq1[j], diff_lk1[j],
                                   diff_lq2[j], diff_lk2[j], diff_subln_g[j], lambda_init, need_ctx)
            w_out = diff_w_out[j]
        w_out = w_out.astype(BF16)
        packed = _ffn_pack_params(ffn_w_up[l], ffn_conv_w[l], ffn_conv_b[l], ffn_w_down[l], fc)
        last = l == depth - 1
        xl = _outproj_call(a_l, xl, mods_l, None, w_out)
        xl = _ffn_call(xl, mods_l, None, g2, packed, final_g[None] if last else None)
        if need_ctx:
            xc = _outproj_call(a_c, xc, mods_l, ctx_row, w_out)
            xc = _ffn_call(xc, mods_l, ctx_row, g2, packed)
    return xl
```

```python
import functools
import math

import jax
import jax.numpy as jnp
from jax import lax
from jax.experimental import pallas as pl
from jax.experimental.pallas import tpu as pltpu

F32 = jnp.float32
BF16 = jnp.bfloat16
EPS = 1e-6
ROPE_THETA = 10000.0
GRID_W = 64
RG_C = 8.0
RG_CONV_LEFT = 2
N_MIXERS = 4
HEAD_DIM = 64
LANES = 128
SUBLANES = 8
NEG = -1e30
FFN_HALO = 16
FFN_SUB = 64
VMEM_LIMIT_BYTES = 56 * 1024 * 1024

_NT = (((1,), (1,)), ((), ()))


def _cparams(sem):
    return pltpu.CompilerParams(dimension_semantics=sem, vmem_limit_bytes=VMEM_LIMIT_BYTES)


def _dot(a, b):
    return jnp.dot(a, b, preferred_element_type=F32)


def _dot_nt(a, b):
    return lax.dot_general(a, b, _NT, preferred_element_type=F32)


def _silu(x):
    return x / (1.0 + jnp.exp(-x))


def _gelu_tanh(x):
    cdf = 0.5 * (1.0 + jnp.tanh(math.sqrt(2.0 / math.pi) * (x + 0.044715 * (x * x * x))))
    return x * cdf


def _rms(x, g):
    y = x * lax.rsqrt(jnp.mean(x * x, axis=-1, keepdims=True) + EPS)
    return y * g


def _normmod(x, g, shift, scale):
    return _rms(x, g) * (1.0 + scale) + shift


def _mods_kernel(cc_ref, w_ref, b_ref, o_ref):
    a = _silu(cc_ref[...]).astype(BF16)
    o_ref[0] = _dot(a, w_ref[0].astype(BF16)) + b_ref[0]


def _mods_call(cc, mod_w, mod_b):
    depth, d, n = mod_w.shape
    r = cc.shape[0]
    tn = n // 4
    return pl.pallas_call(
        _mods_kernel,
        out_shape=jax.ShapeDtypeStruct((depth, r, n), F32),
        grid=(depth, n // tn),
        in_specs=[
            pl.BlockSpec((r, d), lambda l, j: (0, 0)),
            pl.BlockSpec((1, d, tn), lambda l, j: (l, 0, j)),
            pl.BlockSpec((1, 1, tn), lambda l, j: (l, 0, j)),
        ],
        out_specs=pl.BlockSpec((1, r, tn), lambda l, j: (l, 0, j)),
        compiler_params=_cparams(("arbitrary", "arbitrary")),
        name="mods",
    )(cc, mod_w, mod_b.reshape(depth, 1, n))


def _mod_spec(d, row):
    if row is None:
        return pl.BlockSpec((1, 6, d), lambda b, *_: (b, 0, 0))
    return pl.BlockSpec((1, 6, d), lambda b, *_: (row, 0, 0))


def _row_tile(n, pref):
    t = min(n, pref)
    assert n % t == 0
    return t


def _rope_tables(n_tok):
    t = jnp.arange(n_tok)
    row = (t // GRID_W).astype(F32)
    col = (t % GRID_W).astype(F32)
    n = HEAD_DIM // 4
    inv = ROPE_THETA ** (-jnp.arange(n, dtype=F32) / n)
    ang = jnp.concatenate([row[:, None] * inv, col[:, None] * inv], axis=-1)
    cos, sin = jnp.cos(ang), jnp.sin(ang)
    reps = LANES // HEAD_DIM
    cos_t = jnp.tile(jnp.concatenate([cos, cos], axis=-1), (1, reps))
    sin_t = jnp.tile(jnp.concatenate([-sin, sin], axis=-1), (1, reps))
    return cos_t, sin_t


def _swap_halves(x):
    lane = lax.broadcasted_iota(jnp.int32, x.shape, 1)
    lo = (lane & (HEAD_DIM // 2)) == 0
    return jnp.where(lo, pltpu.roll(x, LANES - HEAD_DIM // 2, 1), pltpu.roll(x, HEAD_DIM // 2, 1))


def _rope_cols(u, cos_t, sin_t):
    outs = []
    for c in range(u.shape[1] // LANES):
        xc = u[:, c * LANES:(c + 1) * LANES]
        outs.append(xc * cos_t + _swap_halves(xc) * sin_t)
    return jnp.concatenate(outs, axis=1)


def _rg_proj_kernel(x_ref, mod_ref, g_ref, w_ref, go_ref, xo_ref, *, dr):
    z = _normmod(x_ref[0], g_ref[...], mod_ref[0, 0:1], mod_ref[0, 1:2]).astype(BF16)
    u = _dot(z, w_ref[...])
    go_ref[0] = _gelu_tanh(u[:, :dr])
    xo_ref[0] = u[:, dr:]


def _rg_proj_call(x, mods_l, mod_row, g, w):
    b, n, d = x.shape
    dr = w.shape[1] // 2
    tm = _row_tile(n, 512)
    return pl.pallas_call(
        functools.partial(_rg_proj_kernel, dr=dr),
        out_shape=(jax.ShapeDtypeStruct((b, n, dr), F32), jax.ShapeDtypeStruct((b, n, dr), F32)),
        grid=(b, n // tm),
        in_specs=[
            pl.BlockSpec((1, tm, d), lambda bi, i: (bi, i, 0)),
            _mod_spec(d, mod_row),
            pl.BlockSpec((1, d), lambda bi, i: (0, 0)),
            pl.BlockSpec((d, 2 * dr), lambda bi, i: (0, 0)),
        ],
        out_specs=(pl.BlockSpec((1, tm, dr), lambda bi, i: (bi, i, 0)),
                   pl.BlockSpec((1, tm, dr), lambda bi, i: (bi, i, 0))),
        compiler_params=_cparams(("arbitrary", "arbitrary")),
        name="rg_proj",
    )(x, mods_l, g, w)


def _qkv_proj_kernel(x_ref, mod_ref, g_ref, w_ref, *rest, d, rope, qknorm):
    rest = list(rest)
    if rope:
        cos_ref, sin_ref = rest.pop(0), rest.pop(0)
    if qknorm:
        gain_ref, ind_ref, indt_ref = rest.pop(0), rest.pop(0), rest.pop(0)
    q_ref, k_ref, v_ref = rest
    z = _normmod(x_ref[0], g_ref[...], mod_ref[0, 0:1], mod_ref[0, 1:2]).astype(BF16)
    u = _dot(z, w_ref[...])
    nk = k_ref.shape[-1]
    qk = u[:, :d + nk]
    if qknorm:
        x2 = qk * qk
        hi = x2.astype(BF16)
        lo = (x2 - hi.astype(F32)).astype(BF16)
        ssq = _dot(hi, ind_ref[...]) + _dot(lo, ind_ref[...])
        r = lax.rsqrt(ssq * (1.0 / HEAD_DIM) + EPS)
        rhi = r.astype(BF16)
        rlo = (r - rhi.astype(F32)).astype(BF16)
        rb = _dot(rhi, indt_ref[...]) + _dot(rlo, indt_ref[...])
        qk = (qk * rb) * gain_ref[...]
    if rope:
        qk = _rope_cols(qk, cos_ref[...], sin_ref[...])
    q_ref[0] = (qk[:, :d] * (HEAD_DIM ** -0.5)).astype(q_ref.dtype)
    k_ref[0] = qk[:, d:].astype(k_ref.dtype)
    v_ref[0] = u[:, d + nk:].astype(v_ref.dtype)


def _qkv_proj_call(x, mods_l, mod_row, g, w, nk, *, rope=None, qknorm=None):
    b, n, d = x.shape
    nv = w.shape[1] - d - nk
    tm = _row_tile(n, 512)
    const = lambda bi, i: (0, 0)
    in_specs = [
        pl.BlockSpec((1, tm, d), lambda bi, i: (bi, i, 0)),
        _mod_spec(d, mod_row),
        pl.BlockSpec((1, d), const),
        pl.BlockSpec(w.shape, const),
    ]
    args = [x, mods_l, g, w]
    if rope is not None:
        in_specs += [pl.BlockSpec((tm, LANES), lambda bi, i: (i, 0))] * 2
        args += list(rope)
    if qknorm is not None:
        in_specs += [pl.BlockSpec(a.shape, const) for a in qknorm]
        args += list(qknorm)
    out_spec = lambda w_: pl.BlockSpec((1, tm, w_), lambda bi, i: (bi, i, 0))
    return pl.pallas_call(
        functools.partial(_qkv_proj_kernel, d=d, rope=rope is not None, qknorm=qknorm is not None),
        out_shape=(jax.ShapeDtypeStruct((b, n, d), BF16), jax.ShapeDtypeStruct((b, n, nk), BF16),
                   jax.ShapeDtypeStruct((b, n, nv), BF16)),
        grid=(b, n // tm),
        in_specs=in_specs,
        out_specs=(out_spec(d), out_spec(nk), out_spec(nv)),
        compiler_params=_cparams(("arbitrary", "arbitrary")),
        name="qkv_proj",
    )(*args)


def _shift_rows(big, k, t):
    n = big.shape[0]
    return pltpu.roll(big, k % n, 0)[SUBLANES:SUBLANES + t]


def _rg_conv(x, prev, nxt, cw):
    t = x.shape[0]
    big = jnp.concatenate([prev, x, nxt], axis=0)
    y = cw[4:5] + cw[0:1] * _shift_rows(big, 2, t)
    y = y + cw[1:2] * _shift_rows(big, 1, t)
    y = y + cw[2:3] * x
    y = y + cw[3:4] * _shift_rows(big, -1, t)
    return y


def _rg_gates(xc, w, ba, bx, lam):
    gw = xc.shape[1]
    y = _dot(xc.astype(BF16), w)
    r = jax.nn.sigmoid(y[:, :gw] + ba)
    i = jax.nn.sigmoid(y[:, gw:] + bx)
    nl = -lam
    softplus = jnp.maximum(nl, 0.0) + jnp.log1p(jnp.exp(-jnp.abs(nl)))
    log_a = -RG_C * r * softplus
    th = jnp.tanh(log_a)
    one_minus_a2 = -2.0 * th / (1.0 - th)
    return jnp.exp(log_a), jnp.sqrt(one_minus_a2) * (i * xc)


def _rg_scan_tile(a, b, carry, reverse):
    t = a.shape[0]
    rmod = lax.broadcasted_iota(jnp.int32, (t, 1), 0) & (SUBLANES - 1)
    for s in (1, 2, 4):
        if reverse:
            ok = rmod < SUBLANES - s
            ash, bsh = pltpu.roll(a, t - s, 0), pltpu.roll(b, t - s, 0)
        else:
            ok = rmod >= s
            ash, bsh = pltpu.roll(a, s, 0), pltpu.roll(b, s, 0)
        b = a * jnp.where(ok, bsh, 0.0) + b
        a = a * jnp.where(ok, ash, 1.0)
    nck = t // SUBLANES
    hs = [None] * nck
    for ck in (range(nck - 1, -1, -1) if reverse else range(nck)):
        sl = slice(ck * SUBLANES, (ck + 1) * SUBLANES)
        hc = b[sl] + a[sl] * carry
        carry = hc[0:1] if reverse else hc[SUBLANES - 1:SUBLANES]
        hs[ck] = hc
    return jnp.concatenate(hs, axis=0), carry


def _rg_scan_kernel(xc_ref, gc_ref, xl_ref, xp_ref, xn_ref, gl_ref, cw_ref, w_ref, gp_ref,
                    mc_ref, ml_ref, hf_scr, car_scr, *, nt):
    s = pl.program_id(2)
    gw = xl_ref.shape[-1]
    tl = xl_ref.shape[1]
    cw = cw_ref[0]
    gp = gp_ref[0]

    def gates(xconv, d):
        w = w_ref[0, :, d * 2 * gw:(d + 1) * 2 * gw]
        return _rg_gates(xconv, w, gp[3 * d:3 * d + 1], gp[3 * d + 1:3 * d + 2], gp[3 * d + 2:3 * d + 3])

    @pl.when(s == 0)
    def _ctx():
        zeros8 = jnp.zeros((SUBLANES, gw), F32)
        xconv = _rg_conv(xc_ref[0], zeros8, zeros8, cw)
        zero = jnp.zeros((1, gw), F32)
        a0, b0 = gates(xconv, 0)
        hf, cf = _rg_scan_tile(a0, b0, zero, False)
        a1, b1 = gates(xconv, 1)
        hb, cb = _rg_scan_tile(a1, b1, zero, True)
        car_scr[0:1] = cf
        car_scr[1:2] = cb
        mc_ref[0] = (gc_ref[0] * (hf + hb)).astype(mc_ref.dtype)

    def lat_conv(t):
        prev = xp_ref[0] * (t > 0).astype(F32)
        nxt = xn_ref[0] * (t < nt - 1).astype(F32)
        return _rg_conv(xl_ref[0], prev, nxt, cw)

    @pl.when((s >= 1) & (s <= nt))
    def _fwd():
        t = s - 1
        a0, b0 = gates(lat_conv(t), 0)
        h, c = _rg_scan_tile(a0, b0, car_scr[0:1], False)
        car_scr[0:1] = c
        hf_scr[pl.ds(pl.multiple_of(t * tl, tl), tl), :] = h

    @pl.when(s > nt)
    def _bwd():
        t = 2 * nt - s
        a1, b1 = gates(lat_conv(t), 1)
        h, c = _rg_scan_tile(a1, b1, car_scr[1:2], True)
        car_scr[1:2] = c
        hf = hf_scr[pl.ds(pl.multiple_of(t * tl, tl), tl), :]
        ml_ref[0] = (gl_ref[0] * (hf + h)).astype(ml_ref.dtype)


def _rg_scan_call(xr_c, g_c, xr_l, g_l, cw, wbd, gp):
    b, s_len, dr = xr_l.shape
    c_len = xr_c.shape[1]
    ng, gw = wbd.shape[0], wbd.shape[1]
    tl = _row_tile(s_len, 512)
    nt = s_len // tl
    nb8 = tl // SUBLANES

    def tile(s):
        return jnp.where(s <= nt, jnp.maximum(s - 1, 0), 2 * nt - s)

    def tile_bwd(s):
        return jnp.where(s <= nt, nt - 1, 2 * nt - s)

    return pl.pallas_call(
        functools.partial(_rg_scan_kernel, nt=nt),
        out_shape=(jax.ShapeDtypeStruct((b, c_len, dr), BF16), jax.ShapeDtypeStruct((b, s_len, dr), BF16)),
        grid=(b, ng, 2 * nt + 1),
        in_specs=[
            pl.BlockSpec((1, c_len, gw), lambda bi, h, s: (bi, 0, h)),
            pl.BlockSpec((1, c_len, gw), lambda bi, h, s: (bi, 0, h)),
            pl.BlockSpec((1, tl, gw), lambda bi, h, s: (bi, tile(s), h)),
            pl.BlockSpec((1, SUBLANES, gw), lambda bi, h, s: (bi, jnp.maximum(tile(s) * nb8 - 1, 0), h)),
            pl.BlockSpec((1, SUBLANES, gw),
                         lambda bi, h, s: (bi, jnp.minimum((tile(s) + 1) * nb8, s_len // SUBLANES - 1), h)),
            pl.BlockSpec((1, tl, gw), lambda bi, h, s: (bi, tile_bwd(s), h)),
            pl.BlockSpec((1, SUBLANES, gw), lambda bi, h, s: (h, 0, 0)),
            pl.BlockSpec((1, gw, 4 * gw), lambda bi, h, s: (h, 0, 0)),
            pl.BlockSpec((1, SUBLANES, gw), lambda bi, h, s: (h, 0, 0)),
        ],
        out_specs=(pl.BlockSpec((1, c_len, gw), lambda bi, h, s: (bi, 0, h)),
                   pl.BlockSpec((1, tl, gw), lambda bi, h, s: (bi, tile_bwd(s), h))),
        scratch_shapes=[pltpu.VMEM((s_len, gw), F32), pltpu.VMEM((SUBLANES, gw), F32)],
        compiler_params=_cparams(("arbitrary", "arbitrary", "arbitrary")),
        name="rg_scan",
    )(xr_c, g_c, xr_l, xr_l, xr_l, g_l, cw, wbd, gp)


def _rg_pack_params(conv_w, conv_b, wa, ba, wx, bx, lam):
    nblk, bw = wa.shape[1], wa.shape[2]
    dr = nblk * bw
    per = next(p for p in range(1, nblk + 1) if nblk % p == 0 and (p * bw) % LANES == 0)
    ng, gw = nblk // per, per * bw

    def dense(w):
        w = w.reshape(ng, per, bw, bw)
        eye = jnp.eye(per, dtype=w.dtype)
        return jnp.einsum('gpjk,pq->gpjqk', w, eye).reshape(ng, gw, gw)

    wbd = jnp.concatenate([dense(wa[0]), dense(wx[0]), dense(wa[1]), dense(wx[1])], axis=-1).astype(BF16)
    cw = jnp.concatenate([conv_w, conv_b[None], jnp.zeros((SUBLANES - 1 - conv_w.shape[0], dr), F32)], axis=0)
    cw = cw.reshape(SUBLANES, ng, gw).transpose(1, 0, 2)
    gp = jnp.stack([ba[0], bx[0], lam[0], ba[1], bx[1], lam[1], jnp.zeros_like(lam[0]), jnp.zeros_like(lam[0])])
    gp = gp.reshape(SUBLANES, ng, gw).transpose(1, 0, 2)
    return cw, wbd, gp


def _pair_masks(shape):
    lane = lax.broadcasted_iota(jnp.int32, shape, 1)
    return lane < HEAD_DIM


def _softmax_weights(scores):
    m = scores[0].max(axis=-1, keepdims=True)
    for s in scores[1:]:
        m = jnp.maximum(m, s.max(axis=-1, keepdims=True))
    es = [jnp.exp(s - m) for s in scores]
    l = es[0].sum(axis=-1, keepdims=True)
    for e in es[1:]:
        l = l + e.sum(axis=-1, keepdims=True)
    return es, l


def _plain_attn_kernel(q_ref, *refs, nsrc):
    o_ref = refs[-1]
    ks = [refs[2 * i][0] for i in range(nsrc)]
    vs = [refs[2 * i + 1][0] for i in range(nsrc)]
    q = q_ref[0]
    first = _pair_masks(q.shape)
    outs = []
    for hh in range(2):
        qh = jnp.where(first if hh == 0 else jnp.logical_not(first), q, jnp.zeros_like(q))
        es, l = _softmax_weights([_dot_nt(qh, k) for k in ks])
        o = _dot(es[0].astype(BF16), vs[0])
        for e, v in zip(es[1:], vs[1:]):
            o = o + _dot(e.astype(BF16), v)
        outs.append(o / l)
    o_ref[0] = jnp.where(first, outs[0], outs[1]).astype(o_ref.dtype)


def _plain_attn_call(q, kvs, kv_block):
    b, lq, d = q.shape
    tq = _row_tile(lq, 256)
    in_specs = [pl.BlockSpec((1, tq, LANES), lambda bi, j, i: (bi, i, j))]
    args = [q]
    for k, v in kvs:
        spec = pl.BlockSpec((1, k.shape[1], LANES), lambda bi, j, i: (bi, 0, kv_block(j)))
        in_specs += [spec, spec]
        args += [k, v]
    return pl.pallas_call(
        functools.partial(_plain_attn_kernel, nsrc=len(kvs)),
        out_shape=jax.ShapeDtypeStruct((b, lq, d), BF16),
        grid=(b, d // LANES, lq // tq),
        in_specs=in_specs,
        out_specs=pl.BlockSpec((1, tq, LANES), lambda bi, j, i: (bi, i, j)),
        compiler_params=_cparams(("arbitrary", "arbitrary", "arbitrary")),
        name="plain_attn",
    )(*args)


def _diff_attn_kernel(q_ref, lam_ref, g_ref, *refs, nsrc, lambda_init):
    o_ref = refs[-1]
    ks = [refs[2 * i][0] for i in range(nsrc)]
    vs = [refs[2 * i + 1][0] for i in range(nsrc)]
    lp = lam_ref[...]
    lam = (jnp.exp(jnp.sum(lp[0:1] * lp[1:2], axis=-1, keepdims=True))
           - jnp.exp(jnp.sum(lp[2:3] * lp[3:4], axis=-1, keepdims=True)) + lambda_init)
    q = q_ref[0]
    first = _pair_masks(q.shape)
    ps = []
    for hh in range(2):
        qh = jnp.where(first if hh == 0 else jnp.logical_not(first), q, jnp.zeros_like(q))
        es, l = _softmax_weights([_dot_nt(qh, k) for k in ks])
        ps.append((es, 1.0 / l))
    o = None
    for i in range(nsrc):
        a = (ps[0][0][i] * ps[0][1] - lam * (ps[1][0][i] * ps[1][1])).astype(BF16)
        o = _dot(a, vs[i]) if o is None else o + _dot(a, vs[i])
    o_ref[0] = (_rms(o, g_ref[...]) * (1.0 - lambda_init)).astype(o_ref.dtype)


def _diff_attn_call(q, kvs, lam_params, subln_g, lambda_init):
    b, lq, d = q.shape
    tq = _row_tile(lq, 256)
    const = lambda bi, j, i: (0, 0)
    in_specs = [pl.BlockSpec((1, tq, LANES), lambda bi, j, i: (bi, i, j)),
                pl.BlockSpec(lam_params.shape, const), pl.BlockSpec(subln_g.shape, const)]
    args = [q, lam_params, subln_g]
    for k, v in kvs:
        spec = pl.BlockSpec((1, k.shape[1], LANES), lambda bi, j, i: (bi, 0, j))
        in_specs += [spec, spec]
        args += [k, v]
    return pl.pallas_call(
        functools.partial(_diff_attn_kernel, nsrc=len(kvs), lambda_init=lambda_init),
        out_shape=jax.ShapeDtypeStruct((b, lq, d), BF16),
        grid=(b, d // LANES, lq // tq),
        in_specs=in_specs,
        out_specs=pl.BlockSpec((1, tq, LANES), lambda bi, j, i: (bi, i, j)),
        compiler_params=_cparams(("arbitrary", "arbitrary", "arbitrary")),
        name="diff_attn",
    )(*args)


def _na_attn_kernel(q_ref, k_ref, v_ref, kc_ref, vc_ref, rpb_ref, o_ref, t2_scr, *, rows, kr, kcw, nq):
    w = GRID_W
    nwin = nq + kr
    ntab = 2 * kr
    i = pl.program_id(2)

    @pl.when(i == 0)
    def _build_bias():
        c_io = lax.broadcasted_iota(jnp.int32, (w, LANES), 0)
        cp_io = lax.broadcasted_iota(jnp.int32, (w, LANES), 1)
        cs = jnp.clip(c_io - kcw // 2, 0, w - kcw)
        colok = (cp_io >= cs) & (cp_io < cs + kcw)
        neg = jnp.full((w, LANES), NEG, F32)
        for hh in range(2):
            r = pltpu.roll(rpb_ref[hh], LANES - (kcw - 1), 1)
            ts = []
            for e in range(ntab - 1):
                t = jnp.broadcast_to(r[e:e + 1], (w, LANES))
                t = pltpu.roll(t, 0, 1, stride=1, stride_axis=0)
                ts.append(jnp.where(colok, t, NEG))
            for e in range(ntab):
                lo = ts[e - 1] if e >= 1 else neg
                hi = ts[e] if e < ntab - 1 else neg
                t2_scr[hh, e] = jnp.where(cp_io < w, lo, pltpu.roll(hi, w, 1))

    r0 = i * nq
    ks = jnp.clip(r0 - kr // 2, 0, rows - nwin)
    start = pl.multiple_of(ks * w, w)
    kw = k_ref[0, pl.ds(start, nwin * w), :]
    vw = v_ref[0, pl.ds(start, nwin * w), :]
    kc = kc_ref[0]
    vc = vc_ref[0]
    q = q_ref[0]
    first = _pair_masks(q.shape)
    key_row = ks + jnp.right_shift(lax.broadcasted_iota(jnp.int32, (1, nwin * w), 1), int(math.log2(w)))
    outs = []
    for hh in range(2):
        qh = jnp.where(first if hh == 0 else jnp.logical_not(first), q, jnp.zeros_like(q))
        s = _dot_nt(qh, kw)
        sc = _dot_nt(qh, kc)
        parts = []
        for qi in range(nq):
            r = r0 + qi
            rs = jnp.clip(r - kr // 2, 0, rows - kr)
            rowok = (key_row >= rs) & (key_row < rs + kr)
            blocks = [t2_scr[hh, jnp.clip(ks + 2 * p - r + kr, 0, ntab - 1)] for p in range(nwin // 2)]
            sq = s[qi * w:(qi + 1) * w] + jnp.concatenate(blocks, axis=1)
            parts.append(jnp.where(rowok, sq, NEG))
        (e, ec), l = _softmax_weights([jnp.concatenate(parts, axis=0), sc])
        o = _dot(e.astype(BF16), vw) + _dot(ec.astype(BF16), vc)
        outs.append(o / l)
    o_ref[0] = jnp.where(first, outs[0], outs[1]).astype(o_ref.dtype)


def _na_attn_call(q, k, v, kc, vc, rpb):
    b, s_len, d = q.shape
    c_len = kc.shape[1]
    nh, ndr, ndc = rpb.shape
    kr, kcw = (ndr + 1) // 2, (ndc + 1) // 2
    rows = s_len // GRID_W
    nq = 4
    assert GRID_W == HEAD_DIM and rows % nq == 0 and rows >= nq + kr and (nq + kr) % 2 == 0
    assert ndc <= LANES and 2 * kr <= 2 * SUBLANES
    rpb_p = jnp.full((nh, 2 * SUBLANES, LANES), NEG, F32).at[:, :ndr, :ndc].set(rpb)
    tq = nq * GRID_W
    full = lambda n: pl.BlockSpec((1, n, LANES), lambda bi, j, i: (bi, 0, j))
    return pl.pallas_call(
        functools.partial(_na_attn_kernel, rows=rows, kr=kr, kcw=kcw, nq=nq),
        out_shape=jax.ShapeDtypeStruct((b, s_len, d), BF16),
        grid=(b, d // LANES, s_len // tq),
        in_specs=[pl.BlockSpec((1, tq, LANES), lambda bi, j, i: (bi, i, j)),
                  full(s_len), full(s_len), full(c_len), full(c_len),
                  pl.BlockSpec((2, 2 * SUBLANES, LANES), lambda bi, j, i: (j, 0, 0))],
        out_specs=pl.BlockSpec((1, tq, LANES), lambda bi, j, i: (bi, i, j)),
        scratch_shapes=[pltpu.VMEM((2, 2 * kr, GRID_W, LANES), F32)],
        compiler_params=_cparams(("arbitrary", "arbitrary", "arbitrary")),
        name="na_attn",
    )(q, k, v, kc, vc, rpb_p)


def _outproj_kernel(a_ref, x_ref, mod_ref, w_ref, o_ref):
    o_ref[0] = x_ref[0] + mod_ref[0, 2:3] * _dot(a_ref[0], w_ref[...])


def _outproj_call(a, x, mods_l, mod_row, w):
    b, n, d = x.shape
    k = a.shape[-1]
    tm = _row_tile(n, 512)
    return pl.pallas_call(
        _outproj_kernel,
        out_shape=jax.ShapeDtypeStruct((b, n, d), F32),
        grid=(b, n // tm),
        in_specs=[pl.BlockSpec((1, tm, k), lambda bi, i: (bi, i, 0)),
                  pl.BlockSpec((1, tm, d), lambda bi, i: (bi, i, 0)),
                  _mod_spec(d, mod_row),
                  pl.BlockSpec((k, d), lambda bi, i: (0, 0))],
        out_specs=pl.BlockSpec((1, tm, d), lambda bi, i: (bi, i, 0)),
        compiler_params=_cparams(("arbitrary", "arbitrary")),
        name="outproj",
    )(a, x, mods_l, w)


def _ffn_kernel(x_ref, xp_ref, xn_ref, mod_ref, g_ref, wu_ref, cw_ref, wd_ref, *rest, fc, final):
    if final:
        fg_ref, o_ref, z_scr, u_scr, h_scr = rest
    else:
        o_ref, z_scr, u_scr, h_scr = rest
    i = pl.program_id(1)
    nt = pl.num_programs(1)
    x = x_ref[0]
    tm, d = x.shape
    hb = FFN_HALO
    nch = wu_ref.shape[0]
    g, shift, scale, gate = g_ref[...], mod_ref[0, 3:4], mod_ref[0, 4:5], mod_ref[0, 5:6]
    z_scr[hb:hb + tm] = _normmod(x, g, shift, scale).astype(BF16)
    z_scr[0:hb] = (_normmod(xp_ref[0], g, shift, scale) * (i > 0).astype(F32)).astype(BF16)
    z_scr[hb + tm:] = (_normmod(xn_ref[0], g, shift, scale) * (i < nt - 1).astype(F32)).astype(BF16)

    def up_proj(j):
        u_scr[j % 2] = _dot(z_scr[...], wu_ref[j])

    def conv_act(j):
        cw = cw_ref[j]
        for sb in range(tm // FFN_SUB):
            r0 = hb + sb * FFN_SUB
            cv = cw[3:4] + cw[0:1] * u_scr[j % 2, r0 - 1:r0 - 1 + FFN_SUB]
            cv = cv + cw[1:2] * u_scr[j % 2, r0:r0 + FFN_SUB]
            cv = cv + cw[2:3] * u_scr[j % 2, r0 + 1:r0 + 1 + FFN_SUB]
            h = _silu(cv[:, :fc]) * cv[:, fc:]
            h_scr[sb * FFN_SUB:(sb + 1) * FFN_SUB, j * fc:(j + 1) * fc] = h.astype(BF16)

    for j in range(nch + 1):
        if j < nch:
            up_proj(j)
        if j >= 1:
            conv_act(j - 1)
    out = x + gate * _dot(h_scr[...], wd_ref[...])
    if final:
        out = _rms(out, fg_ref[...])
    o_ref[0] = out


def _ffn_pack_params(w_up, conv_w, conv_b, w_down, fc):
    d, f2 = w_up.shape
    f = f2 // 2
    nch = f // fc
    wu = jnp.concatenate([w_up[:, :f].reshape(d, nch, fc), w_up[:, f:].reshape(d, nch, fc)], axis=-1)
    wu = wu.transpose(1, 0, 2).astype(BF16)
    cw = jnp.concatenate([conv_w, conv_b[None], jnp.zeros((SUBLANES - 1 - conv_w.shape[0], f2), F32)], axis=0)
    cw = jnp.concatenate([cw[:, :f].reshape(SUBLANES, nch, fc), cw[:, f:].reshape(SUBLANES, nch, fc)], axis=-1)
    cw = cw.transpose(1, 0, 2)
    return wu, cw, w_down.astype(BF16)


def _ffn_call(x, mods_l, mod_row, g, packed, final_g=None):
    b, n, d = x.shape
    wu, cw, wd = packed
    nch, _, fc2 = wu.shape
    tm = _row_tile(n, 512)
    hb = FFN_HALO
    nbh = tm // hb
    const3 = lambda bi, i: (0, 0, 0)
    once = dict(pipeline_mode=pl.Buffered(1))
    in_specs = [
        pl.BlockSpec((1, tm, d), lambda bi, i: (bi, i, 0)),
        pl.BlockSpec((1, hb, d), lambda bi, i: (bi, jnp.maximum(i * nbh - 1, 0), 0)),
        pl.BlockSpec((1, hb, d), lambda bi, i: (bi, jnp.minimum((i + 1) * nbh, n // hb - 1), 0)),
        _mod_spec(d, mod_row),
        pl.BlockSpec((1, d), lambda bi, i: (0, 0)),
        pl.BlockSpec(wu.shape, const3, **once),
        pl.BlockSpec(cw.shape, const3),
        pl.BlockSpec(wd.shape, lambda bi, i: (0, 0), **once),
    ]
    args = [x, x, x, mods_l, g, wu, cw, wd]
    if final_g is not None:
        in_specs.append(pl.BlockSpec((1, d), lambda bi, i: (0, 0)))
        args.append(final_g)
    return pl.pallas_call(
        functools.partial(_ffn_kernel, fc=fc2 // 2, final=final_g is not None),
        out_shape=jax.ShapeDtypeStruct((b, n, d), F32),
        grid=(b, n // tm),
        in_specs=in_specs,
        out_specs=pl.BlockSpec((1, tm, d), lambda bi, i: (bi, i, 0)),
        scratch_shapes=[pltpu.VMEM((tm + 2 * hb, d), BF16), pltpu.VMEM((2, tm + 2 * hb, fc2), F32),
                        pltpu.VMEM((tm, nch * fc2 // 2), BF16)],
        compiler_params=_cparams(("arbitrary", "arbitrary")),
        name="ffn",
    )(*args)


def _rg_mixer(xl, xc, mods_l, ctx_row, g, w_in, conv_w, conv_b, wa, ba, wx, bx, lam, need_ctx):
    w = w_in.astype(BF16)
    g_l, xr_l = _rg_proj_call(xl, mods_l, None, g, w)
    g_c, xr_c = _rg_proj_call(xc, mods_l, ctx_row, g, w)
    cw, wbd, gp = _rg_pack_params(conv_w, conv_b, wa, ba, wx, bx, lam)
    m_c, m_l = _rg_scan_call(xr_c, g_c, xr_l, g_l, cw, wbd, gp)
    return m_l, (m_c if need_ctx else None)


def _na_mixer(xl, xc, mods_l, ctx_row, g, w_in, rpb, need_ctx):
    d = xl.shape[-1]
    w = w_in.astype(BF16)
    ql, kl, vl = _qkv_proj_call(xl, mods_l, None, g, w, d)
    qc, kc, vc = _qkv_proj_call(xc, mods_l, ctx_row, g, w, d)
    o_l = _na_attn_call(ql, kl, vl, kc, vc, rpb)
    o_c = _plain_attn_call(qc, [(kc, vc)], lambda j: j) if need_ctx else None
    return o_l, o_c


def _gqa_mixer(xl, xc, mods_l, ctx_row, g, w_in, q_norm, k_norm, need_ctx):
    d = xl.shape[-1]
    nh = d // HEAD_DIM
    nkv = (w_in.shape[1] // HEAD_DIM - nh) // 2
    group = nh // nkv
    assert group % 2 == 0 and q_norm.shape[0] == HEAD_DIM
    dup = lambda wk: jnp.concatenate([wk.reshape(d, nkv, 1, HEAD_DIM)] * 2, axis=2).reshape(d, 2 * nkv * HEAD_DIM)
    nk = 2 * nkv * HEAD_DIM
    w = jnp.concatenate([w_in[:, :d], dup(w_in[:, d:d + nkv * HEAD_DIM]), dup(w_in[:, d + nkv * HEAD_DIM:])],
                        axis=1).astype(BF16)
    nseg = (d + nk) // HEAD_DIM
    assert nseg <= LANES
    seg = jnp.arange(d + nk) // HEAD_DIM
    ind = (seg[:, None] == jnp.arange(LANES)[None, :]).astype(BF16)
    gain = jnp.concatenate([jnp.tile(q_norm, nh), jnp.tile(k_norm, 2 * nkv)])[None]
    qkn = (gain, ind, ind.T)
    ql, kl, vl = _qkv_proj_call(xl, mods_l, None, g, w, nk, rope=_rope_tables(xl.shape[1]), qknorm=qkn)
    qc, kc, vc = _qkv_proj_call(xc, mods_l, ctx_row, g, w, nk, qknorm=qkn)
    kv_block = lambda j: j // (group // 2)
    o_l = _plain_attn_call(ql, [(kl, vl), (kc, vc)], kv_block)
    o_c = _plain_attn_call(qc, [(kc, vc)], kv_block) if need_ctx else None
    return o_l, o_c


def _diff_mixer(xl, xc, mods_l, ctx_row, g, w_in, lq1, lk1, lq2, lk2, subln_g, lambda_init, need_ctx):
    d = xl.shape[-1]
    assert lq1.shape[0] == HEAD_DIM and subln_g.shape[0] == LANES
    w = w_in.astype(BF16)
    ql, kl, vl = _qkv_proj_call(xl, mods_l, None, g, w, d, rope=_rope_tables(xl.shape[1]))
    qc, kc, vc = _qkv_proj_call(xc, mods_l, ctx_row, g, w, d)
    lam_params = jnp.concatenate([jnp.stack([lq1, lk1, lq2, lk2]), jnp.zeros((SUBLANES - 4, HEAD_DIM), F32)])
    o_l = _diff_attn_call(ql, [(kl, vl), (kc, vc)], lam_params, subln_g[None], lambda_init)
    o_c = _diff_attn_call(qc, [(kc, vc)], lam_params, subln_g[None], lambda_init) if need_ctx else None
    return o_l, o_c


def kernel(x, c, ctx, c_ctx, mod_w, mod_b, norm1_g, norm2_g, rg_w_in, rg_conv_w, rg_conv_b, rg_wa, rg_ba, rg_wx, rg_bx, rg_lam, rg_w_out, na_w_in, na_rpb, na_w_out, gqa_w_in, gqa_q_norm, gqa_k_norm, gqa_w_out, diff_w_in, diff_lq1, diff_lk1, diff_lq2, diff_lk2, diff_subln_g, diff_w_out, ffn_w_up, ffn_conv_w, ffn_conv_b, ffn_w_down, final_g):
    bsz, _, d = x.shape
    depth = mod_w.shape[0]
    assert d % LANES == 0
    ctx_row = bsz
    nrow = -(-(bsz + 1) // SUBLANES) * SUBLANES
    cc = jnp.concatenate([c, c_ctx[None], jnp.zeros((nrow - bsz - 1, d), F32)], axis=0)
    mods = _mods_call(cc, mod_w, mod_b).reshape(depth, nrow, 6, d)
    f = ffn_w_down.shape[1]
    fc = next(t for t in (256, 128) if f % t == 0)
    xl, xc = x, ctx
    for l in range(depth):
        m, j = l % N_MIXERS, l // N_MIXERS
        need_ctx = l < depth - 1
        mods_l = mods[l]
        g1, g2 = norm1_g[l][None], norm2_g[l][None]
        if m == 0:
            a_l, a_c = _rg_mixer(xl, xc, mods_l, ctx_row, g1, rg_w_in[j], rg_conv_w[j], rg_conv_b[j], rg_wa[j],
                                 rg_ba[j], rg_wx[j], rg_bx[j], rg_lam[j], need_ctx)
            w_out = rg_w_out[j]
        elif m == 1:
            a_l, a_c = _na_mixer(xl, xc, mods_l, ctx_row, g1, na_w_in[j], na_rpb[j], need_ctx)
            w_out = na_w_out[j]
        elif m == 2:
            a_l, a_c = _gqa_mixer(xl, xc, mods_l, ctx_row, g1, gqa_w_in[j], gqa_q_norm[j], gqa_k_norm[j], need_ctx)
            w_out = gqa_w_out[j]
        else:
            lambda_init = 0.8 - 0.6 * math.exp(-0.3 * l)
            a_l, a_c = _diff_mixer(xl, xc, mods_l, ctx_row, g1, diff_w_in[j], diff_lq1[j], diff_lk1[j],
                                   diff_lq2[j], diff_lk2[j], diff_subln_g[j], lambda_init, need_ctx)
            w_out = diff_w_out[j]
        w_out = w_out.astype(BF16)
        packed = _ffn_pack_params(ffn_w_up[l], ffn_conv_w[l], ffn_conv_b[l], ffn_w_down[l], fc)
        last = l == depth - 1
        xl = _outproj_call(a_l, xl, mods_l, None, w_out)
        xl = _ffn_call(xl, mods_l, None, g2, packed, final_g[None] if last else None)
        if need_ctx:
            xc = _outproj_call(a_c, xc, mods_l, ctx_row, w_out)
            xc = _ffn_call(xc, mods_l, ctx_row, g2, packed)
    return xl
```

```python
import functools
import math

import jax
import jax.numpy as jnp
from jax import lax
from jax.experimental import pallas as pl
from jax.experimental.pallas import tpu as pltpu

F32 = jnp.float32
BF16 = jnp.bfloat16
EPS = 1e-6
ROPE_THETA = 10000.0
GRID_W = 64
RG_C = 8.0
RG_CONV_LEFT = 2
N_MIXERS = 4
HEAD_DIM = 64
LOG2E = 1.4426950408889634
Q_SCALE = HEAD_DIM ** -0.5 * LOG2E
LANES = 128
SUBLANES = 8
NEG = -1e30
FFN_HALO = 16
FFN_SUB = 64
ATT_SUB = 256
ATT_RB = 16
VMEM_LIMIT_BYTES = 56 * 1024 * 1024

_NT = (((1,), (1,)), ((), ()))


def _cparams(sem):
    return pltpu.CompilerParams(dimension_semantics=sem, vmem_limit_bytes=VMEM_LIMIT_BYTES)


def _dot(a, b):
    return jnp.dot(a, b, preferred_element_type=F32)


def _dot_nt(a, b):
    return lax.dot_general(a, b, _NT, preferred_element_type=F32)


def _silu(x):
    return x / (1.0 + jnp.exp(-x))


def _gelu_tanh(x):
    cdf = 0.5 * (1.0 + jnp.tanh(math.sqrt(2.0 / math.pi) * (x + 0.044715 * (x * x * x))))
    return x * cdf


def _rms(x, g):
    y = x * lax.rsqrt(jnp.mean(x * x, axis=-1, keepdims=True) + EPS)
    return y * g


def _normmod(x, g, shift, scale):
    return _rms(x, g) * (1.0 + scale) + shift


def _mods_kernel(cc_ref, w_ref, b_ref, o_ref):
    a = _silu(cc_ref[...]).astype(BF16)
    o_ref[0] = _dot(a, w_ref[0].astype(BF16)) + b_ref[0]


def _mods_call(cc, mod_w, mod_b):
    depth, d, n = mod_w.shape
    r = cc.shape[0]
    tn = n // 4
    return pl.pallas_call(
        _mods_kernel,
        out_shape=jax.ShapeDtypeStruct((depth, r, n), F32),
        grid=(depth, n // tn),
        in_specs=[
            pl.BlockSpec((r, d), lambda l, j: (0, 0)),
            pl.BlockSpec((1, d, tn), lambda l, j: (l, 0, j)),
            pl.BlockSpec((1, 1, tn), lambda l, j: (l, 0, j)),
        ],
        out_specs=pl.BlockSpec((1, r, tn), lambda l, j: (l, 0, j)),
        compiler_params=_cparams(("arbitrary", "arbitrary")),
        name="mods",
    )(cc, mod_w, mod_b.reshape(depth, 1, n))


def _mod_spec(d, row):
    if row is None:
        return pl.BlockSpec((1, 6, d), lambda b, *_: (b, 0, 0))
    return pl.BlockSpec((1, 6, d), lambda b, *_: (row, 0, 0))


def _row_tile(n, pref):
    t = min(n, pref)
    assert n % t == 0
    return t


def _rope_tables(n_tok):
    t = jnp.arange(n_tok)
    row = (t // GRID_W).astype(F32)
    col = (t % GRID_W).astype(F32)
    n = HEAD_DIM // 4
    inv = ROPE_THETA ** (-jnp.arange(n, dtype=F32) / n)
    ang = jnp.concatenate([row[:, None] * inv, col[:, None] * inv], axis=-1)
    cos, sin = jnp.cos(ang), jnp.sin(ang)
    reps = LANES // HEAD_DIM
    cos_t = jnp.tile(jnp.concatenate([cos, cos], axis=-1), (1, reps))
    sin_t = jnp.tile(jnp.concatenate([-sin, sin], axis=-1), (1, reps))
    return cos_t, sin_t


def _swap_halves(x):
    lane = lax.broadcasted_iota(jnp.int32, x.shape, 1)
    lo = (lane & (HEAD_DIM // 2)) == 0
    return jnp.where(lo, pltpu.roll(x, LANES - HEAD_DIM // 2, 1), pltpu.roll(x, HEAD_DIM // 2, 1))


def _rope_cols(u, cos_t, sin_t):
    outs = []
    for c in range(u.shape[1] // LANES):
        xc = u[:, c * LANES:(c + 1) * LANES]
        outs.append(xc * cos_t + _swap_halves(xc) * sin_t)
    return jnp.concatenate(outs, axis=1)


def _rg_proj_kernel(x_ref, mod_ref, g_ref, w_ref, go_ref, xo_ref, *, dr):
    z = _normmod(x_ref[0], g_ref[...], mod_ref[0, 0:1], mod_ref[0, 1:2]).astype(BF16)
    u = _dot(z, w_ref[...])
    go_ref[0] = _gelu_tanh(u[:, :dr])
    xo_ref[0] = u[:, dr:]


def _rg_proj_call(x, mods_l, mod_row, g, w):
    b, n, d = x.shape
    dr = w.shape[1] // 2
    tm = _row_tile(n, 512)
    return pl.pallas_call(
        functools.partial(_rg_proj_kernel, dr=dr),
        out_shape=(jax.ShapeDtypeStruct((b, n, dr), F32), jax.ShapeDtypeStruct((b, n, dr), F32)),
        grid=(b, n // tm),
        in_specs=[
            pl.BlockSpec((1, tm, d), lambda bi, i: (bi, i, 0)),
            _mod_spec(d, mod_row),
            pl.BlockSpec((1, d), lambda bi, i: (0, 0)),
            pl.BlockSpec((d, 2 * dr), lambda bi, i: (0, 0)),
        ],
        out_specs=(pl.BlockSpec((1, tm, dr), lambda bi, i: (bi, i, 0)),
                   pl.BlockSpec((1, tm, dr), lambda bi, i: (bi, i, 0))),
        compiler_params=_cparams(("arbitrary", "arbitrary")),
        name="rg_proj",
    )(x, mods_l, g, w)


def _qkv_proj_kernel(x_ref, mod_ref, g_ref, w_ref, *rest, d, rope, qknorm):
    rest = list(rest)
    if rope:
        cos_ref, sin_ref = rest.pop(0), rest.pop(0)
    if qknorm:
        gain_ref, ind_ref, indt_ref = rest.pop(0), rest.pop(0), rest.pop(0)
    q_ref, k_ref, v_ref = rest
    z = _normmod(x_ref[0], g_ref[...], mod_ref[0, 0:1], mod_ref[0, 1:2]).astype(BF16)
    u = _dot(z, w_ref[...])
    nk = k_ref.shape[-1]
    qk = u[:, :d + nk]
    if qknorm:
        x2 = qk * qk
        hi = x2.astype(BF16)
        lo = (x2 - hi.astype(F32)).astype(BF16)
        ssq = _dot(hi, ind_ref[...]) + _dot(lo, ind_ref[...])
        r = lax.rsqrt(ssq * (1.0 / HEAD_DIM) + EPS)
        rhi = r.astype(BF16)
        rlo = (r - rhi.astype(F32)).astype(BF16)
        rb = _dot(rhi, indt_ref[...]) + _dot(rlo, indt_ref[...])
        qk = (qk * rb) * gain_ref[...]
    if rope:
        qk = _rope_cols(qk, cos_ref[...], sin_ref[...])
    q_ref[0] = (qk[:, :d] * Q_SCALE).astype(q_ref.dtype)
    k_ref[0] = qk[:, d:].astype(k_ref.dtype)
    v_ref[0] = u[:, d + nk:].astype(v_ref.dtype)


def _qkv_proj_call(x, mods_l, mod_row, g, w, nk, *, rope=None, qknorm=None):
    b, n, d = x.shape
    nv = w.shape[1] - d - nk
    tm = _row_tile(n, 512)
    const = lambda bi, i: (0, 0)
    in_specs = [
        pl.BlockSpec((1, tm, d), lambda bi, i: (bi, i, 0)),
        _mod_spec(d, mod_row),
        pl.BlockSpec((1, d), const),
        pl.BlockSpec(w.shape, const),
    ]
    args = [x, mods_l, g, w]
    if rope is not None:
        in_specs += [pl.BlockSpec((tm, LANES), lambda bi, i: (i, 0))] * 2
        args += list(rope)
    if qknorm is not None:
        in_specs += [pl.BlockSpec(a.shape, const) for a in qknorm]
        args += list(qknorm)
    out_spec = lambda w_: pl.BlockSpec((1, tm, w_), lambda bi, i: (bi, i, 0))
    return pl.pallas_call(
        functools.partial(_qkv_proj_kernel, d=d, rope=rope is not None, qknorm=qknorm is not None),
        out_shape=(jax.ShapeDtypeStruct((b, n, d), BF16), jax.ShapeDtypeStruct((b, n, nk), BF16),
                   jax.ShapeDtypeStruct((b, n, nv), BF16)),
        grid=(b, n // tm),
        in_specs=in_specs,
        out_specs=(out_spec(d), out_spec(nk), out_spec(nv)),
        compiler_params=_cparams(("arbitrary", "arbitrary")),
        name="qkv_proj",
    )(*args)


def _rg_conv(xs_scr, x, prev, nxt, cw):
    t = x.shape[0]
    h = SUBLANES
    xs_scr[0:h] = prev
    xs_scr[h:h + t] = x
    xs_scr[h + t:2 * h + t] = nxt
    y = cw[4:5] + cw[0:1] * xs_scr[h - 2:h - 2 + t]
    y = y + cw[1:2] * xs_scr[h - 1:h - 1 + t]
    y = y + cw[2:3] * x
    y = y + cw[3:4] * xs_scr[h + 1:h + 1 + t]
    return y


def _sigmoid(x):
    return 0.5 * jnp.tanh(0.5 * x) + 0.5


def _rg_gates(xc, w, ba, bx, lam):
    gw = xc.shape[1]
    y = _dot(xc.astype(BF16), w)
    r = _sigmoid(y[:, :gw] + ba)
    i = _sigmoid(y[:, gw:] + bx)
    nl = -lam
    softplus = jnp.maximum(nl, 0.0) + jnp.log1p(jnp.exp(-jnp.abs(nl)))
    log_a = r * (-RG_C * softplus)
    a = jnp.exp(log_a)
    one_minus_a2 = -jnp.tanh(log_a) * (a * a + 1.0)
    root = jnp.where(one_minus_a2 > 0.0, one_minus_a2 * lax.rsqrt(one_minus_a2), 0.0)
    return a, root * (i * xc)


def _rg_scan_tile(a, b, carry, reverse):
    t, gw = a.shape
    nck = t // SUBLANES
    a = a.reshape(nck, SUBLANES, gw)
    b = b.reshape(nck, SUBLANES, gw)
    rmod = lax.broadcasted_iota(jnp.int32, (1, SUBLANES, 1), 1)
    for s in (1, 2, 4):
        ok = (rmod < SUBLANES - s) if reverse else (rmod >= s)
        shift = SUBLANES - s if reverse else s
        ash, bsh = pltpu.roll(a, shift, 1), pltpu.roll(b, shift, 1)
        b = a * jnp.where(ok, bsh, 0.0) + b
        a = a * jnp.where(ok, ash, 1.0)
    hs = [None] * nck
    for ck in (range(nck - 1, -1, -1) if reverse else range(nck)):
        hc = b[ck] + a[ck] * carry
        carry = hc[0:1] if reverse else hc[SUBLANES - 1:SUBLANES]
        hs[ck] = hc
    return jnp.concatenate(hs, axis=0), carry


def _rg_scan_kernel(xc_ref, gc_ref, xl_ref, xp_ref, xn_ref, gl_ref, cw_ref, w_ref, gp_ref,
                    mc_ref, ml_ref, hf_scr, xv_scr, car_scr, xs_scr, *, nt):
    s = pl.program_id(2)
    gw = xl_ref.shape[-1]
    tl = xl_ref.shape[1]
    cw = cw_ref[0]
    gp = gp_ref[0]

    def gates(xconv, d):
        w = w_ref[0, :, d * 2 * gw:(d + 1) * 2 * gw]
        return _rg_gates(xconv, w, gp[3 * d:3 * d + 1], gp[3 * d + 1:3 * d + 2], gp[3 * d + 2:3 * d + 3])

    @pl.when(s == 0)
    def _ctx():
        zeros8 = jnp.zeros((SUBLANES, gw), F32)
        xconv = _rg_conv(xs_scr, xc_ref[0], zeros8, zeros8, cw)
        zero = jnp.zeros((1, gw), F32)
        a0, b0 = gates(xconv, 0)
        hf, cf = _rg_scan_tile(a0, b0, zero, False)
        a1, b1 = gates(xconv, 1)
        hb, cb = _rg_scan_tile(a1, b1, zero, True)
        car_scr[0:1] = cf
        car_scr[1:2] = cb
        mc_ref[0] = (gc_ref[0] * (hf + hb)).astype(mc_ref.dtype)

    def lat_conv(t):
        prev = xp_ref[0] * (t > 0).astype(F32)
        nxt = xn_ref[0] * (t < nt - 1).astype(F32)
        return _rg_conv(xs_scr, xl_ref[0], prev, nxt, cw)

    @pl.when((s >= 1) & (s <= nt))
    def _fwd():
        t = s - 1
        xconv = lat_conv(t)
        xv_scr[pl.ds(pl.multiple_of(t * tl, tl), tl), :] = xconv
        a0, b0 = gates(xconv, 0)
        h, c = _rg_scan_tile(a0, b0, car_scr[0:1], False)
        car_scr[0:1] = c
        hf_scr[pl.ds(pl.multiple_of(t * tl, tl), tl), :] = h

    @pl.when(s > nt)
    def _bwd():
        t = 2 * nt - s
        a1, b1 = gates(xv_scr[pl.ds(pl.multiple_of(t * tl, tl), tl), :], 1)
        h, c = _rg_scan_tile(a1, b1, car_scr[1:2], True)
        car_scr[1:2] = c
        hf = hf_scr[pl.ds(pl.multiple_of(t * tl, tl), tl), :]
        ml_ref[0] = (gl_ref[0] * (hf + h)).astype(ml_ref.dtype)


def _rg_scan_call(xr_c, g_c, xr_l, g_l, cw, wbd, gp):
    b, s_len, dr = xr_l.shape
    c_len = xr_c.shape[1]
    ng, gw = wbd.shape[0], wbd.shape[1]
    tl = _row_tile(s_len, 512)
    nt = s_len // tl
    nb8 = tl // SUBLANES

    def tile(s):
        return jnp.clip(s - 1, 0, nt - 1)

    def tile_bwd(s):
        return jnp.where(s <= nt, nt - 1, 2 * nt - s)

    return pl.pallas_call(
        functools.partial(_rg_scan_kernel, nt=nt),
        out_shape=(jax.ShapeDtypeStruct((b, c_len, dr), BF16), jax.ShapeDtypeStruct((b, s_len, dr), BF16)),
        grid=(b, ng, 2 * nt + 1),
        in_specs=[
            pl.BlockSpec((1, c_len, gw), lambda bi, h, s: (bi, 0, h)),
            pl.BlockSpec((1, c_len, gw), lambda bi, h, s: (bi, 0, h)),
            pl.BlockSpec((1, tl, gw), lambda bi, h, s: (bi, tile(s), h)),
            pl.BlockSpec((1, SUBLANES, gw), lambda bi, h, s: (bi, jnp.maximum(tile(s) * nb8 - 1, 0), h)),
            pl.BlockSpec((1, SUBLANES, gw),
                         lambda bi, h, s: (bi, jnp.minimum((tile(s) + 1) * nb8, s_len // SUBLANES - 1), h)),
            pl.BlockSpec((1, tl, gw), lambda bi, h, s: (bi, tile_bwd(s), h)),
            pl.BlockSpec((1, SUBLANES, gw), lambda bi, h, s: (h, 0, 0)),
            pl.BlockSpec((1, gw, 4 * gw), lambda bi, h, s: (h, 0, 0)),
            pl.BlockSpec((1, SUBLANES, gw), lambda bi, h, s: (h, 0, 0)),
        ],
        out_specs=(pl.BlockSpec((1, c_len, gw), lambda bi, h, s: (bi, 0, h)),
                   pl.BlockSpec((1, tl, gw), lambda bi, h, s: (bi, tile_bwd(s), h))),
        scratch_shapes=[pltpu.VMEM((s_len, gw), F32), pltpu.VMEM((s_len, gw), F32), pltpu.VMEM((SUBLANES, gw), F32),
                        pltpu.VMEM((max(tl, c_len) + 2 * SUBLANES, gw), F32)],
        compiler_params=_cparams(("arbitrary", "arbitrary", "arbitrary")),
        name="rg_scan",
    )(xr_c, g_c, xr_l, xr_l, xr_l, g_l, cw, wbd, gp)


def _rg_pack_params(conv_w, conv_b, wa, ba, wx, bx, lam):
    nblk, bw = wa.shape[1], wa.shape[2]
    dr = nblk * bw
    per = next(p for p in range(1, nblk + 1) if nblk % p == 0 and (p * bw) % LANES == 0)
    ng, gw = nblk // per, per * bw

    def dense(w):
        w = w.reshape(ng, per, bw, bw)
        eye = jnp.eye(per, dtype=w.dtype)
        return jnp.einsum('gpjk,pq->gpjqk', w, eye).reshape(ng, gw, gw)

    wbd = jnp.concatenate([dense(wa[0]), dense(wx[0]), dense(wa[1]), dense(wx[1])], axis=-1).astype(BF16)
    cw = jnp.concatenate([conv_w, conv_b[None], jnp.zeros((SUBLANES - 1 - conv_w.shape[0], dr), F32)], axis=0)
    cw = cw.reshape(SUBLANES, ng, gw).transpose(1, 0, 2)
    gp = jnp.stack([ba[0], bx[0], lam[0], ba[1], bx[1], lam[1], jnp.zeros_like(lam[0]), jnp.zeros_like(lam[0])])
    gp = gp.reshape(SUBLANES, ng, gw).transpose(1, 0, 2)
    return cw, wbd, gp


def _pair_masks(shape):
    lane = lax.broadcasted_iota(jnp.int32, shape, 1)
    return lane < HEAD_DIM


def _half_q(q, hh):
    first = _pair_masks(q.shape)
    return jnp.where(first if hh == 0 else jnp.logical_not(first), q, jnp.zeros_like(q))


def _scores_to_scratch(qh, k_refs, s_scr, slot):
    off = 0
    for k_ref in k_refs:
        n = k_ref.shape[1]
        s_scr[slot, :, off:off + n] = _dot_nt(qh, k_ref[0])
        off += n


def _exp_rows(s):
    e = jnp.exp2(s - s.max(axis=-1, keepdims=True))
    return e, e.sum(axis=-1, keepdims=True)


def _softmax_to_scratch(s_scr, p_scr, slot, rows):
    linv = []
    for r in range(0, rows, ATT_RB):
        e, l = _exp_rows(s_scr[slot, r:r + ATT_RB, :])
        p_scr[slot, r:r + ATT_RB, :] = e.astype(BF16)
        linv.append(1.0 / l)
    return jnp.concatenate(linv, axis=0)


def _pv_from_scratch(p_scr, slot, v_refs):
    off, o = 0, None
    for v_ref in v_refs:
        n = v_ref.shape[1]
        t = _dot(p_scr[slot, :, off:off + n], v_ref[0])
        o = t if o is None else o + t
        off += n
    return o


def _plain_attn_kernel(q_ref, *refs, nsrc):
    k_refs, v_refs = refs[0:2 * nsrc:2], refs[1:2 * nsrc:2]
    o_ref, s_scr, p_scr = refs[2 * nsrc:]
    tq, qw = q_ref.shape[1], q_ref.shape[2]
    sub = s_scr.shape[1]
    units = [(r, c, hh) for r in range(0, tq, sub) for c in range(0, qw, LANES) for hh in range(2)]

    def scores(u):
        r, c, hh = units[u]
        _scores_to_scratch(_half_q(q_ref[0, r:r + sub, c:c + LANES], hh), k_refs, s_scr, u % 2)

    scores(0)
    outs = [None, None]
    for u, (r, c, hh) in enumerate(units):
        if u + 1 < len(units):
            scores(u + 1)
        linv = _softmax_to_scratch(s_scr, p_scr, u % 2, sub)
        outs[hh] = _pv_from_scratch(p_scr, u % 2, v_refs) * linv
        if hh == 1:
            o_ref[0, r:r + sub, c:c + LANES] = jnp.where(_pair_masks((sub, LANES)), outs[0], outs[1]).astype(o_ref.dtype)


def _plain_attn_call(q, kvs, qw):
    b, lq, d = q.shape
    tq = _row_tile(lq, 2 * ATT_SUB)
    sub = min(tq, ATT_SUB)
    nk = sum(k.shape[1] for k, _ in kvs)
    in_specs = [pl.BlockSpec((1, tq, qw), lambda bi, j, i: (bi, i, j))]
    args = [q]
    for k, v in kvs:
        spec = pl.BlockSpec((1, k.shape[1], LANES), lambda bi, j, i: (bi, 0, j))
        in_specs += [spec, spec]
        args += [k, v]
    return pl.pallas_call(
        functools.partial(_plain_attn_kernel, nsrc=len(kvs)),
        out_shape=jax.ShapeDtypeStruct((b, lq, d), BF16),
        grid=(b, d // qw, lq // tq),
        in_specs=in_specs,
        out_specs=pl.BlockSpec((1, tq, qw), lambda bi, j, i: (bi, i, j)),
        scratch_shapes=[pltpu.VMEM((2, sub, nk), F32), pltpu.VMEM((2, sub, nk), BF16)],
        compiler_params=_cparams(("arbitrary", "arbitrary", "arbitrary")),
        name="plain_attn",
    )(*args)


def _diff_attn_kernel(q_ref, lam_ref, g_ref, *refs, nsrc, lambda_init):
    k_refs, v_refs = refs[0:2 * nsrc:2], refs[1:2 * nsrc:2]
    o_ref, s_scr, a_scr = refs[2 * nsrc:]
    lp = lam_ref[...]
    lam = (jnp.exp(jnp.sum(lp[0:1] * lp[1:2], axis=-1, keepdims=True))
           - jnp.exp(jnp.sum(lp[2:3] * lp[3:4], axis=-1, keepdims=True)) + lambda_init)
    tq = q_ref.shape[1]
    sub = s_scr.shape[1]
    units = list(range(0, tq, sub))

    def scores(u):
        q = q_ref[0, units[u]:units[u] + sub]
        for hh in range(2):
            _scores_to_scratch(_half_q(q, hh), k_refs, s_scr, 2 * (u % 2) + hh)

    def combine(u):
        for r in range(0, sub, ATT_RB):
            e0, l0 = _exp_rows(s_scr[2 * (u % 2), r:r + ATT_RB, :])
            e1, l1 = _exp_rows(s_scr[2 * (u % 2) + 1, r:r + ATT_RB, :])
            a_scr[u % 2, r:r + ATT_RB, :] = (e0 * (1.0 / l0) - e1 * (lam / l1)).astype(BF16)

    scores(0)
    for u, r in enumerate(units):
        if u + 1 < len(units):
            scores(u + 1)
        combine(u)
        o = _pv_from_scratch(a_scr, u % 2, v_refs)
        o_ref[0, r:r + sub] = (_rms(o, g_ref[...]) * (1.0 - lambda_init)).astype(o_ref.dtype)


def _diff_attn_call(q, kvs, lam_params, subln_g, lambda_init):
    b, lq, d = q.shape
    tq = _row_tile(lq, 4 * ATT_SUB)
    sub = min(tq, ATT_SUB)
    nk = sum(k.shape[1] for k, _ in kvs)
    const = lambda bi, j, i: (0, 0)
    in_specs = [pl.BlockSpec((1, tq, LANES), lambda bi, j, i: (bi, i, j)),
                pl.BlockSpec(lam_params.shape, const), pl.BlockSpec(subln_g.shape, const)]
    args = [q, lam_params, subln_g]
    for k, v in kvs:
        spec = pl.BlockSpec((1, k.shape[1], LANES), lambda bi, j, i: (bi, 0, j))
        in_specs += [spec, spec]
        args += [k, v]
    return pl.pallas_call(
        functools.partial(_diff_attn_kernel, nsrc=len(kvs), lambda_init=lambda_init),
        out_shape=jax.ShapeDtypeStruct((b, lq, d), BF16),
        grid=(b, d // LANES, lq // tq),
        in_specs=in_specs,
        out_specs=pl.BlockSpec((1, tq, LANES), lambda bi, j, i: (bi, i, j)),
        scratch_shapes=[pltpu.VMEM((4, sub, nk), F32), pltpu.VMEM((2, sub, nk), BF16)],
        compiler_params=_cparams(("arbitrary", "arbitrary", "arbitrary")),
        name="diff_attn",
    )(*args)


def _na_attn_kernel(q_ref, k_ref, v_ref, kc_ref, vc_ref, rpb_ref, o_ref, t2_scr, s_scr, p_scr, *, rows, kr, kcw, nq):
    w = GRID_W
    nwin = nq + kr
    ntab = 2 * kr
    nkw = nwin * w
    sub = nq * w
    nsub = q_ref.shape[1] // sub
    i = pl.program_id(2)

    @pl.when(i == 0)
    def _build_bias():
        c_io = lax.broadcasted_iota(jnp.int32, (w, LANES), 0)
        cp_io = lax.broadcasted_iota(jnp.int32, (w, LANES), 1)
        cs = jnp.clip(c_io - kcw // 2, 0, w - kcw)
        colok = (cp_io >= cs) & (cp_io < cs + kcw)
        neg = jnp.full((w, LANES), NEG, F32)
        for hh in range(2):
            r = pltpu.roll(rpb_ref[hh] * LOG2E, LANES - (kcw - 1), 1)
            ts = []
            for e in range(ntab - 1):
                t = jnp.broadcast_to(r[e:e + 1], (w, LANES))
                t = pltpu.roll(t, 0, 1, stride=1, stride_axis=0)
                ts.append(jnp.where(colok, t, NEG))
            for e in range(ntab):
                lo = ts[e - 1] if e >= 1 else neg
                hi = ts[e] if e < ntab - 1 else neg
                t2_scr[hh, e] = jnp.where(cp_io < w, lo, pltpu.roll(hi, w, 1))

    units = [(t, hh) for t in range(nsub) for hh in range(2)]

    def window(t):
        r0 = (i * nsub + t) * nq
        ks = jnp.clip(r0 - kr // 2, 0, rows - nwin)
        return r0, ks, pl.ds(pl.multiple_of(ks * w, w), nkw)

    def scores(u):
        t, hh = units[u]
        _, _, win = window(t)
        qh = _half_q(q_ref[0, t * sub:(t + 1) * sub], hh)
        s_scr[u % 2, :, :nkw] = _dot_nt(qh, k_ref[0, win, :])
        s_scr[u % 2, :, nkw:] = _dot_nt(qh, kc_ref[0])

    def softmax(u):
        t, hh = units[u]
        r0, ks, _ = window(t)
        key_row = ks + jnp.right_shift(lax.broadcasted_iota(jnp.int32, (1, nkw), 1), int(math.log2(w)))
        linv = []
        for qi in range(nq):
            r = r0 + qi
            rs = jnp.clip(r - kr // 2, 0, rows - kr)
            rowok = (key_row >= rs) & (key_row < rs + kr)
            idx = [jnp.clip(ks + 2 * p - r + kr, 0, ntab - 1) for p in range(nwin // 2)]
            for rb in range(0, w, ATT_RB):
                row = qi * w + rb
                bias = jnp.concatenate([t2_scr[hh, e, rb:rb + ATT_RB, :] for e in idx], axis=1)
                sw = jnp.where(rowok, s_scr[u % 2, row:row + ATT_RB, :nkw] + bias, NEG)
                sc = s_scr[u % 2, row:row + ATT_RB, nkw:]
                m = jnp.maximum(sw.max(axis=-1, keepdims=True), sc.max(axis=-1, keepdims=True))
                ew, ec = jnp.exp2(sw - m), jnp.exp2(sc - m)
                p_scr[u % 2, row:row + ATT_RB, :nkw] = ew.astype(BF16)
                p_scr[u % 2, row:row + ATT_RB, nkw:] = ec.astype(BF16)
                linv.append(1.0 / (ew.sum(axis=-1, keepdims=True) + ec.sum(axis=-1, keepdims=True)))
        return jnp.concatenate(linv, axis=0)

    scores(0)
    outs = [None, None]
    for u, (t, hh) in enumerate(units):
        if u + 1 < len(units):
            scores(u + 1)
        linv = softmax(u)
        _, _, win = window(t)
        o = _dot(p_scr[u % 2, :, :nkw], v_ref[0, win, :]) + _dot(p_scr[u % 2, :, nkw:], vc_ref[0])
        outs[hh] = o * linv
        if hh == 1:
            o_ref[0, t * sub:(t + 1) * sub] = jnp.where(_pair_masks((sub, LANES)), outs[0], outs[1]).astype(o_ref.dtype)


def _na_attn_call(q, k, v, kc, vc, rpb):
    b, s_len, d = q.shape
    c_len = kc.shape[1]
    nh, ndr, ndc = rpb.shape
    kr, kcw = (ndr + 1) // 2, (ndc + 1) // 2
    rows = s_len // GRID_W
    nq = 4
    assert GRID_W == HEAD_DIM and rows % nq == 0 and rows >= nq + kr and (nq + kr) % 2 == 0
    assert ndc <= LANES and 2 * kr <= 2 * SUBLANES
    rpb_p = jnp.full((nh, 2 * SUBLANES, LANES), NEG, F32).at[:, :ndr, :ndc].set(rpb)
    sub = nq * GRID_W
    tq = _row_tile(s_len, 2 * sub)
    nk = (nq + kr) * GRID_W + c_len
    full = lambda n: pl.BlockSpec((1, n, LANES), lambda bi, j, i: (bi, 0, j))
    return pl.pallas_call(
        functools.partial(_na_attn_kernel, rows=rows, kr=kr, kcw=kcw, nq=nq),
        out_shape=jax.ShapeDtypeStruct((b, s_len, d), BF16),
        grid=(b, d // LANES, s_len // tq),
        in_specs=[pl.BlockSpec((1, tq, LANES), lambda bi, j, i: (bi, i, j)),
                  full(s_len), full(s_len), full(c_len), full(c_len),
                  pl.BlockSpec((2, 2 * SUBLANES, LANES), lambda bi, j, i: (j, 0, 0))],
        out_specs=pl.BlockSpec((1, tq, LANES), lambda bi, j, i: (bi, i, j)),
        scratch_shapes=[pltpu.VMEM((2, 2 * kr, GRID_W, LANES), F32), pltpu.VMEM((2, sub, nk), F32),
                        pltpu.VMEM((2, sub, nk), BF16)],
        compiler_params=_cparams(("arbitrary", "arbitrary", "arbitrary")),
        name="na_attn",
    )(q, k, v, kc, vc, rpb_p)


def _outproj_kernel(a_ref, x_ref, mod_ref, w_ref, o_ref):
    o_ref[0] = x_ref[0] + mod_ref[0, 2:3] * _dot(a_ref[0], w_ref[...])


def _outproj_call(a, x, mods_l, mod_row, w):
    b, n, d = x.shape
    k = a.shape[-1]
    tm = _row_tile(n, 512)
    return pl.pallas_call(
        _outproj_kernel,
        out_shape=jax.ShapeDtypeStruct((b, n, d), F32),
        grid=(b, n // tm),
        in_specs=[pl.BlockSpec((1, tm, k), lambda bi, i: (bi, i, 0)),
                  pl.BlockSpec((1, tm, d), lambda bi, i: (bi, i, 0)),
                  _mod_spec(d, mod_row),
                  pl.BlockSpec((k, d), lambda bi, i: (0, 0))],
        out_specs=pl.BlockSpec((1, tm, d), lambda bi, i: (bi, i, 0)),
        compiler_params=_cparams(("arbitrary", "arbitrary")),
        name="outproj",
    )(a, x, mods_l, w)


def _ffn_kernel(x_ref, xp_ref, xn_ref, mod_ref, g_ref, wu_ref, cw_ref, wd_ref, *rest, fc, final):
    if final:
        fg_ref, o_ref, z_scr, u_scr, h_scr = rest
    else:
        o_ref, z_scr, u_scr, h_scr = rest
    i = pl.program_id(1)
    nt = pl.num_programs(1)
    x = x_ref[0]
    tm, d = x.shape
    hb = FFN_HALO
    nch = wu_ref.shape[0]
    g, shift, scale, gate = g_ref[...], mod_ref[0, 3:4], mod_ref[0, 4:5], mod_ref[0, 5:6]
    z_scr[hb:hb + tm] = _normmod(x, g, shift, scale).astype(BF16)
    z_scr[0:hb] = (_normmod(xp_ref[0], g, shift, scale) * (i > 0).astype(F32)).astype(BF16)
    z_scr[hb + tm:] = (_normmod(xn_ref[0], g, shift, scale) * (i < nt - 1).astype(F32)).astype(BF16)

    def up_proj(j):
        u_scr[j % 2] = _dot(z_scr[...], wu_ref[j])

    def conv_act(j):
        cw = cw_ref[j]
        for sb in range(tm // FFN_SUB):
            r0 = hb + sb * FFN_SUB
            cv = cw[3:4] + cw[0:1] * u_scr[j % 2, r0 - 1:r0 - 1 + FFN_SUB]
            cv = cv + cw[1:2] * u_scr[j % 2, r0:r0 + FFN_SUB]
            cv = cv + cw[2:3] * u_scr[j % 2, r0 + 1:r0 + 1 + FFN_SUB]
            h = _silu(cv[:, :fc]) * cv[:, fc:]
            h_scr[sb * FFN_SUB:(sb + 1) * FFN_SUB, j * fc:(j + 1) * fc] = h.astype(BF16)

    for j in range(nch + 1):
        if j < nch:
            up_proj(j)
        if j >= 1:
            conv_act(j - 1)
    out = x + gate * _dot(h_scr[...], wd_ref[...])
    if final:
        out = _rms(out, fg_ref[...])
    o_ref[0] = out


def _ffn_pack_params(w_up, conv_w, conv_b, w_down, fc):
    d, f2 = w_up.shape
    f = f2 // 2
    nch = f // fc
    wu = jnp.concatenate([w_up[:, :f].reshape(d, nch, fc), w_up[:, f:].reshape(d, nch, fc)], axis=-1)
    wu = wu.transpose(1, 0, 2).astype(BF16)
    cw = jnp.concatenate([conv_w, conv_b[None], jnp.zeros((SUBLANES - 1 - conv_w.shape[0], f2), F32)], axis=0)
    cw = jnp.concatenate([cw[:, :f].reshape(SUBLANES, nch, fc), cw[:, f:].reshape(SUBLANES, nch, fc)], axis=-1)
    cw = cw.transpose(1, 0, 2)
    return wu, cw, w_down.astype(BF16)


def _ffn_call(x, mods_l, mod_row, g, packed, final_g=None):
    b, n, d = x.shape
    wu, cw, wd = packed
    nch, _, fc2 = wu.shape
    tm = _row_tile(n, 512)
    hb = FFN_HALO
    nbh = tm // hb
    const3 = lambda bi, i: (0, 0, 0)
    once = dict(pipeline_mode=pl.Buffered(1))
    in_specs = [
        pl.BlockSpec((1, tm, d), lambda bi, i: (bi, i, 0)),
        pl.BlockSpec((1, hb, d), lambda bi, i: (bi, jnp.maximum(i * nbh - 1, 0), 0)),
        pl.BlockSpec((1, hb, d), lambda bi, i: (bi, jnp.minimum((i + 1) * nbh, n // hb - 1), 0)),
        _mod_spec(d, mod_row),
        pl.BlockSpec((1, d), lambda bi, i: (0, 0)),
        pl.BlockSpec(wu.shape, const3, **once),
        pl.BlockSpec(cw.shape, const3),
        pl.BlockSpec(wd.shape, lambda bi, i: (0, 0), **once),
    ]
    args = [x, x, x, mods_l, g, wu, cw, wd]
    if final_g is not None:
        in_specs.append(pl.BlockSpec((1, d), lambda bi, i: (0, 0)))
        args.append(final_g)
    return pl.pallas_call(
        functools.partial(_ffn_kernel, fc=fc2 // 2, final=final_g is not None),
        out_shape=jax.ShapeDtypeStruct((b, n, d), F32),
        grid=(b, n // tm),
        in_specs=in_specs,
        out_specs=pl.BlockSpec((1, tm, d), lambda bi, i: (bi, i, 0)),
        scratch_shapes=[pltpu.VMEM((tm + 2 * hb, d), BF16), pltpu.VMEM((2, tm + 2 * hb, fc2), F32),
                        pltpu.VMEM((tm, nch * fc2 // 2), BF16)],
        compiler_params=_cparams(("arbitrary", "arbitrary")),
        name="ffn",
    )(*args)


def _rg_mixer(xl, xc, mods_l, ctx_row, g, w_in, conv_w, conv_b, wa, ba, wx, bx, lam, need_ctx):
    w = w_in.astype(BF16)
    g_l, xr_l = _rg_proj_call(xl, mods_l, None, g, w)
    g_c, xr_c = _rg_proj_call(xc, mods_l, ctx_row, g, w)
    cw, wbd, gp = _rg_pack_params(conv_w, conv_b, wa, ba, wx, bx, lam)
    m_c, m_l = _rg_scan_call(xr_c, g_c, xr_l, g_l, cw, wbd, gp)
    return m_l, (m_c if need_ctx else None)


def _na_mixer(xl, xc, mods_l, ctx_row, g, w_in, rpb, need_ctx):
    d = xl.shape[-1]
    w = w_in.astype(BF16)
    ql, kl, vl = _qkv_proj_call(xl, mods_l, None, g, w, d)
    qc, kc, vc = _qkv_proj_call(xc, mods_l, ctx_row, g, w, d)
    o_l = _na_attn_call(ql, kl, vl, kc, vc, rpb)
    o_c = _plain_attn_call(qc, [(kc, vc)], LANES) if need_ctx else None
    return o_l, o_c


def _gqa_mixer(xl, xc, mods_l, ctx_row, g, w_in, q_norm, k_norm, need_ctx):
    d = xl.shape[-1]
    nh = d // HEAD_DIM
    nkv = (w_in.shape[1] // HEAD_DIM - nh) // 2
    group = nh // nkv
    assert group % 2 == 0 and q_norm.shape[0] == HEAD_DIM
    dup = lambda wk: jnp.concatenate([wk.reshape(d, nkv, 1, HEAD_DIM)] * 2, axis=2).reshape(d, 2 * nkv * HEAD_DIM)
    nk = 2 * nkv * HEAD_DIM
    w = jnp.concatenate([w_in[:, :d], dup(w_in[:, d:d + nkv * HEAD_DIM]), dup(w_in[:, d + nkv * HEAD_DIM:])],
                        axis=1).astype(BF16)
    nseg = (d + nk) // HEAD_DIM
    assert nseg <= LANES
    seg = jnp.arange(d + nk) // HEAD_DIM
    ind = (seg[:, None] == jnp.arange(LANES)[None, :]).astype(BF16)
    gain = jnp.concatenate([jnp.tile(q_norm, nh), jnp.tile(k_norm, 2 * nkv)])[None]
    qkn = (gain, ind, ind.T)
    ql, kl, vl = _qkv_proj_call(xl, mods_l, None, g, w, nk, rope=_rope_tables(xl.shape[1]), qknorm=qkn)
    qc, kc, vc = _qkv_proj_call(xc, mods_l, ctx_row, g, w, nk, qknorm=qkn)
    qw = group * HEAD_DIM
    o_l = _plain_attn_call(ql, [(kl, vl), (kc, vc)], qw)
    o_c = _plain_attn_call(qc, [(kc, vc)], qw) if need_ctx else None
    return o_l, o_c


def _diff_mixer(xl, xc, mods_l, ctx_row, g, w_in, lq1, lk1, lq2, lk2, subln_g, lambda_init, need_ctx):
    d = xl.shape[-1]
    assert lq1.shape[0] == HEAD_DIM and subln_g.shape[0] == LANES
    w = w_in.astype(BF16)
    ql, kl, vl = _qkv_proj_call(xl, mods_l, None, g, w, d, rope=_rope_tables(xl.shape[1]))
    qc, kc, vc = _qkv_proj_call(xc, mods_l, ctx_row, g, w, d)
    lam_params = jnp.concatenate([jnp.stack([lq1, lk1, lq2, lk2]), jnp.zeros((SUBLANES - 4, HEAD_DIM), F32)])
    o_l = _diff_attn_call(ql, [(kl, vl), (kc, vc)], lam_params, subln_g[None], lambda_init)
    o_c = _diff_attn_call(qc, [(kc, vc)], lam_params, subln_g[None], lambda_init) if need_ctx else None
    return o_l, o_c


def kernel(x, c, ctx, c_ctx, mod_w, mod_b, norm1_g, norm2_g, rg_w_in, rg_conv_w, rg_conv_b, rg_wa, rg_ba, rg_wx, rg_bx, rg_lam, rg_w_out, na_w_in, na_rpb, na_w_out, gqa_w_in, gqa_q_norm, gqa_k_norm, gqa_w_out, diff_w_in, diff_lq1, diff_lk1, diff_lq2, diff_lk2, diff_subln_g, diff_w_out, ffn_w_up, ffn_conv_w, ffn_conv_b, ffn_w_down, final_g):
    bsz, _, d = x.shape
    depth = mod_w.shape[0]
    assert d % LANES == 0
    ctx_row = bsz
    nrow = -(-(bsz + 1) // SUBLANES) * SUBLANES
    cc = jnp.concatenate([c, c_ctx[None], jnp.zeros((nrow - bsz - 1, d), F32)], axis=0)
    mods = _mods_call(cc, mod_w, mod_b).reshape(depth, nrow, 6, d)
    f = ffn_w_down.shape[1]
    fc = next(t for t in (256, 128) if f % t == 0)
    xl, xc = x, ctx
    for l in range(depth):
        m, j = l % N_MIXERS, l // N_MIXERS
        need_ctx = l < depth - 1
        mods_l = mods[l]
        g1, g2 = norm1_g[l][None], norm2_g[l][None]
        if m == 0:
            a_l, a_c = _rg_mixer(xl, xc, mods_l, ctx_row, g1, rg_w_in[j], rg_conv_w[j], rg_conv_b[j], rg_wa[j],
                                 rg_ba[j], rg_wx[j], rg_bx[j], rg_lam[j], need_ctx)
            w_out = rg_w_out[j]
        elif m == 1:
            a_l, a_c = _na_mixer(xl, xc, mods_l, ctx_row, g1, na_w_in[j], na_rpb[j], need_ctx)
            w_out = na_w_out[j]
        elif m == 2:
            a_l, a_c = _gqa_mixer(xl, xc, mods_l, ctx_row, g1, gqa_w_in[j], gqa_q_norm[j], gqa_k_norm[j], need_ctx)
            w_out = gqa_w_out[j]
        else:
            lambda_init = 0.8 - 0.6 * math.exp(-0.3 * l)
            a_l, a_c = _diff_mixer(xl, xc, mods_l, ctx_row, g1, diff_w_in[j], diff_lq1[j], diff_lk1[j],
                                   diff_lq2[j], diff_lk2[j], diff_subln_g[j], lambda_init, need_ctx)
            w_out = diff_w_out[j]
        w_out = w_out.astype(BF16)
        packed = _ffn_pack_params(ffn_w_up[l], ffn_conv_w[l], ffn_conv_b[l], ffn_w_down[l], fc)
        last = l == depth - 1
        xl = _outproj_call(a_l, xl, mods_l, None, w_out)
        xl = _ffn_call(xl, mods_l, None, g2, packed, final_g[None] if last else None)
        if need_ctx:
            xc = _outproj_call(a_c, xc, mods_l, ctx_row, w_out)
            xc = _ffn_call(xc, mods_l, ctx_row, g2, packed)
    return xl
```

```python
import functools
import math

import jax
import jax.numpy as jnp
from jax import lax
from jax.experimental import pallas as pl
from jax.experimental.pallas import tpu as pltpu

F32 = jnp.float32
BF16 = jnp.bfloat16
EPS = 1e-6
ROPE_THETA = 10000.0
GRID_W = 64
RG_C = 8.0
RG_CONV_LEFT = 2
N_MIXERS = 4
HEAD_DIM = 64
LOG2E = 1.4426950408889634
Q_SCALE = HEAD_DIM ** -0.5 * LOG2E
LANES = 128
SUBLANES = 8
NEG = -1e30
FFN_HALO = 16
FFN_SUB = 64
ATT_SUB = 256
ATT_RB = 32
DIFF_RB = 16
VMEM_LIMIT_BYTES = 56 * 1024 * 1024

_NT = (((1,), (1,)), ((), ()))


def _cparams(sem):
    return pltpu.CompilerParams(dimension_semantics=sem, vmem_limit_bytes=VMEM_LIMIT_BYTES)


def _dot(a, b):
    return jnp.dot(a, b, preferred_element_type=F32)


def _dot_nt(a, b):
    return lax.dot_general(a, b, _NT, preferred_element_type=F32)


def _silu(x):
    return x / (1.0 + jnp.exp(-x))


def _gelu_tanh(x):
    cdf = 0.5 * (1.0 + jnp.tanh(math.sqrt(2.0 / math.pi) * (x + 0.044715 * (x * x * x))))
    return x * cdf


def _rms(x, g):
    y = x * lax.rsqrt(jnp.mean(x * x, axis=-1, keepdims=True) + EPS)
    return y * g


def _normmod(x, g, shift, scale):
    return _rms(x, g) * (1.0 + scale) + shift


def _mods_kernel(cc_ref, w_ref, b_ref, o_ref):
    a = _silu(cc_ref[...]).astype(BF16)
    o_ref[0] = _dot(a, w_ref[0].astype(BF16)) + b_ref[0]


def _mods_call(cc, mod_w, mod_b):
    depth, d, n = mod_w.shape
    r = cc.shape[0]
    tn = n // 4
    return pl.pallas_call(
        _mods_kernel,
        out_shape=jax.ShapeDtypeStruct((depth, r, n), F32),
        grid=(depth, n // tn),
        in_specs=[
            pl.BlockSpec((r, d), lambda l, j: (0, 0)),
            pl.BlockSpec((1, d, tn), lambda l, j: (l, 0, j)),
            pl.BlockSpec((1, 1, tn), lambda l, j: (l, 0, j)),
        ],
        out_specs=pl.BlockSpec((1, r, tn), lambda l, j: (l, 0, j)),
        compiler_params=_cparams(("arbitrary", "arbitrary")),
        name="mods",
    )(cc, mod_w, mod_b.reshape(depth, 1, n))


def _mod_spec(d, row):
    if row is None:
        return pl.BlockSpec((1, 6, d), lambda b, *_: (b, 0, 0))
    return pl.BlockSpec((1, 6, d), lambda b, *_: (row, 0, 0))


def _row_tile(n, pref):
    t = min(n, pref)
    assert n % t == 0
    return t


def _rope_tables(n_tok):
    t = jnp.arange(n_tok)
    row = (t // GRID_W).astype(F32)
    col = (t % GRID_W).astype(F32)
    n = HEAD_DIM // 4
    inv = ROPE_THETA ** (-jnp.arange(n, dtype=F32) / n)
    ang = jnp.concatenate([row[:, None] * inv, col[:, None] * inv], axis=-1)
    cos, sin = jnp.cos(ang), jnp.sin(ang)
    reps = LANES // HEAD_DIM
    cos_t = jnp.tile(jnp.concatenate([cos, cos], axis=-1), (1, reps))
    sin_t = jnp.tile(jnp.concatenate([-sin, sin], axis=-1), (1, reps))
    return cos_t, sin_t


def _swap_halves(x):
    lane = lax.broadcasted_iota(jnp.int32, x.shape, 1)
    lo = (lane & (HEAD_DIM // 2)) == 0
    return jnp.where(lo, pltpu.roll(x, LANES - HEAD_DIM // 2, 1), pltpu.roll(x, HEAD_DIM // 2, 1))


def _rope_cols(u, cos_t, sin_t):
    outs = []
    for c in range(u.shape[1] // LANES):
        xc = u[:, c * LANES:(c + 1) * LANES]
        outs.append(xc * cos_t + _swap_halves(xc) * sin_t)
    return jnp.concatenate(outs, axis=1)


def _rg_proj_kernel(x_ref, mod_ref, g_ref, w_ref, go_ref, xo_ref, *, dr):
    z = _normmod(x_ref[0], g_ref[...], mod_ref[0, 0:1], mod_ref[0, 1:2]).astype(BF16)
    u = _dot(z, w_ref[...])
    go_ref[0] = _gelu_tanh(u[:, :dr])
    xo_ref[0] = u[:, dr:]


def _rg_proj_call(x, mods_l, mod_row, g, w):
    b, n, d = x.shape
    dr = w.shape[1] // 2
    tm = _row_tile(n, 512)
    return pl.pallas_call(
        functools.partial(_rg_proj_kernel, dr=dr),
        out_shape=(jax.ShapeDtypeStruct((b, n, dr), F32), jax.ShapeDtypeStruct((b, n, dr), F32)),
        grid=(b, n // tm),
        in_specs=[
            pl.BlockSpec((1, tm, d), lambda bi, i: (bi, i, 0)),
            _mod_spec(d, mod_row),
            pl.BlockSpec((1, d), lambda bi, i: (0, 0)),
            pl.BlockSpec((d, 2 * dr), lambda bi, i: (0, 0)),
        ],
        out_specs=(pl.BlockSpec((1, tm, dr), lambda bi, i: (bi, i, 0)),
                   pl.BlockSpec((1, tm, dr), lambda bi, i: (bi, i, 0))),
        compiler_params=_cparams(("arbitrary", "arbitrary")),
        name="rg_proj",
    )(x, mods_l, g, w)


def _qkv_proj_kernel(x_ref, mod_ref, g_ref, w_ref, *rest, d, rope, qknorm):
    rest = list(rest)
    if rope:
        cos_ref, sin_ref = rest.pop(0), rest.pop(0)
    if qknorm:
        gain_ref, ind_ref, indt_ref = rest.pop(0), rest.pop(0), rest.pop(0)
    q_ref, k_ref, v_ref = rest
    z = _normmod(x_ref[0], g_ref[...], mod_ref[0, 0:1], mod_ref[0, 1:2]).astype(BF16)
    u = _dot(z, w_ref[...])
    nk = k_ref.shape[-1]
    qk = u[:, :d + nk]
    if qknorm:
        x2 = qk * qk
        hi = x2.astype(BF16)
        lo = (x2 - hi.astype(F32)).astype(BF16)
        ssq = _dot(hi, ind_ref[...]) + _dot(lo, ind_ref[...])
        r = lax.rsqrt(ssq * (1.0 / HEAD_DIM) + EPS)
        rhi = r.astype(BF16)
        rlo = (r - rhi.astype(F32)).astype(BF16)
        rb = _dot(rhi, indt_ref[...]) + _dot(rlo, indt_ref[...])
        qk = (qk * rb) * gain_ref[...]
    if rope:
        qk = _rope_cols(qk, cos_ref[...], sin_ref[...])
    q_ref[0] = (qk[:, :d] * Q_SCALE).astype(q_ref.dtype)
    k_ref[0] = qk[:, d:].astype(k_ref.dtype)
    v_ref[0] = u[:, d + nk:].astype(v_ref.dtype)


def _qkv_proj_call(x, mods_l, mod_row, g, w, nk, *, rope=None, qknorm=None):
    b, n, d = x.shape
    nv = w.shape[1] - d - nk
    tm = _row_tile(n, 512)
    const = lambda bi, i: (0, 0)
    in_specs = [
        pl.BlockSpec((1, tm, d), lambda bi, i: (bi, i, 0)),
        _mod_spec(d, mod_row),
        pl.BlockSpec((1, d), const),
        pl.BlockSpec(w.shape, const),
    ]
    args = [x, mods_l, g, w]
    if rope is not None:
        in_specs += [pl.BlockSpec((tm, LANES), lambda bi, i: (i, 0))] * 2
        args += list(rope)
    if qknorm is not None:
        in_specs += [pl.BlockSpec(a.shape, const) for a in qknorm]
        args += list(qknorm)
    out_spec = lambda w_: pl.BlockSpec((1, tm, w_), lambda bi, i: (bi, i, 0))
    return pl.pallas_call(
        functools.partial(_qkv_proj_kernel, d=d, rope=rope is not None, qknorm=qknorm is not None),
        out_shape=(jax.ShapeDtypeStruct((b, n, d), BF16), jax.ShapeDtypeStruct((b, n, nk), BF16),
                   jax.ShapeDtypeStruct((b, n, nv), BF16)),
        grid=(b, n // tm),
        in_specs=in_specs,
        out_specs=(out_spec(d), out_spec(nk), out_spec(nv)),
        compiler_params=_cparams(("arbitrary", "arbitrary")),
        name="qkv_proj",
    )(*args)


def _rg_conv(xs_scr, x, prev, nxt, cw):
    t = x.shape[0]
    h = SUBLANES
    xs_scr[0:h] = prev
    xs_scr[h:h + t] = x
    xs_scr[h + t:2 * h + t] = nxt
    y = cw[4:5] + cw[0:1] * xs_scr[h - 2:h - 2 + t]
    y = y + cw[1:2] * xs_scr[h - 1:h - 1 + t]
    y = y + cw[2:3] * x
    y = y + cw[3:4] * xs_scr[h + 1:h + 1 + t]
    return y


def _sigmoid(x):
    return 0.5 * jnp.tanh(0.5 * x) + 0.5


def _rg_gates(xc, w, ba, bx, lam):
    gw = xc.shape[1]
    y = _dot(xc.astype(BF16), w)
    r = _sigmoid(y[:, :gw] + ba)
    i = _sigmoid(y[:, gw:] + bx)
    nl = -lam
    softplus = jnp.maximum(nl, 0.0) + jnp.log1p(jnp.exp(-jnp.abs(nl)))
    log_a = r * (-RG_C * softplus)
    a = jnp.exp(log_a)
    one_minus_a2 = -jnp.tanh(log_a) * (a * a + 1.0)
    root = jnp.where(one_minus_a2 > 0.0, one_minus_a2 * lax.rsqrt(one_minus_a2), 0.0)
    return a, root * (i * xc)


def _rg_scan_tile(a, b, carry, reverse):
    t, gw = a.shape
    nck = t // SUBLANES
    a = a.reshape(nck, SUBLANES, gw)
    b = b.reshape(nck, SUBLANES, gw)
    rmod = lax.broadcasted_iota(jnp.int32, (1, SUBLANES, 1), 1)
    for s in (1, 2, 4):
        ok = (rmod < SUBLANES - s) if reverse else (rmod >= s)
        shift = SUBLANES - s if reverse else s
        ash, bsh = pltpu.roll(a, shift, 1), pltpu.roll(b, shift, 1)
        b = a * jnp.where(ok, bsh, 0.0) + b
        a = a * jnp.where(ok, ash, 1.0)
    hs = [None] * nck
    for ck in (range(nck - 1, -1, -1) if reverse else range(nck)):
        hc = b[ck] + a[ck] * carry
        carry = hc[0:1] if reverse else hc[SUBLANES - 1:SUBLANES]
        hs[ck] = hc
    return jnp.concatenate(hs, axis=0), carry


def _rg_scan_kernel(xc_ref, gc_ref, xl_ref, xp_ref, xn_ref, gl_ref, cw_ref, w_ref, gp_ref,
                    mc_ref, ml_ref, hf_scr, xv_scr, car_scr, xs_scr, *, nt):
    s = pl.program_id(2)
    gw = xl_ref.shape[-1]
    tl = xl_ref.shape[1]
    cw = cw_ref[0]
    gp = gp_ref[0]

    def gates(xconv, d):
        w = w_ref[0, :, d * 2 * gw:(d + 1) * 2 * gw]
        return _rg_gates(xconv, w, gp[3 * d:3 * d + 1], gp[3 * d + 1:3 * d + 2], gp[3 * d + 2:3 * d + 3])

    @pl.when(s == 0)
    def _ctx():
        zeros8 = jnp.zeros((SUBLANES, gw), F32)
        xconv = _rg_conv(xs_scr, xc_ref[0], zeros8, zeros8, cw)
        zero = jnp.zeros((1, gw), F32)
        a0, b0 = gates(xconv, 0)
        hf, cf = _rg_scan_tile(a0, b0, zero, False)
        a1, b1 = gates(xconv, 1)
        hb, cb = _rg_scan_tile(a1, b1, zero, True)
        car_scr[0:1] = cf
        car_scr[1:2] = cb
        mc_ref[0] = (gc_ref[0] * (hf + hb)).astype(mc_ref.dtype)

    def lat_conv(t):
        prev = xp_ref[0] * (t > 0).astype(F32)
        nxt = xn_ref[0] * (t < nt - 1).astype(F32)
        return _rg_conv(xs_scr, xl_ref[0], prev, nxt, cw)

    @pl.when((s >= 1) & (s <= nt))
    def _fwd():
        t = s - 1
        xconv = lat_conv(t)
        xv_scr[pl.ds(pl.multiple_of(t * tl, tl), tl), :] = xconv
        a0, b0 = gates(xconv, 0)
        h, c = _rg_scan_tile(a0, b0, car_scr[0:1], False)
        car_scr[0:1] = c
        hf_scr[pl.ds(pl.multiple_of(t * tl, tl), tl), :] = h

    @pl.when(s > nt)
    def _bwd():
        t = 2 * nt - s
        a1, b1 = gates(xv_scr[pl.ds(pl.multiple_of(t * tl, tl), tl), :], 1)
        h, c = _rg_scan_tile(a1, b1, car_scr[1:2], True)
        car_scr[1:2] = c
        hf = hf_scr[pl.ds(pl.multiple_of(t * tl, tl), tl), :]
        ml_ref[0] = (gl_ref[0] * (hf + h)).astype(ml_ref.dtype)


def _rg_scan_call(xr_c, g_c, xr_l, g_l, cw, wbd, gp):
    b, s_len, dr = xr_l.shape
    c_len = xr_c.shape[1]
    ng, gw = wbd.shape[0], wbd.shape[1]
    tl = _row_tile(s_len, 512)
    nt = s_len // tl
    nb8 = tl // SUBLANES

    def tile(s):
        return jnp.clip(s - 1, 0, nt - 1)

    def tile_bwd(s):
        return jnp.where(s <= nt, nt - 1, 2 * nt - s)

    return pl.pallas_call(
        functools.partial(_rg_scan_kernel, nt=nt),
        out_shape=(jax.ShapeDtypeStruct((b, c_len, dr), BF16), jax.ShapeDtypeStruct((b, s_len, dr), BF16)),
        grid=(b, ng, 2 * nt + 1),
        in_specs=[
            pl.BlockSpec((1, c_len, gw), lambda bi, h, s: (bi, 0, h)),
            pl.BlockSpec((1, c_len, gw), lambda bi, h, s: (bi, 0, h)),
            pl.BlockSpec((1, tl, gw), lambda bi, h, s: (bi, tile(s), h)),
            pl.BlockSpec((1, SUBLANES, gw), lambda bi, h, s: (bi, jnp.maximum(tile(s) * nb8 - 1, 0), h)),
            pl.BlockSpec((1, SUBLANES, gw),
                         lambda bi, h, s: (bi, jnp.minimum((tile(s) + 1) * nb8, s_len // SUBLANES - 1), h)),
            pl.BlockSpec((1, tl, gw), lambda bi, h, s: (bi, tile_bwd(s), h)),
            pl.BlockSpec((1, SUBLANES, gw), lambda bi, h, s: (h, 0, 0)),
            pl.BlockSpec((1, gw, 4 * gw), lambda bi, h, s: (h, 0, 0)),
            pl.BlockSpec((1, SUBLANES, gw), lambda bi, h, s: (h, 0, 0)),
        ],
        out_specs=(pl.BlockSpec((1, c_len, gw), lambda bi, h, s: (bi, 0, h)),
                   pl.BlockSpec((1, tl, gw), lambda bi, h, s: (bi, tile_bwd(s), h))),
        scratch_shapes=[pltpu.VMEM((s_len, gw), F32), pltpu.VMEM((s_len, gw), F32), pltpu.VMEM((SUBLANES, gw), F32),
                        pltpu.VMEM((max(tl, c_len) + 2 * SUBLANES, gw), F32)],
        compiler_params=_cparams(("arbitrary", "arbitrary", "arbitrary")),
        name="rg_scan",
    )(xr_c, g_c, xr_l, xr_l, xr_l, g_l, cw, wbd, gp)


def _rg_pack_params(conv_w, conv_b, wa, ba, wx, bx, lam):
    nblk, bw = wa.shape[1], wa.shape[2]
    dr = nblk * bw
    per = next(p for p in range(1, nblk + 1) if nblk % p == 0 and (p * bw) % LANES == 0)
    ng, gw = nblk // per, per * bw

    def dense(w):
        w = w.reshape(ng, per, bw, bw)
        eye = jnp.eye(per, dtype=w.dtype)
        return jnp.einsum('gpjk,pq->gpjqk', w, eye).reshape(ng, gw, gw)

    wbd = jnp.concatenate([dense(wa[0]), dense(wx[0]), dense(wa[1]), dense(wx[1])], axis=-1).astype(BF16)
    cw = jnp.concatenate([conv_w, conv_b[None], jnp.zeros((SUBLANES - 1 - conv_w.shape[0], dr), F32)], axis=0)
    cw = cw.reshape(SUBLANES, ng, gw).transpose(1, 0, 2)
    gp = jnp.stack([ba[0], bx[0], lam[0], ba[1], bx[1], lam[1], jnp.zeros_like(lam[0]), jnp.zeros_like(lam[0])])
    gp = gp.reshape(SUBLANES, ng, gw).transpose(1, 0, 2)
    return cw, wbd, gp


def _pair_masks(shape):
    lane = lax.broadcasted_iota(jnp.int32, shape, 1)
    return lane < HEAD_DIM


def _half_q(q, hh):
    first = _pair_masks(q.shape)
    return jnp.where(first if hh == 0 else jnp.logical_not(first), q, jnp.zeros_like(q))


def _scores_to_scratch(qh, k_refs, s_scr, slot):
    off = 0
    for k_ref in k_refs:
        n = k_ref.shape[1]
        s_scr[slot, :, off:off + n] = _dot_nt(qh, k_ref[0])
        off += n


def _exp_rows(s):
    e = jnp.exp2(s - s.max(axis=-1, keepdims=True))
    return e, e.sum(axis=-1, keepdims=True)


def _softmax_to_scratch(s_scr, p_scr, slot, rows):
    linv = []
    for r in range(0, rows, ATT_RB):
        e, l = _exp_rows(s_scr[slot, r:r + ATT_RB, :])
        p_scr[slot, r:r + ATT_RB, :] = e.astype(BF16)
        linv.append(1.0 / l)
    return jnp.concatenate(linv, axis=0)


def _pv_from_scratch(p_scr, slot, v_refs):
    off, o = 0, None
    for v_ref in v_refs:
        n = v_ref.shape[1]
        t = _dot(p_scr[slot, :, off:off + n], v_ref[0])
        o = t if o is None else o + t
        off += n
    return o


def _plain_attn_kernel(q_ref, *refs, nsrc):
    k_refs, v_refs = refs[0:2 * nsrc:2], refs[1:2 * nsrc:2]
    o_ref, s_scr, p_scr = refs[2 * nsrc:]
    tq, qw = q_ref.shape[1], q_ref.shape[2]
    sub = s_scr.shape[1]
    units = [(r, c, hh) for r in range(0, tq, sub) for c in range(0, qw, LANES) for hh in range(2)]

    def scores(u):
        r, c, hh = units[u]
        _scores_to_scratch(_half_q(q_ref[0, r:r + sub, c:c + LANES], hh), k_refs, s_scr, u % 2)

    scores(0)
    outs = [None, None]
    for u, (r, c, hh) in enumerate(units):
        if u + 1 < len(units):
            scores(u + 1)
        linv = _softmax_to_scratch(s_scr, p_scr, u % 2, sub)
        outs[hh] = _pv_from_scratch(p_scr, u % 2, v_refs) * linv
        if hh == 1:
            o_ref[0, r:r + sub, c:c + LANES] = jnp.where(_pair_masks((sub, LANES)), outs[0], outs[1]).astype(o_ref.dtype)


def _plain_attn_call(q, kvs, qw):
    b, lq, d = q.shape
    tq = _row_tile(lq, 2 * ATT_SUB)
    sub = min(tq, ATT_SUB)
    nk = sum(k.shape[1] for k, _ in kvs)
    in_specs = [pl.BlockSpec((1, tq, qw), lambda bi, j, i: (bi, i, j))]
    args = [q]
    for k, v in kvs:
        spec = pl.BlockSpec((1, k.shape[1], LANES), lambda bi, j, i: (bi, 0, j))
        in_specs += [spec, spec]
        args += [k, v]
    return pl.pallas_call(
        functools.partial(_plain_attn_kernel, nsrc=len(kvs)),
        out_shape=jax.ShapeDtypeStruct((b, lq, d), BF16),
        grid=(b, d // qw, lq // tq),
        in_specs=in_specs,
        out_specs=pl.BlockSpec((1, tq, qw), lambda bi, j, i: (bi, i, j)),
        scratch_shapes=[pltpu.VMEM((2, sub, nk), F32), pltpu.VMEM((2, sub, nk), BF16)],
        compiler_params=_cparams(("arbitrary", "arbitrary", "arbitrary")),
        name="plain_attn",
    )(*args)


def _diff_attn_kernel(q_ref, lam_ref, g_ref, *refs, nsrc, lambda_init):
    k_refs, v_refs = refs[0:2 * nsrc:2], refs[1:2 * nsrc:2]
    o_ref, s_scr, a_scr = refs[2 * nsrc:]
    lp = lam_ref[...]
    lam = (jnp.exp(jnp.sum(lp[0:1] * lp[1:2], axis=-1, keepdims=True))
           - jnp.exp(jnp.sum(lp[2:3] * lp[3:4], axis=-1, keepdims=True)) + lambda_init)
    tq = q_ref.shape[1]
    sub = s_scr.shape[1]
    units = list(range(0, tq, sub))

    def scores(u):
        q = q_ref[0, units[u]:units[u] + sub]
        for hh in range(2):
            _scores_to_scratch(_half_q(q, hh), k_refs, s_scr, 2 * (u % 2) + hh)

    def combine(u):
        for r in range(0, sub, DIFF_RB):
            e0, l0 = _exp_rows(s_scr[2 * (u % 2), r:r + DIFF_RB, :])
            e1, l1 = _exp_rows(s_scr[2 * (u % 2) + 1, r:r + DIFF_RB, :])
            a_scr[u % 2, r:r + DIFF_RB, :] = (e0 * (1.0 / l0) - e1 * (lam / l1)).astype(BF16)

    scores(0)
    for u, r in enumerate(units):
        if u + 1 < len(units):
            scores(u + 1)
        combine(u)
        o = _pv_from_scratch(a_scr, u % 2, v_refs)
        o_ref[0, r:r + sub] = (_rms(o, g_ref[...]) * (1.0 - lambda_init)).astype(o_ref.dtype)


def _diff_attn_call(q, kvs, lam_params, subln_g, lambda_init):
    b, lq, d = q.shape
    tq = _row_tile(lq, 4 * ATT_SUB)
    sub = min(tq, ATT_SUB)
    nk = sum(k.shape[1] for k, _ in kvs)
    const = lambda bi, j, i: (0, 0)
    in_specs = [pl.BlockSpec((1, tq, LANES), lambda bi, j, i: (bi, i, j)),
                pl.BlockSpec(lam_params.shape, const), pl.BlockSpec(subln_g.shape, const)]
    args = [q, lam_params, subln_g]
    for k, v in kvs:
        spec = pl.BlockSpec((1, k.shape[1], LANES), lambda bi, j, i: (bi, 0, j))
        in_specs += [spec, spec]
        args += [k, v]
    return pl.pallas_call(
        functools.partial(_diff_attn_kernel, nsrc=len(kvs), lambda_init=lambda_init),
        out_shape=jax.ShapeDtypeStruct((b, lq, d), BF16),
        grid=(b, d // LANES, lq // tq),
        in_specs=in_specs,
        out_specs=pl.BlockSpec((1, tq, LANES), lambda bi, j, i: (bi, i, j)),
        scratch_shapes=[pltpu.VMEM((4, sub, nk), F32), pltpu.VMEM((2, sub, nk), BF16)],
        compiler_params=_cparams(("arbitrary", "arbitrary", "arbitrary")),
        name="diff_attn",
    )(*args)


def _na_attn_kernel(q_ref, k_ref, v_ref, kc_ref, vc_ref, rpb_ref, o_ref, t2_scr, s_scr, p_scr, *, rows, kr, kcw, nq):
    w = GRID_W
    nwin = nq + kr
    ntab = 2 * kr
    nkw = nwin * w
    sub = nq * w
    nsub = q_ref.shape[1] // sub
    i = pl.program_id(2)

    @pl.when((i == 0) & (pl.program_id(1) == 0))
    def _build_bias():
        c_io = lax.broadcasted_iota(jnp.int32, (w, LANES), 0)
        cp_io = lax.broadcasted_iota(jnp.int32, (w, LANES), 1)
        cs = jnp.clip(c_io - kcw // 2, 0, w - kcw)
        colok = (cp_io >= cs) & (cp_io < cs + kcw)
        neg = jnp.full((w, LANES), NEG, F32)
        for hh in range(2):
            r = pltpu.roll(rpb_ref[hh] * LOG2E, LANES - (kcw - 1), 1)
            ts = []
            for e in range(ntab - 1):
                t = jnp.broadcast_to(r[e:e + 1], (w, LANES))
                t = pltpu.roll(t, 0, 1, stride=1, stride_axis=0)
                ts.append(jnp.where(colok, t, NEG))
            for e in range(ntab):
                lo = ts[e - 1] if e >= 1 else neg
                hi = ts[e] if e < ntab - 1 else neg
                t2_scr[hh, e] = jnp.where(cp_io < w, lo, pltpu.roll(hi, w, 1))

    units = [(t, hh) for t in range(nsub) for hh in range(2)]

    def window(t):
        r0 = (i * nsub + t) * nq
        ks = jnp.clip(r0 - kr // 2, 0, rows - nwin)
        return r0, ks, pl.ds(pl.multiple_of(ks * w, w), nkw)

    def scores(u):
        t, hh = units[u]
        _, _, win = window(t)
        qh = _half_q(q_ref[0, t * sub:(t + 1) * sub], hh)
        s_scr[u % 2, :, :nkw] = _dot_nt(qh, k_ref[0, win, :])
        s_scr[u % 2, :, nkw:] = _dot_nt(qh, kc_ref[0])

    def softmax(u):
        t, hh = units[u]
        r0, ks, _ = window(t)
        key_row = ks + jnp.right_shift(lax.broadcasted_iota(jnp.int32, (1, nkw), 1), int(math.log2(w)))
        linv = []
        for qi in range(nq):
            r = r0 + qi
            rs = jnp.clip(r - kr // 2, 0, rows - kr)
            rowok = (key_row >= rs) & (key_row < rs + kr)
            idx = [jnp.clip(ks + 2 * p - r + kr, 0, ntab - 1) for p in range(nwin // 2)]
            for rb in range(0, w, ATT_RB):
                row = qi * w + rb
                bias = jnp.concatenate([t2_scr[hh, e, rb:rb + ATT_RB, :] for e in idx], axis=1)
                sw = jnp.where(rowok, s_scr[u % 2, row:row + ATT_RB, :nkw] + bias, NEG)
                sc = s_scr[u % 2, row:row + ATT_RB, nkw:]
                m = jnp.maximum(sw.max(axis=-1, keepdims=True), sc.max(axis=-1, keepdims=True))
                ew, ec = jnp.exp2(sw - m), jnp.exp2(sc - m)
                p_scr[u % 2, row:row + ATT_RB, :nkw] = ew.astype(BF16)
                p_scr[u % 2, row:row + ATT_RB, nkw:] = ec.astype(BF16)
                linv.append(1.0 / (ew.sum(axis=-1, keepdims=True) + ec.sum(axis=-1, keepdims=True)))
        return jnp.concatenate(linv, axis=0)

    scores(0)
    outs = [None, None]
    for u, (t, hh) in enumerate(units):
        if u + 1 < len(units):
            scores(u + 1)
        linv = softmax(u)
        _, _, win = window(t)
        o = _dot(p_scr[u % 2, :, :nkw], v_ref[0, win, :]) + _dot(p_scr[u % 2, :, nkw:], vc_ref[0])
        outs[hh] = o * linv
        if hh == 1:
            o_ref[0, t * sub:(t + 1) * sub] = jnp.where(_pair_masks((sub, LANES)), outs[0], outs[1]).astype(o_ref.dtype)


def _na_attn_call(q, k, v, kc, vc, rpb):
    b, s_len, d = q.shape
    c_len = kc.shape[1]
    nh, ndr, ndc = rpb.shape
    kr, kcw = (ndr + 1) // 2, (ndc + 1) // 2
    rows = s_len // GRID_W
    nq = 4
    assert GRID_W == HEAD_DIM and rows % nq == 0 and rows >= nq + kr and (nq + kr) % 2 == 0
    assert ndc <= LANES and 2 * kr <= 2 * SUBLANES
    rpb_p = jnp.full((nh, 2 * SUBLANES, LANES), NEG, F32).at[:, :ndr, :ndc].set(rpb)
    sub = nq * GRID_W
    tq = _row_tile(s_len, 2 * sub)
    nk = (nq + kr) * GRID_W + c_len
    full = lambda n: pl.BlockSpec((1, n, LANES), lambda j, bi, i: (bi, 0, j))
    return pl.pallas_call(
        functools.partial(_na_attn_kernel, rows=rows, kr=kr, kcw=kcw, nq=nq),
        out_shape=jax.ShapeDtypeStruct((b, s_len, d), BF16),
        grid=(d // LANES, b, s_len // tq),
        in_specs=[pl.BlockSpec((1, tq, LANES), lambda j, bi, i: (bi, i, j)),
                  full(s_len), full(s_len), full(c_len), full(c_len),
                  pl.BlockSpec((2, 2 * SUBLANES, LANES), lambda j, bi, i: (j, 0, 0))],
        out_specs=pl.BlockSpec((1, tq, LANES), lambda j, bi, i: (bi, i, j)),
        scratch_shapes=[pltpu.VMEM((2, 2 * kr, GRID_W, LANES), F32), pltpu.VMEM((2, sub, nk), F32),
                        pltpu.VMEM((2, sub, nk), BF16)],
        compiler_params=_cparams(("arbitrary", "arbitrary", "arbitrary")),
        name="na_attn",
    )(q, k, v, kc, vc, rpb_p)


def _outproj_kernel(a_ref, x_ref, mod_ref, g_ref, w_ref, o_ref, z_ref):
    x = x_ref[0] + mod_ref[0, 2:3] * _dot(a_ref[0], w_ref[...])
    o_ref[0] = x
    z_ref[0] = _normmod(x, g_ref[...], mod_ref[0, 3:4], mod_ref[0, 4:5]).astype(z_ref.dtype)


def _outproj_call(a, x, mods_l, mod_row, g, w):
    b, n, d = x.shape
    k = a.shape[-1]
    tm = _row_tile(n, 512)
    row = lambda w_: pl.BlockSpec((1, tm, w_), lambda bi, i: (bi, i, 0))
    return pl.pallas_call(
        _outproj_kernel,
        out_shape=(jax.ShapeDtypeStruct((b, n, d), F32), jax.ShapeDtypeStruct((b, n, d), BF16)),
        grid=(b, n // tm),
        in_specs=[row(k), row(d), _mod_spec(d, mod_row),
                  pl.BlockSpec((1, d), lambda bi, i: (0, 0)),
                  pl.BlockSpec((k, d), lambda bi, i: (0, 0))],
        out_specs=(row(d), row(d)),
        compiler_params=_cparams(("arbitrary", "arbitrary")),
        name="outproj",
    )(a, x, mods_l, g, w)


def _ffn_kernel(x_ref, z_ref, zp_ref, zn_ref, mod_ref, wu_ref, cw_ref, wd_ref, *rest, fc, final):
    if final:
        fg_ref, o_ref, z_scr, u_scr, h_scr = rest
    else:
        o_ref, z_scr, u_scr, h_scr = rest
    i = pl.program_id(1)
    nt = pl.num_programs(1)
    tm, d = x_ref.shape[1], x_ref.shape[2]
    hb = FFN_HALO
    nch = wu_ref.shape[0]
    z_scr[hb:hb + tm] = z_ref[0]
    z_scr[0:hb] = jnp.where(i > 0, zp_ref[0], jnp.zeros_like(zp_ref[0]))
    z_scr[hb + tm:] = jnp.where(i < nt - 1, zn_ref[0], jnp.zeros_like(zn_ref[0]))

    def up_proj(j):
        u_scr[j % 2] = _dot(z_scr[...], wu_ref[j])

    def conv_act(j):
        cw = cw_ref[j]
        for sb in range(tm // FFN_SUB):
            r0 = hb + sb * FFN_SUB
            cv = cw[3:4] + cw[0:1] * u_scr[j % 2, r0 - 1:r0 - 1 + FFN_SUB]
            cv = cv + cw[1:2] * u_scr[j % 2, r0:r0 + FFN_SUB]
            cv = cv + cw[2:3] * u_scr[j % 2, r0 + 1:r0 + 1 + FFN_SUB]
            hg = 0.5 * cv[:, :fc]
            h = (hg + hg * jnp.tanh(hg)) * cv[:, fc:]
            h_scr[sb * FFN_SUB:(sb + 1) * FFN_SUB, j * fc:(j + 1) * fc] = h.astype(BF16)

    up_proj(0)
    for j in range(nch - 1):
        up_proj(j + 1)
        conv_act(j)
    split = (nch - 1) * fc
    acc = _dot(h_scr[:, :split], wd_ref[:split, :])
    conv_act(nch - 1)
    acc = acc + _dot(h_scr[:, split:], wd_ref[split:, :])
    out = x_ref[0] + mod_ref[0, 5:6] * acc
    if final:
        out = _rms(out, fg_ref[...])
    o_ref[0] = out


def _ffn_pack_params(w_up, conv_w, conv_b, w_down, fc):
    d, f2 = w_up.shape
    f = f2 // 2
    nch = f // fc
    wu = jnp.concatenate([w_up[:, :f].reshape(d, nch, fc), w_up[:, f:].reshape(d, nch, fc)], axis=-1)
    wu = wu.transpose(1, 0, 2).astype(BF16)
    cw = jnp.concatenate([conv_w, conv_b[None], jnp.zeros((SUBLANES - 1 - conv_w.shape[0], f2), F32)], axis=0)
    cw = jnp.concatenate([cw[:, :f].reshape(SUBLANES, nch, fc), cw[:, f:].reshape(SUBLANES, nch, fc)], axis=-1)
    cw = cw.transpose(1, 0, 2)
    return wu, cw, w_down.astype(BF16)


def _ffn_call(x, z, mods_l, mod_row, packed, final_g=None):
    b, n, d = x.shape
    wu, cw, wd = packed
    nch, _, fc2 = wu.shape
    tm = _row_tile(n, 512)
    hb = FFN_HALO
    nbh = tm // hb
    const3 = lambda bi, i: (0, 0, 0)
    once = dict(pipeline_mode=pl.Buffered(1))
    in_specs = [
        pl.BlockSpec((1, tm, d), lambda bi, i: (bi, i, 0)),
        pl.BlockSpec((1, tm, d), lambda bi, i: (bi, i, 0)),
        pl.BlockSpec((1, hb, d), lambda bi, i: (bi, jnp.maximum(i * nbh - 1, 0), 0)),
        pl.BlockSpec((1, hb, d), lambda bi, i: (bi, jnp.minimum((i + 1) * nbh, n // hb - 1), 0)),
        _mod_spec(d, mod_row),
        pl.BlockSpec(wu.shape, const3, **once),
        pl.BlockSpec(cw.shape, const3),
        pl.BlockSpec(wd.shape, lambda bi, i: (0, 0), **once),
    ]
    args = [x, z, z, z, mods_l, wu, cw, wd]
    if final_g is not None:
        in_specs.append(pl.BlockSpec((1, d), lambda bi, i: (0, 0)))
        args.append(final_g)
    return pl.pallas_call(
        functools.partial(_ffn_kernel, fc=fc2 // 2, final=final_g is not None),
        out_shape=jax.ShapeDtypeStruct((b, n, d), F32),
        grid=(b, n // tm),
        in_specs=in_specs,
        out_specs=pl.BlockSpec((1, tm, d), lambda bi, i: (bi, i, 0)),
        scratch_shapes=[pltpu.VMEM((tm + 2 * hb, d), BF16), pltpu.VMEM((2, tm + 2 * hb, fc2), F32),
                        pltpu.VMEM((tm, nch * fc2 // 2), BF16)],
        compiler_params=_cparams(("arbitrary", "arbitrary")),
        name="ffn",
    )(*args)


def _rg_mixer(xl, xc, mods_l, ctx_row, g, w_in, conv_w, conv_b, wa, ba, wx, bx, lam, need_ctx):
    w = w_in.astype(BF16)
    g_l, xr_l = _rg_proj_call(xl, mods_l, None, g, w)
    g_c, xr_c = _rg_proj_call(xc, mods_l, ctx_row, g, w)
    cw, wbd, gp = _rg_pack_params(conv_w, conv_b, wa, ba, wx, bx, lam)
    m_c, m_l = _rg_scan_call(xr_c, g_c, xr_l, g_l, cw, wbd, gp)
    return m_l, (m_c if need_ctx else None)


def _na_mixer(xl, xc, mods_l, ctx_row, g, w_in, rpb, need_ctx):
    d = xl.shape[-1]
    w = w_in.astype(BF16)
    ql, kl, vl = _qkv_proj_call(xl, mods_l, None, g, w, d)
    qc, kc, vc = _qkv_proj_call(xc, mods_l, ctx_row, g, w, d)
    o_l = _na_attn_call(ql, kl, vl, kc, vc, rpb)
    o_c = _plain_attn_call(qc, [(kc, vc)], LANES) if need_ctx else None
    return o_l, o_c


def _gqa_mixer(xl, xc, mods_l, ctx_row, g, w_in, q_norm, k_norm, need_ctx):
    d = xl.shape[-1]
    nh = d // HEAD_DIM
    nkv = (w_in.shape[1] // HEAD_DIM - nh) // 2
    group = nh // nkv
    assert group % 2 == 0 and q_norm.shape[0] == HEAD_DIM
    dup = lambda wk: jnp.concatenate([wk.reshape(d, nkv, 1, HEAD_DIM)] * 2, axis=2).reshape(d, 2 * nkv * HEAD_DIM)
    nk = 2 * nkv * HEAD_DIM
    w = jnp.concatenate([w_in[:, :d], dup(w_in[:, d:d + nkv * HEAD_DIM]), dup(w_in[:, d + nkv * HEAD_DIM:])],
                        axis=1).astype(BF16)
    nseg = (d + nk) // HEAD_DIM
    assert nseg <= LANES
    seg = jnp.arange(d + nk) // HEAD_DIM
    ind = (seg[:, None] == jnp.arange(LANES)[None, :]).astype(BF16)
    gain = jnp.concatenate([jnp.tile(q_norm, nh), jnp.tile(k_norm, 2 * nkv)])[None]
    qkn = (gain, ind, ind.T)
    ql, kl, vl = _qkv_proj_call(xl, mods_l, None, g, w, nk, rope=_rope_tables(xl.shape[1]), qknorm=qkn)
    qc, kc, vc = _qkv_proj_call(xc, mods_l, ctx_row, g, w, nk, qknorm=qkn)
    qw = group * HEAD_DIM
    o_l = _plain_attn_call(ql, [(kl, vl), (kc, vc)], qw)
    o_c = _plain_attn_call(qc, [(kc, vc)], qw) if need_ctx else None
    return o_l, o_c


def _diff_mixer(xl, xc, mods_l, ctx_row, g, w_in, lq1, lk1, lq2, lk2, subln_g, lambda_init, need_ctx):
    d = xl.shape[-1]
    assert lq1.shape[0] == HEAD_DIM and subln_g.shape[0] == LANES
    w = w_in.astype(BF16)
    ql, kl, vl = _qkv_proj_call(xl, mods_l, None, g, w, d, rope=_rope_tables(xl.shape[1]))
    qc, kc, vc = _qkv_proj_call(xc, mods_l, ctx_row, g, w, d)
    lam_params = jnp.concatenate([jnp.stack([lq1, lk1, lq2, lk2]), jnp.zeros((SUBLANES - 4, HEAD_DIM), F32)])
    o_l = _diff_attn_call(ql, [(kl, vl), (kc, vc)], lam_params, subln_g[None], lambda_init)
    o_c = _diff_attn_call(qc, [(kc, vc)], lam_params, subln_g[None], lambda_init) if need_ctx else None
    return o_l, o_c


def kernel(x, c, ctx, c_ctx, mod_w, mod_b, norm1_g, norm2_g, rg_w_in, rg_conv_w, rg_conv_b, rg_wa, rg_ba, rg_wx, rg_bx, rg_lam, rg_w_out, na_w_in, na_rpb, na_w_out, gqa_w_in, gqa_q_norm, gqa_k_norm, gqa_w_out, diff_w_in, diff_lq1, diff_lk1, diff_lq2, diff_lk2, diff_subln_g, diff_w_out, ffn_w_up, ffn_conv_w, ffn_conv_b, ffn_w_down, final_g):
    bsz, _, d = x.shape
    depth = mod_w.shape[0]
    assert d % LANES == 0
    ctx_row = bsz
    nrow = -(-(bsz + 1) // SUBLANES) * SUBLANES
    cc = jnp.concatenate([c, c_ctx[None], jnp.zeros((nrow - bsz - 1, d), F32)], axis=0)
    mods = _mods_call(cc, mod_w, mod_b).reshape(depth, nrow, 6, d)
    f = ffn_w_down.shape[1]
    fc = next(t for t in (256, 128) if f % t == 0)
    xl, xc = x, ctx
    for l in range(depth):
        m, j = l % N_MIXERS, l // N_MIXERS
        need_ctx = l < depth - 1
        mods_l = mods[l]
        g1, g2 = norm1_g[l][None], norm2_g[l][None]
        if m == 0:
            a_l, a_c = _rg_mixer(xl, xc, mods_l, ctx_row, g1, rg_w_in[j], rg_conv_w[j], rg_conv_b[j], rg_wa[j],
                                 rg_ba[j], rg_wx[j], rg_bx[j], rg_lam[j], need_ctx)
            w_out = rg_w_out[j]
        elif m == 1:
            a_l, a_c = _na_mixer(xl, xc, mods_l, ctx_row, g1, na_w_in[j], na_rpb[j], need_ctx)
            w_out = na_w_out[j]
        elif m == 2:
            a_l, a_c = _gqa_mixer(xl, xc, mods_l, ctx_row, g1, gqa_w_in[j], gqa_q_norm[j], gqa_k_norm[j], need_ctx)
            w_out = gqa_w_out[j]
        else:
            lambda_init = 0.8 - 0.6 * math.exp(-0.3 * l)
            a_l, a_c = _diff_mixer(xl, xc, mods_l, ctx_row, g1, diff_w_in[j], diff_lq1[j], diff_lk1[j],
                                   diff_lq2[j], diff_lk2[j], diff_subln_g[j], lambda_init, need_ctx)
            w_out = diff_w_out[j]
        w_out = w_out.astype(BF16)
        packed = _ffn_pack_params(ffn_w_up[l], ffn_conv_w[l], ffn_conv_b[l], ffn_w_down[l], fc)
        last = l == depth - 1
        xl, zl = _outproj_call(a_l, xl, mods_l, None, g2, w_out)
        xl = _ffn_call(xl, zl, mods_l, None, packed, final_g[None] if last else None)
        if need_ctx:
            xc, zc = _outproj_call(a_c, xc, mods_l, ctx_row, g2, w_out)
            xc = _ffn_call(xc, zc, mods_l, ctx_row, packed)
    return xl
```

```python
import functools
import math

import jax
import jax.numpy as jnp
from jax import lax
from jax.experimental import pallas as pl
from jax.experimental.pallas import tpu as pltpu

F32 = jnp.float32
BF16 = jnp.bfloat16
EPS = 1e-6
ROPE_THETA = 10000.0
GRID_W = 64
RG_C = 8.0
RG_CONV_LEFT = 2
N_MIXERS = 4
HEAD_DIM = 64
LOG2E = 1.4426950408889634
Q_SCALE = HEAD_DIM ** -0.5 * LOG2E
LANES = 128
SUBLANES = 8
NEG = -1e30
FFN_HALO = 16
FFN_SUB = 64
ATT_SUB = 256
ATT_RB = 32
DIFF_RB = 16
KEY_SLAB = 64
VMEM_LIMIT_BYTES = 56 * 1024 * 1024

_NT = (((1,), (1,)), ((), ()))


def _cparams(sem):
    return pltpu.CompilerParams(dimension_semantics=sem, vmem_limit_bytes=VMEM_LIMIT_BYTES)


def _dot(a, b):
    return jnp.dot(a, b, preferred_element_type=F32)


def _dot_nt(a, b):
    return lax.dot_general(a, b, _NT, preferred_element_type=F32)


def _silu(x):
    return x / (1.0 + jnp.exp(-x))


def _gelu_tanh(x):
    cdf = 0.5 * (1.0 + jnp.tanh(math.sqrt(2.0 / math.pi) * (x + 0.044715 * (x * x * x))))
    return x * cdf


def _rms(x, g):
    y = x * lax.rsqrt(jnp.mean(x * x, axis=-1, keepdims=True) + EPS)
    return y * g


def _normmod(x, g, shift, scale):
    return _rms(x, g) * (1.0 + scale) + shift


def _mods_kernel(cc_ref, w_ref, b_ref, o_ref):
    a = _silu(cc_ref[...]).astype(BF16)
    o_ref[0] = _dot(a, w_ref[0].astype(BF16)) + b_ref[0]


def _mods_call(cc, mod_w, mod_b):
    depth, d, n = mod_w.shape
    r = cc.shape[0]
    tn = n // 4
    return pl.pallas_call(
        _mods_kernel,
        out_shape=jax.ShapeDtypeStruct((depth, r, n), F32),
        grid=(depth, n // tn),
        in_specs=[
            pl.BlockSpec((r, d), lambda l, j: (0, 0)),
            pl.BlockSpec((1, d, tn), lambda l, j: (l, 0, j)),
            pl.BlockSpec((1, 1, tn), lambda l, j: (l, 0, j)),
        ],
        out_specs=pl.BlockSpec((1, r, tn), lambda l, j: (l, 0, j)),
        compiler_params=_cparams(("arbitrary", "arbitrary")),
        name="mods",
    )(cc, mod_w, mod_b.reshape(depth, 1, n))


def _mod_spec(d, row):
    if row is None:
        return pl.BlockSpec((1, 6, d), lambda b, *_: (b, 0, 0))
    return pl.BlockSpec((1, 6, d), lambda b, *_: (row, 0, 0))


def _row_tile(n, pref):
    t = min(n, pref)
    assert n % t == 0
    return t


def _rope_tables(n_tok):
    t = jnp.arange(n_tok)
    row = (t // GRID_W).astype(F32)
    col = (t % GRID_W).astype(F32)
    n = HEAD_DIM // 4
    inv = ROPE_THETA ** (-jnp.arange(n, dtype=F32) / n)
    ang = jnp.concatenate([row[:, None] * inv, col[:, None] * inv], axis=-1)
    cos, sin = jnp.cos(ang), jnp.sin(ang)
    reps = LANES // HEAD_DIM
    cos_t = jnp.tile(jnp.concatenate([cos, cos], axis=-1), (1, reps))
    sin_t = jnp.tile(jnp.concatenate([-sin, sin], axis=-1), (1, reps))
    return cos_t, sin_t


def _swap_halves(x):
    lane = lax.broadcasted_iota(jnp.int32, x.shape, 1)
    lo = (lane & (HEAD_DIM // 2)) == 0
    return jnp.where(lo, pltpu.roll(x, LANES - HEAD_DIM // 2, 1), pltpu.roll(x, HEAD_DIM // 2, 1))


def _rope_cols(u, cos_t, sin_t):
    outs = []
    for c in range(u.shape[1] // LANES):
        xc = u[:, c * LANES:(c + 1) * LANES]
        outs.append(xc * cos_t + _swap_halves(xc) * sin_t)
    return jnp.concatenate(outs, axis=1)


def _rg_proj_kernel(x_ref, mod_ref, g_ref, w_ref, go_ref, xo_ref, *, dr):
    z = _normmod(x_ref[0], g_ref[...], mod_ref[0, 0:1], mod_ref[0, 1:2]).astype(BF16)
    u = _dot(z, w_ref[...])
    go_ref[0] = _gelu_tanh(u[:, :dr])
    xo_ref[0] = u[:, dr:]


def _rg_proj_call(x, mods_l, mod_row, g, w):
    b, n, d = x.shape
    dr = w.shape[1] // 2
    tm = _row_tile(n, 512)
    return pl.pallas_call(
        functools.partial(_rg_proj_kernel, dr=dr),
        out_shape=(jax.ShapeDtypeStruct((b, n, dr), F32), jax.ShapeDtypeStruct((b, n, dr), F32)),
        grid=(b, n // tm),
        in_specs=[
            pl.BlockSpec((1, tm, d), lambda bi, i: (bi, i, 0)),
            _mod_spec(d, mod_row),
            pl.BlockSpec((1, d), lambda bi, i: (0, 0)),
            pl.BlockSpec((d, 2 * dr), lambda bi, i: (0, 0)),
        ],
        out_specs=(pl.BlockSpec((1, tm, dr), lambda bi, i: (bi, i, 0)),
                   pl.BlockSpec((1, tm, dr), lambda bi, i: (bi, i, 0))),
        compiler_params=_cparams(("arbitrary", "arbitrary")),
        name="rg_proj",
    )(x, mods_l, g, w)


def _qkv_proj_kernel(x_ref, mod_ref, g_ref, w_ref, *rest, d, rope, qknorm):
    rest = list(rest)
    if rope:
        cos_ref, sin_ref = rest.pop(0), rest.pop(0)
    if qknorm:
        gain_ref, ind_ref, indt_ref = rest.pop(0), rest.pop(0), rest.pop(0)
    q_ref, k_ref, v_ref = rest
    z = _normmod(x_ref[0], g_ref[...], mod_ref[0, 0:1], mod_ref[0, 1:2]).astype(BF16)
    u = _dot(z, w_ref[...])
    nk = k_ref.shape[-1]
    qk = u[:, :d + nk]
    if qknorm:
        x2 = qk * qk
        hi = x2.astype(BF16)
        lo = (x2 - hi.astype(F32)).astype(BF16)
        ssq = _dot(hi, ind_ref[...]) + _dot(lo, ind_ref[...])
        r = lax.rsqrt(ssq * (1.0 / HEAD_DIM) + EPS)
        rhi = r.astype(BF16)
        rlo = (r - rhi.astype(F32)).astype(BF16)
        rb = _dot(rhi, indt_ref[...]) + _dot(rlo, indt_ref[...])
        qk = (qk * rb) * gain_ref[...]
    if rope:
        qk = _rope_cols(qk, cos_ref[...], sin_ref[...])
    q_ref[0] = (qk[:, :d] * Q_SCALE).astype(q_ref.dtype)
    k_ref[0] = qk[:, d:].astype(k_ref.dtype)
    v_ref[0] = u[:, d + nk:].astype(v_ref.dtype)


def _qkv_proj_call(x, mods_l, mod_row, g, w, nk, *, rope=None, qknorm=None):
    b, n, d = x.shape
    nv = w.shape[1] - d - nk
    tm = _row_tile(n, 512)
    const = lambda bi, i: (0, 0)
    in_specs = [
        pl.BlockSpec((1, tm, d), lambda bi, i: (bi, i, 0)),
        _mod_spec(d, mod_row),
        pl.BlockSpec((1, d), const),
        pl.BlockSpec(w.shape, const),
    ]
    args = [x, mods_l, g, w]
    if rope is not None:
        in_specs += [pl.BlockSpec((tm, LANES), lambda bi, i: (i, 0))] * 2
        args += list(rope)
    if qknorm is not None:
        in_specs += [pl.BlockSpec(a.shape, const) for a in qknorm]
        args += list(qknorm)
    out_spec = lambda w_: pl.BlockSpec((1, tm, w_), lambda bi, i: (bi, i, 0))
    return pl.pallas_call(
        functools.partial(_qkv_proj_kernel, d=d, rope=rope is not None, qknorm=qknorm is not None),
        out_shape=(jax.ShapeDtypeStruct((b, n, d), BF16), jax.ShapeDtypeStruct((b, n, nk), BF16),
                   jax.ShapeDtypeStruct((b, n, nv), BF16)),
        grid=(b, n // tm),
        in_specs=in_specs,
        out_specs=(out_spec(d), out_spec(nk), out_spec(nv)),
        compiler_params=_cparams(("arbitrary", "arbitrary")),
        name="qkv_proj",
    )(*args)


def _rg_conv(xs_scr, x, prev, nxt, cw):
    t = x.shape[0]
    h = SUBLANES
    xs_scr[0:h] = prev
    xs_scr[h:h + t] = x
    xs_scr[h + t:2 * h + t] = nxt
    y = cw[4:5] + cw[0:1] * xs_scr[h - 2:h - 2 + t]
    y = y + cw[1:2] * xs_scr[h - 1:h - 1 + t]
    y = y + cw[2:3] * x
    y = y + cw[3:4] * xs_scr[h + 1:h + 1 + t]
    return y


def _sigmoid(x):
    return 0.5 * jnp.tanh(0.5 * x) + 0.5


def _rg_gates(xc, w, ba, bx, lam):
    gw = xc.shape[1]
    y = _dot(xc.astype(BF16), w)
    r = _sigmoid(y[:, :gw] + ba)
    i = _sigmoid(y[:, gw:] + bx)
    nl = -lam
    softplus = jnp.maximum(nl, 0.0) + jnp.log1p(jnp.exp(-jnp.abs(nl)))
    log_a = r * (-RG_C * softplus)
    a = jnp.exp(log_a)
    one_minus_a2 = -jnp.tanh(log_a) * (a * a + 1.0)
    root = jnp.where(one_minus_a2 > 0.0, one_minus_a2 * lax.rsqrt(one_minus_a2), 0.0)
    return a, root * (i * xc)


def _rg_scan_tile(a, b, carry, reverse):
    t, gw = a.shape
    nck = t // SUBLANES
    a = a.reshape(nck, SUBLANES, gw)
    b = b.reshape(nck, SUBLANES, gw)
    rmod = lax.broadcasted_iota(jnp.int32, (1, SUBLANES, 1), 1)
    for s in (1, 2, 4):
        ok = (rmod < SUBLANES - s) if reverse else (rmod >= s)
        shift = SUBLANES - s if reverse else s
        ash, bsh = pltpu.roll(a, shift, 1), pltpu.roll(b, shift, 1)
        b = a * jnp.where(ok, bsh, 0.0) + b
        a = a * jnp.where(ok, ash, 1.0)
    hs = [None] * nck
    for ck in (range(nck - 1, -1, -1) if reverse else range(nck)):
        hc = b[ck] + a[ck] * carry
        carry = hc[0:1] if reverse else hc[SUBLANES - 1:SUBLANES]
        hs[ck] = hc
    return jnp.concatenate(hs, axis=0), carry


def _rg_scan_kernel(xc_ref, gc_ref, xl_ref, xp_ref, xn_ref, gl_ref, cw_ref, w_ref, gp_ref,
                    mc_ref, ml_ref, hf_scr, xv_scr, car_scr, xs_scr, *, nt):
    s = pl.program_id(2)
    gw = xl_ref.shape[-1]
    tl = xl_ref.shape[1]
    cw = cw_ref[0]
    gp = gp_ref[0]

    def gates(xconv, d):
        w = w_ref[0, :, d * 2 * gw:(d + 1) * 2 * gw]
        return _rg_gates(xconv, w, gp[3 * d:3 * d + 1], gp[3 * d + 1:3 * d + 2], gp[3 * d + 2:3 * d + 3])

    @pl.when(s == 0)
    def _ctx():
        zeros8 = jnp.zeros((SUBLANES, gw), F32)
        xconv = _rg_conv(xs_scr, xc_ref[0], zeros8, zeros8, cw)
        zero = jnp.zeros((1, gw), F32)
        a0, b0 = gates(xconv, 0)
        hf, cf = _rg_scan_tile(a0, b0, zero, False)
        a1, b1 = gates(xconv, 1)
        hb, cb = _rg_scan_tile(a1, b1, zero, True)
        car_scr[0:1] = cf
        car_scr[1:2] = cb
        mc_ref[0] = (gc_ref[0] * (hf + hb)).astype(mc_ref.dtype)

    def lat_conv(t):
        prev = xp_ref[0] * (t > 0).astype(F32)
        nxt = xn_ref[0] * (t < nt - 1).astype(F32)
        return _rg_conv(xs_scr, xl_ref[0], prev, nxt, cw)

    @pl.when((s >= 1) & (s <= nt))
    def _fwd():
        t = s - 1
        xconv = lat_conv(t)
        xv_scr[pl.ds(pl.multiple_of(t * tl, tl), tl), :] = xconv
        a0, b0 = gates(xconv, 0)
        h, c = _rg_scan_tile(a0, b0, car_scr[0:1], False)
        car_scr[0:1] = c
        hf_scr[pl.ds(pl.multiple_of(t * tl, tl), tl), :] = h

    @pl.when(s > nt)
    def _bwd():
        t = 2 * nt - s
        a1, b1 = gates(xv_scr[pl.ds(pl.multiple_of(t * tl, tl), tl), :], 1)
        h, c = _rg_scan_tile(a1, b1, car_scr[1:2], True)
        car_scr[1:2] = c
        hf = hf_scr[pl.ds(pl.multiple_of(t * tl, tl), tl), :]
        ml_ref[0] = (gl_ref[0] * (hf + h)).astype(ml_ref.dtype)


def _rg_scan_call(xr_c, g_c, xr_l, g_l, cw, wbd, gp):
    b, s_len, dr = xr_l.shape
    c_len = xr_c.shape[1]
    ng, gw = wbd.shape[0], wbd.shape[1]
    tl = _row_tile(s_len, 512)
    nt = s_len // tl
    nb8 = tl // SUBLANES

    def tile(s):
        return jnp.clip(s - 1, 0, nt - 1)

    def tile_bwd(s):
        return jnp.where(s <= nt, nt - 1, 2 * nt - s)

    return pl.pallas_call(
        functools.partial(_rg_scan_kernel, nt=nt),
        out_shape=(jax.ShapeDtypeStruct((b, c_len, dr), BF16), jax.ShapeDtypeStruct((b, s_len, dr), BF16)),
        grid=(b, ng, 2 * nt + 1),
        in_specs=[
            pl.BlockSpec((1, c_len, gw), lambda bi, h, s: (bi, 0, h)),
            pl.BlockSpec((1, c_len, gw), lambda bi, h, s: (bi, 0, h)),
            pl.BlockSpec((1, tl, gw), lambda bi, h, s: (bi, tile(s), h)),
            pl.BlockSpec((1, SUBLANES, gw), lambda bi, h, s: (bi, jnp.maximum(tile(s) * nb8 - 1, 0), h)),
            pl.BlockSpec((1, SUBLANES, gw),
                         lambda bi, h, s: (bi, jnp.minimum((tile(s) + 1) * nb8, s_len // SUBLANES - 1), h)),
            pl.BlockSpec((1, tl, gw), lambda bi, h, s: (bi, tile_bwd(s), h)),
            pl.BlockSpec((1, SUBLANES, gw), lambda bi, h, s: (h, 0, 0)),
            pl.BlockSpec((1, gw, 4 * gw), lambda bi, h, s: (h, 0, 0)),
            pl.BlockSpec((1, SUBLANES, gw), lambda bi, h, s: (h, 0, 0)),
        ],
        out_specs=(pl.BlockSpec((1, c_len, gw), lambda bi, h, s: (bi, 0, h)),
                   pl.BlockSpec((1, tl, gw), lambda bi, h, s: (bi, tile_bwd(s), h))),
        scratch_shapes=[pltpu.VMEM((s_len, gw), F32), pltpu.VMEM((s_len, gw), F32), pltpu.VMEM((SUBLANES, gw), F32),
                        pltpu.VMEM((max(tl, c_len) + 2 * SUBLANES, gw), F32)],
        compiler_params=_cparams(("arbitrary", "arbitrary", "arbitrary")),
        name="rg_scan",
    )(xr_c, g_c, xr_l, xr_l, xr_l, g_l, cw, wbd, gp)


def _rg_pack_params(conv_w, conv_b, wa, ba, wx, bx, lam):
    nblk, bw = wa.shape[1], wa.shape[2]
    dr = nblk * bw
    per = next(p for p in range(1, nblk + 1) if nblk % p == 0 and (p * bw) % LANES == 0)
    ng, gw = nblk // per, per * bw

    def dense(w):
        w = w.reshape(ng, per, bw, bw)
        eye = jnp.eye(per, dtype=w.dtype)
        return jnp.einsum('gpjk,pq->gpjqk', w, eye).reshape(ng, gw, gw)

    wbd = jnp.concatenate([dense(wa[0]), dense(wx[0]), dense(wa[1]), dense(wx[1])], axis=-1).astype(BF16)
    cw = jnp.concatenate([conv_w, conv_b[None], jnp.zeros((SUBLANES - 1 - conv_w.shape[0], dr), F32)], axis=0)
    cw = cw.reshape(SUBLANES, ng, gw).transpose(1, 0, 2)
    gp = jnp.stack([ba[0], bx[0], lam[0], ba[1], bx[1], lam[1], jnp.zeros_like(lam[0]), jnp.zeros_like(lam[0])])
    gp = gp.reshape(SUBLANES, ng, gw).transpose(1, 0, 2)
    return cw, wbd, gp


def _pair_masks(shape):
    lane = lax.broadcasted_iota(jnp.int32, shape, 1)
    return lane < HEAD_DIM


def _half_q(q, hh):
    first = _pair_masks(q.shape)
    return jnp.where(first if hh == 0 else jnp.logical_not(first), q, jnp.zeros_like(q))


def _scores_to_scratch(qh, k_refs, s_scr, slot):
    off = 0
    for k_ref in k_refs:
        n = k_ref.shape[1]
        s_scr[slot, :, off:off + n] = _dot_nt(qh, k_ref[0])
        off += n


def _exp_rows(s):
    e = jnp.exp2(s - s.max(axis=-1, keepdims=True))
    return e, e.sum(axis=-1, keepdims=True)


def _softmax_to_scratch(s_scr, p_scr, slot, rows):
    linv = []
    for r in range(0, rows, ATT_RB):
        e, l = _exp_rows(s_scr[slot, r:r + ATT_RB, :])
        p_scr[slot, r:r + ATT_RB, :] = e.astype(BF16)
        linv.append(1.0 / l)
    return jnp.concatenate(linv, axis=0)


def _pv_from_scratch(p_scr, slot, v_refs):
    off, o = 0, None
    for v_ref in v_refs:
        n = v_ref.shape[1]
        t = _dot(p_scr[slot, :, off:off + n], v_ref[0])
        o = t if o is None else o + t
        off += n
    return o


def _plain_attn_kernel(q_ref, *refs, nsrc):
    k_refs, v_refs = refs[0:2 * nsrc:2], refs[1:2 * nsrc:2]
    o_ref, s_scr, p_scr = refs[2 * nsrc:]
    tq, qw = q_ref.shape[1], q_ref.shape[2]
    sub = s_scr.shape[1]
    units = [(r, c, hh) for r in range(0, tq, sub) for c in range(0, qw, LANES) for hh in range(2)]

    def scores(u):
        r, c, hh = units[u]
        _scores_to_scratch(_half_q(q_ref[0, r:r + sub, c:c + LANES], hh), k_refs, s_scr, u % 2)

    scores(0)
    outs = [None, None]
    for u, (r, c, hh) in enumerate(units):
        if u + 1 < len(units):
            scores(u + 1)
        linv = _softmax_to_scratch(s_scr, p_scr, u % 2, sub)
        outs[hh] = _pv_from_scratch(p_scr, u % 2, v_refs) * linv
        if hh == 1:
            o_ref[0, r:r + sub, c:c + LANES] = jnp.where(_pair_masks((sub, LANES)), outs[0], outs[1]).astype(o_ref.dtype)


def _reduce_keys(x, op):
    nk, nq = x.shape
    slab = next(t for t in (KEY_SLAB, SUBLANES) if nk % t == 0)
    return op(op(x.reshape(nk // slab, slab, nq), axis=0), axis=0, keepdims=True)


def _plain_attn_t_kernel(q_ref, *refs, nsrc):
    k_refs, v_refs = refs[0:2 * nsrc:2], refs[1:2 * nsrc:2]
    o_ref = refs[2 * nsrc]
    vt_scrs = refs[2 * nsrc + 1:3 * nsrc + 1]
    s_scr, p_scr = refs[3 * nsrc + 1:]
    tq, qw = q_ref.shape[1], q_ref.shape[2]
    sub = s_scr.shape[2]

    @pl.when(pl.program_id(2) == 0)
    def _transpose_values():
        for v_ref, vt_scr in zip(v_refs, vt_scrs):
            vt_scr[...] = v_ref[0].astype(F32).T.astype(BF16)

    units = [(r, c, hh) for r in range(0, tq, sub) for c in range(0, qw, LANES) for hh in range(2)]

    def scores(u):
        r, c, hh = units[u]
        qh = _half_q(q_ref[0, r:r + sub, c:c + LANES], hh)
        off = 0
        for k_ref in k_refs:
            n = k_ref.shape[1]
            s_scr[u % ns, off:off + n, :] = _dot_nt(k_ref[0], qh)
            off += n

    ns = s_scr.shape[0]
    for u in range(min(ns - 1, len(units))):
        scores(u)
    outs = [None, None]
    for u, (r, c, hh) in enumerate(units):
        if u + ns - 1 < len(units):
            scores(u + ns - 1)
        slabs = [slice(i, i + KEY_SLAB) for i in range(0, s_scr.shape[1], KEY_SLAB)]
        macc = s_scr[u % ns, slabs[0], :]
        for sl in slabs[1:]:
            macc = jnp.maximum(macc, s_scr[u % ns, sl, :])
        m = jnp.max(macc, axis=0, keepdims=True)
        lacc = None
        for sl in slabs:
            e = jnp.exp2(s_scr[u % ns, sl, :] - m)
            p_scr[u % 2, sl, :] = e.astype(BF16)
            lacc = e if lacc is None else lacc + e
        linv = 1.0 / jnp.sum(lacc, axis=0, keepdims=True)
        off, ot = 0, None
        for vt_scr in vt_scrs:
            n = vt_scr.shape[1]
            t = _dot(vt_scr[hh * HEAD_DIM:(hh + 1) * HEAD_DIM, :], p_scr[u % 2, off:off + n, :])
            ot = t if ot is None else ot + t
            off += n
        outs[hh] = ot * linv
        if hh == 1:
            o_ref[0, r:r + sub, c:c + LANES] = jnp.concatenate(outs, axis=0).T.astype(o_ref.dtype)


def _plain_attn_t_call(q, kvs, qw):
    b, lq, d = q.shape
    tq = _row_tile(lq, 2 * ATT_SUB)
    sub = min(tq, ATT_SUB)
    nk = sum(k.shape[1] for k, _ in kvs)
    in_specs = [pl.BlockSpec((1, tq, qw), lambda bi, j, i: (bi, i, j))]
    args = [q]
    for k, v in kvs:
        spec = pl.BlockSpec((1, k.shape[1], LANES), lambda bi, j, i: (bi, 0, j))
        in_specs += [spec, spec]
        args += [k, v]
    return pl.pallas_call(
        functools.partial(_plain_attn_t_kernel, nsrc=len(kvs)),
        out_shape=jax.ShapeDtypeStruct((b, lq, d), BF16),
        grid=(b, d // qw, lq // tq),
        in_specs=in_specs,
        out_specs=pl.BlockSpec((1, tq, qw), lambda bi, j, i: (bi, i, j)),
        scratch_shapes=([pltpu.VMEM((LANES, k.shape[1]), BF16) for k, _ in kvs]
                        + [pltpu.VMEM((3, nk, sub), F32), pltpu.VMEM((2, nk, sub), BF16)]),
        compiler_params=_cparams(("arbitrary", "arbitrary", "arbitrary")),
        name="plain_attn_t",
    )(*args)


def _plain_attn_call(q, kvs, qw):
    b, lq, d = q.shape
    tq = _row_tile(lq, 2 * ATT_SUB)
    sub = min(tq, ATT_SUB)
    nk = sum(k.shape[1] for k, _ in kvs)
    in_specs = [pl.BlockSpec((1, tq, qw), lambda bi, j, i: (bi, i, j))]
    args = [q]
    for k, v in kvs:
        spec = pl.BlockSpec((1, k.shape[1], LANES), lambda bi, j, i: (bi, 0, j))
        in_specs += [spec, spec]
        args += [k, v]
    return pl.pallas_call(
        functools.partial(_plain_attn_kernel, nsrc=len(kvs)),
        out_shape=jax.ShapeDtypeStruct((b, lq, d), BF16),
        grid=(b, d // qw, lq // tq),
        in_specs=in_specs,
        out_specs=pl.BlockSpec((1, tq, qw), lambda bi, j, i: (bi, i, j)),
        scratch_shapes=[pltpu.VMEM((2, sub, nk), F32), pltpu.VMEM((2, sub, nk), BF16)],
        compiler_params=_cparams(("arbitrary", "arbitrary", "arbitrary")),
        name="plain_attn",
    )(*args)


def _diff_attn_kernel(q_ref, lam_ref, g_ref, *refs, nsrc, lambda_init):
    k_refs, v_refs = refs[0:2 * nsrc:2], refs[1:2 * nsrc:2]
    o_ref, s_scr, a_scr = refs[2 * nsrc:]
    lp = lam_ref[...]
    lam = (jnp.exp(jnp.sum(lp[0:1] * lp[1:2], axis=-1, keepdims=True))
           - jnp.exp(jnp.sum(lp[2:3] * lp[3:4], axis=-1, keepdims=True)) + lambda_init)
    tq = q_ref.shape[1]
    sub = s_scr.shape[1]
    units = list(range(0, tq, sub))

    def scores(u):
        q = q_ref[0, units[u]:units[u] + sub]
        for hh in range(2):
            _scores_to_scratch(_half_q(q, hh), k_refs, s_scr, 2 * (u % 2) + hh)

    def combine(u):
        linv = []
        for r in range(0, sub, DIFF_RB):
            e0, l0 = _exp_rows(s_scr[2 * (u % 2), r:r + DIFF_RB, :])
            e1, l1 = _exp_rows(s_scr[2 * (u % 2) + 1, r:r + DIFF_RB, :])
            a_scr[u % 2, r:r + DIFF_RB, :] = (e0 - e1 * (lam * l0 / l1)).astype(BF16)
            linv.append(1.0 / l0)
        return jnp.concatenate(linv, axis=0)

    scores(0)
    for u, r in enumerate(units):
        if u + 1 < len(units):
            scores(u + 1)
        linv = combine(u)
        o = _pv_from_scratch(a_scr, u % 2, v_refs) * linv
        o_ref[0, r:r + sub] = (_rms(o, g_ref[...]) * (1.0 - lambda_init)).astype(o_ref.dtype)


def _diff_attn_call(q, kvs, lam_params, subln_g, lambda_init):
    b, lq, d = q.shape
    tq = _row_tile(lq, 4 * ATT_SUB)
    sub = min(tq, ATT_SUB)
    nk = sum(k.shape[1] for k, _ in kvs)
    const = lambda bi, j, i: (0, 0)
    in_specs = [pl.BlockSpec((1, tq, LANES), lambda bi, j, i: (bi, i, j)),
                pl.BlockSpec(lam_params.shape, const), pl.BlockSpec(subln_g.shape, const)]
    args = [q, lam_params, subln_g]
    for k, v in kvs:
        spec = pl.BlockSpec((1, k.shape[1], LANES), lambda bi, j, i: (bi, 0, j))
        in_specs += [spec, spec]
        args += [k, v]
    return pl.pallas_call(
        functools.partial(_diff_attn_kernel, nsrc=len(kvs), lambda_init=lambda_init),
        out_shape=jax.ShapeDtypeStruct((b, lq, d), BF16),
        grid=(b, d // LANES, lq // tq),
        in_specs=in_specs,
        out_specs=pl.BlockSpec((1, tq, LANES), lambda bi, j, i: (bi, i, j)),
        scratch_shapes=[pltpu.VMEM((4, sub, nk), F32), pltpu.VMEM((2, sub, nk), BF16)],
        compiler_params=_cparams(("arbitrary", "arbitrary", "arbitrary")),
        name="diff_attn",
    )(*args)


def _na_attn_kernel(q_ref, k_ref, v_ref, kc_ref, vc_ref, rpb_ref, o_ref, t2_scr, s_scr, p_scr, *, rows, kr, kcw, nq):
    w = GRID_W
    nwin = nq + kr
    ntab = 2 * kr
    nkw = nwin * w
    sub = nq * w
    nsub = q_ref.shape[1] // sub
    i = pl.program_id(2)

    @pl.when((i == 0) & (pl.program_id(1) == 0))
    def _build_bias():
        c_io = lax.broadcasted_iota(jnp.int32, (w, LANES), 0)
        cp_io = lax.broadcasted_iota(jnp.int32, (w, LANES), 1)
        cs = jnp.clip(c_io - kcw // 2, 0, w - kcw)
        colok = (cp_io >= cs) & (cp_io < cs + kcw)
        neg = jnp.full((w, LANES), NEG, F32)
        for hh in range(2):
            r = pltpu.roll(rpb_ref[hh] * LOG2E, LANES - (kcw - 1), 1)
            ts = []
            for e in range(ntab - 1):
                t = jnp.broadcast_to(r[e:e + 1], (w, LANES))
                t = pltpu.roll(t, 0, 1, stride=1, stride_axis=0)
                ts.append(jnp.where(colok, t, NEG))
            for e in range(ntab):
                lo = ts[e - 1] if e >= 1 else neg
                hi = ts[e] if e < ntab - 1 else neg
                t2_scr[hh, e] = jnp.where(cp_io < w, lo, pltpu.roll(hi, w, 1))

    units = [(t, hh) for t in range(nsub) for hh in range(2)]

    def window(t):
        r0 = (i * nsub + t) * nq
        ks = jnp.clip(r0 - kr // 2, 0, rows - nwin)
        return r0, ks, pl.ds(pl.multiple_of(ks * w, w), nkw)

    def scores(u):
        t, hh = units[u]
        _, _, win = window(t)
        qh = _half_q(q_ref[0, t * sub:(t + 1) * sub], hh)
        s_scr[u % 2, :, :nkw] = _dot_nt(qh, k_ref[0, win, :])
        s_scr[u % 2, :, nkw:] = _dot_nt(qh, kc_ref[0])

    def softmax(u):
        t, hh = units[u]
        r0, ks, _ = window(t)
        key_row = ks + jnp.right_shift(lax.broadcasted_iota(jnp.int32, (1, nkw), 1), int(math.log2(w)))
        linv = []
        for qi in range(nq):
            r = r0 + qi
            rs = jnp.clip(r - kr // 2, 0, rows - kr)
            rowok = (key_row >= rs) & (key_row < rs + kr)
            idx = [jnp.clip(ks + 2 * p - r + kr, 0, ntab - 1) for p in range(nwin // 2)]
            for rb in range(0, w, ATT_RB):
                row = qi * w + rb
                bias = jnp.concatenate([t2_scr[hh, e, rb:rb + ATT_RB, :] for e in idx], axis=1)
                sw = jnp.where(rowok, s_scr[u % 2, row:row + ATT_RB, :nkw] + bias, NEG)
                sc = s_scr[u % 2, row:row + ATT_RB, nkw:]
                m = jnp.maximum(sw.max(axis=-1, keepdims=True), sc.max(axis=-1, keepdims=True))
                ew, ec = jnp.exp2(sw - m), jnp.exp2(sc - m)
                p_scr[u % 2, row:row + ATT_RB, :nkw] = ew.astype(BF16)
                p_scr[u % 2, row:row + ATT_RB, nkw:] = ec.astype(BF16)
                linv.append(1.0 / (ew.sum(axis=-1, keepdims=True) + ec.sum(axis=-1, keepdims=True)))
        return jnp.concatenate(linv, axis=0)

    scores(0)
    outs = [None, None]
    for u, (t, hh) in enumerate(units):
        if u + 1 < len(units):
            scores(u + 1)
        linv = softmax(u)
        _, _, win = window(t)
        o = _dot(p_scr[u % 2, :, :nkw], v_ref[0, win, :]) + _dot(p_scr[u % 2, :, nkw:], vc_ref[0])
        outs[hh] = o * linv
        if hh == 1:
            o_ref[0, t * sub:(t + 1) * sub] = jnp.where(_pair_masks((sub, LANES)), outs[0], outs[1]).astype(o_ref.dtype)


def _na_attn_call(q, k, v, kc, vc, rpb):
    b, s_len, d = q.shape
    c_len = kc.shape[1]
    nh, ndr, ndc = rpb.shape
    kr, kcw = (ndr + 1) // 2, (ndc + 1) // 2
    rows = s_len // GRID_W
    nq = 4
    assert GRID_W == HEAD_DIM and rows % nq == 0 and rows >= nq + kr and (nq + kr) % 2 == 0
    assert ndc <= LANES and 2 * kr <= 2 * SUBLANES
    rpb_p = jnp.full((nh, 2 * SUBLANES, LANES), NEG, F32).at[:, :ndr, :ndc].set(rpb)
    sub = nq * GRID_W
    tq = _row_tile(s_len, 2 * sub)
    nk = (nq + kr) * GRID_W + c_len
    full = lambda n: pl.BlockSpec((1, n, LANES), lambda j, bi, i: (bi, 0, j))
    return pl.pallas_call(
        functools.partial(_na_attn_kernel, rows=rows, kr=kr, kcw=kcw, nq=nq),
        out_shape=jax.ShapeDtypeStruct((b, s_len, d), BF16),
        grid=(d // LANES, b, s_len // tq),
        in_specs=[pl.BlockSpec((1, tq, LANES), lambda j, bi, i: (bi, i, j)),
                  full(s_len), full(s_len), full(c_len), full(c_len),
                  pl.BlockSpec((2, 2 * SUBLANES, LANES), lambda j, bi, i: (j, 0, 0))],
        out_specs=pl.BlockSpec((1, tq, LANES), lambda j, bi, i: (bi, i, j)),
        scratch_shapes=[pltpu.VMEM((2, 2 * kr, GRID_W, LANES), F32), pltpu.VMEM((2, sub, nk), F32),
                        pltpu.VMEM((2, sub, nk), BF16)],
        compiler_params=_cparams(("arbitrary", "arbitrary", "arbitrary")),
        name="na_attn",
    )(q, k, v, kc, vc, rpb_p)


def _outproj_kernel(a_ref, x_ref, mod_ref, g_ref, w_ref, o_ref, z_ref):
    x = x_ref[0] + mod_ref[0, 2:3] * _dot(a_ref[0], w_ref[...])
    o_ref[0] = x
    z_ref[0] = _normmod(x, g_ref[...], mod_ref[0, 3:4], mod_ref[0, 4:5]).astype(z_ref.dtype)


def _outproj_call(a, x, mods_l, mod_row, g, w):
    b, n, d = x.shape
    k = a.shape[-1]
    tm = _row_tile(n, 512)
    row = lambda w_: pl.BlockSpec((1, tm, w_), lambda bi, i: (bi, i, 0))
    return pl.pallas_call(
        _outproj_kernel,
        out_shape=(jax.ShapeDtypeStruct((b, n, d), F32), jax.ShapeDtypeStruct((b, n, d), BF16)),
        grid=(b, n // tm),
        in_specs=[row(k), row(d), _mod_spec(d, mod_row),
                  pl.BlockSpec((1, d), lambda bi, i: (0, 0)),
                  pl.BlockSpec((k, d), lambda bi, i: (0, 0))],
        out_specs=(row(d), row(d)),
        compiler_params=_cparams(("arbitrary", "arbitrary")),
        name="outproj",
    )(a, x, mods_l, g, w)


def _ffn_kernel(x_ref, z_ref, zp_ref, zn_ref, mod_ref, wu_ref, cw_ref, wd_ref, *rest, fc, final):
    if final:
        fg_ref, o_ref, z_scr, u_scr, h_scr = rest
    else:
        o_ref, z_scr, u_scr, h_scr = rest
    i = pl.program_id(1)
    nt = pl.num_programs(1)
    tm, d = x_ref.shape[1], x_ref.shape[2]
    hb = FFN_HALO
    nch = wu_ref.shape[0]
    z_scr[hb:hb + tm] = z_ref[0]
    z_scr[0:hb] = jnp.where(i > 0, zp_ref[0], jnp.zeros_like(zp_ref[0]))
    z_scr[hb + tm:] = jnp.where(i < nt - 1, zn_ref[0], jnp.zeros_like(zn_ref[0]))

    def up_proj(j):
        u_scr[j % 2] = _dot(z_scr[...], wu_ref[j])

    def conv_act(j):
        cw = cw_ref[j]
        for sb in range(tm // FFN_SUB):
            r0 = hb + sb * FFN_SUB
            cv = cw[3:4] + cw[0:1] * u_scr[j % 2, r0 - 1:r0 - 1 + FFN_SUB]
            cv = cv + cw[1:2] * u_scr[j % 2, r0:r0 + FFN_SUB]
            cv = cv + cw[2:3] * u_scr[j % 2, r0 + 1:r0 + 1 + FFN_SUB]
            hg = 0.5 * cv[:, :fc]
            h = (hg + hg * jnp.tanh(hg)) * cv[:, fc:]
            h_scr[sb * FFN_SUB:(sb + 1) * FFN_SUB, j * fc:(j + 1) * fc] = h.astype(BF16)

    up_proj(0)
    for j in range(nch - 1):
        up_proj(j + 1)
        conv_act(j)
    split = (nch - 1) * fc
    acc = _dot(h_scr[:, :split], wd_ref[:split, :])
    conv_act(nch - 1)
    acc = acc + _dot(h_scr[:, split:], wd_ref[split:, :])
    out = x_ref[0] + mod_ref[0, 5:6] * acc
    if final:
        out = _rms(out, fg_ref[...])
    o_ref[0] = out


def _ffn_pack_params(w_up, conv_w, conv_b, w_down, fc):
    d, f2 = w_up.shape
    f = f2 // 2
    nch = f // fc
    wu = jnp.concatenate([w_up[:, :f].reshape(d, nch, fc), w_up[:, f:].reshape(d, nch, fc)], axis=-1)
    wu = wu.transpose(1, 0, 2).astype(BF16)
    cw = jnp.concatenate([conv_w, conv_b[None], jnp.zeros((SUBLANES - 1 - conv_w.shape[0], f2), F32)], axis=0)
    cw = jnp.concatenate([cw[:, :f].reshape(SUBLANES, nch, fc), cw[:, f:].reshape(SUBLANES, nch, fc)], axis=-1)
    cw = cw.transpose(1, 0, 2)
    return wu, cw, w_down.astype(BF16)


def _ffn_call(x, z, mods_l, mod_row, packed, final_g=None):
    b, n, d = x.shape
    wu, cw, wd = packed
    nch, _, fc2 = wu.shape
    tm = _row_tile(n, 512)
    hb = FFN_HALO
    nbh = tm // hb
    const3 = lambda bi, i: (0, 0, 0)
    once = dict(pipeline_mode=pl.Buffered(1))
    in_specs = [
        pl.BlockSpec((1, tm, d), lambda bi, i: (bi, i, 0)),
        pl.BlockSpec((1, tm, d), lambda bi, i: (bi, i, 0)),
        pl.BlockSpec((1, hb, d), lambda bi, i: (bi, jnp.maximum(i * nbh - 1, 0), 0)),
        pl.BlockSpec((1, hb, d), lambda bi, i: (bi, jnp.minimum((i + 1) * nbh, n // hb - 1), 0)),
        _mod_spec(d, mod_row),
        pl.BlockSpec(wu.shape, const3, **once),
        pl.BlockSpec(cw.shape, const3),
        pl.BlockSpec(wd.shape, lambda bi, i: (0, 0), **once),
    ]
    args = [x, z, z, z, mods_l, wu, cw, wd]
    if final_g is not None:
        in_specs.append(pl.BlockSpec((1, d), lambda bi, i: (0, 0)))
        args.append(final_g)
    return pl.pallas_call(
        functools.partial(_ffn_kernel, fc=fc2 // 2, final=final_g is not None),
        out_shape=jax.ShapeDtypeStruct((b, n, d), F32),
        grid=(b, n // tm),
        in_specs=in_specs,
        out_specs=pl.BlockSpec((1, tm, d), lambda bi, i: (bi, i, 0)),
        scratch_shapes=[pltpu.VMEM((tm + 2 * hb, d), BF16), pltpu.VMEM((2, tm + 2 * hb, fc2), F32),
                        pltpu.VMEM((tm, nch * fc2 // 2), BF16)],
        compiler_params=_cparams(("arbitrary", "arbitrary")),
        name="ffn",
    )(*args)


def _rg_mixer(xl, xc, mods_l, ctx_row, g, w_in, conv_w, conv_b, wa, ba, wx, bx, lam, need_ctx):
    w = w_in.astype(BF16)
    g_l, xr_l = _rg_proj_call(xl, mods_l, None, g, w)
    g_c, xr_c = _rg_proj_call(xc, mods_l, ctx_row, g, w)
    cw, wbd, gp = _rg_pack_params(conv_w, conv_b, wa, ba, wx, bx, lam)
    m_c, m_l = _rg_scan_call(xr_c, g_c, xr_l, g_l, cw, wbd, gp)
    return m_l, (m_c if need_ctx else None)


def _na_mixer(xl, xc, mods_l, ctx_row, g, w_in, rpb, need_ctx):
    d = xl.shape[-1]
    w = w_in.astype(BF16)
    ql, kl, vl = _qkv_proj_call(xl, mods_l, None, g, w, d)
    qc, kc, vc = _qkv_proj_call(xc, mods_l, ctx_row, g, w, d)
    o_l = _na_attn_call(ql, kl, vl, kc, vc, rpb)
    o_c = _plain_attn_call(qc, [(kc, vc)], LANES) if need_ctx else None
    return o_l, o_c


def _gqa_mixer(xl, xc, mods_l, ctx_row, g, w_in, q_norm, k_norm, need_ctx):
    d = xl.shape[-1]
    nh = d // HEAD_DIM
    nkv = (w_in.shape[1] // HEAD_DIM - nh) // 2
    group = nh // nkv
    assert group % 2 == 0 and q_norm.shape[0] == HEAD_DIM
    dup = lambda wk: jnp.concatenate([wk.reshape(d, nkv, 1, HEAD_DIM)] * 2, axis=2).reshape(d, 2 * nkv * HEAD_DIM)
    nk = 2 * nkv * HEAD_DIM
    w = jnp.concatenate([w_in[:, :d], dup(w_in[:, d:d + nkv * HEAD_DIM]), dup(w_in[:, d + nkv * HEAD_DIM:])],
                        axis=1).astype(BF16)
    nseg = (d + nk) // HEAD_DIM
    assert nseg <= LANES
    seg = jnp.arange(d + nk) // HEAD_DIM
    ind = (seg[:, None] == jnp.arange(LANES)[None, :]).astype(BF16)
    gain = jnp.concatenate([jnp.tile(q_norm, nh), jnp.tile(k_norm, 2 * nkv)])[None]
    qkn = (gain, ind, ind.T)
    ql, kl, vl = _qkv_proj_call(xl, mods_l, None, g, w, nk, rope=_rope_tables(xl.shape[1]), qknorm=qkn)
    qc, kc, vc = _qkv_proj_call(xc, mods_l, ctx_row, g, w, nk, qknorm=qkn)
    qw = group * HEAD_DIM
    o_l = _plain_attn_t_call(ql, [(kl, vl), (kc, vc)], qw)
    o_c = _plain_attn_call(qc, [(kc, vc)], qw) if need_ctx else None
    return o_l, o_c


def _diff_mixer(xl, xc, mods_l, ctx_row, g, w_in, lq1, lk1, lq2, lk2, subln_g, lambda_init, need_ctx):
    d = xl.shape[-1]
    assert lq1.shape[0] == HEAD_DIM and subln_g.shape[0] == LANES
    w = w_in.astype(BF16)
    ql, kl, vl = _qkv_proj_call(xl, mods_l, None, g, w, d, rope=_rope_tables(xl.shape[1]))
    qc, kc, vc = _qkv_proj_call(xc, mods_l, ctx_row, g, w, d)
    lam_params = jnp.concatenate([jnp.stack([lq1, lk1, lq2, lk2]), jnp.zeros((SUBLANES - 4, HEAD_DIM), F32)])
    o_l = _diff_attn_call(ql, [(kl, vl), (kc, vc)], lam_params, subln_g[None], lambda_init)
    o_c = _diff_attn_call(qc, [(kc, vc)], lam_params, subln_g[None], lambda_init) if need_ctx else None
    return o_l, o_c


def kernel(x, c, ctx, c_ctx, mod_w, mod_b, norm1_g, norm2_g, rg_w_in, rg_conv_w, rg_conv_b, rg_wa, rg_ba, rg_wx, rg_bx, rg_lam, rg_w_out, na_w_in, na_rpb, na_w_out, gqa_w_in, gqa_q_norm, gqa_k_norm, gqa_w_out, diff_w_in, diff_lq1, diff_lk1, diff_lq2, diff_lk2, diff_subln_g, diff_w_out, ffn_w_up, ffn_conv_w, ffn_conv_b, ffn_w_down, final_g):
    bsz, _, d = x.shape
    depth = mod_w.shape[0]
    assert d % LANES == 0
    ctx_row = bsz
    nrow = -(-(bsz + 1) // SUBLANES) * SUBLANES
    cc = jnp.concatenate([c, c_ctx[None], jnp.zeros((nrow - bsz - 1, d), F32)], axis=0)
    mods = _mods_call(cc, mod_w, mod_b).reshape(depth, nrow, 6, d)
    f = ffn_w_down.shape[1]
    fc = next(t for t in (256, 128) if f % t == 0)
    xl, xc = x, ctx
    for l in range(depth):
        m, j = l % N_MIXERS, l // N_MIXERS
        need_ctx = l < depth - 1
        mods_l = mods[l]
        g1, g2 = norm1_g[l][None], norm2_g[l][None]
        if m == 0:
            a_l, a_c = _rg_mixer(xl, xc, mods_l, ctx_row, g1, rg_w_in[j], rg_conv_w[j], rg_conv_b[j], rg_wa[j],
                                 rg_ba[j], rg_wx[j], rg_bx[j], rg_lam[j], need_ctx)
            w_out = rg_w_out[j]
        elif m == 1:
            a_l, a_c = _na_mixer(xl, xc, mods_l, ctx_row, g1, na_w_in[j], na_rpb[j], need_ctx)
            w_out = na_w_out[j]
        elif m == 2:
            a_l, a_c = _gqa_mixer(xl, xc, mods_l, ctx_row, g1, gqa_w_in[j], gqa_q_norm[j], gqa_k_norm[j], need_ctx)
            w_out = gqa_w_out[j]
        else:
            lambda_init = 0.8 - 0.6 * math.exp(-0.3 * l)
            a_l, a_c = _diff_mixer(xl, xc, mods_l, ctx_row, g1, diff_w_in[j], diff_lq1[j], diff_lk1[j],
                                   diff_lq2[j], diff_lk2[j], diff_subln_g[j], lambda_init, need_ctx)
            w_out = diff_w_out[j]
        w_out = w_out.astype(BF16)
        packed = _ffn_pack_params(ffn_w_up[l], ffn_conv_w[l], ffn_conv_b[l], ffn_w_down[l], fc)
        last = l == depth - 1
        xl, zl = _outproj_call(a_l, xl, mods_l, None, g2, w_out)
        xl = _ffn_call(xl, zl, mods_l, None, packed, final_g[None] if last else None)
        if need_ctx:
            xc, zc = _outproj_call(a_c, xc, mods_l, ctx_row, g2, w_out)
            xc = _ffn_call(xc, zc, mods_l, ctx_row, packed)
    return xl
```

```python
import functools
import math

import jax
import jax.numpy as jnp
from jax import lax
from jax.experimental import pallas as pl
from jax.experimental.pallas import tpu as pltpu

F32 = jnp.float32
BF16 = jnp.bfloat16
EPS = 1e-6
ROPE_THETA = 10000.0
GRID_W = 64
RG_C = 8.0
RG_CONV_LEFT = 2
N_MIXERS = 4
HEAD_DIM = 64
LOG2E = 1.4426950408889634
Q_SCALE = HEAD_DIM ** -0.5 * LOG2E
LANES = 128
SUBLANES = 8
NEG = -1e30
FFN_HALO = 16
FFN_SUB = 512
FFN_TM = 512
ATT_SUB = 256
ATT_RB = 32
DIFF_RB = 16
KEY_SLAB = 32
VMEM_LIMIT_BYTES = 56 * 1024 * 1024

_NT = (((1,), (1,)), ((), ()))


def _cparams(sem):
    return pltpu.CompilerParams(dimension_semantics=sem, vmem_limit_bytes=VMEM_LIMIT_BYTES)


def _dot(a, b):
    return jnp.dot(a, b, preferred_element_type=F32)


def _dot_nt(a, b):
    return lax.dot_general(a, b, _NT, preferred_element_type=F32)


def _silu(x):
    return x / (1.0 + jnp.exp(-x))


def _gelu_tanh(x):
    cdf = 0.5 * (1.0 + jnp.tanh(math.sqrt(2.0 / math.pi) * (x + 0.044715 * (x * x * x))))
    return x * cdf


def _rms(x, g):
    y = x * lax.rsqrt(jnp.mean(x * x, axis=-1, keepdims=True) + EPS)
    return y * g


def _normmod(x, g, shift, scale):
    return _rms(x, g) * (1.0 + scale) + shift


def _mods_kernel(cc_ref, w_ref, b_ref, o_ref):
    a = _silu(cc_ref[...]).astype(BF16)
    o_ref[0] = _dot(a, w_ref[0].astype(BF16)) + b_ref[0]


def _mods_call(cc, mod_w, mod_b):
    depth, d, n = mod_w.shape
    r = cc.shape[0]
    tn = n // 4
    return pl.pallas_call(
        _mods_kernel,
        out_shape=jax.ShapeDtypeStruct((depth, r, n), F32),
        grid=(depth, n // tn),
        in_specs=[
            pl.BlockSpec((r, d), lambda l, j: (0, 0)),
            pl.BlockSpec((1, d, tn), lambda l, j: (l, 0, j)),
            pl.BlockSpec((1, 1, tn), lambda l, j: (l, 0, j)),
        ],
        out_specs=pl.BlockSpec((1, r, tn), lambda l, j: (l, 0, j)),
        compiler_params=_cparams(("arbitrary", "arbitrary")),
        name="mods",
    )(cc, mod_w, mod_b.reshape(depth, 1, n))


def _mod_spec(d, row):
    if row is None:
        return pl.BlockSpec((1, 6, d), lambda b, *_: (b, 0, 0))
    return pl.BlockSpec((1, 6, d), lambda b, *_: (row, 0, 0))


def _row_tile(n, pref):
    t = min(n, pref)
    assert n % t == 0
    return t


def _rope_tables(n_tok):
    t = jnp.arange(n_tok)
    row = (t // GRID_W).astype(F32)
    col = (t % GRID_W).astype(F32)
    n = HEAD_DIM // 4
    inv = ROPE_THETA ** (-jnp.arange(n, dtype=F32) / n)
    ang = jnp.concatenate([row[:, None] * inv, col[:, None] * inv], axis=-1)
    cos, sin = jnp.cos(ang), jnp.sin(ang)
    reps = LANES // HEAD_DIM
    cos_t = jnp.tile(jnp.concatenate([cos, cos], axis=-1), (1, reps))
    sin_t = jnp.tile(jnp.concatenate([-sin, sin], axis=-1), (1, reps))
    return cos_t, sin_t


def _swap_halves(x):
    lane = lax.broadcasted_iota(jnp.int32, x.shape, 1)
    lo = (lane & (HEAD_DIM // 2)) == 0
    return jnp.where(lo, pltpu.roll(x, LANES - HEAD_DIM // 2, 1), pltpu.roll(x, HEAD_DIM // 2, 1))


def _rope_cols(u, cos_t, sin_t):
    outs = []
    for c in range(u.shape[1] // LANES):
        xc = u[:, c * LANES:(c + 1) * LANES]
        outs.append(xc * cos_t + _swap_halves(xc) * sin_t)
    return jnp.concatenate(outs, axis=1)


def _rg_proj_kernel(x_ref, mod_ref, g_ref, w_ref, go_ref, xo_ref, *, dr):
    z = _normmod(x_ref[0], g_ref[...], mod_ref[0, 0:1], mod_ref[0, 1:2]).astype(BF16)
    u = _dot(z, w_ref[...])
    go_ref[0] = _gelu_tanh(u[:, :dr])
    xo_ref[0] = u[:, dr:]


def _rg_proj_call(x, mods_l, mod_row, g, w):
    b, n, d = x.shape
    dr = w.shape[1] // 2
    tm = _row_tile(n, 512)
    return pl.pallas_call(
        functools.partial(_rg_proj_kernel, dr=dr),
        out_shape=(jax.ShapeDtypeStruct((b, n, dr), F32), jax.ShapeDtypeStruct((b, n, dr), F32)),
        grid=(b, n // tm),
        in_specs=[
            pl.BlockSpec((1, tm, d), lambda bi, i: (bi, i, 0)),
            _mod_spec(d, mod_row),
            pl.BlockSpec((1, d), lambda bi, i: (0, 0)),
            pl.BlockSpec((d, 2 * dr), lambda bi, i: (0, 0)),
        ],
        out_specs=(pl.BlockSpec((1, tm, dr), lambda bi, i: (bi, i, 0)),
                   pl.BlockSpec((1, tm, dr), lambda bi, i: (bi, i, 0))),
        compiler_params=_cparams(("arbitrary", "arbitrary")),
        name="rg_proj",
    )(x, mods_l, g, w)


def _qkv_proj_kernel(x_ref, mod_ref, g_ref, w_ref, *rest, d, rope, qknorm):
    rest = list(rest)
    if rope:
        cos_ref, sin_ref = rest.pop(0), rest.pop(0)
    if qknorm:
        gain_ref, ind_ref, indt_ref = rest.pop(0), rest.pop(0), rest.pop(0)
    q_ref, k_ref, v_ref = rest
    z = _normmod(x_ref[0], g_ref[...], mod_ref[0, 0:1], mod_ref[0, 1:2]).astype(BF16)
    u = _dot(z, w_ref[...])
    nk = k_ref.shape[-1]
    qk = u[:, :d + nk]
    if qknorm:
        x2 = qk * qk
        hi = x2.astype(BF16)
        lo = (x2 - hi.astype(F32)).astype(BF16)
        ssq = _dot(hi, ind_ref[...]) + _dot(lo, ind_ref[...])
        r = lax.rsqrt(ssq * (1.0 / HEAD_DIM) + EPS)
        rhi = r.astype(BF16)
        rlo = (r - rhi.astype(F32)).astype(BF16)
        rb = _dot(rhi, indt_ref[...]) + _dot(rlo, indt_ref[...])
        qk = (qk * rb) * gain_ref[...]
    if rope:
        qk = _rope_cols(qk, cos_ref[...], sin_ref[...])
    q_ref[0] = (qk[:, :d] * Q_SCALE).astype(q_ref.dtype)
    k_ref[0] = qk[:, d:].astype(k_ref.dtype)
    v_ref[0] = u[:, d + nk:].astype(v_ref.dtype)


def _qkv_proj_call(x, mods_l, mod_row, g, w, nk, *, rope=None, qknorm=None):
    b, n, d = x.shape
    nv = w.shape[1] - d - nk
    tm = _row_tile(n, 512)
    const = lambda bi, i: (0, 0)
    in_specs = [
        pl.BlockSpec((1, tm, d), lambda bi, i: (bi, i, 0)),
        _mod_spec(d, mod_row),
        pl.BlockSpec((1, d), const),
        pl.BlockSpec(w.shape, const),
    ]
    args = [x, mods_l, g, w]
    if rope is not None:
        in_specs += [pl.BlockSpec((tm, LANES), lambda bi, i: (i, 0))] * 2
        args += list(rope)
    if qknorm is not None:
        in_specs += [pl.BlockSpec(a.shape, const) for a in qknorm]
        args += list(qknorm)
    out_spec = lambda w_: pl.BlockSpec((1, tm, w_), lambda bi, i: (bi, i, 0))
    return pl.pallas_call(
        functools.partial(_qkv_proj_kernel, d=d, rope=rope is not None, qknorm=qknorm is not None),
        out_shape=(jax.ShapeDtypeStruct((b, n, d), BF16), jax.ShapeDtypeStruct((b, n, nk), BF16),
                   jax.ShapeDtypeStruct((b, n, nv), BF16)),
        grid=(b, n // tm),
        in_specs=in_specs,
        out_specs=(out_spec(d), out_spec(nk), out_spec(nv)),
        compiler_params=_cparams(("arbitrary", "arbitrary")),
        name="qkv_proj",
    )(*args)


def _rg_conv(xs_scr, x, prev, nxt, cw):
    t = x.shape[0]
    h = SUBLANES
    xs_scr[0:h] = prev
    xs_scr[h:h + t] = x
    xs_scr[h + t:2 * h + t] = nxt
    y = cw[4:5] + cw[0:1] * xs_scr[h - 2:h - 2 + t]
    y = y + cw[1:2] * xs_scr[h - 1:h - 1 + t]
    y = y + cw[2:3] * x
    y = y + cw[3:4] * xs_scr[h + 1:h + 1 + t]
    return y


def _sigmoid(x):
    return 0.5 * jnp.tanh(0.5 * x) + 0.5


def _rg_gates(xc, w, ba, bx, lam):
    gw = xc.shape[1]
    y = _dot(xc.astype(BF16), w)
    r = _sigmoid(y[:, :gw] + ba)
    i = _sigmoid(y[:, gw:] + bx)
    nl = -lam
    softplus = jnp.maximum(nl, 0.0) + jnp.log1p(jnp.exp(-jnp.abs(nl)))
    log_a = r * (-RG_C * softplus)
    a = jnp.exp(log_a)
    one_minus_a2 = -jnp.tanh(log_a) * (a * a + 1.0)
    root = jnp.where(one_minus_a2 > 0.0, one_minus_a2 * lax.rsqrt(one_minus_a2), 0.0)
    return a, root * (i * xc)


def _rg_scan_tile(a, b, carry, reverse):
    t, gw = a.shape
    nck = t // SUBLANES
    a = a.reshape(nck, SUBLANES, gw)
    b = b.reshape(nck, SUBLANES, gw)
    rmod = lax.broadcasted_iota(jnp.int32, (1, SUBLANES, 1), 1)
    for s in (1, 2, 4):
        ok = (rmod < SUBLANES - s) if reverse else (rmod >= s)
        shift = SUBLANES - s if reverse else s
        ash, bsh = pltpu.roll(a, shift, 1), pltpu.roll(b, shift, 1)
        b = a * jnp.where(ok, bsh, 0.0) + b
        a = a * jnp.where(ok, ash, 1.0)
    hs = [None] * nck
    for ck in (range(nck - 1, -1, -1) if reverse else range(nck)):
        hc = b[ck] + a[ck] * carry
        carry = hc[0:1] if reverse else hc[SUBLANES - 1:SUBLANES]
        hs[ck] = hc
    return jnp.concatenate(hs, axis=0), carry


def _rg_scan_kernel(xc_ref, gc_ref, xl_ref, xp_ref, xn_ref, gl_ref, cw_ref, w_ref, gp_ref,
                    mc_ref, ml_ref, hf_scr, xv_scr, car_scr, xs_scr, *, nt):
    s = pl.program_id(2)
    gw = xl_ref.shape[-1]
    tl = xl_ref.shape[1]
    cw = cw_ref[0]
    gp = gp_ref[0]

    def gates(xconv, d):
        w = w_ref[0, :, d * 2 * gw:(d + 1) * 2 * gw]
        return _rg_gates(xconv, w, gp[3 * d:3 * d + 1], gp[3 * d + 1:3 * d + 2], gp[3 * d + 2:3 * d + 3])

    @pl.when(s == 0)
    def _ctx():
        zeros8 = jnp.zeros((SUBLANES, gw), F32)
        xconv = _rg_conv(xs_scr, xc_ref[0], zeros8, zeros8, cw)
        zero = jnp.zeros((1, gw), F32)
        a0, b0 = gates(xconv, 0)
        hf, cf = _rg_scan_tile(a0, b0, zero, False)
        a1, b1 = gates(xconv, 1)
        hb, cb = _rg_scan_tile(a1, b1, zero, True)
        car_scr[0:1] = cf
        car_scr[1:2] = cb
        mc_ref[0] = (gc_ref[0] * (hf + hb)).astype(mc_ref.dtype)

    def lat_conv(t):
        prev = xp_ref[0] * (t > 0).astype(F32)
        nxt = xn_ref[0] * (t < nt - 1).astype(F32)
        return _rg_conv(xs_scr, xl_ref[0], prev, nxt, cw)

    @pl.when((s >= 1) & (s <= nt))
    def _fwd():
        t = s - 1
        xconv = lat_conv(t)
        xv_scr[pl.ds(pl.multiple_of(t * tl, tl), tl), :] = xconv
        a0, b0 = gates(xconv, 0)
        h, c = _rg_scan_tile(a0, b0, car_scr[0:1], False)
        car_scr[0:1] = c
        hf_scr[pl.ds(pl.multiple_of(t * tl, tl), tl), :] = h

    @pl.when(s > nt)
    def _bwd():
        t = 2 * nt - s
        a1, b1 = gates(xv_scr[pl.ds(pl.multiple_of(t * tl, tl), tl), :], 1)
        h, c = _rg_scan_tile(a1, b1, car_scr[1:2], True)
        car_scr[1:2] = c
        hf = hf_scr[pl.ds(pl.multiple_of(t * tl, tl), tl), :]
        ml_ref[0] = (gl_ref[0] * (hf + h)).astype(ml_ref.dtype)


def _rg_scan_call(xr_c, g_c, xr_l, g_l, cw, wbd, gp):
    b, s_len, dr = xr_l.shape
    c_len = xr_c.shape[1]
    ng, gw = wbd.shape[0], wbd.shape[1]
    tl = _row_tile(s_len, 512)
    nt = s_len // tl
    nb8 = tl // SUBLANES

    def tile(s):
        return jnp.clip(s - 1, 0, nt - 1)

    def tile_bwd(s):
        return jnp.where(s <= nt, nt - 1, 2 * nt - s)

    return pl.pallas_call(
        functools.partial(_rg_scan_kernel, nt=nt),
        out_shape=(jax.ShapeDtypeStruct((b, c_len, dr), BF16), jax.ShapeDtypeStruct((b, s_len, dr), BF16)),
        grid=(b, ng, 2 * nt + 1),
        in_specs=[
            pl.BlockSpec((1, c_len, gw), lambda bi, h, s: (bi, 0, h)),
            pl.BlockSpec((1, c_len, gw), lambda bi, h, s: (bi, 0, h)),
            pl.BlockSpec((1, tl, gw), lambda bi, h, s: (bi, tile(s), h)),
            pl.BlockSpec((1, SUBLANES, gw), lambda bi, h, s: (bi, jnp.maximum(tile(s) * nb8 - 1, 0), h)),
            pl.BlockSpec((1, SUBLANES, gw),
                         lambda bi, h, s: (bi, jnp.minimum((tile(s) + 1) * nb8, s_len // SUBLANES - 1), h)),
            pl.BlockSpec((1, tl, gw), lambda bi, h, s: (bi, tile_bwd(s), h)),
            pl.BlockSpec((1, SUBLANES, gw), lambda bi, h, s: (h, 0, 0)),
            pl.BlockSpec((1, gw, 4 * gw), lambda bi, h, s: (h, 0, 0)),
            pl.BlockSpec((1, SUBLANES, gw), lambda bi, h, s: (h, 0, 0)),
        ],
        out_specs=(pl.BlockSpec((1, c_len, gw), lambda bi, h, s: (bi, 0, h)),
                   pl.BlockSpec((1, tl, gw), lambda bi, h, s: (bi, tile_bwd(s), h))),
        scratch_shapes=[pltpu.VMEM((s_len, gw), F32), pltpu.VMEM((s_len, gw), F32), pltpu.VMEM((SUBLANES, gw), F32),
                        pltpu.VMEM((max(tl, c_len) + 2 * SUBLANES, gw), F32)],
        compiler_params=_cparams(("arbitrary", "arbitrary", "arbitrary")),
        name="rg_scan",
    )(xr_c, g_c, xr_l, xr_l, xr_l, g_l, cw, wbd, gp)


def _rg_pack_params(conv_w, conv_b, wa, ba, wx, bx, lam):
    nblk, bw = wa.shape[1], wa.shape[2]
    dr = nblk * bw
    per = next(p for p in range(1, nblk + 1) if nblk % p == 0 and (p * bw) % LANES == 0)
    ng, gw = nblk // per, per * bw

    def dense(w):
        w = w.reshape(ng, per, bw, bw)
        eye = jnp.eye(per, dtype=w.dtype)
        return jnp.einsum('gpjk,pq->gpjqk', w, eye).reshape(ng, gw, gw)

    wbd = jnp.concatenate([dense(wa[0]), dense(wx[0]), dense(wa[1]), dense(wx[1])], axis=-1).astype(BF16)
    cw = jnp.concatenate([conv_w, conv_b[None], jnp.zeros((SUBLANES - 1 - conv_w.shape[0], dr), F32)], axis=0)
    cw = cw.reshape(SUBLANES, ng, gw).transpose(1, 0, 2)
    gp = jnp.stack([ba[0], bx[0], lam[0], ba[1], bx[1], lam[1], jnp.zeros_like(lam[0]), jnp.zeros_like(lam[0])])
    gp = gp.reshape(SUBLANES, ng, gw).transpose(1, 0, 2)
    return cw, wbd, gp


def _pair_masks(shape):
    lane = lax.broadcasted_iota(jnp.int32, shape, 1)
    return lane < HEAD_DIM


def _half_q(q, hh):
    first = _pair_masks(q.shape)
    return jnp.where(first if hh == 0 else jnp.logical_not(first), q, jnp.zeros_like(q))


def _scores_to_scratch(qh, k_refs, s_scr, slot):
    off = 0
    for k_ref in k_refs:
        n = k_ref.shape[1]
        s_scr[slot, :, off:off + n] = _dot_nt(qh, k_ref[0])
        off += n


def _exp_rows(s):
    e = jnp.exp2(s - s.max(axis=-1, keepdims=True))
    return e, e.sum(axis=-1, keepdims=True)


def _softmax_to_scratch(s_scr, p_scr, slot, rows):
    linv = []
    for r in range(0, rows, ATT_RB):
        e, l = _exp_rows(s_scr[slot, r:r + ATT_RB, :])
        p_scr[slot, r:r + ATT_RB, :] = e.astype(BF16)
        linv.append(1.0 / l)
    return jnp.concatenate(linv, axis=0)


def _pv_from_scratch(p_scr, slot, v_refs):
    off, o = 0, None
    for v_ref in v_refs:
        n = v_ref.shape[1]
        t = _dot(p_scr[slot, :, off:off + n], v_ref[0])
        o = t if o is None else o + t
        off += n
    return o


def _plain_attn_kernel(q_ref, *refs, nsrc):
    k_refs, v_refs = refs[0:2 * nsrc:2], refs[1:2 * nsrc:2]
    o_ref, s_scr, p_scr = refs[2 * nsrc:]
    tq, qw = q_ref.shape[1], q_ref.shape[2]
    sub = s_scr.shape[1]
    units = [(r, c, hh) for r in range(0, tq, sub) for c in range(0, qw, LANES) for hh in range(2)]

    def scores(u):
        r, c, hh = units[u]
        _scores_to_scratch(_half_q(q_ref[0, r:r + sub, c:c + LANES], hh), k_refs, s_scr, u % 2)

    scores(0)
    outs = [None, None]
    for u, (r, c, hh) in enumerate(units):
        if u + 1 < len(units):
            scores(u + 1)
        linv = _softmax_to_scratch(s_scr, p_scr, u % 2, sub)
        outs[hh] = _pv_from_scratch(p_scr, u % 2, v_refs) * linv
        if hh == 1:
            o_ref[0, r:r + sub, c:c + LANES] = jnp.where(_pair_masks((sub, LANES)), outs[0], outs[1]).astype(o_ref.dtype)


def _reduce_keys(x, op):
    nk, nq = x.shape
    slab = next(t for t in (KEY_SLAB, SUBLANES) if nk % t == 0)
    return op(op(x.reshape(nk // slab, slab, nq), axis=0), axis=0, keepdims=True)


def _plain_attn_t_kernel(q_ref, *refs, nsrc):
    k_refs, v_refs = refs[0:2 * nsrc:2], refs[1:2 * nsrc:2]
    o_ref = refs[2 * nsrc]
    vt_scrs = refs[2 * nsrc + 1:3 * nsrc + 1]
    s_scr, p_scr = refs[3 * nsrc + 1:]
    tq, qw = q_ref.shape[1], q_ref.shape[2]
    sub = s_scr.shape[2]

    @pl.when(pl.program_id(2) == 0)
    def _transpose_values():
        for v_ref, vt_scr in zip(v_refs, vt_scrs):
            vt_scr[...] = v_ref[0].astype(F32).T.astype(BF16)

    units = [(r, c, hh) for r in range(0, tq, sub) for c in range(0, qw, LANES) for hh in range(2)]

    def scores(u):
        r, c, hh = units[u]
        qh = _half_q(q_ref[0, r:r + sub, c:c + LANES], hh)
        off = 0
        for k_ref in k_refs:
            n = k_ref.shape[1]
            s_scr[u % ns, off:off + n, :] = _dot_nt(k_ref[0], qh)
            off += n

    ns = s_scr.shape[0]
    for u in range(min(ns - 1, len(units))):
        scores(u)
    outs = [None, None]
    for u, (r, c, hh) in enumerate(units):
        if u + ns - 1 < len(units):
            scores(u + ns - 1)
        slabs = [slice(i, i + KEY_SLAB) for i in range(0, s_scr.shape[1], KEY_SLAB)]
        macc = s_scr[u % ns, slabs[0], :]
        for sl in slabs[1:]:
            macc = jnp.maximum(macc, s_scr[u % ns, sl, :])
        m = jnp.max(macc, axis=0, keepdims=True)
        lacc = None
        for sl in slabs:
            e = jnp.exp2(s_scr[u % ns, sl, :] - m)
            p_scr[u % 2, sl, :] = e.astype(BF16)
            lacc = e if lacc is None else lacc + e
        linv = 1.0 / jnp.sum(lacc, axis=0, keepdims=True)
        off, ot = 0, None
        for vt_scr in vt_scrs:
            n = vt_scr.shape[1]
            t = _dot(vt_scr[hh * HEAD_DIM:(hh + 1) * HEAD_DIM, :], p_scr[u % 2, off:off + n, :])
            ot = t if ot is None else ot + t
            off += n
        outs[hh] = ot * linv
        if hh == 1:
            o_ref[0, r:r + sub, c:c + LANES] = jnp.concatenate(outs, axis=0).T.astype(o_ref.dtype)


def _plain_attn_t_call(q, kvs, qw):
    b, lq, d = q.shape
    tq = _row_tile(lq, 4 * ATT_SUB)
    sub = min(tq, ATT_SUB)
    nk = sum(k.shape[1] for k, _ in kvs)
    in_specs = [pl.BlockSpec((1, tq, qw), lambda bi, j, i: (bi, i, j))]
    args = [q]
    for k, v in kvs:
        spec = pl.BlockSpec((1, k.shape[1], LANES), lambda bi, j, i: (bi, 0, j))
        in_specs += [spec, spec]
        args += [k, v]
    return pl.pallas_call(
        functools.partial(_plain_attn_t_kernel, nsrc=len(kvs)),
        out_shape=jax.ShapeDtypeStruct((b, lq, d), BF16),
        grid=(b, d // qw, lq // tq),
        in_specs=in_specs,
        out_specs=pl.BlockSpec((1, tq, qw), lambda bi, j, i: (bi, i, j)),
        scratch_shapes=([pltpu.VMEM((LANES, k.shape[1]), BF16) for k, _ in kvs]
                        + [pltpu.VMEM((3, nk, sub), F32), pltpu.VMEM((2, nk, sub), BF16)]),
        compiler_params=_cparams(("arbitrary", "arbitrary", "arbitrary")),
        name="plain_attn_t",
    )(*args)


def _plain_attn_call(q, kvs, qw):
    b, lq, d = q.shape
    tq = _row_tile(lq, 2 * ATT_SUB)
    sub = min(tq, ATT_SUB)
    nk = sum(k.shape[1] for k, _ in kvs)
    in_specs = [pl.BlockSpec((1, tq, qw), lambda bi, j, i: (bi, i, j))]
    args = [q]
    for k, v in kvs:
        spec = pl.BlockSpec((1, k.shape[1], LANES), lambda bi, j, i: (bi, 0, j))
        in_specs += [spec, spec]
        args += [k, v]
    return pl.pallas_call(
        functools.partial(_plain_attn_kernel, nsrc=len(kvs)),
        out_shape=jax.ShapeDtypeStruct((b, lq, d), BF16),
        grid=(b, d // qw, lq // tq),
        in_specs=in_specs,
        out_specs=pl.BlockSpec((1, tq, qw), lambda bi, j, i: (bi, i, j)),
        scratch_shapes=[pltpu.VMEM((2, sub, nk), F32), pltpu.VMEM((2, sub, nk), BF16)],
        compiler_params=_cparams(("arbitrary", "arbitrary", "arbitrary")),
        name="plain_attn",
    )(*args)


def _diff_attn_kernel(q_ref, lam_ref, g_ref, *refs, nsrc, lambda_init):
    k_refs, v_refs = refs[0:2 * nsrc:2], refs[1:2 * nsrc:2]
    o_ref, s_scr, a_scr = refs[2 * nsrc:]
    lp = lam_ref[...]
    lam = (jnp.exp(jnp.sum(lp[0:1] * lp[1:2], axis=-1, keepdims=True))
           - jnp.exp(jnp.sum(lp[2:3] * lp[3:4], axis=-1, keepdims=True)) + lambda_init)
    tq = q_ref.shape[1]
    sub = s_scr.shape[1]
    units = list(range(0, tq, sub))

    def scores(u):
        q = q_ref[0, units[u]:units[u] + sub]
        for hh in range(2):
            _scores_to_scratch(_half_q(q, hh), k_refs, s_scr, 2 * (u % 2) + hh)

    def combine(u):
        linv = []
        for r in range(0, sub, DIFF_RB):
            e0, l0 = _exp_rows(s_scr[2 * (u % 2), r:r + DIFF_RB, :])
            e1, l1 = _exp_rows(s_scr[2 * (u % 2) + 1, r:r + DIFF_RB, :])
            a_scr[u % 2, r:r + DIFF_RB, :] = (e0 - e1 * (lam * l0 / l1)).astype(BF16)
            linv.append(1.0 / l0)
        return jnp.concatenate(linv, axis=0)

    scores(0)
    for u, r in enumerate(units):
        if u + 1 < len(units):
            scores(u + 1)
        linv = combine(u)
        o = _pv_from_scratch(a_scr, u % 2, v_refs) * linv
        o_ref[0, r:r + sub] = (_rms(o, g_ref[...]) * (1.0 - lambda_init)).astype(o_ref.dtype)


def _diff_attn_call(q, kvs, lam_params, subln_g, lambda_init):
    b, lq, d = q.shape
    tq = _row_tile(lq, 4 * ATT_SUB)
    sub = min(tq, ATT_SUB)
    nk = sum(k.shape[1] for k, _ in kvs)
    const = lambda bi, j, i: (0, 0)
    in_specs = [pl.BlockSpec((1, tq, LANES), lambda bi, j, i: (bi, i, j)),
                pl.BlockSpec(lam_params.shape, const), pl.BlockSpec(subln_g.shape, const)]
    args = [q, lam_params, subln_g]
    for k, v in kvs:
        spec = pl.BlockSpec((1, k.shape[1], LANES), lambda bi, j, i: (bi, 0, j))
        in_specs += [spec, spec]
        args += [k, v]
    return pl.pallas_call(
        functools.partial(_diff_attn_kernel, nsrc=len(kvs), lambda_init=lambda_init),
        out_shape=jax.ShapeDtypeStruct((b, lq, d), BF16),
        grid=(b, d // LANES, lq // tq),
        in_specs=in_specs,
        out_specs=pl.BlockSpec((1, tq, LANES), lambda bi, j, i: (bi, i, j)),
        scratch_shapes=[pltpu.VMEM((4, sub, nk), F32), pltpu.VMEM((2, sub, nk), BF16)],
        compiler_params=_cparams(("arbitrary", "arbitrary", "arbitrary")),
        name="diff_attn",
    )(*args)


def _na_attn_kernel(q_ref, k_ref, v_ref, kc_ref, vc_ref, rpb_ref, o_ref, t2_scr, s_scr, p_scr, *, rows, kr, kcw, nq):
    w = GRID_W
    nwin = nq + kr
    ntab = 2 * kr
    nkw = nwin * w
    sub = nq * w
    nsub = q_ref.shape[1] // sub
    i = pl.program_id(2)

    @pl.when((i == 0) & (pl.program_id(1) == 0))
    def _build_bias():
        c_io = lax.broadcasted_iota(jnp.int32, (w, LANES), 0)
        cp_io = lax.broadcasted_iota(jnp.int32, (w, LANES), 1)
        cs = jnp.clip(c_io - kcw // 2, 0, w - kcw)
        colok = (cp_io >= cs) & (cp_io < cs + kcw)
        neg = jnp.full((w, LANES), NEG, F32)
        for hh in range(2):
            r = pltpu.roll(rpb_ref[hh] * LOG2E, LANES - (kcw - 1), 1)
            ts = []
            for e in range(ntab - 1):
                t = jnp.broadcast_to(r[e:e + 1], (w, LANES))
                t = pltpu.roll(t, 0, 1, stride=1, stride_axis=0)
                ts.append(jnp.where(colok, t, NEG))
            for e in range(ntab):
                lo = ts[e - 1] if e >= 1 else neg
                hi = ts[e] if e < ntab - 1 else neg
                t2_scr[hh, e] = jnp.where(cp_io < w, lo, pltpu.roll(hi, w, 1))

    units = [(t, hh) for t in range(nsub) for hh in range(2)]

    def window(t):
        r0 = (i * nsub + t) * nq
        ks = jnp.clip(r0 - kr // 2, 0, rows - nwin)
        return r0, ks, pl.ds(pl.multiple_of(ks * w, w), nkw)

    def scores(u):
        t, hh = units[u]
        _, _, win = window(t)
        qh = _half_q(q_ref[0, t * sub:(t + 1) * sub], hh)
        s_scr[u % 2, :, :nkw] = _dot_nt(qh, k_ref[0, win, :])
        s_scr[u % 2, :, nkw:] = _dot_nt(qh, kc_ref[0])

    def softmax(u):
        t, hh = units[u]
        r0, ks, _ = window(t)
        key_row = ks + jnp.right_shift(lax.broadcasted_iota(jnp.int32, (1, nkw), 1), int(math.log2(w)))
        linv = []
        for qi in range(nq):
            r = r0 + qi
            rs = jnp.clip(r - kr // 2, 0, rows - kr)
            rowok = (key_row >= rs) & (key_row < rs + kr)
            idx = [jnp.clip(ks + 2 * p - r + kr, 0, ntab - 1) for p in range(nwin // 2)]
            for rb in range(0, w, ATT_RB):
                row = qi * w + rb
                bias = jnp.concatenate([t2_scr[hh, e, rb:rb + ATT_RB, :] for e in idx], axis=1)
                sw = jnp.where(rowok, s_scr[u % 2, row:row + ATT_RB, :nkw] + bias, NEG)
                sc = s_scr[u % 2, row:row + ATT_RB, nkw:]
                m = jnp.maximum(sw.max(axis=-1, keepdims=True), sc.max(axis=-1, keepdims=True))
                ew, ec = jnp.exp2(sw - m), jnp.exp2(sc - m)
                p_scr[u % 2, row:row + ATT_RB, :nkw] = ew.astype(BF16)
                p_scr[u % 2, row:row + ATT_RB, nkw:] = ec.astype(BF16)
                linv.append(1.0 / (ew.sum(axis=-1, keepdims=True) + ec.sum(axis=-1, keepdims=True)))
        return jnp.concatenate(linv, axis=0)

    scores(0)
    outs = [None, None]
    for u, (t, hh) in enumerate(units):
        if u + 1 < len(units):
            scores(u + 1)
        linv = softmax(u)
        _, _, win = window(t)
        o = _dot(p_scr[u % 2, :, :nkw], v_ref[0, win, :]) + _dot(p_scr[u % 2, :, nkw:], vc_ref[0])
        outs[hh] = o * linv
        if hh == 1:
            o_ref[0, t * sub:(t + 1) * sub] = jnp.where(_pair_masks((sub, LANES)), outs[0], outs[1]).astype(o_ref.dtype)


def _na_attn_call(q, k, v, kc, vc, rpb):
    b, s_len, d = q.shape
    c_len = kc.shape[1]
    nh, ndr, ndc = rpb.shape
    kr, kcw = (ndr + 1) // 2, (ndc + 1) // 2
    rows = s_len // GRID_W
    nq = 4
    assert GRID_W == HEAD_DIM and rows % nq == 0 and rows >= nq + kr and (nq + kr) % 2 == 0
    assert ndc <= LANES and 2 * kr <= 2 * SUBLANES
    rpb_p = jnp.full((nh, 2 * SUBLANES, LANES), NEG, F32).at[:, :ndr, :ndc].set(rpb)
    sub = nq * GRID_W
    tq = _row_tile(s_len, 4 * sub)
    nk = (nq + kr) * GRID_W + c_len
    full = lambda n: pl.BlockSpec((1, n, LANES), lambda j, bi, i: (bi, 0, j))
    return pl.pallas_call(
        functools.partial(_na_attn_kernel, rows=rows, kr=kr, kcw=kcw, nq=nq),
        out_shape=jax.ShapeDtypeStruct((b, s_len, d), BF16),
        grid=(d // LANES, b, s_len // tq),
        in_specs=[pl.BlockSpec((1, tq, LANES), lambda j, bi, i: (bi, i, j)),
                  full(s_len), full(s_len), full(c_len), full(c_len),
                  pl.BlockSpec((2, 2 * SUBLANES, LANES), lambda j, bi, i: (j, 0, 0))],
        out_specs=pl.BlockSpec((1, tq, LANES), lambda j, bi, i: (bi, i, j)),
        scratch_shapes=[pltpu.VMEM((2, 2 * kr, GRID_W, LANES), F32), pltpu.VMEM((2, sub, nk), F32),
                        pltpu.VMEM((2, sub, nk), BF16)],
        compiler_params=_cparams(("arbitrary", "arbitrary", "arbitrary")),
        name="na_attn",
    )(q, k, v, kc, vc, rpb_p)


def _outproj_kernel(a_ref, x_ref, mod_ref, g_ref, w_ref, o_ref, z_ref):
    x = x_ref[0] + mod_ref[0, 2:3] * _dot(a_ref[0], w_ref[...])
    o_ref[0] = x
    z_ref[0] = _normmod(x, g_ref[...], mod_ref[0, 3:4], mod_ref[0, 4:5]).astype(z_ref.dtype)


def _outproj_call(a, x, mods_l, mod_row, g, w):
    b, n, d = x.shape
    k = a.shape[-1]
    tm = _row_tile(n, 512)
    row = lambda w_: pl.BlockSpec((1, tm, w_), lambda bi, i: (bi, i, 0))
    return pl.pallas_call(
        _outproj_kernel,
        out_shape=(jax.ShapeDtypeStruct((b, n, d), F32), jax.ShapeDtypeStruct((b, n, d), BF16)),
        grid=(b, n // tm),
        in_specs=[row(k), row(d), _mod_spec(d, mod_row),
                  pl.BlockSpec((1, d), lambda bi, i: (0, 0)),
                  pl.BlockSpec((k, d), lambda bi, i: (0, 0))],
        out_specs=(row(d), row(d)),
        compiler_params=_cparams(("arbitrary", "arbitrary")),
        name="outproj",
    )(a, x, mods_l, g, w)


def _ffn_kernel(x_ref, z_ref, zp_ref, zn_ref, mod_ref, wu_ref, cw_ref, wd_ref, *rest, fc, final):
    if final:
        fg_ref, o_ref, z_scr, u_scr, h_scr = rest
    else:
        o_ref, z_scr, u_scr, h_scr = rest
    i = pl.program_id(1)
    nt = pl.num_programs(1)
    tm, d = x_ref.shape[1], x_ref.shape[2]
    hb = FFN_HALO
    nch = wu_ref.shape[0]
    z_scr[hb:hb + tm] = z_ref[0]
    z_scr[0:hb] = jnp.where(i > 0, zp_ref[0], jnp.zeros_like(zp_ref[0]))
    z_scr[hb + tm:] = jnp.where(i < nt - 1, zn_ref[0], jnp.zeros_like(zn_ref[0]))

    def up_proj(j):
        u_scr[j % 2] = _dot(z_scr[...], wu_ref[j])

    def conv_act(j):
        cw = cw_ref[j]
        sub = min(FFN_SUB, tm)
        for sb in range(tm // sub):
            r0 = hb + sb * sub
            cv = cw[3:4] + cw[0:1] * u_scr[j % 2, r0 - 1:r0 - 1 + sub]
            cv = cv + cw[1:2] * u_scr[j % 2, r0:r0 + sub]
            cv = cv + cw[2:3] * u_scr[j % 2, r0 + 1:r0 + 1 + sub]
            hg = 0.5 * cv[:, :fc]
            h = (hg + hg * jnp.tanh(hg)) * cv[:, fc:]
            h_scr[sb * sub:(sb + 1) * sub, j * fc:(j + 1) * fc] = h.astype(BF16)

    up_proj(0)
    for j in range(nch - 1):
        up_proj(j + 1)
        conv_act(j)
    split = (nch - 1) * fc
    acc = _dot(h_scr[:, :split], wd_ref[:split, :])
    conv_act(nch - 1)
    acc = acc + _dot(h_scr[:, split:], wd_ref[split:, :])
    out = x_ref[0] + mod_ref[0, 5:6] * acc
    if final:
        out = _rms(out, fg_ref[...])
    o_ref[0] = out


def _ffn_pack_params(w_up, conv_w, conv_b, w_down, fc):
    d, f2 = w_up.shape
    f = f2 // 2
    nch = f // fc
    wu = jnp.concatenate([w_up[:, :f].reshape(d, nch, fc), w_up[:, f:].reshape(d, nch, fc)], axis=-1)
    wu = wu.transpose(1, 0, 2).astype(BF16)
    cw = jnp.concatenate([conv_w, conv_b[None], jnp.zeros((SUBLANES - 1 - conv_w.shape[0], f2), F32)], axis=0)
    cw = jnp.concatenate([cw[:, :f].reshape(SUBLANES, nch, fc), cw[:, f:].reshape(SUBLANES, nch, fc)], axis=-1)
    cw = cw.transpose(1, 0, 2)
    return wu, cw, w_down.astype(BF16)


def _ffn_call(x, z, mods_l, mod_row, packed, final_g=None):
    b, n, d = x.shape
    wu, cw, wd = packed
    nch, _, fc2 = wu.shape
    tm = _row_tile(n, FFN_TM)
    hb = FFN_HALO
    nbh = tm // hb
    const3 = lambda bi, i: (0, 0, 0)
    once = dict(pipeline_mode=pl.Buffered(1))
    in_specs = [
        pl.BlockSpec((1, tm, d), lambda bi, i: (bi, i, 0)),
        pl.BlockSpec((1, tm, d), lambda bi, i: (bi, i, 0)),
        pl.BlockSpec((1, hb, d), lambda bi, i: (bi, jnp.maximum(i * nbh - 1, 0), 0)),
        pl.BlockSpec((1, hb, d), lambda bi, i: (bi, jnp.minimum((i + 1) * nbh, n // hb - 1), 0)),
        _mod_spec(d, mod_row),
        pl.BlockSpec(wu.shape, const3, **once),
        pl.BlockSpec(cw.shape, const3),
        pl.BlockSpec(wd.shape, lambda bi, i: (0, 0), **once),
    ]
    args = [x, z, z, z, mods_l, wu, cw, wd]
    if final_g is not None:
        in_specs.append(pl.BlockSpec((1, d), lambda bi, i: (0, 0)))
        args.append(final_g)
    return pl.pallas_call(
        functools.partial(_ffn_kernel, fc=fc2 // 2, final=final_g is not None),
        out_shape=jax.ShapeDtypeStruct((b, n, d), F32),
        grid=(b, n // tm),
        in_specs=in_specs,
        out_specs=pl.BlockSpec((1, tm, d), lambda bi, i: (bi, i, 0)),
        scratch_shapes=[pltpu.VMEM((tm + 2 * hb, d), BF16), pltpu.VMEM((2, tm + 2 * hb, fc2), F32),
                        pltpu.VMEM((tm, nch * fc2 // 2), BF16)],
        compiler_params=_cparams(("arbitrary", "arbitrary")),
        name="ffn",
    )(*args)


def _rg_mixer(xl, xc, mods_l, ctx_row, g, w_in, conv_w, conv_b, wa, ba, wx, bx, lam, need_ctx):
    w = w_in.astype(BF16)
    g_l, xr_l = _rg_proj_call(xl, mods_l, None, g, w)
    g_c, xr_c = _rg_proj_call(xc, mods_l, ctx_row, g, w)
    cw, wbd, gp = _rg_pack_params(conv_w, conv_b, wa, ba, wx, bx, lam)
    m_c, m_l = _rg_scan_call(xr_c, g_c, xr_l, g_l, cw, wbd, gp)
    return m_l, (m_c if need_ctx else None)


def _na_mixer(xl, xc, mods_l, ctx_row, g, w_in, rpb, need_ctx):
    d = xl.shape[-1]
    w = w_in.astype(BF16)
    ql, kl, vl = _qkv_proj_call(xl, mods_l, None, g, w, d)
    qc, kc, vc = _qkv_proj_call(xc, mods_l, ctx_row, g, w, d)
    o_l = _na_attn_call(ql, kl, vl, kc, vc, rpb)
    o_c = _plain_attn_call(qc, [(kc, vc)], LANES) if need_ctx else None
    return o_l, o_c


def _gqa_mixer(xl, xc, mods_l, ctx_row, g, w_in, q_norm, k_norm, need_ctx):
    d = xl.shape[-1]
    nh = d // HEAD_DIM
    nkv = (w_in.shape[1] // HEAD_DIM - nh) // 2
    group = nh // nkv
    assert group % 2 == 0 and q_norm.shape[0] == HEAD_DIM
    dup = lambda wk: jnp.concatenate([wk.reshape(d, nkv, 1, HEAD_DIM)] * 2, axis=2).reshape(d, 2 * nkv * HEAD_DIM)
    nk = 2 * nkv * HEAD_DIM
    w = jnp.concatenate([w_in[:, :d], dup(w_in[:, d:d + nkv * HEAD_DIM]), dup(w_in[:, d + nkv * HEAD_DIM:])],
                        axis=1).astype(BF16)
    nseg = (d + nk) // HEAD_DIM
    assert nseg <= LANES
    seg = jnp.arange(d + nk) // HEAD_DIM
    ind = (seg[:, None] == jnp.arange(LANES)[None, :]).astype(BF16)
    gain = jnp.concatenate([jnp.tile(q_norm, nh), jnp.tile(k_norm, 2 * nkv)])[None]
    qkn = (gain, ind, ind.T)
    ql, kl, vl = _qkv_proj_call(xl, mods_l, None, g, w, nk, rope=_rope_tables(xl.shape[1]), qknorm=qkn)
    qc, kc, vc = _qkv_proj_call(xc, mods_l, ctx_row, g, w, nk, qknorm=qkn)
    qw = group * HEAD_DIM
    o_l = _plain_attn_t_call(ql, [(kl, vl), (kc, vc)], qw)
    o_c = _plain_attn_call(qc, [(kc, vc)], qw) if need_ctx else None
    return o_l, o_c


def _diff_mixer(xl, xc, mods_l, ctx_row, g, w_in, lq1, lk1, lq2, lk2, subln_g, lambda_init, need_ctx):
    d = xl.shape[-1]
    assert lq1.shape[0] == HEAD_DIM and subln_g.shape[0] == LANES
    w = w_in.astype(BF16)
    ql, kl, vl = _qkv_proj_call(xl, mods_l, None, g, w, d, rope=_rope_tables(xl.shape[1]))
    qc, kc, vc = _qkv_proj_call(xc, mods_l, ctx_row, g, w, d)
    lam_params = jnp.concatenate([jnp.stack([lq1, lk1, lq2, lk2]), jnp.zeros((SUBLANES - 4, HEAD_DIM), F32)])
    o_l = _diff_attn_call(ql, [(kl, vl), (kc, vc)], lam_params, subln_g[None], lambda_init)
    o_c = _diff_attn_call(qc, [(kc, vc)], lam_params, subln_g[None], lambda_init) if need_ctx else None
    return o_l, o_c


def kernel(x, c, ctx, c_ctx, mod_w, mod_b, norm1_g, norm2_g, rg_w_in, rg_conv_w, rg_conv_b, rg_wa, rg_ba, rg_wx, rg_bx, rg_lam, rg_w_out, na_w_in, na_rpb, na_w_out, gqa_w_in, gqa_q_norm, gqa_k_norm, gqa_w_out, diff_w_in, diff_lq1, diff_lk1, diff_lq2, diff_lk2, diff_subln_g, diff_w_out, ffn_w_up, ffn_conv_w, ffn_conv_b, ffn_w_down, final_g):
    bsz, _, d = x.shape
    depth = mod_w.shape[0]
    assert d % LANES == 0
    ctx_row = bsz
    nrow = -(-(bsz + 1) // SUBLANES) * SUBLANES
    cc = jnp.concatenate([c, c_ctx[None], jnp.zeros((nrow - bsz - 1, d), F32)], axis=0)
    mods = _mods_call(cc, mod_w, mod_b).reshape(depth, nrow, 6, d)
    f = ffn_w_down.shape[1]
    fc = next(t for t in (256, 128) if f % t == 0)
    xl, xc = x, ctx
    for l in range(depth):
        m, j = l % N_MIXERS, l // N_MIXERS
        need_ctx = l < depth - 1
        mods_l = mods[l]
        g1, g2 = norm1_g[l][None], norm2_g[l][None]
        if m == 0:
            a_l, a_c = _rg_mixer(xl, xc, mods_l, ctx_row, g1, rg_w_in[j], rg_conv_w[j], rg_conv_b[j], rg_wa[j],
                                 rg_ba[j], rg_wx[j], rg_bx[j], rg_lam[j], need_ctx)
            w_out = rg_w_out[j]
        elif m == 1:
            a_l, a_c = _na_mixer(xl, xc, mods_l, ctx_row, g1, na_w_in[j], na_rpb[j], need_ctx)
            w_out = na_w_out[j]
        elif m == 2:
            a_l, a_c = _gqa_mixer(xl, xc, mods_l, ctx_row, g1, gqa_w_in[j], gqa_q_norm[j], gqa_k_norm[j], need_ctx)
            w_out = gqa_w_out[j]
        else:
            lambda_init = 0.8 - 0.6 * math.exp(-0.3 * l)
            a_l, a_c = _diff_mixer(xl, xc, mods_l, ctx_row, g1, diff_w_in[j], diff_lq1[j], diff_lk1[j],
                                   diff_lq2[j], diff_lk2[j], diff_subln_g[j], lambda_init, need_ctx)
            w_out = diff_w_out[j]
        w_out = w_out.astype(BF16)
        packed = _ffn_pack_params(ffn_w_up[l], ffn_conv_w[l], ffn_conv_b[l], ffn_w_down[l], fc)
        last = l == depth - 1
        xl, zl = _outproj_call(a_l, xl, mods_l, None, g2, w_out)
        xl = _ffn_call(xl, zl, mods_l, None, packed, final_g[None] if last else None)
        if need_ctx:
            xc, zc = _outproj_call(a_c, xc, mods_l, ctx_row, g2, w_out)
            xc = _ffn_call(xc, zc, mods_l, ctx_row, packed)
    return xl
```

```python
import functools
import math

import jax
import jax.numpy as jnp
from jax import lax
from jax.experimental import pallas as pl
from jax.experimental.pallas import tpu as pltpu

F32 = jnp.float32
BF16 = jnp.bfloat16
EPS = 1e-6
ROPE_THETA = 10000.0
GRID_W = 64
RG_C = 8.0
RG_CONV_LEFT = 2
N_MIXERS = 4
HEAD_DIM = 64
LOG2E = 1.4426950408889634
Q_SCALE = HEAD_DIM ** -0.5 * LOG2E
LANES = 128
SUBLANES = 8
NEG = -1e30
FFN_HALO = 16
FFN_SUB = 512
FFN_TM = 512
PROJ_SPLIT = 2
PROJ_MIN_ROWS = 256
ATT_SUB = 256
ATT_RB = 32
DIFF_RB = 16
KEY_SLAB = 32
VMEM_LIMIT_BYTES = 56 * 1024 * 1024

_NT = (((1,), (1,)), ((), ()))


def _cparams(sem):
    return pltpu.CompilerParams(dimension_semantics=sem, vmem_limit_bytes=VMEM_LIMIT_BYTES)


def _dot(a, b):
    return jnp.dot(a, b, preferred_element_type=F32)


def _dot_nt(a, b):
    return lax.dot_general(a, b, _NT, preferred_element_type=F32)


def _silu(x):
    return x / (1.0 + jnp.exp(-x))


def _gelu_tanh(x):
    cdf = 0.5 * (1.0 + jnp.tanh(math.sqrt(2.0 / math.pi) * (x + 0.044715 * (x * x * x))))
    return x * cdf


def _rms(x, g):
    y = x * lax.rsqrt(jnp.mean(x * x, axis=-1, keepdims=True) + EPS)
    return y * g


def _normmod(x, g, shift, scale):
    return _rms(x, g) * (1.0 + scale) + shift


def _mods_kernel(cc_ref, w_ref, b_ref, o_ref):
    a = _silu(cc_ref[...]).astype(BF16)
    o_ref[0] = _dot(a, w_ref[0].astype(BF16)) + b_ref[0]


def _mods_call(cc, mod_w, mod_b):
    depth, d, n = mod_w.shape
    r = cc.shape[0]
    tn = n // 4
    return pl.pallas_call(
        _mods_kernel,
        out_shape=jax.ShapeDtypeStruct((depth, r, n), F32),
        grid=(depth, n // tn),
        in_specs=[
            pl.BlockSpec((r, d), lambda l, j: (0, 0)),
            pl.BlockSpec((1, d, tn), lambda l, j: (l, 0, j)),
            pl.BlockSpec((1, 1, tn), lambda l, j: (l, 0, j)),
        ],
        out_specs=pl.BlockSpec((1, r, tn), lambda l, j: (l, 0, j)),
        compiler_params=_cparams(("arbitrary", "arbitrary")),
        name="mods",
    )(cc, mod_w, mod_b.reshape(depth, 1, n))


def _mod_spec(d, row):
    if row is None:
        return pl.BlockSpec((1, 6, d), lambda b, *_: (b, 0, 0))
    return pl.BlockSpec((1, 6, d), lambda b, *_: (row, 0, 0))


def _split_rows(tm):
    sub = tm // PROJ_SPLIT
    return sub if (tm % PROJ_SPLIT == 0 and sub >= PROJ_MIN_ROWS) else tm


def _row_tile(n, pref):
    t = min(n, pref)
    assert n % t == 0
    return t


def _rope_tables(n_tok):
    t = jnp.arange(n_tok)
    row = (t // GRID_W).astype(F32)
    col = (t % GRID_W).astype(F32)
    n = HEAD_DIM // 4
    inv = ROPE_THETA ** (-jnp.arange(n, dtype=F32) / n)
    ang = jnp.concatenate([row[:, None] * inv, col[:, None] * inv], axis=-1)
    cos, sin = jnp.cos(ang), jnp.sin(ang)
    reps = LANES // HEAD_DIM
    cos_t = jnp.tile(jnp.concatenate([cos, cos], axis=-1), (1, reps))
    sin_t = jnp.tile(jnp.concatenate([-sin, sin], axis=-1), (1, reps))
    return cos_t, sin_t


def _swap_halves(x):
    lane = lax.broadcasted_iota(jnp.int32, x.shape, 1)
    lo = (lane & (HEAD_DIM // 2)) == 0
    return jnp.where(lo, pltpu.roll(x, LANES - HEAD_DIM // 2, 1), pltpu.roll(x, HEAD_DIM // 2, 1))


def _rope_cols(u, cos_t, sin_t):
    outs = []
    for c in range(u.shape[1] // LANES):
        xc = u[:, c * LANES:(c + 1) * LANES]
        outs.append(xc * cos_t + _swap_halves(xc) * sin_t)
    return jnp.concatenate(outs, axis=1)


def _rg_proj_kernel(x_ref, mod_ref, g_ref, w_ref, go_ref, xo_ref, *, dr):
    tm = x_ref.shape[1]
    sub = _split_rows(tm)
    for r in range(0, tm, sub):
        rows = slice(r, r + sub)
        z = _normmod(x_ref[0, rows], g_ref[...], mod_ref[0, 0:1], mod_ref[0, 1:2]).astype(BF16)
        u = _dot(z, w_ref[...])
        go_ref[0, rows] = _gelu_tanh(u[:, :dr])
        xo_ref[0, rows] = u[:, dr:]


def _rg_proj_call(x, mods_l, mod_row, g, w):
    b, n, d = x.shape
    dr = w.shape[1] // 2
    tm = _row_tile(n, 512)
    return pl.pallas_call(
        functools.partial(_rg_proj_kernel, dr=dr),
        out_shape=(jax.ShapeDtypeStruct((b, n, dr), F32), jax.ShapeDtypeStruct((b, n, dr), F32)),
        grid=(b, n // tm),
        in_specs=[
            pl.BlockSpec((1, tm, d), lambda bi, i: (bi, i, 0)),
            _mod_spec(d, mod_row),
            pl.BlockSpec((1, d), lambda bi, i: (0, 0)),
            pl.BlockSpec((d, 2 * dr), lambda bi, i: (0, 0)),
        ],
        out_specs=(pl.BlockSpec((1, tm, dr), lambda bi, i: (bi, i, 0)),
                   pl.BlockSpec((1, tm, dr), lambda bi, i: (bi, i, 0))),
        compiler_params=_cparams(("arbitrary", "arbitrary")),
        name="rg_proj",
    )(x, mods_l, g, w)


def _qkv_proj_kernel(x_ref, mod_ref, g_ref, w_ref, *rest, d, rope, qknorm):
    rest = list(rest)
    if rope:
        cos_ref, sin_ref = rest.pop(0), rest.pop(0)
    if qknorm:
        gain_ref, ind_ref, indt_ref = rest.pop(0), rest.pop(0), rest.pop(0)
    q_ref, k_ref, v_ref = rest
    nk = k_ref.shape[-1]
    tm = x_ref.shape[1]
    sub = _split_rows(tm)
    for r in range(0, tm, sub):
        rows = slice(r, r + sub)
        z = _normmod(x_ref[0, rows], g_ref[...], mod_ref[0, 0:1], mod_ref[0, 1:2]).astype(BF16)
        u = _dot(z, w_ref[...])
        qk = u[:, :d + nk]
        if qknorm:
            x2 = qk * qk
            hi = x2.astype(BF16)
            lo = (x2 - hi.astype(F32)).astype(BF16)
            ssq = _dot(hi, ind_ref[...]) + _dot(lo, ind_ref[...])
            rs = lax.rsqrt(ssq * (1.0 / HEAD_DIM) + EPS)
            rhi = rs.astype(BF16)
            rlo = (rs - rhi.astype(F32)).astype(BF16)
            rb = _dot(rhi, indt_ref[...]) + _dot(rlo, indt_ref[...])
            qk = (qk * rb) * gain_ref[...]
        if rope:
            qk = _rope_cols(qk, cos_ref[rows], sin_ref[rows])
        q_ref[0, rows] = (qk[:, :d] * Q_SCALE).astype(q_ref.dtype)
        k_ref[0, rows] = qk[:, d:].astype(k_ref.dtype)
        v_ref[0, rows] = u[:, d + nk:].astype(v_ref.dtype)


def _qkv_proj_call(x, mods_l, mod_row, g, w, nk, *, rope=None, qknorm=None):
    b, n, d = x.shape
    nv = w.shape[1] - d - nk
    tm = _row_tile(n, 512)
    const = lambda bi, i: (0, 0)
    in_specs = [
        pl.BlockSpec((1, tm, d), lambda bi, i: (bi, i, 0)),
        _mod_spec(d, mod_row),
        pl.BlockSpec((1, d), const),
        pl.BlockSpec(w.shape, const),
    ]
    args = [x, mods_l, g, w]
    if rope is not None:
        in_specs += [pl.BlockSpec((tm, LANES), lambda bi, i: (i, 0))] * 2
        args += list(rope)
    if qknorm is not None:
        in_specs += [pl.BlockSpec(a.shape, const) for a in qknorm]
        args += list(qknorm)
    out_spec = lambda w_: pl.BlockSpec((1, tm, w_), lambda bi, i: (bi, i, 0))
    return pl.pallas_call(
        functools.partial(_qkv_proj_kernel, d=d, rope=rope is not None, qknorm=qknorm is not None),
        out_shape=(jax.ShapeDtypeStruct((b, n, d), BF16), jax.ShapeDtypeStruct((b, n, nk), BF16),
                   jax.ShapeDtypeStruct((b, n, nv), BF16)),
        grid=(b, n // tm),
        in_specs=in_specs,
        out_specs=(out_spec(d), out_spec(nk), out_spec(nv)),
        compiler_params=_cparams(("arbitrary", "arbitrary")),
        name="qkv_proj",
    )(*args)


def _rg_conv(xs_scr, x, prev, nxt, cw):
    t = x.shape[0]
    h = SUBLANES
    xs_scr[0:h] = prev
    xs_scr[h:h + t] = x
    xs_scr[h + t:2 * h + t] = nxt
    y = cw[4:5] + cw[0:1] * xs_scr[h - 2:h - 2 + t]
    y = y + cw[1:2] * xs_scr[h - 1:h - 1 + t]
    y = y + cw[2:3] * x
    y = y + cw[3:4] * xs_scr[h + 1:h + 1 + t]
    return y


def _sigmoid(x):
    return 0.5 * jnp.tanh(0.5 * x) + 0.5


def _rg_gates(xc, w, ba, bx, lam):
    gw = xc.shape[1]
    y = _dot(xc.astype(BF16), w)
    r = _sigmoid(y[:, :gw] + ba)
    i = _sigmoid(y[:, gw:] + bx)
    nl = -lam
    softplus = jnp.maximum(nl, 0.0) + jnp.log1p(jnp.exp(-jnp.abs(nl)))
    log_a = r * (-RG_C * softplus)
    a = jnp.exp(log_a)
    one_minus_a2 = -jnp.tanh(log_a) * (a * a + 1.0)
    root = jnp.where(one_minus_a2 > 0.0, one_minus_a2 * lax.rsqrt(one_minus_a2), 0.0)
    return a, root * (i * xc)


def _rg_scan_tile(a, b, carry, reverse):
    t, gw = a.shape
    nck = t // SUBLANES
    a = a.reshape(nck, SUBLANES, gw)
    b = b.reshape(nck, SUBLANES, gw)
    rmod = lax.broadcasted_iota(jnp.int32, (1, SUBLANES, 1), 1)
    for s in (1, 2, 4):
        ok = (rmod < SUBLANES - s) if reverse else (rmod >= s)
        shift = SUBLANES - s if reverse else s
        ash, bsh = pltpu.roll(a, shift, 1), pltpu.roll(b, shift, 1)
        b = a * jnp.where(ok, bsh, 0.0) + b
        a = a * jnp.where(ok, ash, 1.0)
    hs = [None] * nck
    for ck in (range(nck - 1, -1, -1) if reverse else range(nck)):
        hc = b[ck] + a[ck] * carry
        carry = hc[0:1] if reverse else hc[SUBLANES - 1:SUBLANES]
        hs[ck] = hc
    return jnp.concatenate(hs, axis=0), carry


def _rg_scan_kernel(xc_ref, gc_ref, xl_ref, xp_ref, xn_ref, gl_ref, cw_ref, w_ref, gp_ref,
                    mc_ref, ml_ref, hf_scr, xv_scr, car_scr, xs_scr, *, nt):
    s = pl.program_id(2)
    gw = xl_ref.shape[-1]
    tl = xl_ref.shape[1]
    cw = cw_ref[0]
    gp = gp_ref[0]

    def gates(xconv, d):
        w = w_ref[0, :, d * 2 * gw:(d + 1) * 2 * gw]
        return _rg_gates(xconv, w, gp[3 * d:3 * d + 1], gp[3 * d + 1:3 * d + 2], gp[3 * d + 2:3 * d + 3])

    @pl.when(s == 0)
    def _ctx():
        zeros8 = jnp.zeros((SUBLANES, gw), F32)
        xconv = _rg_conv(xs_scr, xc_ref[0], zeros8, zeros8, cw)
        zero = jnp.zeros((1, gw), F32)
        a0, b0 = gates(xconv, 0)
        hf, cf = _rg_scan_tile(a0, b0, zero, False)
        a1, b1 = gates(xconv, 1)
        hb, cb = _rg_scan_tile(a1, b1, zero, True)
        car_scr[0:1] = cf
        car_scr[1:2] = cb
        mc_ref[0] = (gc_ref[0] * (hf + hb)).astype(mc_ref.dtype)

    def lat_conv(t):
        prev = xp_ref[0] * (t > 0).astype(F32)
        nxt = xn_ref[0] * (t < nt - 1).astype(F32)
        return _rg_conv(xs_scr, xl_ref[0], prev, nxt, cw)

    @pl.when((s >= 1) & (s <= nt))
    def _fwd():
        t = s - 1
        xconv = lat_conv(t)
        xv_scr[pl.ds(pl.multiple_of(t * tl, tl), tl), :] = xconv
        a0, b0 = gates(xconv, 0)
        h, c = _rg_scan_tile(a0, b0, car_scr[0:1], False)
        car_scr[0:1] = c
        hf_scr[pl.ds(pl.multiple_of(t * tl, tl), tl), :] = h

    @pl.when(s > nt)
    def _bwd():
        t = 2 * nt - s
        a1, b1 = gates(xv_scr[pl.ds(pl.multiple_of(t * tl, tl), tl), :], 1)
        h, c = _rg_scan_tile(a1, b1, car_scr[1:2], True)
        car_scr[1:2] = c
        hf = hf_scr[pl.ds(pl.multiple_of(t * tl, tl), tl), :]
        ml_ref[0] = (gl_ref[0] * (hf + h)).astype(ml_ref.dtype)


def _rg_scan_call(xr_c, g_c, xr_l, g_l, cw, wbd, gp):
    b, s_len, dr = xr_l.shape
    c_len = xr_c.shape[1]
    ng, gw = wbd.shape[0], wbd.shape[1]
    tl = _row_tile(s_len, 512)
    nt = s_len // tl
    nb8 = tl // SUBLANES

    def tile(s):
        return jnp.clip(s - 1, 0, nt - 1)

    def tile_bwd(s):
        return jnp.where(s <= nt, nt - 1, 2 * nt - s)

    return pl.pallas_call(
        functools.partial(_rg_scan_kernel, nt=nt),
        out_shape=(jax.ShapeDtypeStruct((b, c_len, dr), BF16), jax.ShapeDtypeStruct((b, s_len, dr), BF16)),
        grid=(b, ng, 2 * nt + 1),
        in_specs=[
            pl.BlockSpec((1, c_len, gw), lambda bi, h, s: (bi, 0, h)),
            pl.BlockSpec((1, c_len, gw), lambda bi, h, s: (bi, 0, h)),
            pl.BlockSpec((1, tl, gw), lambda bi, h, s: (bi, tile(s), h)),
            pl.BlockSpec((1, SUBLANES, gw), lambda bi, h, s: (bi, jnp.maximum(tile(s) * nb8 - 1, 0), h)),
            pl.BlockSpec((1, SUBLANES, gw),
                         lambda bi, h, s: (bi, jnp.minimum((tile(s) + 1) * nb8, s_len // SUBLANES - 1), h)),
            pl.BlockSpec((1, tl, gw), lambda bi, h, s: (bi, tile_bwd(s), h)),
            pl.BlockSpec((1, SUBLANES, gw), lambda bi, h, s: (h, 0, 0)),
            pl.BlockSpec((1, gw, 4 * gw), lambda bi, h, s: (h, 0, 0)),
            pl.BlockSpec((1, SUBLANES, gw), lambda bi, h, s: (h, 0, 0)),
        ],
        out_specs=(pl.BlockSpec((1, c_len, gw), lambda bi, h, s: (bi, 0, h)),
                   pl.BlockSpec((1, tl, gw), lambda bi, h, s: (bi, tile_bwd(s), h))),
        scratch_shapes=[pltpu.VMEM((s_len, gw), F32), pltpu.VMEM((s_len, gw), F32), pltpu.VMEM((SUBLANES, gw), F32),
                        pltpu.VMEM((max(tl, c_len) + 2 * SUBLANES, gw), F32)],
        compiler_params=_cparams(("arbitrary", "arbitrary", "arbitrary")),
        name="rg_scan",
    )(xr_c, g_c, xr_l, xr_l, xr_l, g_l, cw, wbd, gp)


def _rg_pack_params(conv_w, conv_b, wa, ba, wx, bx, lam):
    nblk, bw = wa.shape[1], wa.shape[2]
    dr = nblk * bw
    per = next(p for p in range(1, nblk + 1) if nblk % p == 0 and (p * bw) % LANES == 0)
    ng, gw = nblk // per, per * bw

    def dense(w):
        w = w.reshape(ng, per, bw, bw)
        eye = jnp.eye(per, dtype=w.dtype)
        return jnp.einsum('gpjk,pq->gpjqk', w, eye).reshape(ng, gw, gw)

    wbd = jnp.concatenate([dense(wa[0]), dense(wx[0]), dense(wa[1]), dense(wx[1])], axis=-1).astype(BF16)
    cw = jnp.concatenate([conv_w, conv_b[None], jnp.zeros((SUBLANES - 1 - conv_w.shape[0], dr), F32)], axis=0)
    cw = cw.reshape(SUBLANES, ng, gw).transpose(1, 0, 2)
    gp = jnp.stack([ba[0], bx[0], lam[0], ba[1], bx[1], lam[1], jnp.zeros_like(lam[0]), jnp.zeros_like(lam[0])])
    gp = gp.reshape(SUBLANES, ng, gw).transpose(1, 0, 2)
    return cw, wbd, gp


def _pair_masks(shape):
    lane = lax.broadcasted_iota(jnp.int32, shape, 1)
    return lane < HEAD_DIM


def _half_q(q, hh):
    first = _pair_masks(q.shape)
    return jnp.where(first if hh == 0 else jnp.logical_not(first), q, jnp.zeros_like(q))


def _scores_to_scratch(qh, k_refs, s_scr, slot):
    off = 0
    for k_ref in k_refs:
        n = k_ref.shape[1]
        s_scr[slot, :, off:off + n] = _dot_nt(qh, k_ref[0])
        off += n


def _exp_rows(s):
    e = jnp.exp2(s - s.max(axis=-1, keepdims=True))
    return e, e.sum(axis=-1, keepdims=True)


def _softmax_to_scratch(s_scr, p_scr, slot, rows):
    linv = []
    for r in range(0, rows, ATT_RB):
        e, l = _exp_rows(s_scr[slot, r:r + ATT_RB, :])
        p_scr[slot, r:r + ATT_RB, :] = e.astype(BF16)
        linv.append(1.0 / l)
    return jnp.concatenate(linv, axis=0)


def _pv_from_scratch(p_scr, slot, v_refs):
    off, o = 0, None
    for v_ref in v_refs:
        n = v_ref.shape[1]
        t = _dot(p_scr[slot, :, off:off + n], v_ref[0])
        o = t if o is None else o + t
        off += n
    return o


def _plain_attn_kernel(q_ref, *refs, nsrc):
    k_refs, v_refs = refs[0:2 * nsrc:2], refs[1:2 * nsrc:2]
    o_ref, s_scr, p_scr = refs[2 * nsrc:]
    tq, qw = q_ref.shape[1], q_ref.shape[2]
    sub = s_scr.shape[1]
    units = [(r, c, hh) for r in range(0, tq, sub) for c in range(0, qw, LANES) for hh in range(2)]

    def scores(u):
        r, c, hh = units[u]
        _scores_to_scratch(_half_q(q_ref[0, r:r + sub, c:c + LANES], hh), k_refs, s_scr, u % 2)

    scores(0)
    outs = [None, None]
    for u, (r, c, hh) in enumerate(units):
        if u + 1 < len(units):
            scores(u + 1)
        linv = _softmax_to_scratch(s_scr, p_scr, u % 2, sub)
        outs[hh] = _pv_from_scratch(p_scr, u % 2, v_refs) * linv
        if hh == 1:
            o_ref[0, r:r + sub, c:c + LANES] = jnp.where(_pair_masks((sub, LANES)), outs[0], outs[1]).astype(o_ref.dtype)


def _reduce_keys(x, op):
    nk, nq = x.shape
    slab = next(t for t in (KEY_SLAB, SUBLANES) if nk % t == 0)
    return op(op(x.reshape(nk // slab, slab, nq), axis=0), axis=0, keepdims=True)


def _plain_attn_t_kernel(q_ref, *refs, nsrc):
    k_refs, v_refs = refs[0:2 * nsrc:2], refs[1:2 * nsrc:2]
    o_ref = refs[2 * nsrc]
    vt_scrs = refs[2 * nsrc + 1:3 * nsrc + 1]
    s_scr, p_scr = refs[3 * nsrc + 1:]
    tq, qw = q_ref.shape[1], q_ref.shape[2]
    sub = s_scr.shape[2]

    @pl.when(pl.program_id(2) == 0)
    def _transpose_values():
        for v_ref, vt_scr in zip(v_refs, vt_scrs):
            vt_scr[...] = v_ref[0].astype(F32).T.astype(BF16)

    units = [(r, c, hh) for r in range(0, tq, sub) for c in range(0, qw, LANES) for hh in range(2)]

    def scores(u):
        r, c, hh = units[u]
        qh = _half_q(q_ref[0, r:r + sub, c:c + LANES], hh)
        off = 0
        for k_ref in k_refs:
            n = k_ref.shape[1]
            s_scr[u % ns, off:off + n, :] = _dot_nt(k_ref[0], qh)
            off += n

    ns = s_scr.shape[0]
    for u in range(min(ns - 1, len(units))):
        scores(u)
    outs = [None, None]
    for u, (r, c, hh) in enumerate(units):
        if u + ns - 1 < len(units):
            scores(u + ns - 1)
        slabs = [slice(i, i + KEY_SLAB) for i in range(0, s_scr.shape[1], KEY_SLAB)]
        macc = s_scr[u % ns, slabs[0], :]
        for sl in slabs[1:]:
            macc = jnp.maximum(macc, s_scr[u % ns, sl, :])
        m = jnp.max(macc, axis=0, keepdims=True)
        lacc = None
        for sl in slabs:
            e = jnp.exp2(s_scr[u % ns, sl, :] - m)
            p_scr[u % 2, sl, :] = e.astype(BF16)
            lacc = e if lacc is None else lacc + e
        linv = 1.0 / jnp.sum(lacc, axis=0, keepdims=True)
        off, ot = 0, None
        for vt_scr in vt_scrs:
            n = vt_scr.shape[1]
            t = _dot(vt_scr[hh * HEAD_DIM:(hh + 1) * HEAD_DIM, :], p_scr[u % 2, off:off + n, :])
            ot = t if ot is None else ot + t
            off += n
        outs[hh] = ot * linv
        if hh == 1:
            o_ref[0, r:r + sub, c:c + LANES] = jnp.concatenate(outs, axis=0).T.astype(o_ref.dtype)


def _plain_attn_t_call(q, kvs, qw):
    b, lq, d = q.shape
    tq = _row_tile(lq, 4 * ATT_SUB)
    sub = min(tq, ATT_SUB)
    nk = sum(k.shape[1] for k, _ in kvs)
    in_specs = [pl.BlockSpec((1, tq, qw), lambda bi, j, i: (bi, i, j))]
    args = [q]
    for k, v in kvs:
        spec = pl.BlockSpec((1, k.shape[1], LANES), lambda bi, j, i: (bi, 0, j))
        in_specs += [spec, spec]
        args += [k, v]
    return pl.pallas_call(
        functools.partial(_plain_attn_t_kernel, nsrc=len(kvs)),
        out_shape=jax.ShapeDtypeStruct((b, lq, d), BF16),
        grid=(b, d // qw, lq // tq),
        in_specs=in_specs,
        out_specs=pl.BlockSpec((1, tq, qw), lambda bi, j, i: (bi, i, j)),
        scratch_shapes=([pltpu.VMEM((LANES, k.shape[1]), BF16) for k, _ in kvs]
                        + [pltpu.VMEM((3, nk, sub), F32), pltpu.VMEM((2, nk, sub), BF16)]),
        compiler_params=_cparams(("arbitrary", "arbitrary", "arbitrary")),
        name="plain_attn_t",
    )(*args)


def _plain_attn_call(q, kvs, qw):
    b, lq, d = q.shape
    tq = _row_tile(lq, 2 * ATT_SUB)
    sub = min(tq, ATT_SUB)
    nk = sum(k.shape[1] for k, _ in kvs)
    in_specs = [pl.BlockSpec((1, tq, qw), lambda bi, j, i: (bi, i, j))]
    args = [q]
    for k, v in kvs:
        spec = pl.BlockSpec((1, k.shape[1], LANES), lambda bi, j, i: (bi, 0, j))
        in_specs += [spec, spec]
        args += [k, v]
    return pl.pallas_call(
        functools.partial(_plain_attn_kernel, nsrc=len(kvs)),
        out_shape=jax.ShapeDtypeStruct((b, lq, d), BF16),
        grid=(b, d // qw, lq // tq),
        in_specs=in_specs,
        out_specs=pl.BlockSpec((1, tq, qw), lambda bi, j, i: (bi, i, j)),
        scratch_shapes=[pltpu.VMEM((2, sub, nk), F32), pltpu.VMEM((2, sub, nk), BF16)],
        compiler_params=_cparams(("arbitrary", "arbitrary", "arbitrary")),
        name="plain_attn",
    )(*args)


def _diff_attn_kernel(q_ref, lam_ref, g_ref, *refs, nsrc, lambda_init):
    k_refs, v_refs = refs[0:2 * nsrc:2], refs[1:2 * nsrc:2]
    o_ref, s_scr, a_scr = refs[2 * nsrc:]
    lp = lam_ref[...]
    lam = (jnp.exp(jnp.sum(lp[0:1] * lp[1:2], axis=-1, keepdims=True))
           - jnp.exp(jnp.sum(lp[2:3] * lp[3:4], axis=-1, keepdims=True)) + lambda_init)
    tq = q_ref.shape[1]
    sub = s_scr.shape[1]
    units = list(range(0, tq, sub))

    def scores(u):
        q = q_ref[0, units[u]:units[u] + sub]
        for hh in range(2):
            _scores_to_scratch(_half_q(q, hh), k_refs, s_scr, 2 * (u % 2) + hh)

    def combine(u):
        linv = []
        for r in range(0, sub, DIFF_RB):
            e0, l0 = _exp_rows(s_scr[2 * (u % 2), r:r + DIFF_RB, :])
            e1, l1 = _exp_rows(s_scr[2 * (u % 2) + 1, r:r + DIFF_RB, :])
            a_scr[u % 2, r:r + DIFF_RB, :] = (e0 - e1 * (lam * l0 / l1)).astype(BF16)
            linv.append(1.0 / l0)
        return jnp.concatenate(linv, axis=0)

    scores(0)
    for u, r in enumerate(units):
        if u + 1 < len(units):
            scores(u + 1)
        linv = combine(u)
        o = _pv_from_scratch(a_scr, u % 2, v_refs) * linv
        o_ref[0, r:r + sub] = (_rms(o, g_ref[...]) * (1.0 - lambda_init)).astype(o_ref.dtype)


def _diff_attn_call(q, kvs, lam_params, subln_g, lambda_init):
    b, lq, d = q.shape
    tq = _row_tile(lq, 4 * ATT_SUB)
    sub = min(tq, ATT_SUB)
    nk = sum(k.shape[1] for k, _ in kvs)
    const = lambda bi, j, i: (0, 0)
    in_specs = [pl.BlockSpec((1, tq, LANES), lambda bi, j, i: (bi, i, j)),
                pl.BlockSpec(lam_params.shape, const), pl.BlockSpec(subln_g.shape, const)]
    args = [q, lam_params, subln_g]
    for k, v in kvs:
        spec = pl.BlockSpec((1, k.shape[1], LANES), lambda bi, j, i: (bi, 0, j))
        in_specs += [spec, spec]
        args += [k, v]
    return pl.pallas_call(
        functools.partial(_diff_attn_kernel, nsrc=len(kvs), lambda_init=lambda_init),
        out_shape=jax.ShapeDtypeStruct((b, lq, d), BF16),
        grid=(b, d // LANES, lq // tq),
        in_specs=in_specs,
        out_specs=pl.BlockSpec((1, tq, LANES), lambda bi, j, i: (bi, i, j)),
        scratch_shapes=[pltpu.VMEM((4, sub, nk), F32), pltpu.VMEM((2, sub, nk), BF16)],
        compiler_params=_cparams(("arbitrary", "arbitrary", "arbitrary")),
        name="diff_attn",
    )(*args)


def _na_attn_kernel(q_ref, k_ref, v_ref, kc_ref, vc_ref, rpb_ref, o_ref, t2_scr, s_scr, p_scr, *, rows, kr, kcw, nq):
    w = GRID_W
    nwin = nq + kr
    ntab = 2 * kr
    nkw = nwin * w
    sub = nq * w
    nsub = q_ref.shape[1] // sub
    i = pl.program_id(2)

    @pl.when((i == 0) & (pl.program_id(1) == 0))
    def _build_bias():
        c_io = lax.broadcasted_iota(jnp.int32, (w, LANES), 0)
        cp_io = lax.broadcasted_iota(jnp.int32, (w, LANES), 1)
        cs = jnp.clip(c_io - kcw // 2, 0, w - kcw)
        colok = (cp_io >= cs) & (cp_io < cs + kcw)
        neg = jnp.full((w, LANES), NEG, F32)
        for hh in range(2):
            r = pltpu.roll(rpb_ref[hh] * LOG2E, LANES - (kcw - 1), 1)
            ts = []
            for e in range(ntab - 1):
                t = jnp.broadcast_to(r[e:e + 1], (w, LANES))
                t = pltpu.roll(t, 0, 1, stride=1, stride_axis=0)
                ts.append(jnp.where(colok, t, NEG))
            for e in range(ntab):
                lo = ts[e - 1] if e >= 1 else neg
                hi = ts[e] if e < ntab - 1 else neg
                t2_scr[hh, e] = jnp.where(cp_io < w, lo, pltpu.roll(hi, w, 1))

    units = [(t, hh) for t in range(nsub) for hh in range(2)]

    def window(t):
        r0 = (i * nsub + t) * nq
        ks = jnp.clip(r0 - kr // 2, 0, rows - nwin)
        return r0, ks, pl.ds(pl.multiple_of(ks * w, w), nkw)

    def scores(u):
        t, hh = units[u]
        _, _, win = window(t)
        qh = _half_q(q_ref[0, t * sub:(t + 1) * sub], hh)
        s_scr[u % 2, :, :nkw] = _dot_nt(qh, k_ref[0, win, :])
        s_scr[u % 2, :, nkw:] = _dot_nt(qh, kc_ref[0])

    def softmax(u):
        t, hh = units[u]
        r0, ks, _ = window(t)
        key_row = ks + jnp.right_shift(lax.broadcasted_iota(jnp.int32, (1, nkw), 1), int(math.log2(w)))
        linv = []
        for qi in range(nq):
            r = r0 + qi
            rs = jnp.clip(r - kr // 2, 0, rows - kr)
            rowok = (key_row >= rs) & (key_row < rs + kr)
            idx = [jnp.clip(ks + 2 * p - r + kr, 0, ntab - 1) for p in range(nwin // 2)]
            for rb in range(0, w, ATT_RB):
                row = qi * w + rb
                bias = jnp.concatenate([t2_scr[hh, e, rb:rb + ATT_RB, :] for e in idx], axis=1)
                sw = jnp.where(rowok, s_scr[u % 2, row:row + ATT_RB, :nkw] + bias, NEG)
                sc = s_scr[u % 2, row:row + ATT_RB, nkw:]
                m = jnp.maximum(sw.max(axis=-1, keepdims=True), sc.max(axis=-1, keepdims=True))
                ew, ec = jnp.exp2(sw - m), jnp.exp2(sc - m)
                p_scr[u % 2, row:row + ATT_RB, :nkw] = ew.astype(BF16)
                p_scr[u % 2, row:row + ATT_RB, nkw:] = ec.astype(BF16)
                linv.append(1.0 / (ew.sum(axis=-1, keepdims=True) + ec.sum(axis=-1, keepdims=True)))
        return jnp.concatenate(linv, axis=0)

    scores(0)
    outs = [None, None]
    for u, (t, hh) in enumerate(units):
        if u + 1 < len(units):
            scores(u + 1)
        linv = softmax(u)
        _, _, win = window(t)
        o = _dot(p_scr[u % 2, :, :nkw], v_ref[0, win, :]) + _dot(p_scr[u % 2, :, nkw:], vc_ref[0])
        outs[hh] = o * linv
        if hh == 1:
            o_ref[0, t * sub:(t + 1) * sub] = jnp.where(_pair_masks((sub, LANES)), outs[0], outs[1]).astype(o_ref.dtype)


def _na_attn_call(q, k, v, kc, vc, rpb):
    b, s_len, d = q.shape
    c_len = kc.shape[1]
    nh, ndr, ndc = rpb.shape
    kr, kcw = (ndr + 1) // 2, (ndc + 1) // 2
    rows = s_len // GRID_W
    nq = 4
    assert GRID_W == HEAD_DIM and rows % nq == 0 and rows >= nq + kr and (nq + kr) % 2 == 0
    assert ndc <= LANES and 2 * kr <= 2 * SUBLANES
    rpb_p = jnp.full((nh, 2 * SUBLANES, LANES), NEG, F32).at[:, :ndr, :ndc].set(rpb)
    sub = nq * GRID_W
    tq = _row_tile(s_len, 8 * sub)
    nk = (nq + kr) * GRID_W + c_len
    full = lambda n: pl.BlockSpec((1, n, LANES), lambda j, bi, i: (bi, 0, j))
    return pl.pallas_call(
        functools.partial(_na_attn_kernel, rows=rows, kr=kr, kcw=kcw, nq=nq),
        out_shape=jax.ShapeDtypeStruct((b, s_len, d), BF16),
        grid=(d // LANES, b, s_len // tq),
        in_specs=[pl.BlockSpec((1, tq, LANES), lambda j, bi, i: (bi, i, j)),
                  full(s_len), full(s_len), full(c_len), full(c_len),
                  pl.BlockSpec((2, 2 * SUBLANES, LANES), lambda j, bi, i: (j, 0, 0))],
        out_specs=pl.BlockSpec((1, tq, LANES), lambda j, bi, i: (bi, i, j)),
        scratch_shapes=[pltpu.VMEM((2, 2 * kr, GRID_W, LANES), F32), pltpu.VMEM((2, sub, nk), F32),
                        pltpu.VMEM((2, sub, nk), BF16)],
        compiler_params=_cparams(("arbitrary", "arbitrary", "arbitrary")),
        name="na_attn",
    )(q, k, v, kc, vc, rpb_p)


def _outproj_kernel(a_ref, x_ref, mod_ref, g_ref, w_ref, o_ref, z_ref):
    tm = x_ref.shape[1]
    sub = _split_rows(tm)
    for r in range(0, tm, sub):
        rows = slice(r, r + sub)
        x = x_ref[0, rows] + mod_ref[0, 2:3] * _dot(a_ref[0, rows], w_ref[...])
        o_ref[0, rows] = x
        z_ref[0, rows] = _normmod(x, g_ref[...], mod_ref[0, 3:4], mod_ref[0, 4:5]).astype(z_ref.dtype)


def _outproj_call(a, x, mods_l, mod_row, g, w):
    b, n, d = x.shape
    k = a.shape[-1]
    tm = _row_tile(n, 512)
    row = lambda w_: pl.BlockSpec((1, tm, w_), lambda bi, i: (bi, i, 0))
    return pl.pallas_call(
        _outproj_kernel,
        out_shape=(jax.ShapeDtypeStruct((b, n, d), F32), jax.ShapeDtypeStruct((b, n, d), BF16)),
        grid=(b, n // tm),
        in_specs=[row(k), row(d), _mod_spec(d, mod_row),
                  pl.BlockSpec((1, d), lambda bi, i: (0, 0)),
                  pl.BlockSpec((k, d), lambda bi, i: (0, 0))],
        out_specs=(row(d), row(d)),
        compiler_params=_cparams(("arbitrary", "arbitrary")),
        name="outproj",
    )(a, x, mods_l, g, w)


def _ffn_kernel(x_ref, z_ref, zp_ref, zn_ref, mod_ref, wu_ref, cw_ref, wd_ref, *rest, fc, final):
    if final:
        fg_ref, o_ref, z_scr, u_scr, h_scr = rest
    else:
        o_ref, z_scr, u_scr, h_scr = rest
    i = pl.program_id(1)
    nt = pl.num_programs(1)
    tm, d = x_ref.shape[1], x_ref.shape[2]
    hb = FFN_HALO
    nch = wu_ref.shape[0]
    z_scr[hb:hb + tm] = z_ref[0]
    z_scr[0:hb] = jnp.where(i > 0, zp_ref[0], jnp.zeros_like(zp_ref[0]))
    z_scr[hb + tm:] = jnp.where(i < nt - 1, zn_ref[0], jnp.zeros_like(zn_ref[0]))

    def up_proj(j):
        u_scr[j % 2] = _dot(z_scr[...], wu_ref[j])

    def conv_act(j):
        cw = cw_ref[j]
        sub = min(FFN_SUB, tm)
        for sb in range(tm // sub):
            r0 = hb + sb * sub
            cv = cw[3:4] + cw[0:1] * u_scr[j % 2, r0 - 1:r0 - 1 + sub]
            cv = cv + cw[1:2] * u_scr[j % 2, r0:r0 + sub]
            cv = cv + cw[2:3] * u_scr[j % 2, r0 + 1:r0 + 1 + sub]
            hg = 0.5 * cv[:, :fc]
            h = (hg + hg * jnp.tanh(hg)) * cv[:, fc:]
            h_scr[sb * sub:(sb + 1) * sub, j * fc:(j + 1) * fc] = h.astype(BF16)

    up_proj(0)
    for j in range(nch - 1):
        up_proj(j + 1)
        conv_act(j)
    split = (nch - 1) * fc
    acc = _dot(h_scr[:, :split], wd_ref[:split, :])
    conv_act(nch - 1)
    acc = acc + _dot(h_scr[:, split:], wd_ref[split:, :])
    out = x_ref[0] + mod_ref[0, 5:6] * acc
    if final:
        out = _rms(out, fg_ref[...])
    o_ref[0] = out


def _ffn_pack_params(w_up, conv_w, conv_b, w_down, fc):
    d, f2 = w_up.shape
    f = f2 // 2
    nch = f // fc
    wu = jnp.concatenate([w_up[:, :f].reshape(d, nch, fc), w_up[:, f:].reshape(d, nch, fc)], axis=-1)
    wu = wu.transpose(1, 0, 2).astype(BF16)
    cw = jnp.concatenate([conv_w, conv_b[None], jnp.zeros((SUBLANES - 1 - conv_w.shape[0], f2), F32)], axis=0)
    cw = jnp.concatenate([cw[:, :f].reshape(SUBLANES, nch, fc), cw[:, f:].reshape(SUBLANES, nch, fc)], axis=-1)
    cw = cw.transpose(1, 0, 2)
    return wu, cw, w_down.astype(BF16)


def _ffn_call(x, z, mods_l, mod_row, packed, layer, final_g=None):
    b, n, d = x.shape
    wu, cw, wd = packed
    _, nch, _, fc2 = wu.shape
    tm = _row_tile(n, FFN_TM)
    hb = FFN_HALO
    nbh = tm // hb
    once = dict(pipeline_mode=pl.Buffered(1))
    layer_spec = lambda a, **kw: pl.BlockSpec((None,) + a.shape[1:], lambda bi, i: (layer,) + (0,) * (a.ndim - 1), **kw)
    in_specs = [
        pl.BlockSpec((1, tm, d), lambda bi, i: (bi, i, 0)),
        pl.BlockSpec((1, tm, d), lambda bi, i: (bi, i, 0)),
        pl.BlockSpec((1, hb, d), lambda bi, i: (bi, jnp.maximum(i * nbh - 1, 0), 0)),
        pl.BlockSpec((1, hb, d), lambda bi, i: (bi, jnp.minimum((i + 1) * nbh, n // hb - 1), 0)),
        _mod_spec(d, mod_row),
        layer_spec(wu, **once),
        layer_spec(cw),
        layer_spec(wd, **once),
    ]
    args = [x, z, z, z, mods_l, wu, cw, wd]
    if final_g is not None:
        in_specs.append(pl.BlockSpec((1, d), lambda bi, i: (0, 0)))
        args.append(final_g)
    return pl.pallas_call(
        functools.partial(_ffn_kernel, fc=fc2 // 2, final=final_g is not None),
        out_shape=jax.ShapeDtypeStruct((b, n, d), F32),
        grid=(b, n // tm),
        in_specs=in_specs,
        out_specs=pl.BlockSpec((1, tm, d), lambda bi, i: (bi, i, 0)),
        scratch_shapes=[pltpu.VMEM((tm + 2 * hb, d), BF16), pltpu.VMEM((2, tm + 2 * hb, fc2), F32),
                        pltpu.VMEM((tm, nch * fc2 // 2), BF16)],
        compiler_params=_cparams(("arbitrary", "arbitrary")),
        name="ffn",
    )(*args)


def _rg_mixer(xl, xc, mods_l, ctx_row, g, w_in, conv_w, conv_b, wa, ba, wx, bx, lam, need_ctx):
    w = w_in.astype(BF16)
    g_l, xr_l = _rg_proj_call(xl, mods_l, None, g, w)
    g_c, xr_c = _rg_proj_call(xc, mods_l, ctx_row, g, w)
    cw, wbd, gp = _rg_pack_params(conv_w, conv_b, wa, ba, wx, bx, lam)
    m_c, m_l = _rg_scan_call(xr_c, g_c, xr_l, g_l, cw, wbd, gp)
    return m_l, (m_c if need_ctx else None)


def _na_mixer(xl, xc, mods_l, ctx_row, g, w_in, rpb, need_ctx):
    d = xl.shape[-1]
    w = w_in.astype(BF16)
    ql, kl, vl = _qkv_proj_call(xl, mods_l, None, g, w, d)
    qc, kc, vc = _qkv_proj_call(xc, mods_l, ctx_row, g, w, d)
    o_l = _na_attn_call(ql, kl, vl, kc, vc, rpb)
    o_c = _plain_attn_call(qc, [(kc, vc)], LANES) if need_ctx else None
    return o_l, o_c


def _gqa_mixer(xl, xc, mods_l, ctx_row, g, w_in, q_norm, k_norm, need_ctx):
    d = xl.shape[-1]
    nh = d // HEAD_DIM
    nkv = (w_in.shape[1] // HEAD_DIM - nh) // 2
    group = nh // nkv
    assert group % 2 == 0 and q_norm.shape[0] == HEAD_DIM
    dup = lambda wk: jnp.concatenate([wk.reshape(d, nkv, 1, HEAD_DIM)] * 2, axis=2).reshape(d, 2 * nkv * HEAD_DIM)
    nk = 2 * nkv * HEAD_DIM
    w = jnp.concatenate([w_in[:, :d], dup(w_in[:, d:d + nkv * HEAD_DIM]), dup(w_in[:, d + nkv * HEAD_DIM:])],
                        axis=1).astype(BF16)
    nseg = (d + nk) // HEAD_DIM
    assert nseg <= LANES
    seg = jnp.arange(d + nk) // HEAD_DIM
    ind = (seg[:, None] == jnp.arange(LANES)[None, :]).astype(BF16)
    gain = jnp.concatenate([jnp.tile(q_norm, nh), jnp.tile(k_norm, 2 * nkv)])[None]
    qkn = (gain, ind, ind.T)
    ql, kl, vl = _qkv_proj_call(xl, mods_l, None, g, w, nk, rope=_rope_tables(xl.shape[1]), qknorm=qkn)
    qc, kc, vc = _qkv_proj_call(xc, mods_l, ctx_row, g, w, nk, qknorm=qkn)
    qw = group * HEAD_DIM
    o_l = _plain_attn_t_call(ql, [(kl, vl), (kc, vc)], qw)
    o_c = _plain_attn_call(qc, [(kc, vc)], qw) if need_ctx else None
    return o_l, o_c


def _diff_mixer(xl, xc, mods_l, ctx_row, g, w_in, lq1, lk1, lq2, lk2, subln_g, lambda_init, need_ctx):
    d = xl.shape[-1]
    assert lq1.shape[0] == HEAD_DIM and subln_g.shape[0] == LANES
    w = w_in.astype(BF16)
    ql, kl, vl = _qkv_proj_call(xl, mods_l, None, g, w, d, rope=_rope_tables(xl.shape[1]))
    qc, kc, vc = _qkv_proj_call(xc, mods_l, ctx_row, g, w, d)
    lam_params = jnp.concatenate([jnp.stack([lq1, lk1, lq2, lk2]), jnp.zeros((SUBLANES - 4, HEAD_DIM), F32)])
    o_l = _diff_attn_call(ql, [(kl, vl), (kc, vc)], lam_params, subln_g[None], lambda_init)
    o_c = _diff_attn_call(qc, [(kc, vc)], lam_params, subln_g[None], lambda_init) if need_ctx else None
    return o_l, o_c


def kernel(x, c, ctx, c_ctx, mod_w, mod_b, norm1_g, norm2_g, rg_w_in, rg_conv_w, rg_conv_b, rg_wa, rg_ba, rg_wx, rg_bx, rg_lam, rg_w_out, na_w_in, na_rpb, na_w_out, gqa_w_in, gqa_q_norm, gqa_k_norm, gqa_w_out, diff_w_in, diff_lq1, diff_lk1, diff_lq2, diff_lk2, diff_subln_g, diff_w_out, ffn_w_up, ffn_conv_w, ffn_conv_b, ffn_w_down, final_g):
    bsz, _, d = x.shape
    depth = mod_w.shape[0]
    assert d % LANES == 0
    ctx_row = bsz
    nrow = -(-(bsz + 1) // SUBLANES) * SUBLANES
    cc = jnp.concatenate([c, c_ctx[None], jnp.zeros((nrow - bsz - 1, d), F32)], axis=0)
    mods = _mods_call(cc, mod_w, mod_b).reshape(depth, nrow, 6, d)
    f = ffn_w_down.shape[1]
    fc = next(t for t in (256, 128) if f % t == 0)
    packed = jax.vmap(functools.partial(_ffn_pack_params, fc=fc))(ffn_w_up, ffn_conv_w, ffn_conv_b, ffn_w_down)
    xl, xc = x, ctx
    for l in range(depth):
        m, j = l % N_MIXERS, l // N_MIXERS
        need_ctx = l < depth - 1
        mods_l = mods[l]
        g1, g2 = norm1_g[l][None], norm2_g[l][None]
        if m == 0:
            a_l, a_c = _rg_mixer(xl, xc, mods_l, ctx_row, g1, rg_w_in[j], rg_conv_w[j], rg_conv_b[j], rg_wa[j],
                                 rg_ba[j], rg_wx[j], rg_bx[j], rg_lam[j], need_ctx)
            w_out = rg_w_out[j]
        elif m == 1:
            a_l, a_c = _na_mixer(xl, xc, mods_l, ctx_row, g1, na_w_in[j], na_rpb[j], need_ctx)
            w_out = na_w_out[j]
        elif m == 2:
            a_l, a_c = _gqa_mixer(xl, xc, mods_l, ctx_row, g1, gqa_w_in[j], gqa_q_norm[j], gqa_k_norm[j], need_ctx)
            w_out = gqa_w_out[j]
        else:
            lambda_init = 0.8 - 0.6 * math.exp(-0.3 * l)
            a_l, a_c = _diff_mixer(xl, xc, mods_l, ctx_row, g1, diff_w_in[j], diff_lq1[j], diff_lk1[j],
                                   diff_lq2[j], diff_lk2[j], diff_subln_g[j], lambda_init, need_ctx)
            w_out = diff_w_out[j]
        w_out = w_out.astype(BF16)
        last = l == depth - 1
        xl, zl = _outproj_call(a_l, xl, mods_l, None, g2, w_out)
        xl = _ffn_call(xl, zl, mods_l, None, packed, l, final_g[None] if last else None)
        if need_ctx:
            xc, zc = _outproj_call(a_c, xc, mods_l, ctx_row, g2, w_out)
            xc = _ffn_call(xc, zc, mods_l, ctx_row, packed, l)
    return xl
```

```python
import functools
import math

import jax
import jax.numpy as jnp
from jax import lax
from jax.experimental import pallas as pl
from jax.experimental.pallas import tpu as pltpu

F32 = jnp.float32
BF16 = jnp.bfloat16
EPS = 1e-6
ROPE_THETA = 10000.0
GRID_W = 64
RG_C = 8.0
RG_CONV_LEFT = 2
N_MIXERS = 4
HEAD_DIM = 64
LOG2E = 1.4426950408889634
Q_SCALE = HEAD_DIM ** -0.5 * LOG2E
LANES = 128
SUBLANES = 8
NEG = -1e30
FFN_HALO = 16
FFN_SUB = 512
FFN_TM = 512
PROJ_SPLIT = 2
PROJ_MIN_ROWS = 256
ATT_SUB = 256
ATT_RB = 32
DIFF_RB = 16
KEY_SLAB = 32
VMEM_LIMIT_BYTES = 56 * 1024 * 1024

_NT = (((1,), (1,)), ((), ()))


def _cparams(sem):
    return pltpu.CompilerParams(dimension_semantics=sem, vmem_limit_bytes=VMEM_LIMIT_BYTES)


def _dot(a, b):
    return jnp.dot(a, b, preferred_element_type=F32)


def _dot_nt(a, b):
    return lax.dot_general(a, b, _NT, preferred_element_type=F32)


def _silu(x):
    return x / (1.0 + jnp.exp(-x))


def _gelu_tanh(x):
    cdf = 0.5 * (1.0 + jnp.tanh(math.sqrt(2.0 / math.pi) * (x + 0.044715 * (x * x * x))))
    return x * cdf


def _rms(x, g):
    y = x * lax.rsqrt(jnp.mean(x * x, axis=-1, keepdims=True) + EPS)
    return y * g


def _normmod(x, g, shift, scale):
    return _rms(x, g) * (1.0 + scale) + shift


def _mods_kernel(cc_ref, w_ref, b_ref, o_ref):
    a = _silu(cc_ref[...]).astype(BF16)
    o_ref[0] = _dot(a, w_ref[0].astype(BF16)) + b_ref[0]


def _mods_call(cc, mod_w, mod_b):
    depth, d, n = mod_w.shape
    r = cc.shape[0]
    tn = n // 4
    return pl.pallas_call(
        _mods_kernel,
        out_shape=jax.ShapeDtypeStruct((depth, r, n), F32),
        grid=(depth, n // tn),
        in_specs=[
            pl.BlockSpec((r, d), lambda l, j: (0, 0)),
            pl.BlockSpec((1, d, tn), lambda l, j: (l, 0, j)),
            pl.BlockSpec((1, 1, tn), lambda l, j: (l, 0, j)),
        ],
        out_specs=pl.BlockSpec((1, r, tn), lambda l, j: (l, 0, j)),
        compiler_params=_cparams(("arbitrary", "arbitrary")),
        name="mods",
    )(cc, mod_w, mod_b.reshape(depth, 1, n))


def _mod_spec(d, row):
    if row is None:
        return pl.BlockSpec((1, 6, d), lambda b, *_: (b, 0, 0))
    return pl.BlockSpec((1, 6, d), lambda b, *_: (row, 0, 0))


def _split_rows(tm):
    sub = tm // PROJ_SPLIT
    return sub if (tm % PROJ_SPLIT == 0 and sub >= PROJ_MIN_ROWS) else tm


def _row_tile(n, pref):
    t = min(n, pref)
    assert n % t == 0
    return t


def _rope_tables(n_tok):
    t = jnp.arange(n_tok)
    row = (t // GRID_W).astype(F32)
    col = (t % GRID_W).astype(F32)
    n = HEAD_DIM // 4
    inv = ROPE_THETA ** (-jnp.arange(n, dtype=F32) / n)
    ang = jnp.concatenate([row[:, None] * inv, col[:, None] * inv], axis=-1)
    cos, sin = jnp.cos(ang), jnp.sin(ang)
    reps = LANES // HEAD_DIM
    cos_t = jnp.tile(jnp.concatenate([cos, cos], axis=-1), (1, reps))
    sin_t = jnp.tile(jnp.concatenate([-sin, sin], axis=-1), (1, reps))
    return cos_t, sin_t


def _swap_halves(x):
    lane = lax.broadcasted_iota(jnp.int32, x.shape, 1)
    lo = (lane & (HEAD_DIM // 2)) == 0
    return jnp.where(lo, pltpu.roll(x, LANES - HEAD_DIM // 2, 1), pltpu.roll(x, HEAD_DIM // 2, 1))


def _rope_cols(u, cos_t, sin_t):
    outs = []
    for c in range(u.shape[1] // LANES):
        xc = u[:, c * LANES:(c + 1) * LANES]
        outs.append(xc * cos_t + _swap_halves(xc) * sin_t)
    return jnp.concatenate(outs, axis=1)


def _rg_proj_kernel(x_ref, mod_ref, g_ref, w_ref, go_ref, xo_ref, *, dr):
    tm = x_ref.shape[1]
    sub = _split_rows(tm)
    for r in range(0, tm, sub):
        rows = slice(r, r + sub)
        z = _normmod(x_ref[0, rows], g_ref[...], mod_ref[0, 0:1], mod_ref[0, 1:2]).astype(BF16)
        u = _dot(z, w_ref[...])
        go_ref[0, rows] = _gelu_tanh(u[:, :dr])
        xo_ref[0, rows] = u[:, dr:]


def _rg_proj_call(x, mods_l, mod_row, g, w):
    b, n, d = x.shape
    dr = w.shape[1] // 2
    tm = _row_tile(n, 512)
    return pl.pallas_call(
        functools.partial(_rg_proj_kernel, dr=dr),
        out_shape=(jax.ShapeDtypeStruct((b, n, dr), F32), jax.ShapeDtypeStruct((b, n, dr), F32)),
        grid=(b, n // tm),
        in_specs=[
            pl.BlockSpec((1, tm, d), lambda bi, i: (bi, i, 0)),
            _mod_spec(d, mod_row),
            pl.BlockSpec((1, d), lambda bi, i: (0, 0)),
            pl.BlockSpec((d, 2 * dr), lambda bi, i: (0, 0)),
        ],
        out_specs=(pl.BlockSpec((1, tm, dr), lambda bi, i: (bi, i, 0)),
                   pl.BlockSpec((1, tm, dr), lambda bi, i: (bi, i, 0))),
        compiler_params=_cparams(("arbitrary", "arbitrary")),
        name="rg_proj",
    )(x, mods_l, g, w)


def _qkv_proj_kernel(x_ref, mod_ref, g_ref, w_ref, *rest, d, rope, qknorm):
    rest = list(rest)
    if rope:
        cos_ref, sin_ref = rest.pop(0), rest.pop(0)
    if qknorm:
        gain_ref, ind_ref, indt_ref = rest.pop(0), rest.pop(0), rest.pop(0)
    q_ref, k_ref, v_ref = rest
    nk = k_ref.shape[-1]
    tm = x_ref.shape[1]
    sub = _split_rows(tm)
    for r in range(0, tm, sub):
        rows = slice(r, r + sub)
        z = _normmod(x_ref[0, rows], g_ref[...], mod_ref[0, 0:1], mod_ref[0, 1:2]).astype(BF16)
        u = _dot(z, w_ref[...])
        qk = u[:, :d + nk]
        if qknorm:
            x2 = qk * qk
            hi = x2.astype(BF16)
            lo = (x2 - hi.astype(F32)).astype(BF16)
            ssq = _dot(hi, ind_ref[...]) + _dot(lo, ind_ref[...])
            rs = lax.rsqrt(ssq * (1.0 / HEAD_DIM) + EPS)
            rhi = rs.astype(BF16)
            rlo = (rs - rhi.astype(F32)).astype(BF16)
            rb = _dot(rhi, indt_ref[...]) + _dot(rlo, indt_ref[...])
            qk = (qk * rb) * gain_ref[...]
        if rope:
            qk = _rope_cols(qk, cos_ref[rows], sin_ref[rows])
        q_ref[0, rows] = (qk[:, :d] * Q_SCALE).astype(q_ref.dtype)
        k_ref[0, rows] = qk[:, d:].astype(k_ref.dtype)
        v_ref[0, rows] = u[:, d + nk:].astype(v_ref.dtype)


def _qkv_proj_call(x, mods_l, mod_row, g, w, nk, *, rope=None, qknorm=None):
    b, n, d = x.shape
    nv = w.shape[1] - d - nk
    tm = _row_tile(n, 512)
    const = lambda bi, i: (0, 0)
    in_specs = [
        pl.BlockSpec((1, tm, d), lambda bi, i: (bi, i, 0)),
        _mod_spec(d, mod_row),
        pl.BlockSpec((1, d), const),
        pl.BlockSpec(w.shape, const),
    ]
    args = [x, mods_l, g, w]
    if rope is not None:
        in_specs += [pl.BlockSpec((tm, LANES), lambda bi, i: (i, 0))] * 2
        args += list(rope)
    if qknorm is not None:
        in_specs += [pl.BlockSpec(a.shape, const) for a in qknorm]
        args += list(qknorm)
    out_spec = lambda w_: pl.BlockSpec((1, tm, w_), lambda bi, i: (bi, i, 0))
    return pl.pallas_call(
        functools.partial(_qkv_proj_kernel, d=d, rope=rope is not None, qknorm=qknorm is not None),
        out_shape=(jax.ShapeDtypeStruct((b, n, d), BF16), jax.ShapeDtypeStruct((b, n, nk), BF16),
                   jax.ShapeDtypeStruct((b, n, nv), BF16)),
        grid=(b, n // tm),
        in_specs=in_specs,
        out_specs=(out_spec(d), out_spec(nk), out_spec(nv)),
        compiler_params=_cparams(("arbitrary", "arbitrary")),
        name="qkv_proj",
    )(*args)


def _rg_conv(xs_scr, x, prev, nxt, cw):
    t = x.shape[0]
    h = SUBLANES
    xs_scr[0:h] = prev
    xs_scr[h:h + t] = x
    xs_scr[h + t:2 * h + t] = nxt
    y = cw[4:5] + cw[0:1] * xs_scr[h - 2:h - 2 + t]
    y = y + cw[1:2] * xs_scr[h - 1:h - 1 + t]
    y = y + cw[2:3] * x
    y = y + cw[3:4] * xs_scr[h + 1:h + 1 + t]
    return y


def _sigmoid(x):
    return 0.5 * jnp.tanh(0.5 * x) + 0.5


def _rg_gates(xc, w, ba, bx, lam):
    gw = xc.shape[1]
    y = _dot(xc.astype(BF16), w)
    r = _sigmoid(y[:, :gw] + ba)
    i = _sigmoid(y[:, gw:] + bx)
    nl = -lam
    softplus = jnp.maximum(nl, 0.0) + jnp.log1p(jnp.exp(-jnp.abs(nl)))
    log_a = r * (-RG_C * softplus)
    a = jnp.exp(log_a)
    one_minus_a2 = -jnp.tanh(log_a) * (a * a + 1.0)
    root = jnp.where(one_minus_a2 > 0.0, one_minus_a2 * lax.rsqrt(one_minus_a2), 0.0)
    return a, root * (i * xc)


def _rg_scan_tile(a, b, carry, reverse):
    t, gw = a.shape
    nck = t // SUBLANES
    a = a.reshape(nck, SUBLANES, gw)
    b = b.reshape(nck, SUBLANES, gw)
    rmod = lax.broadcasted_iota(jnp.int32, (1, SUBLANES, 1), 1)
    for s in (1, 2, 4):
        ok = (rmod < SUBLANES - s) if reverse else (rmod >= s)
        shift = SUBLANES - s if reverse else s
        ash, bsh = pltpu.roll(a, shift, 1), pltpu.roll(b, shift, 1)
        b = a * jnp.where(ok, bsh, 0.0) + b
        a = a * jnp.where(ok, ash, 1.0)
    hs = [None] * nck
    for ck in (range(nck - 1, -1, -1) if reverse else range(nck)):
        hc = b[ck] + a[ck] * carry
        carry = hc[0:1] if reverse else hc[SUBLANES - 1:SUBLANES]
        hs[ck] = hc
    return jnp.concatenate(hs, axis=0), carry


def _rg_scan_kernel(xc_ref, gc_ref, xl_ref, xp_ref, xn_ref, gl_ref, cw_ref, w_ref, gp_ref,
                    mc_ref, ml_ref, hf_scr, xv_scr, car_scr, xs_scr, *, nt):
    s = pl.program_id(2)
    gw = xl_ref.shape[-1]
    tl = xl_ref.shape[1]
    cw = cw_ref[0]
    gp = gp_ref[0]

    def gates(xconv, d):
        w = w_ref[0, :, d * 2 * gw:(d + 1) * 2 * gw]
        return _rg_gates(xconv, w, gp[3 * d:3 * d + 1], gp[3 * d + 1:3 * d + 2], gp[3 * d + 2:3 * d + 3])

    @pl.when(s == 0)
    def _ctx():
        zeros8 = jnp.zeros((SUBLANES, gw), F32)
        xconv = _rg_conv(xs_scr, xc_ref[0], zeros8, zeros8, cw)
        zero = jnp.zeros((1, gw), F32)
        a0, b0 = gates(xconv, 0)
        hf, cf = _rg_scan_tile(a0, b0, zero, False)
        a1, b1 = gates(xconv, 1)
        hb, cb = _rg_scan_tile(a1, b1, zero, True)
        car_scr[0:1] = cf
        car_scr[1:2] = cb
        mc_ref[0] = (gc_ref[0] * (hf + hb)).astype(mc_ref.dtype)

    def lat_conv(t):
        prev = xp_ref[0] * (t > 0).astype(F32)
        nxt = xn_ref[0] * (t < nt - 1).astype(F32)
        return _rg_conv(xs_scr, xl_ref[0], prev, nxt, cw)

    @pl.when((s >= 1) & (s <= nt))
    def _fwd():
        t = s - 1
        xconv = lat_conv(t)
        xv_scr[pl.ds(pl.multiple_of(t * tl, tl), tl), :] = xconv
        a0, b0 = gates(xconv, 0)
        h, c = _rg_scan_tile(a0, b0, car_scr[0:1], False)
        car_scr[0:1] = c
        hf_scr[pl.ds(pl.multiple_of(t * tl, tl), tl), :] = h

    @pl.when(s > nt)
    def _bwd():
        t = 2 * nt - s
        a1, b1 = gates(xv_scr[pl.ds(pl.multiple_of(t * tl, tl), tl), :], 1)
        h, c = _rg_scan_tile(a1, b1, car_scr[1:2], True)
        car_scr[1:2] = c
        hf = hf_scr[pl.ds(pl.multiple_of(t * tl, tl), tl), :]
        ml_ref[0] = (gl_ref[0] * (hf + h)).astype(ml_ref.dtype)


def _rg_scan_call(xr_c, g_c, xr_l, g_l, cw, wbd, gp):
    b, s_len, dr = xr_l.shape
    c_len = xr_c.shape[1]
    ng, gw = wbd.shape[0], wbd.shape[1]
    tl = _row_tile(s_len, 512)
    nt = s_len // tl
    nb8 = tl // SUBLANES

    def tile(s):
        return jnp.clip(s - 1, 0, nt - 1)

    def tile_bwd(s):
        return jnp.where(s <= nt, nt - 1, 2 * nt - s)

    return pl.pallas_call(
        functools.partial(_rg_scan_kernel, nt=nt),
        out_shape=(jax.ShapeDtypeStruct((b, c_len, dr), BF16), jax.ShapeDtypeStruct((b, s_len, dr), BF16)),
        grid=(b, ng, 2 * nt + 1),
        in_specs=[
            pl.BlockSpec((1, c_len, gw), lambda bi, h, s: (bi, 0, h)),
            pl.BlockSpec((1, c_len, gw), lambda bi, h, s: (bi, 0, h)),
            pl.BlockSpec((1, tl, gw), lambda bi, h, s: (bi, tile(s), h)),
            pl.BlockSpec((1, SUBLANES, gw), lambda bi, h, s: (bi, jnp.maximum(tile(s) * nb8 - 1, 0), h)),
            pl.BlockSpec((1, SUBLANES, gw),
                         lambda bi, h, s: (bi, jnp.minimum((tile(s) + 1) * nb8, s_len // SUBLANES - 1), h)),
            pl.BlockSpec((1, tl, gw), lambda bi, h, s: (bi, tile_bwd(s), h)),
            pl.BlockSpec((1, SUBLANES, gw), lambda bi, h, s: (h, 0, 0)),
            pl.BlockSpec((1, gw, 4 * gw), lambda bi, h, s: (h, 0, 0)),
            pl.BlockSpec((1, SUBLANES, gw), lambda bi, h, s: (h, 0, 0)),
        ],
        out_specs=(pl.BlockSpec((1, c_len, gw), lambda bi, h, s: (bi, 0, h)),
                   pl.BlockSpec((1, tl, gw), lambda bi, h, s: (bi, tile_bwd(s), h))),
        scratch_shapes=[pltpu.VMEM((s_len, gw), F32), pltpu.VMEM((s_len, gw), F32), pltpu.VMEM((SUBLANES, gw), F32),
                        pltpu.VMEM((max(tl, c_len) + 2 * SUBLANES, gw), F32)],
        compiler_params=_cparams(("arbitrary", "arbitrary", "arbitrary")),
        name="rg_scan",
    )(xr_c, g_c, xr_l, xr_l, xr_l, g_l, cw, wbd, gp)


def _rg_pack_params(conv_w, conv_b, wa, ba, wx, bx, lam):
    nblk, bw = wa.shape[1], wa.shape[2]
    dr = nblk * bw
    per = next(p for p in range(1, nblk + 1) if nblk % p == 0 and (p * bw) % LANES == 0)
    ng, gw = nblk // per, per * bw

    def dense(w):
        w = w.reshape(ng, per, bw, bw)
        eye = jnp.eye(per, dtype=w.dtype)
        return jnp.einsum('gpjk,pq->gpjqk', w, eye).reshape(ng, gw, gw)

    wbd = jnp.concatenate([dense(wa[0]), dense(wx[0]), dense(wa[1]), dense(wx[1])], axis=-1).astype(BF16)
    cw = jnp.concatenate([conv_w, conv_b[None], jnp.zeros((SUBLANES - 1 - conv_w.shape[0], dr), F32)], axis=0)
    cw = cw.reshape(SUBLANES, ng, gw).transpose(1, 0, 2)
    gp = jnp.stack([ba[0], bx[0], lam[0], ba[1], bx[1], lam[1], jnp.zeros_like(lam[0]), jnp.zeros_like(lam[0])])
    gp = gp.reshape(SUBLANES, ng, gw).transpose(1, 0, 2)
    return cw, wbd, gp


def _pair_masks(shape):
    lane = lax.broadcasted_iota(jnp.int32, shape, 1)
    return lane < HEAD_DIM


def _half_q(q, hh):
    first = _pair_masks(q.shape)
    return jnp.where(first if hh == 0 else jnp.logical_not(first), q, jnp.zeros_like(q))


def _scores_to_scratch(qh, k_refs, s_scr, slot):
    off = 0
    for k_ref in k_refs:
        n = k_ref.shape[1]
        s_scr[slot, :, off:off + n] = _dot_nt(qh, k_ref[0])
        off += n


def _exp_rows(s):
    e = jnp.exp2(s - s.max(axis=-1, keepdims=True))
    return e, e.sum(axis=-1, keepdims=True)


def _softmax_to_scratch(s_scr, p_scr, slot, rows):
    linv = []
    for r in range(0, rows, ATT_RB):
        e, l = _exp_rows(s_scr[slot, r:r + ATT_RB, :])
        p_scr[slot, r:r + ATT_RB, :] = e.astype(BF16)
        linv.append(1.0 / l)
    return jnp.concatenate(linv, axis=0)


def _pv_from_scratch(p_scr, slot, v_refs):
    off, o = 0, None
    for v_ref in v_refs:
        n = v_ref.shape[1]
        t = _dot(p_scr[slot, :, off:off + n], v_ref[0])
        o = t if o is None else o + t
        off += n
    return o


def _plain_attn_kernel(q_ref, *refs, nsrc):
    k_refs, v_refs = refs[0:2 * nsrc:2], refs[1:2 * nsrc:2]
    o_ref, s_scr, p_scr = refs[2 * nsrc:]
    tq, qw = q_ref.shape[1], q_ref.shape[2]
    sub = s_scr.shape[1]
    units = [(r, c, hh) for r in range(0, tq, sub) for c in range(0, qw, LANES) for hh in range(2)]

    def scores(u):
        r, c, hh = units[u]
        _scores_to_scratch(_half_q(q_ref[0, r:r + sub, c:c + LANES], hh), k_refs, s_scr, u % 2)

    scores(0)
    outs = [None, None]
    for u, (r, c, hh) in enumerate(units):
        if u + 1 < len(units):
            scores(u + 1)
        linv = _softmax_to_scratch(s_scr, p_scr, u % 2, sub)
        outs[hh] = _pv_from_scratch(p_scr, u % 2, v_refs) * linv
        if hh == 1:
            o_ref[0, r:r + sub, c:c + LANES] = jnp.where(_pair_masks((sub, LANES)), outs[0], outs[1]).astype(o_ref.dtype)


def _plain_attn_t_kernel(q_ref, *refs, nsrc):
    k_refs, v_refs = refs[0:2 * nsrc:2], refs[1:2 * nsrc:2]
    o_ref = refs[2 * nsrc]
    vt_scrs = refs[2 * nsrc + 1:3 * nsrc + 1]
    s_scr, p_scr = refs[3 * nsrc + 1:]
    tq, qw = q_ref.shape[1], q_ref.shape[2]
    sub = s_scr.shape[2]

    @pl.when(pl.program_id(2) == 0)
    def _transpose_values():
        for v_ref, vt_scr in zip(v_refs, vt_scrs):
            vt_scr[...] = v_ref[0].astype(F32).T.astype(BF16)

    units = [(r, c, hh) for r in range(0, tq, sub) for c in range(0, qw, LANES) for hh in range(2)]

    def scores(u):
        r, c, hh = units[u]
        qh = _half_q(q_ref[0, r:r + sub, c:c + LANES], hh)
        off = 0
        for k_ref in k_refs:
            n = k_ref.shape[1]
            s_scr[u % ns, off:off + n, :] = _dot_nt(k_ref[0], qh)
            off += n

    ns = s_scr.shape[0]
    for u in range(min(ns - 1, len(units))):
        scores(u)
    outs = [None, None]
    for u, (r, c, hh) in enumerate(units):
        if u + ns - 1 < len(units):
            scores(u + ns - 1)
        slabs = [slice(i, i + KEY_SLAB) for i in range(0, s_scr.shape[1], KEY_SLAB)]
        macc = s_scr[u % ns, slabs[0], :]
        for sl in slabs[1:]:
            macc = jnp.maximum(macc, s_scr[u % ns, sl, :])
        m = jnp.max(macc, axis=0, keepdims=True)
        lacc = None
        for sl in slabs:
            e = jnp.exp2(s_scr[u % ns, sl, :] - m)
            p_scr[u % 2, sl, :] = e.astype(BF16)
            lacc = e if lacc is None else lacc + e
        linv = 1.0 / jnp.sum(lacc, axis=0, keepdims=True)
        off, ot = 0, None
        for vt_scr in vt_scrs:
            n = vt_scr.shape[1]
            t = _dot(vt_scr[hh * HEAD_DIM:(hh + 1) * HEAD_DIM, :], p_scr[u % 2, off:off + n, :])
            ot = t if ot is None else ot + t
            off += n
        outs[hh] = ot * linv
        if hh == 1:
            o_ref[0, r:r + sub, c:c + LANES] = jnp.concatenate(outs, axis=0).T.astype(o_ref.dtype)


def _plain_attn_t_call(q, kvs, qw):
    b, lq, d = q.shape
    tq = _row_tile(lq, 4 * ATT_SUB)
    sub = min(tq, ATT_SUB)
    nk = sum(k.shape[1] for k, _ in kvs)
    in_specs = [pl.BlockSpec((1, tq, qw), lambda bi, j, i: (bi, i, j))]
    args = [q]
    for k, v in kvs:
        spec = pl.BlockSpec((1, k.shape[1], LANES), lambda bi, j, i: (bi, 0, j))
        in_specs += [spec, spec]
        args += [k, v]
    return pl.pallas_call(
        functools.partial(_plain_attn_t_kernel, nsrc=len(kvs)),
        out_shape=jax.ShapeDtypeStruct((b, lq, d), BF16),
        grid=(b, d // qw, lq // tq),
        in_specs=in_specs,
        out_specs=pl.BlockSpec((1, tq, qw), lambda bi, j, i: (bi, i, j)),
        scratch_shapes=([pltpu.VMEM((LANES, k.shape[1]), BF16) for k, _ in kvs]
                        + [pltpu.VMEM((3, nk, sub), F32), pltpu.VMEM((2, nk, sub), BF16)]),
        compiler_params=_cparams(("arbitrary", "arbitrary", "arbitrary")),
        name="plain_attn_t",
    )(*args)


def _plain_attn_call(q, kvs, qw):
    b, lq, d = q.shape
    tq = _row_tile(lq, 2 * ATT_SUB)
    sub = min(tq, ATT_SUB)
    nk = sum(k.shape[1] for k, _ in kvs)
    in_specs = [pl.BlockSpec((1, tq, qw), lambda bi, j, i: (bi, i, j))]
    args = [q]
    for k, v in kvs:
        spec = pl.BlockSpec((1, k.shape[1], LANES), lambda bi, j, i: (bi, 0, j))
        in_specs += [spec, spec]
        args += [k, v]
    return pl.pallas_call(
        functools.partial(_plain_attn_kernel, nsrc=len(kvs)),
        out_shape=jax.ShapeDtypeStruct((b, lq, d), BF16),
        grid=(b, d // qw, lq // tq),
        in_specs=in_specs,
        out_specs=pl.BlockSpec((1, tq, qw), lambda bi, j, i: (bi, i, j)),
        scratch_shapes=[pltpu.VMEM((2, sub, nk), F32), pltpu.VMEM((2, sub, nk), BF16)],
        compiler_params=_cparams(("arbitrary", "arbitrary", "arbitrary")),
        name="plain_attn",
    )(*args)


def _diff_attn_kernel(q_ref, lam_ref, g_ref, *refs, nsrc, lambda_init):
    k_refs, v_refs = refs[0:2 * nsrc:2], refs[1:2 * nsrc:2]
    o_ref, s_scr, a_scr = refs[2 * nsrc:]
    lp = lam_ref[...]
    lam = (jnp.exp(jnp.sum(lp[0:1] * lp[1:2], axis=-1, keepdims=True))
           - jnp.exp(jnp.sum(lp[2:3] * lp[3:4], axis=-1, keepdims=True)) + lambda_init)
    tq = q_ref.shape[1]
    sub = s_scr.shape[1]
    units = list(range(0, tq, sub))

    def scores(u):
        q = q_ref[0, units[u]:units[u] + sub]
        for hh in range(2):
            _scores_to_scratch(_half_q(q, hh), k_refs, s_scr, 2 * (u % 2) + hh)

    def combine(u):
        linv = []
        for r in range(0, sub, DIFF_RB):
            e0, l0 = _exp_rows(s_scr[2 * (u % 2), r:r + DIFF_RB, :])
            e1, l1 = _exp_rows(s_scr[2 * (u % 2) + 1, r:r + DIFF_RB, :])
            a_scr[u % 2, r:r + DIFF_RB, :] = (e0 - e1 * (lam * l0 / l1)).astype(BF16)
            linv.append(1.0 / l0)
        return jnp.concatenate(linv, axis=0)

    scores(0)
    for u, r in enumerate(units):
        if u + 1 < len(units):
            scores(u + 1)
        linv = combine(u)
        o = _pv_from_scratch(a_scr, u % 2, v_refs) * linv
        o_ref[0, r:r + sub] = (_rms(o, g_ref[...]) * (1.0 - lambda_init)).astype(o_ref.dtype)


def _diff_attn_call(q, kvs, lam_params, subln_g, lambda_init):
    b, lq, d = q.shape
    tq = _row_tile(lq, 8 * ATT_SUB)
    sub = min(tq, ATT_SUB)
    nk = sum(k.shape[1] for k, _ in kvs)
    const = lambda bi, j, i: (0, 0)
    in_specs = [pl.BlockSpec((1, tq, LANES), lambda bi, j, i: (bi, i, j)),
                pl.BlockSpec(lam_params.shape, const), pl.BlockSpec(subln_g.shape, const)]
    args = [q, lam_params, subln_g]
    for k, v in kvs:
        spec = pl.BlockSpec((1, k.shape[1], LANES), lambda bi, j, i: (bi, 0, j))
        in_specs += [spec, spec]
        args += [k, v]
    return pl.pallas_call(
        functools.partial(_diff_attn_kernel, nsrc=len(kvs), lambda_init=lambda_init),
        out_shape=jax.ShapeDtypeStruct((b, lq, d), BF16),
        grid=(b, d // LANES, lq // tq),
        in_specs=in_specs,
        out_specs=pl.BlockSpec((1, tq, LANES), lambda bi, j, i: (bi, i, j)),
        scratch_shapes=[pltpu.VMEM((4, sub, nk), F32), pltpu.VMEM((2, sub, nk), BF16)],
        compiler_params=_cparams(("arbitrary", "arbitrary", "arbitrary")),
        name="diff_attn",
    )(*args)


def _na_attn_kernel(q_ref, k_ref, v_ref, kc_ref, vc_ref, rpb_ref, o_ref, t2_scr, s_scr, p_scr, *, rows, kr, kcw, nq):
    w = GRID_W
    nwin = nq + kr
    ntab = 2 * kr
    nkw = nwin * w
    sub = nq * w
    nsub = q_ref.shape[1] // sub
    i = pl.program_id(2)

    @pl.when((i == 0) & (pl.program_id(1) == 0))
    def _build_bias():
        c_io = lax.broadcasted_iota(jnp.int32, (w, LANES), 0)
        cp_io = lax.broadcasted_iota(jnp.int32, (w, LANES), 1)
        cs = jnp.clip(c_io - kcw // 2, 0, w - kcw)
        colok = (cp_io >= cs) & (cp_io < cs + kcw)
        neg = jnp.full((w, LANES), NEG, F32)
        for hh in range(2):
            r = pltpu.roll(rpb_ref[hh] * LOG2E, LANES - (kcw - 1), 1)
            ts = []
            for e in range(ntab - 1):
                t = jnp.broadcast_to(r[e:e + 1], (w, LANES))
                t = pltpu.roll(t, 0, 1, stride=1, stride_axis=0)
                ts.append(jnp.where(colok, t, NEG))
            for e in range(ntab):
                lo = ts[e - 1] if e >= 1 else neg
                hi = ts[e] if e < ntab - 1 else neg
                t2_scr[hh, e] = jnp.where(cp_io < w, lo, pltpu.roll(hi, w, 1))

    units = [(t, hh) for t in range(nsub) for hh in range(2)]

    def window(t):
        r0 = (i * nsub + t) * nq
        ks = jnp.clip(r0 - kr // 2, 0, rows - nwin)
        return r0, ks, pl.ds(pl.multiple_of(ks * w, w), nkw)

    def scores(u):
        t, hh = units[u]
        _, _, win = window(t)
        qh = _half_q(q_ref[0, t * sub:(t + 1) * sub], hh)
        s_scr[u % 2, :, :nkw] = _dot_nt(qh, k_ref[0, win, :])
        s_scr[u % 2, :, nkw:] = _dot_nt(qh, kc_ref[0])

    def softmax(u):
        t, hh = units[u]
        r0, ks, _ = window(t)
        key_row = ks + jnp.right_shift(lax.broadcasted_iota(jnp.int32, (1, nkw), 1), int(math.log2(w)))
        linv = []
        for qi in range(nq):
            r = r0 + qi
            rs = jnp.clip(r - kr // 2, 0, rows - kr)
            rowok = (key_row >= rs) & (key_row < rs + kr)
            idx = [jnp.clip(ks + 2 * p - r + kr, 0, ntab - 1) for p in range(nwin // 2)]
            for rb in range(0, w, ATT_RB):
                row = qi * w + rb
                bias = jnp.concatenate([t2_scr[hh, e, rb:rb + ATT_RB, :] for e in idx], axis=1)
                sw = jnp.where(rowok, s_scr[u % 2, row:row + ATT_RB, :nkw] + bias, NEG)
                sc = s_scr[u % 2, row:row + ATT_RB, nkw:]
                m = jnp.maximum(sw.max(axis=-1, keepdims=True), sc.max(axis=-1, keepdims=True))
                ew, ec = jnp.exp2(sw - m), jnp.exp2(sc - m)
                p_scr[u % 2, row:row + ATT_RB, :nkw] = ew.astype(BF16)
                p_scr[u % 2, row:row + ATT_RB, nkw:] = ec.astype(BF16)
                linv.append(1.0 / (ew.sum(axis=-1, keepdims=True) + ec.sum(axis=-1, keepdims=True)))
        return jnp.concatenate(linv, axis=0)

    scores(0)
    outs = [None, None]
    for u, (t, hh) in enumerate(units):
        if u + 1 < len(units):
            scores(u + 1)
        linv = softmax(u)
        _, _, win = window(t)
        o = _dot(p_scr[u % 2, :, :nkw], v_ref[0, win, :]) + _dot(p_scr[u % 2, :, nkw:], vc_ref[0])
        outs[hh] = o * linv
        if hh == 1:
            o_ref[0, t * sub:(t + 1) * sub] = jnp.where(_pair_masks((sub, LANES)), outs[0], outs[1]).astype(o_ref.dtype)


def _na_attn_call(q, k, v, kc, vc, rpb):
    b, s_len, d = q.shape
    c_len = kc.shape[1]
    nh, ndr, ndc = rpb.shape
    kr, kcw = (ndr + 1) // 2, (ndc + 1) // 2
    rows = s_len // GRID_W
    nq = 4
    assert GRID_W == HEAD_DIM and rows % nq == 0 and rows >= nq + kr and (nq + kr) % 2 == 0
    assert ndc <= LANES and 2 * kr <= 2 * SUBLANES
    rpb_p = jnp.full((nh, 2 * SUBLANES, LANES), NEG, F32).at[:, :ndr, :ndc].set(rpb)
    sub = nq * GRID_W
    tq = _row_tile(s_len, 8 * sub)
    nk = (nq + kr) * GRID_W + c_len
    full = lambda n: pl.BlockSpec((1, n, LANES), lambda j, bi, i: (bi, 0, j))
    return pl.pallas_call(
        functools.partial(_na_attn_kernel, rows=rows, kr=kr, kcw=kcw, nq=nq),
        out_shape=jax.ShapeDtypeStruct((b, s_len, d), BF16),
        grid=(d // LANES, b, s_len // tq),
        in_specs=[pl.BlockSpec((1, tq, LANES), lambda j, bi, i: (bi, i, j)),
                  full(s_len), full(s_len), full(c_len), full(c_len),
                  pl.BlockSpec((2, 2 * SUBLANES, LANES), lambda j, bi, i: (j, 0, 0))],
        out_specs=pl.BlockSpec((1, tq, LANES), lambda j, bi, i: (bi, i, j)),
        scratch_shapes=[pltpu.VMEM((2, 2 * kr, GRID_W, LANES), F32), pltpu.VMEM((2, sub, nk), F32),
                        pltpu.VMEM((2, sub, nk), BF16)],
        compiler_params=_cparams(("arbitrary", "arbitrary", "arbitrary")),
        name="na_attn",
    )(q, k, v, kc, vc, rpb_p)


def _outproj_kernel(a_ref, x_ref, mod_ref, g_ref, w_ref, o_ref, z_ref):
    tm = x_ref.shape[1]
    sub = _split_rows(tm)
    for r in range(0, tm, sub):
        rows = slice(r, r + sub)
        x = x_ref[0, rows] + mod_ref[0, 2:3] * _dot(a_ref[0, rows], w_ref[...])
        o_ref[0, rows] = x
        z_ref[0, rows] = _normmod(x, g_ref[...], mod_ref[0, 3:4], mod_ref[0, 4:5]).astype(z_ref.dtype)


def _outproj_call(a, x, mods_l, mod_row, g, w):
    b, n, d = x.shape
    k = a.shape[-1]
    tm = _row_tile(n, 512)
    row = lambda w_: pl.BlockSpec((1, tm, w_), lambda bi, i: (bi, i, 0))
    return pl.pallas_call(
        _outproj_kernel,
        out_shape=(jax.ShapeDtypeStruct((b, n, d), F32), jax.ShapeDtypeStruct((b, n, d), BF16)),
        grid=(b, n // tm),
        in_specs=[row(k), row(d), _mod_spec(d, mod_row),
                  pl.BlockSpec((1, d), lambda bi, i: (0, 0)),
                  pl.BlockSpec((k, d), lambda bi, i: (0, 0))],
        out_specs=(row(d), row(d)),
        compiler_params=_cparams(("arbitrary", "arbitrary")),
        name="outproj",
    )(a, x, mods_l, g, w)


def _ffn_kernel(x_ref, z_ref, zp_ref, zn_ref, mod_ref, wu_ref, cw_ref, wd_ref, *rest, fc, final):
    if final:
        fg_ref, o_ref, z_scr, u_scr, h_scr = rest
    else:
        o_ref, z_scr, u_scr, h_scr = rest
    i = pl.program_id(1)
    nt = pl.num_programs(1)
    tm, d = x_ref.shape[1], x_ref.shape[2]
    hb = FFN_HALO
    nch = wd_ref.shape[0] // fc
    z_scr[hb:hb + tm] = z_ref[0]
    z_scr[0:hb] = jnp.where(i > 0, zp_ref[0], jnp.zeros_like(zp_ref[0]))
    z_scr[hb + tm:] = jnp.where(i < nt - 1, zn_ref[0], jnp.zeros_like(zn_ref[0]))

    f = wd_ref.shape[0]
    gate_cols = lambda j: slice(j * fc, (j + 1) * fc)
    value_cols = lambda j: slice(f + j * fc, f + (j + 1) * fc)

    def up_proj(j):
        z = z_scr[...]
        u_scr[j % 2, :, :fc] = _dot(z, wu_ref[:, gate_cols(j)])
        u_scr[j % 2, :, fc:] = _dot(z, wu_ref[:, value_cols(j)])

    def conv(j, cols, lanes, r0, sub):
        cw = cw_ref[:, cols]
        cv = cw[3:4] + cw[0:1] * u_scr[j % 2, r0 - 1:r0 - 1 + sub, lanes]
        cv = cv + cw[1:2] * u_scr[j % 2, r0:r0 + sub, lanes]
        return cv + cw[2:3] * u_scr[j % 2, r0 + 1:r0 + 1 + sub, lanes]

    def conv_act(j):
        sub = min(FFN_SUB, tm)
        for sb in range(tm // sub):
            r0 = hb + sb * sub
            hg = 0.5 * conv(j, gate_cols(j), slice(0, fc), r0, sub)
            h = (hg + hg * jnp.tanh(hg)) * conv(j, value_cols(j), slice(fc, 2 * fc), r0, sub)
            h_scr[sb * sub:(sb + 1) * sub, j * fc:(j + 1) * fc] = h.astype(BF16)

    up_proj(0)
    for j in range(nch - 1):
        up_proj(j + 1)
        conv_act(j)
    split = (nch - 1) * fc
    acc = _dot(h_scr[:, :split], wd_ref[:split, :])
    conv_act(nch - 1)
    acc = acc + _dot(h_scr[:, split:], wd_ref[split:, :])
    out = x_ref[0] + mod_ref[0, 5:6] * acc
    if final:
        out = _rms(out, fg_ref[...])
    o_ref[0] = out


def _ffn_pack_params(w_up, conv_w, conv_b, w_down, fc):
    f2 = w_up.shape[1]
    assert (f2 // 2) % fc == 0
    cw = jnp.concatenate([conv_w, conv_b[None], jnp.zeros((SUBLANES - 1 - conv_w.shape[0], f2), F32)], axis=0)
    return w_up.astype(BF16), cw, w_down.astype(BF16)


def _ffn_call(x, z, mods_l, mod_row, packed, layer, fc, final_g=None):
    b, n, d = x.shape
    wu, cw, wd = packed
    nch, fc2 = wd.shape[1] // fc, 2 * fc
    tm = _row_tile(n, FFN_TM)
    hb = FFN_HALO
    nbh = tm // hb
    once = dict(pipeline_mode=pl.Buffered(1))
    layer_spec = lambda a, **kw: pl.BlockSpec((None,) + a.shape[1:], lambda bi, i: (layer,) + (0,) * (a.ndim - 1), **kw)
    in_specs = [
        pl.BlockSpec((1, tm, d), lambda bi, i: (bi, i, 0)),
        pl.BlockSpec((1, tm, d), lambda bi, i: (bi, i, 0)),
        pl.BlockSpec((1, hb, d), lambda bi, i: (bi, jnp.maximum(i * nbh - 1, 0), 0)),
        pl.BlockSpec((1, hb, d), lambda bi, i: (bi, jnp.minimum((i + 1) * nbh, n // hb - 1), 0)),
        _mod_spec(d, mod_row),
        layer_spec(wu, **once),
        layer_spec(cw),
        layer_spec(wd, **once),
    ]
    args = [x, z, z, z, mods_l, wu, cw, wd]
    if final_g is not None:
        in_specs.append(pl.BlockSpec((1, d), lambda bi, i: (0, 0)))
        args.append(final_g)
    return pl.pallas_call(
        functools.partial(_ffn_kernel, fc=fc2 // 2, final=final_g is not None),
        out_shape=jax.ShapeDtypeStruct((b, n, d), F32),
        grid=(b, n // tm),
        in_specs=in_specs,
        out_specs=pl.BlockSpec((1, tm, d), lambda bi, i: (bi, i, 0)),
        scratch_shapes=[pltpu.VMEM((tm + 2 * hb, d), BF16), pltpu.VMEM((2, tm + 2 * hb, fc2), F32),
                        pltpu.VMEM((tm, nch * fc2 // 2), BF16)],
        compiler_params=_cparams(("arbitrary", "arbitrary")),
        name="ffn",
    )(*args)


def _rg_mixer(xl, xc, mods_l, ctx_row, g, w_in, conv_w, conv_b, wa, ba, wx, bx, lam, need_ctx):
    w = w_in.astype(BF16)
    g_l, xr_l = _rg_proj_call(xl, mods_l, None, g, w)
    g_c, xr_c = _rg_proj_call(xc, mods_l, ctx_row, g, w)
    cw, wbd, gp = _rg_pack_params(conv_w, conv_b, wa, ba, wx, bx, lam)
    m_c, m_l = _rg_scan_call(xr_c, g_c, xr_l, g_l, cw, wbd, gp)
    return m_l, (m_c if need_ctx else None)


def _na_mixer(xl, xc, mods_l, ctx_row, g, w_in, rpb, need_ctx):
    d = xl.shape[-1]
    w = w_in.astype(BF16)
    ql, kl, vl = _qkv_proj_call(xl, mods_l, None, g, w, d)
    qc, kc, vc = _qkv_proj_call(xc, mods_l, ctx_row, g, w, d)
    o_l = _na_attn_call(ql, kl, vl, kc, vc, rpb)
    o_c = _plain_attn_call(qc, [(kc, vc)], LANES) if need_ctx else None
    return o_l, o_c


def _gqa_mixer(xl, xc, mods_l, ctx_row, g, w_in, q_norm, k_norm, need_ctx):
    d = xl.shape[-1]
    nh = d // HEAD_DIM
    nkv = (w_in.shape[1] // HEAD_DIM - nh) // 2
    group = nh // nkv
    assert group % 2 == 0 and q_norm.shape[0] == HEAD_DIM
    dup = lambda wk: jnp.concatenate([wk.reshape(d, nkv, 1, HEAD_DIM)] * 2, axis=2).reshape(d, 2 * nkv * HEAD_DIM)
    nk = 2 * nkv * HEAD_DIM
    w = jnp.concatenate([w_in[:, :d], dup(w_in[:, d:d + nkv * HEAD_DIM]), dup(w_in[:, d + nkv * HEAD_DIM:])],
                        axis=1).astype(BF16)
    nseg = (d + nk) // HEAD_DIM
    assert nseg <= LANES
    seg = jnp.arange(d + nk) // HEAD_DIM
    ind = (seg[:, None] == jnp.arange(LANES)[None, :]).astype(BF16)
    gain = jnp.concatenate([jnp.tile(q_norm, nh), jnp.tile(k_norm, 2 * nkv)])[None]
    qkn = (gain, ind, ind.T)
    ql, kl, vl = _qkv_proj_call(xl, mods_l, None, g, w, nk, rope=_rope_tables(xl.shape[1]), qknorm=qkn)
    qc, kc, vc = _qkv_proj_call(xc, mods_l, ctx_row, g, w, nk, qknorm=qkn)
    qw = group * HEAD_DIM
    o_l = _plain_attn_t_call(ql, [(kl, vl), (kc, vc)], qw)
    o_c = _plain_attn_call(qc, [(kc, vc)], qw) if need_ctx else None
    return o_l, o_c


def _diff_mixer(xl, xc, mods_l, ctx_row, g, w_in, lq1, lk1, lq2, lk2, subln_g, lambda_init, need_ctx):
    d = xl.shape[-1]
    assert lq1.shape[0] == HEAD_DIM and subln_g.shape[0] == LANES
    w = w_in.astype(BF16)
    ql, kl, vl = _qkv_proj_call(xl, mods_l, None, g, w, d, rope=_rope_tables(xl.shape[1]))
    qc, kc, vc = _qkv_proj_call(xc, mods_l, ctx_row, g, w, d)
    lam_params = jnp.concatenate([jnp.stack([lq1, lk1, lq2, lk2]), jnp.zeros((SUBLANES - 4, HEAD_DIM), F32)])
    o_l = _diff_attn_call(ql, [(kl, vl), (kc, vc)], lam_params, subln_g[None], lambda_init)
    o_c = _diff_attn_call(qc, [(kc, vc)], lam_params, subln_g[None], lambda_init) if need_ctx else None
    return o_l, o_c


def kernel(x, c, ctx, c_ctx, mod_w, mod_b, norm1_g, norm2_g, rg_w_in, rg_conv_w, rg_conv_b, rg_wa, rg_ba, rg_wx, rg_bx, rg_lam, rg_w_out, na_w_in, na_rpb, na_w_out, gqa_w_in, gqa_q_norm, gqa_k_norm, gqa_w_out, diff_w_in, diff_lq1, diff_lk1, diff_lq2, diff_lk2, diff_subln_g, diff_w_out, ffn_w_up, ffn_conv_w, ffn_conv_b, ffn_w_down, final_g):
    bsz, _, d = x.shape
    depth = mod_w.shape[0]
    assert d % LANES == 0
    ctx_row = bsz
    nrow = -(-(bsz + 1) // SUBLANES) * SUBLANES
    cc = jnp.concatenate([c, c_ctx[None], jnp.zeros((nrow - bsz - 1, d), F32)], axis=0)
    mods = _mods_call(cc, mod_w, mod_b).reshape(depth, nrow, 6, d)
    f = ffn_w_down.shape[1]
    fc = next(t for t in (256, 128) if f % t == 0)
    packed = jax.vmap(functools.partial(_ffn_pack_params, fc=fc))(ffn_w_up, ffn_conv_w, ffn_conv_b, ffn_w_down)
    xl, xc = x, ctx
    for l in range(depth):
        m, j = l % N_MIXERS, l // N_MIXERS
        need_ctx = l < depth - 1
        mods_l = mods[l]
        g1, g2 = norm1_g[l][None], norm2_g[l][None]
        if m == 0:
            a_l, a_c = _rg_mixer(xl, xc, mods_l, ctx_row, g1, rg_w_in[j], rg_conv_w[j], rg_conv_b[j], rg_wa[j],
                                 rg_ba[j], rg_wx[j], rg_bx[j], rg_lam[j], need_ctx)
            w_out = rg_w_out[j]
        elif m == 1:
            a_l, a_c = _na_mixer(xl, xc, mods_l, ctx_row, g1, na_w_in[j], na_rpb[j], need_ctx)
            w_out = na_w_out[j]
        elif m == 2:
            a_l, a_c = _gqa_mixer(xl, xc, mods_l, ctx_row, g1, gqa_w_in[j], gqa_q_norm[j], gqa_k_norm[j], need_ctx)
            w_out = gqa_w_out[j]
        else:
            lambda_init = 0.8 - 0.6 * math.exp(-0.3 * l)
            a_l, a_c = _diff_mixer(xl, xc, mods_l, ctx_row, g1, diff_w_in[j], diff_lq1[j], diff_lk1[j],
                                   diff_lq2[j], diff_lk2[j], diff_subln_g[j], lambda_init, need_ctx)
            w_out = diff_w_out[j]
        w_out = w_out.astype(BF16)
        last = l == depth - 1
        xl, zl = _outproj_call(a_l, xl, mods_l, None, g2, w_out)
        xl = _ffn_call(xl, zl, mods_l, None, packed, l, fc, final_g[None] if last else None)
        if need_ctx:
            xc, zc = _outproj_call(a_c, xc, mods_l, ctx_row, g2, w_out)
            xc = _ffn_call(xc, zc, mods_l, ctx_row, packed, l, fc)
    return xl
```

```python
import functools
import math

import jax
import jax.numpy as jnp
from jax import lax
from jax.experimental import pallas as pl
from jax.experimental.pallas import tpu as pltpu

F32 = jnp.float32
BF16 = jnp.bfloat16
EPS = 1e-6
ROPE_THETA = 10000.0
GRID_W = 64
RG_C = 8.0
RG_CONV_LEFT = 2
N_MIXERS = 4
HEAD_DIM = 64
LOG2E = 1.4426950408889634
Q_SCALE = HEAD_DIM ** -0.5 * LOG2E
LANES = 128
SUBLANES = 8
NEG = -1e30
FFN_HALO = 16
FFN_SUB = 512
FFN_TM = 512
PROJ_SPLIT = 2
PROJ_MIN_ROWS = 256
ATT_SUB = 256
ATT_RB = 32
DIFF_RB = 16
KEY_SLAB = 32
VMEM_LIMIT_BYTES = 56 * 1024 * 1024

_NT = (((1,), (1,)), ((), ()))


def _cparams(sem):
    return pltpu.CompilerParams(dimension_semantics=sem, vmem_limit_bytes=VMEM_LIMIT_BYTES)


def _dot(a, b):
    return jnp.dot(a, b, preferred_element_type=F32)


def _dot_nt(a, b):
    return lax.dot_general(a, b, _NT, preferred_element_type=F32)


def _silu(x):
    return x / (1.0 + jnp.exp(-x))


def _gelu_tanh(x):
    cdf = 0.5 * (1.0 + jnp.tanh(math.sqrt(2.0 / math.pi) * (x + 0.044715 * (x * x * x))))
    return x * cdf


def _rms(x, g):
    y = x * lax.rsqrt(jnp.mean(x * x, axis=-1, keepdims=True) + EPS)
    return y * g


def _normmod(x, g, shift, scale):
    return _rms(x, g) * (1.0 + scale) + shift


def _mods_kernel(cc_ref, w_ref, b_ref, o_ref):
    a = _silu(cc_ref[...]).astype(BF16)
    o_ref[0] = _dot(a, w_ref[0].astype(BF16)) + b_ref[0]


def _mods_call(cc, mod_w, mod_b):
    depth, d, n = mod_w.shape
    r = cc.shape[0]
    tn = n // 4
    return pl.pallas_call(
        _mods_kernel,
        out_shape=jax.ShapeDtypeStruct((depth, r, n), F32),
        grid=(depth, n // tn),
        in_specs=[
            pl.BlockSpec((r, d), lambda l, j: (0, 0)),
            pl.BlockSpec((1, d, tn), lambda l, j: (l, 0, j)),
            pl.BlockSpec((1, 1, tn), lambda l, j: (l, 0, j)),
        ],
        out_specs=pl.BlockSpec((1, r, tn), lambda l, j: (l, 0, j)),
        compiler_params=_cparams(("arbitrary", "arbitrary")),
        name="mods",
    )(cc, mod_w, mod_b.reshape(depth, 1, n))


def _mod_spec(d, row):
    if row is None:
        return pl.BlockSpec((1, 6, d), lambda b, *_: (b, 0, 0))
    return pl.BlockSpec((1, 6, d), lambda b, *_: (row, 0, 0))


def _split_rows(tm):
    sub = tm // PROJ_SPLIT
    return sub if (tm % PROJ_SPLIT == 0 and sub >= PROJ_MIN_ROWS) else tm


def _row_tile(n, pref):
    t = min(n, pref)
    assert n % t == 0
    return t


def _rope_tables(n_tok):
    t = jnp.arange(n_tok)
    row = (t // GRID_W).astype(F32)
    col = (t % GRID_W).astype(F32)
    n = HEAD_DIM // 4
    inv = ROPE_THETA ** (-jnp.arange(n, dtype=F32) / n)
    ang = jnp.concatenate([row[:, None] * inv, col[:, None] * inv], axis=-1)
    cos, sin = jnp.cos(ang), jnp.sin(ang)
    reps = LANES // HEAD_DIM
    cos_t = jnp.tile(jnp.concatenate([cos, cos], axis=-1), (1, reps))
    sin_t = jnp.tile(jnp.concatenate([-sin, sin], axis=-1), (1, reps))
    return cos_t, sin_t


def _swap_halves(x):
    lane = lax.broadcasted_iota(jnp.int32, x.shape, 1)
    lo = (lane & (HEAD_DIM // 2)) == 0
    return jnp.where(lo, pltpu.roll(x, LANES - HEAD_DIM // 2, 1), pltpu.roll(x, HEAD_DIM // 2, 1))


def _rope_cols(u, cos_t, sin_t):
    outs = []
    for c in range(u.shape[1] // LANES):
        xc = u[:, c * LANES:(c + 1) * LANES]
        outs.append(xc * cos_t + _swap_halves(xc) * sin_t)
    return jnp.concatenate(outs, axis=1)


def _rg_proj_kernel(x_ref, mod_ref, g_ref, w_ref, go_ref, xo_ref, *, dr):
    tm = x_ref.shape[1]
    sub = _split_rows(tm)
    for r in range(0, tm, sub):
        rows = slice(r, r + sub)
        z = _normmod(x_ref[0, rows], g_ref[...], mod_ref[0, 0:1], mod_ref[0, 1:2]).astype(BF16)
        u = _dot(z, w_ref[...])
        go_ref[0, rows] = _gelu_tanh(u[:, :dr])
        xo_ref[0, rows] = u[:, dr:]


def _rg_proj_call(x, mods_l, mod_row, g, w):
    b, n, d = x.shape
    dr = w.shape[1] // 2
    tm = _row_tile(n, 512)
    return pl.pallas_call(
        functools.partial(_rg_proj_kernel, dr=dr),
        out_shape=(jax.ShapeDtypeStruct((b, n, dr), F32), jax.ShapeDtypeStruct((b, n, dr), F32)),
        grid=(b, n // tm),
        in_specs=[
            pl.BlockSpec((1, tm, d), lambda bi, i: (bi, i, 0)),
            _mod_spec(d, mod_row),
            pl.BlockSpec((1, d), lambda bi, i: (0, 0)),
            pl.BlockSpec((d, 2 * dr), lambda bi, i: (0, 0)),
        ],
        out_specs=(pl.BlockSpec((1, tm, dr), lambda bi, i: (bi, i, 0)),
                   pl.BlockSpec((1, tm, dr), lambda bi, i: (bi, i, 0))),
        compiler_params=_cparams(("arbitrary", "arbitrary")),
        name="rg_proj",
    )(x, mods_l, g, w)


def _qkv_proj_kernel(x_ref, mod_ref, g_ref, w_ref, *rest, d, rope, qknorm):
    rest = list(rest)
    if rope:
        cos_ref, sin_ref = rest.pop(0), rest.pop(0)
    if qknorm:
        gain_ref, ind_ref, indt_ref = rest.pop(0), rest.pop(0), rest.pop(0)
    q_ref, k_ref, v_ref = rest
    nk = k_ref.shape[-1]
    tm = x_ref.shape[1]
    sub = _split_rows(tm)
    for r in range(0, tm, sub):
        rows = slice(r, r + sub)
        z = _normmod(x_ref[0, rows], g_ref[...], mod_ref[0, 0:1], mod_ref[0, 1:2]).astype(BF16)
        u = _dot(z, w_ref[...])
        qk = u[:, :d + nk]
        if qknorm:
            x2 = qk * qk
            hi = x2.astype(BF16)
            lo = (x2 - hi.astype(F32)).astype(BF16)
            ssq = _dot(hi, ind_ref[...]) + _dot(lo, ind_ref[...])
            rs = lax.rsqrt(ssq * (1.0 / HEAD_DIM) + EPS)
            rhi = rs.astype(BF16)
            rlo = (rs - rhi.astype(F32)).astype(BF16)
            rb = _dot(rhi, indt_ref[...]) + _dot(rlo, indt_ref[...])
            qk = (qk * rb) * gain_ref[...]
        if rope:
            qk = _rope_cols(qk, cos_ref[rows], sin_ref[rows])
        q_ref[0, rows] = (qk[:, :d] * Q_SCALE).astype(q_ref.dtype)
        k_ref[0, rows] = qk[:, d:].astype(k_ref.dtype)
        v_ref[0, rows] = u[:, d + nk:].astype(v_ref.dtype)


def _qkv_proj_call(x, mods_l, mod_row, g, w, nk, *, rope=None, qknorm=None):
    b, n, d = x.shape
    nv = w.shape[1] - d - nk
    tm = _row_tile(n, 512)
    const = lambda bi, i: (0, 0)
    in_specs = [
        pl.BlockSpec((1, tm, d), lambda bi, i: (bi, i, 0)),
        _mod_spec(d, mod_row),
        pl.BlockSpec((1, d), const),
        pl.BlockSpec(w.shape, const),
    ]
    args = [x, mods_l, g, w]
    if rope is not None:
        in_specs += [pl.BlockSpec((tm, LANES), lambda bi, i: (i, 0))] * 2
        args += list(rope)
    if qknorm is not None:
        in_specs += [pl.BlockSpec(a.shape, const) for a in qknorm]
        args += list(qknorm)
    out_spec = lambda w_: pl.BlockSpec((1, tm, w_), lambda bi, i: (bi, i, 0))
    return pl.pallas_call(
        functools.partial(_qkv_proj_kernel, d=d, rope=rope is not None, qknorm=qknorm is not None),
        out_shape=(jax.ShapeDtypeStruct((b, n, d), BF16), jax.ShapeDtypeStruct((b, n, nk), BF16),
                   jax.ShapeDtypeStruct((b, n, nv), BF16)),
        grid=(b, n // tm),
        in_specs=in_specs,
        out_specs=(out_spec(d), out_spec(nk), out_spec(nv)),
        compiler_params=_cparams(("arbitrary", "arbitrary")),
        name="qkv_proj",
    )(*args)


def _rg_conv(xs_scr, x, prev, nxt, cw):
    t = x.shape[0]
    h = SUBLANES
    xs_scr[0:h] = prev
    xs_scr[h:h + t] = x
    xs_scr[h + t:2 * h + t] = nxt
    y = cw[4:5] + cw[0:1] * xs_scr[h - 2:h - 2 + t]
    y = y + cw[1:2] * xs_scr[h - 1:h - 1 + t]
    y = y + cw[2:3] * x
    y = y + cw[3:4] * xs_scr[h + 1:h + 1 + t]
    return y


def _sigmoid(x):
    return 0.5 * jnp.tanh(0.5 * x) + 0.5


def _rg_gates(xc, w, ba, bx, lam):
    gw = xc.shape[1]
    y = _dot(xc.astype(BF16), w)
    r = _sigmoid(y[:, :gw] + ba)
    i = _sigmoid(y[:, gw:] + bx)
    nl = -lam
    softplus = jnp.maximum(nl, 0.0) + jnp.log1p(jnp.exp(-jnp.abs(nl)))
    log_a = r * (-RG_C * softplus)
    a = jnp.exp(log_a)
    one_minus_a2 = -jnp.tanh(log_a) * (a * a + 1.0)
    root = jnp.where(one_minus_a2 > 0.0, one_minus_a2 * lax.rsqrt(one_minus_a2), 0.0)
    return a, root * (i * xc)


def _rg_scan_tile(a, b, carry, reverse):
    t, gw = a.shape
    nck = t // SUBLANES
    a = a.reshape(nck, SUBLANES, gw)
    b = b.reshape(nck, SUBLANES, gw)
    rmod = lax.broadcasted_iota(jnp.int32, (1, SUBLANES, 1), 1)
    for s in (1, 2, 4):
        ok = (rmod < SUBLANES - s) if reverse else (rmod >= s)
        shift = SUBLANES - s if reverse else s
        ash, bsh = pltpu.roll(a, shift, 1), pltpu.roll(b, shift, 1)
        b = a * jnp.where(ok, bsh, 0.0) + b
        a = a * jnp.where(ok, ash, 1.0)
    hs = [None] * nck
    for ck in (range(nck - 1, -1, -1) if reverse else range(nck)):
        hc = b[ck] + a[ck] * carry
        carry = hc[0:1] if reverse else hc[SUBLANES - 1:SUBLANES]
        hs[ck] = hc
    return jnp.concatenate(hs, axis=0), carry


def _rg_scan_kernel(xc_ref, gc_ref, xl_ref, xp_ref, xn_ref, gl_ref, cw_ref, w_ref, gp_ref,
                    mc_ref, ml_ref, hf_scr, xv_scr, car_scr, xs_scr, *, nt):
    s = pl.program_id(2)
    gw = xl_ref.shape[-1]
    tl = xl_ref.shape[1]
    cw = cw_ref[0]
    gp = gp_ref[0]

    def gates(xconv, d):
        w = w_ref[0, :, d * 2 * gw:(d + 1) * 2 * gw]
        return _rg_gates(xconv, w, gp[3 * d:3 * d + 1], gp[3 * d + 1:3 * d + 2], gp[3 * d + 2:3 * d + 3])

    @pl.when(s == 0)
    def _ctx():
        zeros8 = jnp.zeros((SUBLANES, gw), F32)
        xconv = _rg_conv(xs_scr, xc_ref[0], zeros8, zeros8, cw)
        zero = jnp.zeros((1, gw), F32)
        a0, b0 = gates(xconv, 0)
        hf, cf = _rg_scan_tile(a0, b0, zero, False)
        a1, b1 = gates(xconv, 1)
        hb, cb = _rg_scan_tile(a1, b1, zero, True)
        car_scr[0:1] = cf
        car_scr[1:2] = cb
        mc_ref[0] = (gc_ref[0] * (hf + hb)).astype(mc_ref.dtype)

    def lat_conv(t):
        prev = xp_ref[0] * (t > 0).astype(F32)
        nxt = xn_ref[0] * (t < nt - 1).astype(F32)
        return _rg_conv(xs_scr, xl_ref[0], prev, nxt, cw)

    @pl.when((s >= 1) & (s <= nt))
    def _fwd():
        t = s - 1
        xconv = lat_conv(t)
        xv_scr[pl.ds(pl.multiple_of(t * tl, tl), tl), :] = xconv
        a0, b0 = gates(xconv, 0)
        h, c = _rg_scan_tile(a0, b0, car_scr[0:1], False)
        car_scr[0:1] = c
        hf_scr[pl.ds(pl.multiple_of(t * tl, tl), tl), :] = h

    @pl.when(s > nt)
    def _bwd():
        t = 2 * nt - s
        a1, b1 = gates(xv_scr[pl.ds(pl.multiple_of(t * tl, tl), tl), :], 1)
        h, c = _rg_scan_tile(a1, b1, car_scr[1:2], True)
        car_scr[1:2] = c
        hf = hf_scr[pl.ds(pl.multiple_of(t * tl, tl), tl), :]
        ml_ref[0] = (gl_ref[0] * (hf + h)).astype(ml_ref.dtype)


def _rg_scan_call(xr_c, g_c, xr_l, g_l, cw, wbd, gp):
    b, s_len, dr = xr_l.shape
    c_len = xr_c.shape[1]
    ng, gw = wbd.shape[0], wbd.shape[1]
    tl = _row_tile(s_len, 512)
    nt = s_len // tl
    nb8 = tl // SUBLANES

    def tile(s):
        return jnp.clip(s - 1, 0, nt - 1)

    def tile_bwd(s):
        return jnp.where(s <= nt, nt - 1, 2 * nt - s)

    return pl.pallas_call(
        functools.partial(_rg_scan_kernel, nt=nt),
        out_shape=(jax.ShapeDtypeStruct((b, c_len, dr), BF16), jax.ShapeDtypeStruct((b, s_len, dr), BF16)),
        grid=(b, ng, 2 * nt + 1),
        in_specs=[
            pl.BlockSpec((1, c_len, gw), lambda bi, h, s: (bi, 0, h)),
            pl.BlockSpec((1, c_len, gw), lambda bi, h, s: (bi, 0, h)),
            pl.BlockSpec((1, tl, gw), lambda bi, h, s: (bi, tile(s), h)),
            pl.BlockSpec((1, SUBLANES, gw), lambda bi, h, s: (bi, jnp.maximum(tile(s) * nb8 - 1, 0), h)),
            pl.BlockSpec((1, SUBLANES, gw),
                         lambda bi, h, s: (bi, jnp.minimum((tile(s) + 1) * nb8, s_len // SUBLANES - 1), h)),
            pl.BlockSpec((1, tl, gw), lambda bi, h, s: (bi, tile_bwd(s), h)),
            pl.BlockSpec((1, SUBLANES, gw), lambda bi, h, s: (h, 0, 0)),
            pl.BlockSpec((1, gw, 4 * gw), lambda bi, h, s: (h, 0, 0)),
            pl.BlockSpec((1, SUBLANES, gw), lambda bi, h, s: (h, 0, 0)),
        ],
        out_specs=(pl.BlockSpec((1, c_len, gw), lambda bi, h, s: (bi, 0, h)),
                   pl.BlockSpec((1, tl, gw), lambda bi, h, s: (bi, tile_bwd(s), h))),
        scratch_shapes=[pltpu.VMEM((s_len, gw), F32), pltpu.VMEM((s_len, gw), F32), pltpu.VMEM((SUBLANES, gw), F32),
                        pltpu.VMEM((max(tl, c_len) + 2 * SUBLANES, gw), F32)],
        compiler_params=_cparams(("arbitrary", "arbitrary", "arbitrary")),
        name="rg_scan",
    )(xr_c, g_c, xr_l, xr_l, xr_l, g_l, cw, wbd, gp)


def _rg_pack_params(conv_w, conv_b, wa, ba, wx, bx, lam):
    nblk, bw = wa.shape[1], wa.shape[2]
    dr = nblk * bw
    per = next(p for p in range(1, nblk + 1) if nblk % p == 0 and (p * bw) % LANES == 0)
    ng, gw = nblk // per, per * bw

    def dense(w):
        w = w.reshape(ng, per, bw, bw)
        eye = jnp.eye(per, dtype=w.dtype)
        return jnp.einsum('gpjk,pq->gpjqk', w, eye).reshape(ng, gw, gw)

    wbd = jnp.concatenate([dense(wa[0]), dense(wx[0]), dense(wa[1]), dense(wx[1])], axis=-1).astype(BF16)
    cw = jnp.concatenate([conv_w, conv_b[None], jnp.zeros((SUBLANES - 1 - conv_w.shape[0], dr), F32)], axis=0)
    cw = cw.reshape(SUBLANES, ng, gw).transpose(1, 0, 2)
    gp = jnp.stack([ba[0], bx[0], lam[0], ba[1], bx[1], lam[1], jnp.zeros_like(lam[0]), jnp.zeros_like(lam[0])])
    gp = gp.reshape(SUBLANES, ng, gw).transpose(1, 0, 2)
    return cw, wbd, gp


def _pair_masks(shape):
    lane = lax.broadcasted_iota(jnp.int32, shape, 1)
    return lane < HEAD_DIM


def _half_q(q, hh):
    first = _pair_masks(q.shape)
    return jnp.where(first if hh == 0 else jnp.logical_not(first), q, jnp.zeros_like(q))


def _scores_to_scratch(qh, k_refs, s_scr, slot):
    off = 0
    for k_ref in k_refs:
        n = k_ref.shape[1]
        s_scr[slot, :, off:off + n] = _dot_nt(qh, k_ref[0])
        off += n


def _exp_rows(s):
    e = jnp.exp2(s - s.max(axis=-1, keepdims=True))
    return e, e.sum(axis=-1, keepdims=True)


def _softmax_to_scratch(s_scr, p_scr, slot, rows):
    linv = []
    for r in range(0, rows, ATT_RB):
        e, l = _exp_rows(s_scr[slot, r:r + ATT_RB, :])
        p_scr[slot, r:r + ATT_RB, :] = e.astype(BF16)
        linv.append(1.0 / l)
    return jnp.concatenate(linv, axis=0)


def _pv_from_scratch(p_scr, slot, v_refs):
    off, o = 0, None
    for v_ref in v_refs:
        n = v_ref.shape[1]
        t = _dot(p_scr[slot, :, off:off + n], v_ref[0])
        o = t if o is None else o + t
        off += n
    return o


def _plain_attn_kernel(q_ref, *refs, nsrc):
    k_refs, v_refs = refs[0:2 * nsrc:2], refs[1:2 * nsrc:2]
    o_ref, s_scr, p_scr = refs[2 * nsrc:]
    tq, qw = q_ref.shape[1], q_ref.shape[2]
    sub = s_scr.shape[1]
    units = [(r, c, hh) for r in range(0, tq, sub) for c in range(0, qw, LANES) for hh in range(2)]

    def scores(u):
        r, c, hh = units[u]
        _scores_to_scratch(_half_q(q_ref[0, r:r + sub, c:c + LANES], hh), k_refs, s_scr, u % 2)

    scores(0)
    outs = [None, None]
    for u, (r, c, hh) in enumerate(units):
        if u + 1 < len(units):
            scores(u + 1)
        linv = _softmax_to_scratch(s_scr, p_scr, u % 2, sub)
        outs[hh] = _pv_from_scratch(p_scr, u % 2, v_refs) * linv
        if hh == 1:
            o_ref[0, r:r + sub, c:c + LANES] = jnp.where(_pair_masks((sub, LANES)), outs[0], outs[1]).astype(o_ref.dtype)


def _plain_attn_t_kernel(q_ref, *refs, nsrc):
    k_refs, v_refs = refs[0:2 * nsrc:2], refs[1:2 * nsrc:2]
    o_ref = refs[2 * nsrc]
    vt_scrs = refs[2 * nsrc + 1:3 * nsrc + 1]
    s_scr, p_scr = refs[3 * nsrc + 1:]
    tq, qw = q_ref.shape[1], q_ref.shape[2]
    sub = s_scr.shape[2]

    @pl.when(pl.program_id(2) == 0)
    def _transpose_values():
        for v_ref, vt_scr in zip(v_refs, vt_scrs):
            vt_scr[...] = v_ref[0].astype(F32).T.astype(BF16)

    units = [(r, c, hh) for r in range(0, tq, sub) for c in range(0, qw, LANES) for hh in range(2)]

    def scores(u):
        r, c, hh = units[u]
        qh = _half_q(q_ref[0, r:r + sub, c:c + LANES], hh)
        off = 0
        for k_ref in k_refs:
            n = k_ref.shape[1]
            s_scr[u % ns, off:off + n, :] = _dot_nt(k_ref[0], qh)
            off += n

    ns = s_scr.shape[0]
    for u in range(min(ns - 1, len(units))):
        scores(u)
    outs = [None, None]
    for u, (r, c, hh) in enumerate(units):
        if u + ns - 1 < len(units):
            scores(u + ns - 1)
        slabs = [slice(i, i + KEY_SLAB) for i in range(0, s_scr.shape[1], KEY_SLAB)]
        macc = s_scr[u % ns, slabs[0], :]
        for sl in slabs[1:]:
            macc = jnp.maximum(macc, s_scr[u % ns, sl, :])
        m = jnp.max(macc, axis=0, keepdims=True)
        lacc = None
        for sl in slabs:
            e = jnp.exp2(s_scr[u % ns, sl, :] - m)
            p_scr[u % 2, sl, :] = e.astype(BF16)
            lacc = e if lacc is None else lacc + e
        linv = 1.0 / jnp.sum(lacc, axis=0, keepdims=True)
        off, ot = 0, None
        for vt_scr in vt_scrs:
            n = vt_scr.shape[1]
            t = _dot(vt_scr[hh * HEAD_DIM:(hh + 1) * HEAD_DIM, :], p_scr[u % 2, off:off + n, :])
            ot = t if ot is None else ot + t
            off += n
        outs[hh] = ot * linv
        if hh == 1:
            o_ref[0, r:r + sub, c:c + LANES] = jnp.concatenate(outs, axis=0).T.astype(o_ref.dtype)


def _plain_attn_t_call(q, kvs, qw):
    b, lq, d = q.shape
    tq = _row_tile(lq, 4 * ATT_SUB)
    sub = min(tq, ATT_SUB)
    nk = sum(k.shape[1] for k, _ in kvs)
    in_specs = [pl.BlockSpec((1, tq, qw), lambda bi, j, i: (bi, i, j))]
    args = [q]
    for k, v in kvs:
        spec = pl.BlockSpec((1, k.shape[1], LANES), lambda bi, j, i: (bi, 0, j))
        in_specs += [spec, spec]
        args += [k, v]
    return pl.pallas_call(
        functools.partial(_plain_attn_t_kernel, nsrc=len(kvs)),
        out_shape=jax.ShapeDtypeStruct((b, lq, d), BF16),
        grid=(b, d // qw, lq // tq),
        in_specs=in_specs,
        out_specs=pl.BlockSpec((1, tq, qw), lambda bi, j, i: (bi, i, j)),
        scratch_shapes=([pltpu.VMEM((LANES, k.shape[1]), BF16) for k, _ in kvs]
                        + [pltpu.VMEM((3, nk, sub), F32), pltpu.VMEM((2, nk, sub), BF16)]),
        compiler_params=_cparams(("arbitrary", "arbitrary", "arbitrary")),
        name="plain_attn_t",
    )(*args)


def _plain_attn_call(q, kvs, qw):
    b, lq, d = q.shape
    tq = _row_tile(lq, 2 * ATT_SUB)
    sub = min(tq, ATT_SUB)
    nk = sum(k.shape[1] for k, _ in kvs)
    in_specs = [pl.BlockSpec((1, tq, qw), lambda bi, j, i: (bi, i, j))]
    args = [q]
    for k, v in kvs:
        spec = pl.BlockSpec((1, k.shape[1], LANES), lambda bi, j, i: (bi, 0, j))
        in_specs += [spec, spec]
        args += [k, v]
    return pl.pallas_call(
        functools.partial(_plain_attn_kernel, nsrc=len(kvs)),
        out_shape=jax.ShapeDtypeStruct((b, lq, d), BF16),
        grid=(b, d // qw, lq // tq),
        in_specs=in_specs,
        out_specs=pl.BlockSpec((1, tq, qw), lambda bi, j, i: (bi, i, j)),
        scratch_shapes=[pltpu.VMEM((2, sub, nk), F32), pltpu.VMEM((2, sub, nk), BF16)],
        compiler_params=_cparams(("arbitrary", "arbitrary", "arbitrary")),
        name="plain_attn",
    )(*args)


def _diff_attn_kernel(q_ref, lam_ref, g_ref, *refs, nsrc, lambda_init):
    k_refs, v_refs = refs[0:2 * nsrc:2], refs[1:2 * nsrc:2]
    o_ref, s_scr, a_scr = refs[2 * nsrc:]
    lp = lam_ref[...]
    lam = (jnp.exp(jnp.sum(lp[0:1] * lp[1:2], axis=-1, keepdims=True))
           - jnp.exp(jnp.sum(lp[2:3] * lp[3:4], axis=-1, keepdims=True)) + lambda_init)
    tq = q_ref.shape[1]
    sub = s_scr.shape[1]
    ns = s_scr.shape[0] // 2
    units = list(range(0, tq, sub))

    def scores(u):
        q = q_ref[0, units[u]:units[u] + sub]
        for hh in range(2):
            _scores_to_scratch(_half_q(q, hh), k_refs, s_scr, 2 * (u % ns) + hh)

    def combine(u):
        linv = []
        for r in range(0, sub, DIFF_RB):
            e0, l0 = _exp_rows(s_scr[2 * (u % ns), r:r + DIFF_RB, :])
            e1, l1 = _exp_rows(s_scr[2 * (u % ns) + 1, r:r + DIFF_RB, :])
            a_scr[u % 2, r:r + DIFF_RB, :] = (e0 - e1 * (lam * l0 / l1)).astype(BF16)
            linv.append(1.0 / l0)
        return jnp.concatenate(linv, axis=0)

    for u in range(min(ns - 1, len(units))):
        scores(u)
    for u, r in enumerate(units):
        if u + ns - 1 < len(units):
            scores(u + ns - 1)
        linv = combine(u)
        o = _pv_from_scratch(a_scr, u % 2, v_refs) * linv
        o_ref[0, r:r + sub] = (_rms(o, g_ref[...]) * (1.0 - lambda_init)).astype(o_ref.dtype)


def _diff_attn_call(q, kvs, lam_params, subln_g, lambda_init):
    b, lq, d = q.shape
    tq = _row_tile(lq, 8 * ATT_SUB)
    sub = min(tq, ATT_SUB)
    nk = sum(k.shape[1] for k, _ in kvs)
    const = lambda bi, j, i: (0, 0)
    in_specs = [pl.BlockSpec((1, tq, LANES), lambda bi, j, i: (bi, i, j)),
                pl.BlockSpec(lam_params.shape, const), pl.BlockSpec(subln_g.shape, const)]
    args = [q, lam_params, subln_g]
    for k, v in kvs:
        spec = pl.BlockSpec((1, k.shape[1], LANES), lambda bi, j, i: (bi, 0, j))
        in_specs += [spec, spec]
        args += [k, v]
    return pl.pallas_call(
        functools.partial(_diff_attn_kernel, nsrc=len(kvs), lambda_init=lambda_init),
        out_shape=jax.ShapeDtypeStruct((b, lq, d), BF16),
        grid=(b, d // LANES, lq // tq),
        in_specs=in_specs,
        out_specs=pl.BlockSpec((1, tq, LANES), lambda bi, j, i: (bi, i, j)),
        scratch_shapes=[pltpu.VMEM((6, sub, nk), F32), pltpu.VMEM((2, sub, nk), BF16)],
        compiler_params=_cparams(("arbitrary", "arbitrary", "arbitrary")),
        name="diff_attn",
    )(*args)


def _na_attn_kernel(q_ref, k_ref, v_ref, kc_ref, vc_ref, rpb_ref, o_ref, t2_scr, s_scr, p_scr, *, rows, kr, kcw, nq):
    w = GRID_W
    nwin = nq + kr
    ntab = 2 * kr
    nkw = nwin * w
    sub = nq * w
    nsub = q_ref.shape[1] // sub
    i = pl.program_id(2)

    @pl.when((i == 0) & (pl.program_id(1) == 0))
    def _build_bias():
        c_io = lax.broadcasted_iota(jnp.int32, (w, LANES), 0)
        cp_io = lax.broadcasted_iota(jnp.int32, (w, LANES), 1)
        cs = jnp.clip(c_io - kcw // 2, 0, w - kcw)
        colok = (cp_io >= cs) & (cp_io < cs + kcw)
        neg = jnp.full((w, LANES), NEG, F32)
        for hh in range(2):
            r = pltpu.roll(rpb_ref[hh] * LOG2E, LANES - (kcw - 1), 1)
            ts = []
            for e in range(ntab - 1):
                t = jnp.broadcast_to(r[e:e + 1], (w, LANES))
                t = pltpu.roll(t, 0, 1, stride=1, stride_axis=0)
                ts.append(jnp.where(colok, t, NEG))
            for e in range(ntab):
                lo = ts[e - 1] if e >= 1 else neg
                hi = ts[e] if e < ntab - 1 else neg
                t2_scr[hh, e] = jnp.where(cp_io < w, lo, pltpu.roll(hi, w, 1))

    units = [(t, hh) for t in range(nsub) for hh in range(2)]
    ns = s_scr.shape[0]

    def window(t):
        r0 = (i * nsub + t) * nq
        ks = jnp.clip(r0 - kr // 2, 0, rows - nwin)
        return r0, ks, pl.ds(pl.multiple_of(ks * w, w), nkw)

    def scores(u):
        t, hh = units[u]
        _, _, win = window(t)
        qh = _half_q(q_ref[0, t * sub:(t + 1) * sub], hh)
        s_scr[u % ns, :, :nkw] = _dot_nt(qh, k_ref[0, win, :])
        s_scr[u % ns, :, nkw:] = _dot_nt(qh, kc_ref[0])

    def softmax(u):
        t, hh = units[u]
        r0, ks, _ = window(t)
        key_row = ks + jnp.right_shift(lax.broadcasted_iota(jnp.int32, (1, nkw), 1), int(math.log2(w)))
        linv = []
        for qi in range(nq):
            r = r0 + qi
            rs = jnp.clip(r - kr // 2, 0, rows - kr)
            rowok = (key_row >= rs) & (key_row < rs + kr)
            idx = [jnp.clip(ks + 2 * p - r + kr, 0, ntab - 1) for p in range(nwin // 2)]
            for rb in range(0, w, ATT_RB):
                row = qi * w + rb
                bias = jnp.concatenate([t2_scr[hh, e, rb:rb + ATT_RB, :] for e in idx], axis=1)
                sw = jnp.where(rowok, s_scr[u % ns, row:row + ATT_RB, :nkw] + bias, NEG)
                e, l = _exp_rows(jnp.concatenate([sw, s_scr[u % ns, row:row + ATT_RB, nkw:]], axis=1))
                p_scr[u % 2, row:row + ATT_RB, :] = e.astype(BF16)
                linv.append(1.0 / l)
        return jnp.concatenate(linv, axis=0)

    for u in range(min(ns - 1, len(units))):
        scores(u)
    outs = [None, None]
    for u, (t, hh) in enumerate(units):
        if u + ns - 1 < len(units):
            scores(u + ns - 1)
        linv = softmax(u)
        _, _, win = window(t)
        o = _dot(p_scr[u % 2, :, :nkw], v_ref[0, win, :]) + _dot(p_scr[u % 2, :, nkw:], vc_ref[0])
        outs[hh] = o * linv
        if hh == 1:
            o_ref[0, t * sub:(t + 1) * sub] = jnp.where(_pair_masks((sub, LANES)), outs[0], outs[1]).astype(o_ref.dtype)


def _na_attn_call(q, k, v, kc, vc, rpb):
    b, s_len, d = q.shape
    c_len = kc.shape[1]
    nh, ndr, ndc = rpb.shape
    kr, kcw = (ndr + 1) // 2, (ndc + 1) // 2
    rows = s_len // GRID_W
    nq = 4
    assert GRID_W == HEAD_DIM and rows % nq == 0 and rows >= nq + kr and (nq + kr) % 2 == 0
    assert ndc <= LANES and 2 * kr <= 2 * SUBLANES
    rpb_p = jnp.full((nh, 2 * SUBLANES, LANES), NEG, F32).at[:, :ndr, :ndc].set(rpb)
    sub = nq * GRID_W
    tq = _row_tile(s_len, 8 * sub)
    nk = (nq + kr) * GRID_W + c_len
    full = lambda n: pl.BlockSpec((1, n, LANES), lambda j, bi, i: (bi, 0, j))
    return pl.pallas_call(
        functools.partial(_na_attn_kernel, rows=rows, kr=kr, kcw=kcw, nq=nq),
        out_shape=jax.ShapeDtypeStruct((b, s_len, d), BF16),
        grid=(d // LANES, b, s_len // tq),
        in_specs=[pl.BlockSpec((1, tq, LANES), lambda j, bi, i: (bi, i, j)),
                  full(s_len), full(s_len), full(c_len), full(c_len),
                  pl.BlockSpec((2, 2 * SUBLANES, LANES), lambda j, bi, i: (j, 0, 0))],
        out_specs=pl.BlockSpec((1, tq, LANES), lambda j, bi, i: (bi, i, j)),
        scratch_shapes=[pltpu.VMEM((2, 2 * kr, GRID_W, LANES), F32), pltpu.VMEM((3, sub, nk), F32),
                        pltpu.VMEM((2, sub, nk), BF16)],
        compiler_params=_cparams(("arbitrary", "arbitrary", "arbitrary")),
        name="na_attn",
    )(q, k, v, kc, vc, rpb_p)


def _outproj_kernel(a_ref, x_ref, mod_ref, g_ref, w_ref, o_ref, z_ref):
    tm = x_ref.shape[1]
    sub = _split_rows(tm)
    for r in range(0, tm, sub):
        rows = slice(r, r + sub)
        x = x_ref[0, rows] + mod_ref[0, 2:3] * _dot(a_ref[0, rows], w_ref[...])
        o_ref[0, rows] = x
        z_ref[0, rows] = _normmod(x, g_ref[...], mod_ref[0, 3:4], mod_ref[0, 4:5]).astype(z_ref.dtype)


def _outproj_call(a, x, mods_l, mod_row, g, w):
    b, n, d = x.shape
    k = a.shape[-1]
    tm = _row_tile(n, 512)
    row = lambda w_: pl.BlockSpec((1, tm, w_), lambda bi, i: (bi, i, 0))
    return pl.pallas_call(
        _outproj_kernel,
        out_shape=(jax.ShapeDtypeStruct((b, n, d), F32), jax.ShapeDtypeStruct((b, n, d), BF16)),
        grid=(b, n // tm),
        in_specs=[row(k), row(d), _mod_spec(d, mod_row),
                  pl.BlockSpec((1, d), lambda bi, i: (0, 0)),
                  pl.BlockSpec((k, d), lambda bi, i: (0, 0))],
        out_specs=(row(d), row(d)),
        compiler_params=_cparams(("arbitrary", "arbitrary")),
        name="outproj",
    )(a, x, mods_l, g, w)


def _ffn_kernel(x_ref, z_ref, zp_ref, zn_ref, mod_ref, wu_ref, cw_ref, wd_ref, *rest, fc, final):
    if final:
        fg_ref, o_ref, z_scr, u_scr, h_scr = rest
    else:
        o_ref, z_scr, u_scr, h_scr = rest
    i = pl.program_id(1)
    nt = pl.num_programs(1)
    tm, d = x_ref.shape[1], x_ref.shape[2]
    hb = FFN_HALO
    nch = wd_ref.shape[0] // fc
    z_scr[hb:hb + tm] = z_ref[0]
    z_scr[0:hb] = jnp.where(i > 0, zp_ref[0], jnp.zeros_like(zp_ref[0]))
    z_scr[hb + tm:] = jnp.where(i < nt - 1, zn_ref[0], jnp.zeros_like(zn_ref[0]))

    f = wd_ref.shape[0]
    gate_cols = lambda j: slice(j * fc, (j + 1) * fc)
    value_cols = lambda j: slice(f + j * fc, f + (j + 1) * fc)

    def up_proj(j):
        z = z_scr[...]
        u_scr[j % 2, :, :fc] = _dot(z, wu_ref[:, gate_cols(j)])
        u_scr[j % 2, :, fc:] = _dot(z, wu_ref[:, value_cols(j)])

    def conv(j, cols, lanes, r0, sub):
        cw = cw_ref[:, cols]
        cv = cw[3:4] + cw[0:1] * u_scr[j % 2, r0 - 1:r0 - 1 + sub, lanes]
        cv = cv + cw[1:2] * u_scr[j % 2, r0:r0 + sub, lanes]
        return cv + cw[2:3] * u_scr[j % 2, r0 + 1:r0 + 1 + sub, lanes]

    def conv_act(j):
        sub = min(FFN_SUB, tm)
        for sb in range(tm // sub):
            r0 = hb + sb * sub
            hg = 0.5 * conv(j, gate_cols(j), slice(0, fc), r0, sub)
            h = (hg + hg * jnp.tanh(hg)) * conv(j, value_cols(j), slice(fc, 2 * fc), r0, sub)
            h_scr[sb * sub:(sb + 1) * sub, j * fc:(j + 1) * fc] = h.astype(BF16)

    up_proj(0)
    for j in range(nch - 1):
        up_proj(j + 1)
        conv_act(j)
    split = (nch - 1) * fc
    acc = _dot(h_scr[:, :split], wd_ref[:split, :])
    conv_act(nch - 1)
    acc = acc + _dot(h_scr[:, split:], wd_ref[split:, :])
    out = x_ref[0] + mod_ref[0, 5:6] * acc
    if final:
        out = _rms(out, fg_ref[...])
    o_ref[0] = out


def _ffn_pack_params(w_up, conv_w, conv_b, w_down, fc):
    f2 = w_up.shape[1]
    assert (f2 // 2) % fc == 0
    cw = jnp.concatenate([conv_w, conv_b[None], jnp.zeros((SUBLANES - 1 - conv_w.shape[0], f2), F32)], axis=0)
    return w_up.astype(BF16), cw, w_down.astype(BF16)


def _ffn_call(x, z, mods_l, mod_row, packed, layer, fc, final_g=None):
    b, n, d = x.shape
    wu, cw, wd = packed
    nch, fc2 = wd.shape[1] // fc, 2 * fc
    tm = _row_tile(n, FFN_TM)
    hb = FFN_HALO
    nbh = tm // hb
    once = dict(pipeline_mode=pl.Buffered(1))
    layer_spec = lambda a, **kw: pl.BlockSpec((None,) + a.shape[1:], lambda bi, i: (layer,) + (0,) * (a.ndim - 1), **kw)
    in_specs = [
        pl.BlockSpec((1, tm, d), lambda bi, i: (bi, i, 0)),
        pl.BlockSpec((1, tm, d), lambda bi, i: (bi, i, 0)),
        pl.BlockSpec((1, hb, d), lambda bi, i: (bi, jnp.maximum(i * nbh - 1, 0), 0)),
        pl.BlockSpec((1, hb, d), lambda bi, i: (bi, jnp.minimum((i + 1) * nbh, n // hb - 1), 0)),
        _mod_spec(d, mod_row),
        layer_spec(wu, **once),
        layer_spec(cw),
        layer_spec(wd, **once),
    ]
    args = [x, z, z, z, mods_l, wu, cw, wd]
    if final_g is not None:
        in_specs.append(pl.BlockSpec((1, d), lambda bi, i: (0, 0)))
        args.append(final_g)
    return pl.pallas_call(
        functools.partial(_ffn_kernel, fc=fc2 // 2, final=final_g is not None),
        out_shape=jax.ShapeDtypeStruct((b, n, d), F32),
        grid=(b, n // tm),
        in_specs=in_specs,
        out_specs=pl.BlockSpec((1, tm, d), lambda bi, i: (bi, i, 0)),
        scratch_shapes=[pltpu.VMEM((tm + 2 * hb, d), BF16), pltpu.VMEM((2, tm + 2 * hb, fc2), F32),
                        pltpu.VMEM((tm, nch * fc2 // 2), BF16)],
        compiler_params=_cparams(("arbitrary", "arbitrary")),
        name="ffn",
    )(*args)


def _rg_mixer(xl, xc, mods_l, ctx_row, g, w_in, conv_w, conv_b, wa, ba, wx, bx, lam, need_ctx):
    w = w_in.astype(BF16)
    g_l, xr_l = _rg_proj_call(xl, mods_l, None, g, w)
    g_c, xr_c = _rg_proj_call(xc, mods_l, ctx_row, g, w)
    cw, wbd, gp = _rg_pack_params(conv_w, conv_b, wa, ba, wx, bx, lam)
    m_c, m_l = _rg_scan_call(xr_c, g_c, xr_l, g_l, cw, wbd, gp)
    return m_l, (m_c if need_ctx else None)


def _na_mixer(xl, xc, mods_l, ctx_row, g, w_in, rpb, need_ctx):
    d = xl.shape[-1]
    w = w_in.astype(BF16)
    ql, kl, vl = _qkv_proj_call(xl, mods_l, None, g, w, d)
    qc, kc, vc = _qkv_proj_call(xc, mods_l, ctx_row, g, w, d)
    o_l = _na_attn_call(ql, kl, vl, kc, vc, rpb)
    o_c = _plain_attn_call(qc, [(kc, vc)], LANES) if need_ctx else None
    return o_l, o_c


def _gqa_mixer(xl, xc, mods_l, ctx_row, g, w_in, q_norm, k_norm, need_ctx):
    d = xl.shape[-1]
    nh = d // HEAD_DIM
    nkv = (w_in.shape[1] // HEAD_DIM - nh) // 2
    group = nh // nkv
    assert group % 2 == 0 and q_norm.shape[0] == HEAD_DIM
    dup = lambda wk: jnp.concatenate([wk.reshape(d, nkv, 1, HEAD_DIM)] * 2, axis=2).reshape(d, 2 * nkv * HEAD_DIM)
    nk = 2 * nkv * HEAD_DIM
    w = jnp.concatenate([w_in[:, :d], dup(w_in[:, d:d + nkv * HEAD_DIM]), dup(w_in[:, d + nkv * HEAD_DIM:])],
                        axis=1).astype(BF16)
    nseg = (d + nk) // HEAD_DIM
    assert nseg <= LANES
    seg = jnp.arange(d + nk) // HEAD_DIM
    ind = (seg[:, None] == jnp.arange(LANES)[None, :]).astype(BF16)
    gain = jnp.concatenate([jnp.tile(q_norm, nh), jnp.tile(k_norm, 2 * nkv)])[None]
    qkn = (gain, ind, ind.T)
    ql, kl, vl = _qkv_proj_call(xl, mods_l, None, g, w, nk, rope=_rope_tables(xl.shape[1]), qknorm=qkn)
    qc, kc, vc = _qkv_proj_call(xc, mods_l, ctx_row, g, w, nk, qknorm=qkn)
    qw = group * HEAD_DIM
    o_l = _plain_attn_t_call(ql, [(kl, vl), (kc, vc)], qw)
    o_c = _plain_attn_call(qc, [(kc, vc)], qw) if need_ctx else None
    return o_l, o_c


def _diff_mixer(xl, xc, mods_l, ctx_row, g, w_in, lq1, lk1, lq2, lk2, subln_g, lambda_init, need_ctx):
    d = xl.shape[-1]
    assert lq1.shape[0] == HEAD_DIM and subln_g.shape[0] == LANES
    w = w_in.astype(BF16)
    ql, kl, vl = _qkv_proj_call(xl, mods_l, None, g, w, d, rope=_rope_tables(xl.shape[1]))
    qc, kc, vc = _qkv_proj_call(xc, mods_l, ctx_row, g, w, d)
    lam_params = jnp.concatenate([jnp.stack([lq1, lk1, lq2, lk2]), jnp.zeros((SUBLANES - 4, HEAD_DIM), F32)])
    o_l = _diff_attn_call(ql, [(kl, vl), (kc, vc)], lam_params, subln_g[None], lambda_init)
    o_c = _diff_attn_call(qc, [(kc, vc)], lam_params, subln_g[None], lambda_init) if need_ctx else None
    return o_l, o_c


def kernel(x, c, ctx, c_ctx, mod_w, mod_b, norm1_g, norm2_g, rg_w_in, rg_conv_w, rg_conv_b, rg_wa, rg_ba, rg_wx, rg_bx, rg_lam, rg_w_out, na_w_in, na_rpb, na_w_out, gqa_w_in, gqa_q_norm, gqa_k_norm, gqa_w_out, diff_w_in, diff_lq1, diff_lk1, diff_lq2, diff_lk2, diff_subln_g, diff_w_out, ffn_w_up, ffn_conv_w, ffn_conv_b, ffn_w_down, final_g):
    bsz, _, d = x.shape
    depth = mod_w.shape[0]
    assert d % LANES == 0
    ctx_row = bsz
    nrow = -(-(bsz + 1) // SUBLANES) * SUBLANES
    cc = jnp.concatenate([c, c_ctx[None], jnp.zeros((nrow - bsz - 1, d), F32)], axis=0)
    mods = _mods_call(cc, mod_w, mod_b).reshape(depth, nrow, 6, d)
    f = ffn_w_down.shape[1]
    fc = next(t for t in (256, 128) if f % t == 0)
    packed = jax.vmap(functools.partial(_ffn_pack_params, fc=fc))(ffn_w_up, ffn_conv_w, ffn_conv_b, ffn_w_down)
    xl, xc = x, ctx
    for l in range(depth):
        m, j = l % N_MIXERS, l // N_MIXERS
        need_ctx = l < depth - 1
        mods_l = mods[l]
        g1, g2 = norm1_g[l][None], norm2_g[l][None]
        if m == 0:
            a_l, a_c = _rg_mixer(xl, xc, mods_l, ctx_row, g1, rg_w_in[j], rg_conv_w[j], rg_conv_b[j], rg_wa[j],
                                 rg_ba[j], rg_wx[j], rg_bx[j], rg_lam[j], need_ctx)
            w_out = rg_w_out[j]
        elif m == 1:
            a_l, a_c = _na_mixer(xl, xc, mods_l, ctx_row, g1, na_w_in[j], na_rpb[j], need_ctx)
            w_out = na_w_out[j]
        elif m == 2:
            a_l, a_c = _gqa_mixer(xl, xc, mods_l, ctx_row, g1, gqa_w_in[j], gqa_q_norm[j], gqa_k_norm[j], need_ctx)
            w_out = gqa_w_out[j]
        else:
            lambda_init = 0.8 - 0.6 * math.exp(-0.3 * l)
            a_l, a_c = _diff_mixer(xl, xc, mods_l, ctx_row, g1, diff_w_in[j], diff_lq1[j], diff_lk1[j],
                                   diff_lq2[j], diff_lk2[j], diff_subln_g[j], lambda_init, need_ctx)
            w_out = diff_w_out[j]
        w_out = w_out.astype(BF16)
        last = l == depth - 1
        xl, zl = _outproj_call(a_l, xl, mods_l, None, g2, w_out)
        xl = _ffn_call(xl, zl, mods_l, None, packed, l, fc, final_g[None] if last else None)
        if need_ctx:
            xc, zc = _outproj_call(a_c, xc, mods_l, ctx_row, g2, w_out)
            xc = _ffn_call(xc, zc, mods_l, ctx_row, packed, l, fc)
    return xl
```

```python
import functools
import math

import jax
import jax.numpy as jnp
from jax import lax
from jax.experimental import pallas as pl
from jax.experimental.pallas import tpu as pltpu

F32 = jnp.float32
BF16 = jnp.bfloat16
EPS = 1e-6
ROPE_THETA = 10000.0
GRID_W = 64
RG_C = 8.0
RG_CONV_LEFT = 2
N_MIXERS = 4
HEAD_DIM = 64
LOG2E = 1.4426950408889634
Q_SCALE = HEAD_DIM ** -0.5 * LOG2E
LANES = 128
SUBLANES = 8
NEG = -1e30
FFN_HALO = 16
FFN_SUB = 512
FFN_TM = 512
PROJ_SPLIT = 2
PROJ_MIN_ROWS = 256
ATT_SUB = 256
ATT_RB = 32
DIFF_RB = 16
KEY_SLAB = 32
VMEM_LIMIT_BYTES = 56 * 1024 * 1024

_NT = (((1,), (1,)), ((), ()))


def _cparams(sem):
    return pltpu.CompilerParams(dimension_semantics=sem, vmem_limit_bytes=VMEM_LIMIT_BYTES)


def _dot(a, b):
    return jnp.dot(a, b, preferred_element_type=F32)


def _dot_nt(a, b):
    return lax.dot_general(a, b, _NT, preferred_element_type=F32)


def _silu(x):
    return x / (1.0 + jnp.exp(-x))


def _gelu_tanh(x):
    cdf = 0.5 * (1.0 + jnp.tanh(math.sqrt(2.0 / math.pi) * (x + 0.044715 * (x * x * x))))
    return x * cdf


def _rms(x, g):
    y = x * lax.rsqrt(jnp.mean(x * x, axis=-1, keepdims=True) + EPS)
    return y * g


def _normmod(x, g, shift, scale):
    return _rms(x, g) * (1.0 + scale) + shift


def _mods_kernel(cc_ref, w_ref, b_ref, o_ref):
    a = _silu(cc_ref[...]).astype(BF16)
    o_ref[0] = _dot(a, w_ref[0].astype(BF16)) + b_ref[0]


def _mods_call(cc, mod_w, mod_b):
    depth, d, n = mod_w.shape
    r = cc.shape[0]
    tn = n // 4
    return pl.pallas_call(
        _mods_kernel,
        out_shape=jax.ShapeDtypeStruct((depth, r, n), F32),
        grid=(depth, n // tn),
        in_specs=[
            pl.BlockSpec((r, d), lambda l, j: (0, 0)),
            pl.BlockSpec((1, d, tn), lambda l, j: (l, 0, j)),
            pl.BlockSpec((1, 1, tn), lambda l, j: (l, 0, j)),
        ],
        out_specs=pl.BlockSpec((1, r, tn), lambda l, j: (l, 0, j)),
        compiler_params=_cparams(("arbitrary", "arbitrary")),
        name="mods",
    )(cc, mod_w, mod_b.reshape(depth, 1, n))


def _mod_spec(d, row):
    if row is None:
        return pl.BlockSpec((1, 6, d), lambda b, *_: (b, 0, 0))
    return pl.BlockSpec((1, 6, d), lambda b, *_: (row, 0, 0))


def _split_rows(tm):
    sub = tm // PROJ_SPLIT
    return sub if (tm % PROJ_SPLIT == 0 and sub >= PROJ_MIN_ROWS) else tm


def _row_tile(n, pref):
    t = min(n, pref)
    assert n % t == 0
    return t


def _rope_tables(n_tok):
    t = jnp.arange(n_tok)
    row = (t // GRID_W).astype(F32)
    col = (t % GRID_W).astype(F32)
    n = HEAD_DIM // 4
    inv = ROPE_THETA ** (-jnp.arange(n, dtype=F32) / n)
    ang = jnp.concatenate([row[:, None] * inv, col[:, None] * inv], axis=-1)
    cos, sin = jnp.cos(ang), jnp.sin(ang)
    reps = LANES // HEAD_DIM
    cos_t = jnp.tile(jnp.concatenate([cos, cos], axis=-1), (1, reps))
    sin_t = jnp.tile(jnp.concatenate([-sin, sin], axis=-1), (1, reps))
    return cos_t, sin_t


def _swap_halves(x):
    lane = lax.broadcasted_iota(jnp.int32, x.shape, 1)
    lo = (lane & (HEAD_DIM // 2)) == 0
    return jnp.where(lo, pltpu.roll(x, LANES - HEAD_DIM // 2, 1), pltpu.roll(x, HEAD_DIM // 2, 1))


def _rope_cols(u, cos_t, sin_t):
    outs = []
    for c in range(u.shape[1] // LANES):
        xc = u[:, c * LANES:(c + 1) * LANES]
        outs.append(xc * cos_t + _swap_halves(xc) * sin_t)
    return jnp.concatenate(outs, axis=1)


def _rg_proj_kernel(x_ref, mod_ref, g_ref, w_ref, go_ref, xo_ref, *, dr):
    tm = x_ref.shape[1]
    sub = _split_rows(tm)
    for r in range(0, tm, sub):
        rows = slice(r, r + sub)
        z = _normmod(x_ref[0, rows], g_ref[...], mod_ref[0, 0:1], mod_ref[0, 1:2]).astype(BF16)
        u = _dot(z, w_ref[...])
        go_ref[0, rows] = _gelu_tanh(u[:, :dr])
        xo_ref[0, rows] = u[:, dr:]


def _rg_proj_call(x, mods_l, mod_row, g, w):
    b, n, d = x.shape
    dr = w.shape[1] // 2
    tm = _row_tile(n, 512)
    return pl.pallas_call(
        functools.partial(_rg_proj_kernel, dr=dr),
        out_shape=(jax.ShapeDtypeStruct((b, n, dr), F32), jax.ShapeDtypeStruct((b, n, dr), F32)),
        grid=(b, n // tm),
        in_specs=[
            pl.BlockSpec((1, tm, d), lambda bi, i: (bi, i, 0)),
            _mod_spec(d, mod_row),
            pl.BlockSpec((1, d), lambda bi, i: (0, 0)),
            pl.BlockSpec((d, 2 * dr), lambda bi, i: (0, 0)),
        ],
        out_specs=(pl.BlockSpec((1, tm, dr), lambda bi, i: (bi, i, 0)),
                   pl.BlockSpec((1, tm, dr), lambda bi, i: (bi, i, 0))),
        compiler_params=_cparams(("arbitrary", "arbitrary")),
        name="rg_proj",
    )(x, mods_l, g, w)


def _qkv_proj_kernel(x_ref, mod_ref, g_ref, w_ref, *rest, d, rope, qknorm):
    rest = list(rest)
    if rope:
        cos_ref, sin_ref = rest.pop(0), rest.pop(0)
    if qknorm:
        gain_ref, ind_ref, indt_ref = rest.pop(0), rest.pop(0), rest.pop(0)
    q_ref, k_ref, v_ref = rest
    nk = k_ref.shape[-1]
    tm = x_ref.shape[1]
    sub = _split_rows(tm)
    for r in range(0, tm, sub):
        rows = slice(r, r + sub)
        z = _normmod(x_ref[0, rows], g_ref[...], mod_ref[0, 0:1], mod_ref[0, 1:2]).astype(BF16)
        u = _dot(z, w_ref[...])
        qk = u[:, :d + nk]
        if qknorm:
            x2 = qk * qk
            hi = x2.astype(BF16)
            lo = (x2 - hi.astype(F32)).astype(BF16)
            ssq = _dot(hi, ind_ref[...]) + _dot(lo, ind_ref[...])
            rs = lax.rsqrt(ssq * (1.0 / HEAD_DIM) + EPS)
            rhi = rs.astype(BF16)
            rlo = (rs - rhi.astype(F32)).astype(BF16)
            rb = _dot(rhi, indt_ref[...]) + _dot(rlo, indt_ref[...])
            qk = (qk * rb) * gain_ref[...]
        if rope:
            qk = _rope_cols(qk, cos_ref[rows], sin_ref[rows])
        q_ref[0, rows] = (qk[:, :d] * Q_SCALE).astype(q_ref.dtype)
        k_ref[0, rows] = qk[:, d:].astype(k_ref.dtype)
        v_ref[0, rows] = u[:, d + nk:].astype(v_ref.dtype)


def _qkv_proj_call(x, mods_l, mod_row, g, w, nk, *, rope=None, qknorm=None):
    b, n, d = x.shape
    nv = w.shape[1] - d - nk
    tm = _row_tile(n, 512)
    const = lambda bi, i: (0, 0)
    in_specs = [
        pl.BlockSpec((1, tm, d), lambda bi, i: (bi, i, 0)),
        _mod_spec(d, mod_row),
        pl.BlockSpec((1, d), const),
        pl.BlockSpec(w.shape, const),
    ]
    args = [x, mods_l, g, w]
    if rope is not None:
        in_specs += [pl.BlockSpec((tm, LANES), lambda bi, i: (i, 0))] * 2
        args += list(rope)
    if qknorm is not None:
        in_specs += [pl.BlockSpec(a.shape, const) for a in qknorm]
        args += list(qknorm)
    out_spec = lambda w_: pl.BlockSpec((1, tm, w_), lambda bi, i: (bi, i, 0))
    return pl.pallas_call(
        functools.partial(_qkv_proj_kernel, d=d, rope=rope is not None, qknorm=qknorm is not None),
        out_shape=(jax.ShapeDtypeStruct((b, n, d), BF16), jax.ShapeDtypeStruct((b, n, nk), BF16),
                   jax.ShapeDtypeStruct((b, n, nv), BF16)),
        grid=(b, n // tm),
        in_specs=in_specs,
        out_specs=(out_spec(d), out_spec(nk), out_spec(nv)),
        compiler_params=_cparams(("arbitrary", "arbitrary")),
        name="qkv_proj",
    )(*args)


def _rg_conv(xs_scr, x, prev, nxt, cw):
    t = x.shape[0]
    h = SUBLANES
    xs_scr[0:h] = prev
    xs_scr[h:h + t] = x
    xs_scr[h + t:2 * h + t] = nxt
    y = cw[4:5] + cw[0:1] * xs_scr[h - 2:h - 2 + t]
    y = y + cw[1:2] * xs_scr[h - 1:h - 1 + t]
    y = y + cw[2:3] * x
    y = y + cw[3:4] * xs_scr[h + 1:h + 1 + t]
    return y


def _sigmoid(x):
    return 0.5 * jnp.tanh(0.5 * x) + 0.5


def _rg_gates(xc, w, ba, bx, lam):
    gw = xc.shape[1]
    y = _dot(xc.astype(BF16), w)
    r = _sigmoid(y[:, :gw] + ba)
    i = _sigmoid(y[:, gw:] + bx)
    nl = -lam
    softplus = jnp.maximum(nl, 0.0) + jnp.log1p(jnp.exp(-jnp.abs(nl)))
    log_a = r * (-RG_C * softplus)
    a = jnp.exp(log_a)
    one_minus_a2 = -jnp.tanh(log_a) * (a * a + 1.0)
    root = jnp.where(one_minus_a2 > 0.0, one_minus_a2 * lax.rsqrt(one_minus_a2), 0.0)
    return a, root * (i * xc)


def _rg_scan_tile(a, b, carry, reverse):
    t, gw = a.shape
    nck = t // SUBLANES
    a = a.reshape(nck, SUBLANES, gw)
    b = b.reshape(nck, SUBLANES, gw)
    rmod = lax.broadcasted_iota(jnp.int32, (1, SUBLANES, 1), 1)
    for s in (1, 2, 4):
        ok = (rmod < SUBLANES - s) if reverse else (rmod >= s)
        shift = SUBLANES - s if reverse else s
        ash, bsh = pltpu.roll(a, shift, 1), pltpu.roll(b, shift, 1)
        b = a * jnp.where(ok, bsh, 0.0) + b
        a = a * jnp.where(ok, ash, 1.0)
    hs = [None] * nck
    for ck in (range(nck - 1, -1, -1) if reverse else range(nck)):
        hc = b[ck] + a[ck] * carry
        carry = hc[0:1] if reverse else hc[SUBLANES - 1:SUBLANES]
        hs[ck] = hc
    return jnp.concatenate(hs, axis=0), carry


def _rg_scan_kernel(xc_ref, gc_ref, xl_ref, xp_ref, xn_ref, gl_ref, cw_ref, w_ref, gp_ref,
                    mc_ref, ml_ref, hf_scr, xv_scr, car_scr, xs_scr, *, nt):
    s = pl.program_id(2)
    gw = xl_ref.shape[-1]
    tl = xl_ref.shape[1]
    cw = cw_ref[0]
    gp = gp_ref[0]

    def gates(xconv, d):
        w = w_ref[0, :, d * 2 * gw:(d + 1) * 2 * gw]
        return _rg_gates(xconv, w, gp[3 * d:3 * d + 1], gp[3 * d + 1:3 * d + 2], gp[3 * d + 2:3 * d + 3])

    @pl.when(s == 0)
    def _ctx():
        zeros8 = jnp.zeros((SUBLANES, gw), F32)
        xconv = _rg_conv(xs_scr, xc_ref[0], zeros8, zeros8, cw)
        zero = jnp.zeros((1, gw), F32)
        a0, b0 = gates(xconv, 0)
        hf, cf = _rg_scan_tile(a0, b0, zero, False)
        a1, b1 = gates(xconv, 1)
        hb, cb = _rg_scan_tile(a1, b1, zero, True)
        car_scr[0:1] = cf
        car_scr[1:2] = cb
        mc_ref[0] = (gc_ref[0] * (hf + hb)).astype(mc_ref.dtype)

    def lat_conv(t):
        prev = xp_ref[0] * (t > 0).astype(F32)
        nxt = xn_ref[0] * (t < nt - 1).astype(F32)
        return _rg_conv(xs_scr, xl_ref[0], prev, nxt, cw)

    @pl.when((s >= 1) & (s <= nt))
    def _fwd():
        t = s - 1
        xconv = lat_conv(t)
        xv_scr[pl.ds(pl.multiple_of(t * tl, tl), tl), :] = xconv
        a0, b0 = gates(xconv, 0)
        h, c = _rg_scan_tile(a0, b0, car_scr[0:1], False)
        car_scr[0:1] = c
        hf_scr[pl.ds(pl.multiple_of(t * tl, tl), tl), :] = h

    @pl.when(s > nt)
    def _bwd():
        t = 2 * nt - s
        a1, b1 = gates(xv_scr[pl.ds(pl.multiple_of(t * tl, tl), tl), :], 1)
        h, c = _rg_scan_tile(a1, b1, car_scr[1:2], True)
        car_scr[1:2] = c
        hf = hf_scr[pl.ds(pl.multiple_of(t * tl, tl), tl), :]
        ml_ref[0] = (gl_ref[0] * (hf + h)).astype(ml_ref.dtype)


def _rg_scan_call(xr_c, g_c, xr_l, g_l, cw, wbd, gp):
    b, s_len, dr = xr_l.shape
    c_len = xr_c.shape[1]
    ng, gw = wbd.shape[0], wbd.shape[1]
    tl = _row_tile(s_len, 512)
    nt = s_len // tl
    nb8 = tl // SUBLANES

    def tile(s):
        return jnp.clip(s - 1, 0, nt - 1)

    def tile_bwd(s):
        return jnp.where(s <= nt, nt - 1, 2 * nt - s)

    return pl.pallas_call(
        functools.partial(_rg_scan_kernel, nt=nt),
        out_shape=(jax.ShapeDtypeStruct((b, c_len, dr), BF16), jax.ShapeDtypeStruct((b, s_len, dr), BF16)),
        grid=(b, ng, 2 * nt + 1),
        in_specs=[
            pl.BlockSpec((1, c_len, gw), lambda bi, h, s: (bi, 0, h)),
            pl.BlockSpec((1, c_len, gw), lambda bi, h, s: (bi, 0, h)),
            pl.BlockSpec((1, tl, gw), lambda bi, h, s: (bi, tile(s), h)),
            pl.BlockSpec((1, SUBLANES, gw), lambda bi, h, s: (bi, jnp.maximum(tile(s) * nb8 - 1, 0), h)),
            pl.BlockSpec((1, SUBLANES, gw),
                         lambda bi, h, s: (bi, jnp.minimum((tile(s) + 1) * nb8, s_len // SUBLANES - 1), h)),
            pl.BlockSpec((1, tl, gw), lambda bi, h, s: (bi, tile_bwd(s), h)),
            pl.BlockSpec((1, SUBLANES, gw), lambda bi, h, s: (h, 0, 0)),
            pl.BlockSpec((1, gw, 4 * gw), lambda bi, h, s: (h, 0, 0)),
            pl.BlockSpec((1, SUBLANES, gw), lambda bi, h, s: (h, 0, 0)),
        ],
        out_specs=(pl.BlockSpec((1, c_len, gw), lambda bi, h, s: (bi, 0, h)),
                   pl.BlockSpec((1, tl, gw), lambda bi, h, s: (bi, tile_bwd(s), h))),
        scratch_shapes=[pltpu.VMEM((s_len, gw), F32), pltpu.VMEM((s_len, gw), F32), pltpu.VMEM((SUBLANES, gw), F32),
                        pltpu.VMEM((max(tl, c_len) + 2 * SUBLANES, gw), F32)],
        compiler_params=_cparams(("arbitrary", "arbitrary", "arbitrary")),
        name="rg_scan",
    )(xr_c, g_c, xr_l, xr_l, xr_l, g_l, cw, wbd, gp)


def _rg_pack_params(conv_w, conv_b, wa, ba, wx, bx, lam):
    nblk, bw = wa.shape[1], wa.shape[2]
    dr = nblk * bw
    per = next(p for p in range(1, nblk + 1) if nblk % p == 0 and (p * bw) % LANES == 0)
    ng, gw = nblk // per, per * bw

    def dense(w):
        w = w.reshape(ng, per, bw, bw)
        eye = jnp.eye(per, dtype=w.dtype)
        return jnp.einsum('gpjk,pq->gpjqk', w, eye).reshape(ng, gw, gw)

    wbd = jnp.concatenate([dense(wa[0]), dense(wx[0]), dense(wa[1]), dense(wx[1])], axis=-1).astype(BF16)
    cw = jnp.concatenate([conv_w, conv_b[None], jnp.zeros((SUBLANES - 1 - conv_w.shape[0], dr), F32)], axis=0)
    cw = cw.reshape(SUBLANES, ng, gw).transpose(1, 0, 2)
    gp = jnp.stack([ba[0], bx[0], lam[0], ba[1], bx[1], lam[1], jnp.zeros_like(lam[0]), jnp.zeros_like(lam[0])])
    gp = gp.reshape(SUBLANES, ng, gw).transpose(1, 0, 2)
    return cw, wbd, gp


def _pair_masks(shape):
    lane = lax.broadcasted_iota(jnp.int32, shape, 1)
    return lane < HEAD_DIM


def _half_q(q, hh):
    first = _pair_masks(q.shape)
    return jnp.where(first if hh == 0 else jnp.logical_not(first), q, jnp.zeros_like(q))


def _scores_to_scratch(qh, k_refs, s_scr, slot):
    off = 0
    for k_ref in k_refs:
        n = k_ref.shape[1]
        s_scr[slot, :, off:off + n] = _dot_nt(qh, k_ref[0])
        off += n


def _exp_rows(s):
    e = jnp.exp2(s - s.max(axis=-1, keepdims=True))
    return e, e.sum(axis=-1, keepdims=True)


def _softmax_to_scratch(s_scr, p_scr, slot, rows):
    linv = []
    for r in range(0, rows, ATT_RB):
        e, l = _exp_rows(s_scr[slot, r:r + ATT_RB, :])
        p_scr[slot, r:r + ATT_RB, :] = e.astype(BF16)
        linv.append(1.0 / l)
    return jnp.concatenate(linv, axis=0)


def _pv_from_scratch(p_scr, slot, v_refs):
    off, o = 0, None
    for v_ref in v_refs:
        n = v_ref.shape[1]
        t = _dot(p_scr[slot, :, off:off + n], v_ref[0])
        o = t if o is None else o + t
        off += n
    return o


def _plain_attn_kernel(q_ref, *refs, nsrc):
    k_refs, v_refs = refs[0:2 * nsrc:2], refs[1:2 * nsrc:2]
    o_ref, s_scr, p_scr = refs[2 * nsrc:]
    tq, qw = q_ref.shape[1], q_ref.shape[2]
    sub = s_scr.shape[1]
    units = [(r, c, hh) for r in range(0, tq, sub) for c in range(0, qw, LANES) for hh in range(2)]

    def scores(u):
        r, c, hh = units[u]
        _scores_to_scratch(_half_q(q_ref[0, r:r + sub, c:c + LANES], hh), k_refs, s_scr, u % 2)

    scores(0)
    outs = [None, None]
    for u, (r, c, hh) in enumerate(units):
        if u + 1 < len(units):
            scores(u + 1)
        linv = _softmax_to_scratch(s_scr, p_scr, u % 2, sub)
        outs[hh] = _pv_from_scratch(p_scr, u % 2, v_refs) * linv
        if hh == 1:
            o_ref[0, r:r + sub, c:c + LANES] = jnp.where(_pair_masks((sub, LANES)), outs[0], outs[1]).astype(o_ref.dtype)


def _plain_attn_t_kernel(q_ref, *refs, nsrc):
    k_refs, v_refs = refs[0:2 * nsrc:2], refs[1:2 * nsrc:2]
    o_ref = refs[2 * nsrc]
    vt_scrs = refs[2 * nsrc + 1:3 * nsrc + 1]
    s_scr, p_scr = refs[3 * nsrc + 1:]
    tq, qw = q_ref.shape[1], q_ref.shape[2]
    sub = s_scr.shape[2]

    @pl.when(pl.program_id(2) == 0)
    def _transpose_values():
        for v_ref, vt_scr in zip(v_refs, vt_scrs):
            vt_scr[...] = v_ref[0].astype(F32).T.astype(BF16)

    units = [(r, c, hh) for r in range(0, tq, sub) for c in range(0, qw, LANES) for hh in range(2)]

    def scores(u):
        r, c, hh = units[u]
        qh = _half_q(q_ref[0, r:r + sub, c:c + LANES], hh)
        off = 0
        for k_ref in k_refs:
            n = k_ref.shape[1]
            s_scr[u % ns, off:off + n, :] = _dot_nt(k_ref[0], qh)
            off += n

    ns = s_scr.shape[0]
    for u in range(min(ns - 1, len(units))):
        scores(u)
    outs = [None, None]
    for u, (r, c, hh) in enumerate(units):
        if u + ns - 1 < len(units):
            scores(u + ns - 1)
        slabs = [slice(i, i + KEY_SLAB) for i in range(0, s_scr.shape[1], KEY_SLAB)]
        macc = s_scr[u % ns, slabs[0], :]
        for sl in slabs[1:]:
            macc = jnp.maximum(macc, s_scr[u % ns, sl, :])
        m = jnp.max(macc, axis=0, keepdims=True)
        lacc = None
        for sl in slabs:
            e = jnp.exp2(s_scr[u % ns, sl, :] - m)
            p_scr[u % 2, sl, :] = e.astype(BF16)
            lacc = e if lacc is None else lacc + e
        linv = 1.0 / jnp.sum(lacc, axis=0, keepdims=True)
        off, ot = 0, None
        for vt_scr in vt_scrs:
            n = vt_scr.shape[1]
            t = _dot(vt_scr[hh * HEAD_DIM:(hh + 1) * HEAD_DIM, :], p_scr[u % 2, off:off + n, :])
            ot = t if ot is None else ot + t
            off += n
        outs[hh] = ot * linv
        if hh == 1:
            o_ref[0, r:r + sub, c:c + LANES] = jnp.concatenate(outs, axis=0).T.astype(o_ref.dtype)


def _plain_attn_t_call(q, kvs, qw):
    b, lq, d = q.shape
    tq = _row_tile(lq, 4 * ATT_SUB)
    sub = min(tq, ATT_SUB)
    nk = sum(k.shape[1] for k, _ in kvs)
    in_specs = [pl.BlockSpec((1, tq, qw), lambda bi, j, i: (bi, i, j))]
    args = [q]
    for k, v in kvs:
        spec = pl.BlockSpec((1, k.shape[1], LANES), lambda bi, j, i: (bi, 0, j))
        in_specs += [spec, spec]
        args += [k, v]
    return pl.pallas_call(
        functools.partial(_plain_attn_t_kernel, nsrc=len(kvs)),
        out_shape=jax.ShapeDtypeStruct((b, lq, d), BF16),
        grid=(b, d // qw, lq // tq),
        in_specs=in_specs,
        out_specs=pl.BlockSpec((1, tq, qw), lambda bi, j, i: (bi, i, j)),
        scratch_shapes=([pltpu.VMEM((LANES, k.shape[1]), BF16) for k, _ in kvs]
                        + [pltpu.VMEM((3, nk, sub), F32), pltpu.VMEM((2, nk, sub), BF16)]),
        compiler_params=_cparams(("arbitrary", "arbitrary", "arbitrary")),
        name="plain_attn_t",
    )(*args)


def _plain_attn_call(q, kvs, qw):
    b, lq, d = q.shape
    tq = _row_tile(lq, 2 * ATT_SUB)
    sub = min(tq, ATT_SUB)
    nk = sum(k.shape[1] for k, _ in kvs)
    in_specs = [pl.BlockSpec((1, tq, qw), lambda bi, j, i: (bi, i, j))]
    args = [q]
    for k, v in kvs:
        spec = pl.BlockSpec((1, k.shape[1], LANES), lambda bi, j, i: (bi, 0, j))
        in_specs += [spec, spec]
        args += [k, v]
    return pl.pallas_call(
        functools.partial(_plain_attn_kernel, nsrc=len(kvs)),
        out_shape=jax.ShapeDtypeStruct((b, lq, d), BF16),
        grid=(b, d // qw, lq // tq),
        in_specs=in_specs,
        out_specs=pl.BlockSpec((1, tq, qw), lambda bi, j, i: (bi, i, j)),
        scratch_shapes=[pltpu.VMEM((2, sub, nk), F32), pltpu.VMEM((2, sub, nk), BF16)],
        compiler_params=_cparams(("arbitrary", "arbitrary", "arbitrary")),
        name="plain_attn",
    )(*args)


def _diff_attn_kernel(q_ref, lam_ref, g_ref, *refs, nsrc, lambda_init):
    k_refs, v_refs = refs[0:2 * nsrc:2], refs[1:2 * nsrc:2]
    o_ref, s_scr, a_scr = refs[2 * nsrc:]
    lp = lam_ref[...]
    lam = (jnp.exp(jnp.sum(lp[0:1] * lp[1:2], axis=-1, keepdims=True))
           - jnp.exp(jnp.sum(lp[2:3] * lp[3:4], axis=-1, keepdims=True)) + lambda_init)
    tq = q_ref.shape[1]
    sub = s_scr.shape[1]
    ns = s_scr.shape[0] // 2
    units = list(range(0, tq, sub))

    def scores(u):
        q = q_ref[0, units[u]:units[u] + sub]
        for hh in range(2):
            _scores_to_scratch(_half_q(q, hh), k_refs, s_scr, 2 * (u % ns) + hh)

    def combine(u):
        linv = []
        for r in range(0, sub, DIFF_RB):
            e0, l0 = _exp_rows(s_scr[2 * (u % ns), r:r + DIFF_RB, :])
            e1, l1 = _exp_rows(s_scr[2 * (u % ns) + 1, r:r + DIFF_RB, :])
            a_scr[u % 2, r:r + DIFF_RB, :] = (e0 - e1 * (lam * l0 / l1)).astype(BF16)
            linv.append(1.0 / l0)
        return jnp.concatenate(linv, axis=0)

    for u in range(min(ns - 1, len(units))):
        scores(u)
    for u, r in enumerate(units):
        if u + ns - 1 < len(units):
            scores(u + ns - 1)
        linv = combine(u)
        o = _pv_from_scratch(a_scr, u % 2, v_refs) * linv
        o_ref[0, r:r + sub] = (_rms(o, g_ref[...]) * (1.0 - lambda_init)).astype(o_ref.dtype)


def _diff_attn_call(q, kvs, lam_params, subln_g, lambda_init):
    b, lq, d = q.shape
    tq = _row_tile(lq, 8 * ATT_SUB)
    sub = min(tq, ATT_SUB)
    nk = sum(k.shape[1] for k, _ in kvs)
    const = lambda bi, j, i: (0, 0)
    in_specs = [pl.BlockSpec((1, tq, LANES), lambda bi, j, i: (bi, i, j)),
                pl.BlockSpec(lam_params.shape, const), pl.BlockSpec(subln_g.shape, const)]
    args = [q, lam_params, subln_g]
    for k, v in kvs:
        spec = pl.BlockSpec((1, k.shape[1], LANES), lambda bi, j, i: (bi, 0, j))
        in_specs += [spec, spec]
        args += [k, v]
    return pl.pallas_call(
        functools.partial(_diff_attn_kernel, nsrc=len(kvs), lambda_init=lambda_init),
        out_shape=jax.ShapeDtypeStruct((b, lq, d), BF16),
        grid=(b, d // LANES, lq // tq),
        in_specs=in_specs,
        out_specs=pl.BlockSpec((1, tq, LANES), lambda bi, j, i: (bi, i, j)),
        scratch_shapes=[pltpu.VMEM((6, sub, nk), F32), pltpu.VMEM((2, sub, nk), BF16)],
        compiler_params=_cparams(("arbitrary", "arbitrary", "arbitrary")),
        name="diff_attn",
    )(*args)


def _na_attn_kernel(q_ref, k_ref, v_ref, kc_ref, vc_ref, rpb_ref, o_ref, t2_scr, s_scr, p_scr, *, rows, kr, kcw, nq):
    w = GRID_W
    nwin = nq + kr
    ntab = 2 * kr
    nkw = nwin * w
    sub = nq * w
    nsub = q_ref.shape[1] // sub
    i = pl.program_id(2)

    @pl.when((i == 0) & (pl.program_id(1) == 0))
    def _build_bias():
        c_io = lax.broadcasted_iota(jnp.int32, (w, LANES), 0)
        cp_io = lax.broadcasted_iota(jnp.int32, (w, LANES), 1)
        cs = jnp.clip(c_io - kcw // 2, 0, w - kcw)
        colok = (cp_io >= cs) & (cp_io < cs + kcw)
        neg = jnp.full((w, LANES), NEG, F32)
        for hh in range(2):
            r = pltpu.roll(rpb_ref[hh] * LOG2E, LANES - (kcw - 1), 1)
            ts = []
            for e in range(ntab - 1):
                t = jnp.broadcast_to(r[e:e + 1], (w, LANES))
                t = pltpu.roll(t, 0, 1, stride=1, stride_axis=0)
                ts.append(jnp.where(colok, t, NEG))
            for e in range(ntab):
                lo = ts[e - 1] if e >= 1 else neg
                hi = ts[e] if e < ntab - 1 else neg
                t2_scr[hh, e] = jnp.where(cp_io < w, lo, pltpu.roll(hi, w, 1))

    units = [(t, hh) for t in range(nsub) for hh in range(2)]
    ns = s_scr.shape[0]

    def window(t):
        r0 = (i * nsub + t) * nq
        ks = jnp.clip(r0 - kr // 2, 0, rows - nwin)
        return r0, ks, pl.ds(pl.multiple_of(ks * w, w), nkw)

    def scores(u):
        t, hh = units[u]
        _, _, win = window(t)
        qh = _half_q(q_ref[0, t * sub:(t + 1) * sub], hh)
        s_scr[u % ns, :, :nkw] = _dot_nt(qh, k_ref[0, win, :])
        s_scr[u % ns, :, nkw:] = _dot_nt(qh, kc_ref[0])

    def softmax(u):
        t, hh = units[u]
        r0, ks, _ = window(t)
        key_row = ks + jnp.right_shift(lax.broadcasted_iota(jnp.int32, (1, nkw), 1), int(math.log2(w)))
        linv = []
        for qi in range(nq):
            r = r0 + qi
            rs = jnp.clip(r - kr // 2, 0, rows - kr)
            rowok = (key_row >= rs) & (key_row < rs + kr)
            idx = [jnp.clip(ks + 2 * p - r + kr, 0, ntab - 1) for p in range(nwin // 2)]
            for rb in range(0, w, ATT_RB):
                row = qi * w + rb
                bias = jnp.concatenate([t2_scr[hh, e, rb:rb + ATT_RB, :] for e in idx], axis=1)
                sw = jnp.where(rowok, s_scr[u % ns, row:row + ATT_RB, :nkw] + bias, NEG)
                e, l = _exp_rows(jnp.concatenate([sw, s_scr[u % ns, row:row + ATT_RB, nkw:]], axis=1))
                p_scr[u % 2, row:row + ATT_RB, :] = e.astype(BF16)
                linv.append(1.0 / l)
        return jnp.concatenate(linv, axis=0)

    for u in range(min(ns - 1, len(units))):
        scores(u)
    outs = [None, None]
    for u, (t, hh) in enumerate(units):
        if u + ns - 1 < len(units):
            scores(u + ns - 1)
        linv = softmax(u)
        _, _, win = window(t)
        o = _dot(p_scr[u % 2, :, :nkw], v_ref[0, win, :]) + _dot(p_scr[u % 2, :, nkw:], vc_ref[0])
        outs[hh] = o * linv
        if hh == 1:
            o_ref[0, t * sub:(t + 1) * sub] = jnp.where(_pair_masks((sub, LANES)), outs[0], outs[1]).astype(o_ref.dtype)


def _na_attn_t_kernel(q_ref, k_ref, v_ref, kc_ref, vc_ref, rpb_ref, o_ref, tt2_scr, vt_scr, vct_scr, s_scr, p_scr, *,
                      rows, kr, kcw, nq):
    w = GRID_W
    nwin = nq + kr
    ntab = 2 * kr
    nkw = nwin * w
    sub = nq * w
    nk = s_scr.shape[1]
    nsub = q_ref.shape[1] // sub
    i = pl.program_id(2)
    shift = int(math.log2(w))

    @pl.when((i == 0) & (pl.program_id(1) == 0))
    def _build_bias():
        cp_io = lax.broadcasted_iota(jnp.int32, (w, LANES), 0)
        c_io = lax.broadcasted_iota(jnp.int32, (w, LANES), 1)
        cs = jnp.clip(c_io - kcw // 2, 0, w - kcw)
        colok = (cp_io >= cs) & (cp_io < cs + kcw)
        neg = jnp.full((w, LANES), NEG, F32)
        for hh in range(2):
            r = pltpu.roll(rpb_ref[hh] * LOG2E, LANES - (kcw - 1), 1)
            ts = []
            for e in range(ntab - 1):
                t = jnp.broadcast_to(r[e:e + 1], (w, LANES))
                t = pltpu.roll(t, 0, 1, stride=1, stride_axis=0)
                ts.append(jnp.where(colok, t, NEG))
            for x in range(ntab):
                lo = ts[x] if x < ntab - 1 else neg
                hi = ts[x - 1] if x >= 1 else neg
                tt2_scr[hh, x] = jnp.where(c_io < w, lo, pltpu.roll(hi, w, 1))

    @pl.when(i == 0)
    def _transpose_values():
        for c in range(vt_scr.shape[0]):
            vt_scr[c] = v_ref[0, c * sub:(c + 1) * sub, :].astype(F32).T.astype(BF16)
        vct_scr[...] = vc_ref[0].astype(F32).T.astype(BF16)

    units = [(t, hh) for t in range(nsub) for hh in range(2)]
    ns = s_scr.shape[0]

    def window(t):
        r0 = (i * nsub + t) * nq
        return r0, jnp.clip(r0 - kr // 2, 0, rows - nwin)

    def scores(u):
        t, hh = units[u]
        _, ks = window(t)
        qh = _half_q(q_ref[0, t * sub:(t + 1) * sub], hh)
        s_scr[u % ns, :nkw, :] = _dot_nt(k_ref[0, pl.ds(pl.multiple_of(ks * w, sub), nkw), :], qh)
        s_scr[u % ns, nkw:, :] = _dot_nt(kc_ref[0], qh)

    def softmax(u):
        t, hh = units[u]
        r0, ks = window(t)
        q_row = r0 + jnp.right_shift(lax.broadcasted_iota(jnp.int32, (1, sub), 1), shift)
        rs = jnp.clip(q_row - kr // 2, 0, rows - kr)
        macc = None
        for j in range(nwin):
            key_row = ks + j
            ok = (key_row >= rs) & (key_row < rs + kr)
            d0 = key_row - r0 + kr - 1
            bias = jnp.concatenate([tt2_scr[hh, jnp.clip(d0 - 2 * g, 0, ntab - 1)] for g in range(sub // LANES)], axis=1)
            rows_j = slice(j * w, (j + 1) * w)
            s = jnp.where(ok, s_scr[u % ns, rows_j, :] + bias, NEG)
            s_scr[u % ns, rows_j, :] = s
            macc = s if macc is None else jnp.maximum(macc, s)
        for j in range(nkw, nk, w):
            macc = jnp.maximum(macc, s_scr[u % ns, j:j + w, :])
        m = jnp.max(macc, axis=0, keepdims=True)
        lacc = None
        for j in range(0, nk, KEY_SLAB):
            e = jnp.exp2(s_scr[u % ns, j:j + KEY_SLAB, :] - m)
            p_scr[u % 2, j:j + KEY_SLAB, :] = e.astype(BF16)
            lacc = e if lacc is None else lacc + e
        return 1.0 / jnp.sum(lacc, axis=0, keepdims=True)

    for u in range(min(ns - 1, len(units))):
        scores(u)
    outs = [None, None]
    for u, (t, hh) in enumerate(units):
        if u + ns - 1 < len(units):
            scores(u + ns - 1)
        linv = softmax(u)
        _, ks = window(t)
        head = slice(hh * HEAD_DIM, (hh + 1) * HEAD_DIM)
        ot = _dot(vct_scr[head, :], p_scr[u % 2, nkw:, :])
        for c in range(nwin // nq):
            ot = ot + _dot(vt_scr[ks // nq + c, head, :], p_scr[u % 2, c * sub:(c + 1) * sub, :])
        outs[hh] = ot * linv
        if hh == 1:
            o_ref[0, t * sub:(t + 1) * sub] = jnp.concatenate(outs, axis=0).T.astype(o_ref.dtype)


def _na_attn_t_call(q, k, v, kc, vc, rpb):
    b, s_len, d = q.shape
    c_len = kc.shape[1]
    nh, ndr, ndc = rpb.shape
    kr, kcw = (ndr + 1) // 2, (ndc + 1) // 2
    rows = s_len // GRID_W
    nq = 4
    assert GRID_W == HEAD_DIM and rows % nq == 0 and rows >= nq + kr and (nq + kr) % nq == 0 and (kr // 2) % nq == 0
    assert ndc <= LANES and 2 * kr <= 2 * SUBLANES and nq * GRID_W == 2 * LANES
    rpb_p = jnp.full((nh, 2 * SUBLANES, LANES), NEG, F32).at[:, :ndr, :ndc].set(rpb[:, :, ::-1])
    sub = nq * GRID_W
    tq = _row_tile(s_len, 8 * sub)
    nk = (nq + kr) * GRID_W + c_len
    full = lambda n: pl.BlockSpec((1, n, LANES), lambda j, bi, i: (bi, 0, j))
    return pl.pallas_call(
        functools.partial(_na_attn_t_kernel, rows=rows, kr=kr, kcw=kcw, nq=nq),
        out_shape=jax.ShapeDtypeStruct((b, s_len, d), BF16),
        grid=(d // LANES, b, s_len // tq),
        in_specs=[pl.BlockSpec((1, tq, LANES), lambda j, bi, i: (bi, i, j)),
                  full(s_len), full(s_len), full(c_len), full(c_len),
                  pl.BlockSpec((2, 2 * SUBLANES, LANES), lambda j, bi, i: (j, 0, 0))],
        out_specs=pl.BlockSpec((1, tq, LANES), lambda j, bi, i: (bi, i, j)),
        scratch_shapes=[pltpu.VMEM((2, 2 * kr, GRID_W, LANES), F32), pltpu.VMEM((s_len // sub, LANES, sub), BF16),
                        pltpu.VMEM((LANES, c_len), BF16), pltpu.VMEM((3, nk, sub), F32),
                        pltpu.VMEM((2, nk, sub), BF16)],
        compiler_params=_cparams(("arbitrary", "arbitrary", "arbitrary")),
        name="na_attn_t",
    )(q, k, v, kc, vc, rpb_p)


def _na_attn_call(q, k, v, kc, vc, rpb):
    b, s_len, d = q.shape
    c_len = kc.shape[1]
    nh, ndr, ndc = rpb.shape
    kr, kcw = (ndr + 1) // 2, (ndc + 1) // 2
    rows = s_len // GRID_W
    nq = 4
    assert GRID_W == HEAD_DIM and rows % nq == 0 and rows >= nq + kr and (nq + kr) % 2 == 0
    assert ndc <= LANES and 2 * kr <= 2 * SUBLANES
    rpb_p = jnp.full((nh, 2 * SUBLANES, LANES), NEG, F32).at[:, :ndr, :ndc].set(rpb)
    sub = nq * GRID_W
    tq = _row_tile(s_len, 8 * sub)
    nk = (nq + kr) * GRID_W + c_len
    full = lambda n: pl.BlockSpec((1, n, LANES), lambda j, bi, i: (bi, 0, j))
    return pl.pallas_call(
        functools.partial(_na_attn_kernel, rows=rows, kr=kr, kcw=kcw, nq=nq),
        out_shape=jax.ShapeDtypeStruct((b, s_len, d), BF16),
        grid=(d // LANES, b, s_len // tq),
        in_specs=[pl.BlockSpec((1, tq, LANES), lambda j, bi, i: (bi, i, j)),
                  full(s_len), full(s_len), full(c_len), full(c_len),
                  pl.BlockSpec((2, 2 * SUBLANES, LANES), lambda j, bi, i: (j, 0, 0))],
        out_specs=pl.BlockSpec((1, tq, LANES), lambda j, bi, i: (bi, i, j)),
        scratch_shapes=[pltpu.VMEM((2, 2 * kr, GRID_W, LANES), F32), pltpu.VMEM((3, sub, nk), F32),
                        pltpu.VMEM((2, sub, nk), BF16)],
        compiler_params=_cparams(("arbitrary", "arbitrary", "arbitrary")),
        name="na_attn",
    )(q, k, v, kc, vc, rpb_p)


def _outproj_kernel(a_ref, x_ref, mod_ref, g_ref, w_ref, o_ref, z_ref):
    tm = x_ref.shape[1]
    sub = _split_rows(tm)
    for r in range(0, tm, sub):
        rows = slice(r, r + sub)
        x = x_ref[0, rows] + mod_ref[0, 2:3] * _dot(a_ref[0, rows], w_ref[...])
        o_ref[0, rows] = x
        z_ref[0, rows] = _normmod(x, g_ref[...], mod_ref[0, 3:4], mod_ref[0, 4:5]).astype(z_ref.dtype)


def _outproj_call(a, x, mods_l, mod_row, g, w):
    b, n, d = x.shape
    k = a.shape[-1]
    tm = _row_tile(n, 512)
    row = lambda w_: pl.BlockSpec((1, tm, w_), lambda bi, i: (bi, i, 0))
    return pl.pallas_call(
        _outproj_kernel,
        out_shape=(jax.ShapeDtypeStruct((b, n, d), F32), jax.ShapeDtypeStruct((b, n, d), BF16)),
        grid=(b, n // tm),
        in_specs=[row(k), row(d), _mod_spec(d, mod_row),
                  pl.BlockSpec((1, d), lambda bi, i: (0, 0)),
                  pl.BlockSpec((k, d), lambda bi, i: (0, 0))],
        out_specs=(row(d), row(d)),
        compiler_params=_cparams(("arbitrary", "arbitrary")),
        name="outproj",
    )(a, x, mods_l, g, w)


def _ffn_kernel(x_ref, z_ref, zp_ref, zn_ref, mod_ref, wu_ref, cw_ref, wd_ref, *rest, fc, final):
    if final:
        fg_ref, o_ref, z_scr, u_scr, h_scr = rest
    else:
        o_ref, z_scr, u_scr, h_scr = rest
    i = pl.program_id(1)
    nt = pl.num_programs(1)
    tm, d = x_ref.shape[1], x_ref.shape[2]
    hb = FFN_HALO
    nch = wd_ref.shape[0] // fc
    z_scr[hb:hb + tm] = z_ref[0]
    z_scr[0:hb] = jnp.where(i > 0, zp_ref[0], jnp.zeros_like(zp_ref[0]))
    z_scr[hb + tm:] = jnp.where(i < nt - 1, zn_ref[0], jnp.zeros_like(zn_ref[0]))

    f = wd_ref.shape[0]
    gate_cols = lambda j: slice(j * fc, (j + 1) * fc)
    value_cols = lambda j: slice(f + j * fc, f + (j + 1) * fc)

    def up_proj(j):
        z = z_scr[...]
        u_scr[j % 2, :, :fc] = _dot(z, wu_ref[:, gate_cols(j)])
        u_scr[j % 2, :, fc:] = _dot(z, wu_ref[:, value_cols(j)])

    def conv(j, cols, lanes, r0, sub):
        cw = cw_ref[:, cols]
        cv = cw[3:4] + cw[0:1] * u_scr[j % 2, r0 - 1:r0 - 1 + sub, lanes]
        cv = cv + cw[1:2] * u_scr[j % 2, r0:r0 + sub, lanes]
        return cv + cw[2:3] * u_scr[j % 2, r0 + 1:r0 + 1 + sub, lanes]

    def conv_act(j):
        sub = min(FFN_SUB, tm)
        for sb in range(tm // sub):
            r0 = hb + sb * sub
            hg = 0.5 * conv(j, gate_cols(j), slice(0, fc), r0, sub)
            h = (hg + hg * jnp.tanh(hg)) * conv(j, value_cols(j), slice(fc, 2 * fc), r0, sub)
            h_scr[sb * sub:(sb + 1) * sub, j * fc:(j + 1) * fc] = h.astype(BF16)

    up_proj(0)
    for j in range(nch - 1):
        up_proj(j + 1)
        conv_act(j)
    split = (nch - 1) * fc
    acc = _dot(h_scr[:, :split], wd_ref[:split, :])
    conv_act(nch - 1)
    acc = acc + _dot(h_scr[:, split:], wd_ref[split:, :])
    out = x_ref[0] + mod_ref[0, 5:6] * acc
    if final:
        out = _rms(out, fg_ref[...])
    o_ref[0] = out


def _ffn_pack_params(w_up, conv_w, conv_b, w_down, fc):
    f2 = w_up.shape[1]
    assert (f2 // 2) % fc == 0
    cw = jnp.concatenate([conv_w, conv_b[None], jnp.zeros((SUBLANES - 1 - conv_w.shape[0], f2), F32)], axis=0)
    return w_up.astype(BF16), cw, w_down.astype(BF16)


def _ffn_call(x, z, mods_l, mod_row, packed, layer, fc, final_g=None):
    b, n, d = x.shape
    wu, cw, wd = packed
    nch, fc2 = wd.shape[1] // fc, 2 * fc
    tm = _row_tile(n, FFN_TM)
    hb = FFN_HALO
    nbh = tm // hb
    once = dict(pipeline_mode=pl.Buffered(1))
    layer_spec = lambda a, **kw: pl.BlockSpec((None,) + a.shape[1:], lambda bi, i: (layer,) + (0,) * (a.ndim - 1), **kw)
    in_specs = [
        pl.BlockSpec((1, tm, d), lambda bi, i: (bi, i, 0)),
        pl.BlockSpec((1, tm, d), lambda bi, i: (bi, i, 0)),
        pl.BlockSpec((1, hb, d), lambda bi, i: (bi, jnp.maximum(i * nbh - 1, 0), 0)),
        pl.BlockSpec((1, hb, d), lambda bi, i: (bi, jnp.minimum((i + 1) * nbh, n // hb - 1), 0)),
        _mod_spec(d, mod_row),
        layer_spec(wu, **once),
        layer_spec(cw),
        layer_spec(wd, **once),
    ]
    args = [x, z, z, z, mods_l, wu, cw, wd]
    if final_g is not None:
        in_specs.append(pl.BlockSpec((1, d), lambda bi, i: (0, 0)))
        args.append(final_g)
    return pl.pallas_call(
        functools.partial(_ffn_kernel, fc=fc2 // 2, final=final_g is not None),
        out_shape=jax.ShapeDtypeStruct((b, n, d), F32),
        grid=(b, n // tm),
        in_specs=in_specs,
        out_specs=pl.BlockSpec((1, tm, d), lambda bi, i: (bi, i, 0)),
        scratch_shapes=[pltpu.VMEM((tm + 2 * hb, d), BF16), pltpu.VMEM((2, tm + 2 * hb, fc2), F32),
                        pltpu.VMEM((tm, nch * fc2 // 2), BF16)],
        compiler_params=_cparams(("arbitrary", "arbitrary")),
        name="ffn",
    )(*args)


def _rg_mixer(xl, xc, mods_l, ctx_row, g, w_in, conv_w, conv_b, wa, ba, wx, bx, lam, need_ctx):
    w = w_in.astype(BF16)
    g_l, xr_l = _rg_proj_call(xl, mods_l, None, g, w)
    g_c, xr_c = _rg_proj_call(xc, mods_l, ctx_row, g, w)
    cw, wbd, gp = _rg_pack_params(conv_w, conv_b, wa, ba, wx, bx, lam)
    m_c, m_l = _rg_scan_call(xr_c, g_c, xr_l, g_l, cw, wbd, gp)
    return m_l, (m_c if need_ctx else None)


def _na_mixer(xl, xc, mods_l, ctx_row, g, w_in, rpb, need_ctx):
    d = xl.shape[-1]
    w = w_in.astype(BF16)
    ql, kl, vl = _qkv_proj_call(xl, mods_l, None, g, w, d)
    qc, kc, vc = _qkv_proj_call(xc, mods_l, ctx_row, g, w, d)
    o_l = _na_attn_t_call(ql, kl, vl, kc, vc, rpb)
    o_c = _plain_attn_call(qc, [(kc, vc)], LANES) if need_ctx else None
    return o_l, o_c


def _gqa_mixer(xl, xc, mods_l, ctx_row, g, w_in, q_norm, k_norm, need_ctx):
    d = xl.shape[-1]
    nh = d // HEAD_DIM
    nkv = (w_in.shape[1] // HEAD_DIM - nh) // 2
    group = nh // nkv
    assert group % 2 == 0 and q_norm.shape[0] == HEAD_DIM
    dup = lambda wk: jnp.concatenate([wk.reshape(d, nkv, 1, HEAD_DIM)] * 2, axis=2).reshape(d, 2 * nkv * HEAD_DIM)
    nk = 2 * nkv * HEAD_DIM
    w = jnp.concatenate([w_in[:, :d], dup(w_in[:, d:d + nkv * HEAD_DIM]), dup(w_in[:, d + nkv * HEAD_DIM:])],
                        axis=1).astype(BF16)
    nseg = (d + nk) // HEAD_DIM
    assert nseg <= LANES
    seg = jnp.arange(d + nk) // HEAD_DIM
    ind = (seg[:, None] == jnp.arange(LANES)[None, :]).astype(BF16)
    gain = jnp.concatenate([jnp.tile(q_norm, nh), jnp.tile(k_norm, 2 * nkv)])[None]
    qkn = (gain, ind, ind.T)
    ql, kl, vl = _qkv_proj_call(xl, mods_l, None, g, w, nk, rope=_rope_tables(xl.shape[1]), qknorm=qkn)
    qc, kc, vc = _qkv_proj_call(xc, mods_l, ctx_row, g, w, nk, qknorm=qkn)
    qw = group * HEAD_DIM
    o_l = _plain_attn_t_call(ql, [(kl, vl), (kc, vc)], qw)
    o_c = _plain_attn_call(qc, [(kc, vc)], qw) if need_ctx else None
    return o_l, o_c


def _diff_mixer(xl, xc, mods_l, ctx_row, g, w_in, lq1, lk1, lq2, lk2, subln_g, lambda_init, need_ctx):
    d = xl.shape[-1]
    assert lq1.shape[0] == HEAD_DIM and subln_g.shape[0] == LANES
    w = w_in.astype(BF16)
    ql, kl, vl = _qkv_proj_call(xl, mods_l, None, g, w, d, rope=_rope_tables(xl.shape[1]))
    qc, kc, vc = _qkv_proj_call(xc, mods_l, ctx_row, g, w, d)
    lam_params = jnp.concatenate([jnp.stack([lq1, lk1, lq2, lk2]), jnp.zeros((SUBLANES - 4, HEAD_DIM), F32)])
    o_l = _diff_attn_call(ql, [(kl, vl), (kc, vc)], lam_params, subln_g[None], lambda_init)
    o_c = _diff_attn_call(qc, [(kc, vc)], lam_params, subln_g[None], lambda_init) if need_ctx else None
    return o_l, o_c


def kernel(x, c, ctx, c_ctx, mod_w, mod_b, norm1_g, norm2_g, rg_w_in, rg_conv_w, rg_conv_b, rg_wa, rg_ba, rg_wx, rg_bx, rg_lam, rg_w_out, na_w_in, na_rpb, na_w_out, gqa_w_in, gqa_q_norm, gqa_k_norm, gqa_w_out, diff_w_in, diff_lq1, diff_lk1, diff_lq2, diff_lk2, diff_subln_g, diff_w_out, ffn_w_up, ffn_conv_w, ffn_conv_b, ffn_w_down, final_g):
    bsz, _, d = x.shape
    depth = mod_w.shape[0]
    assert d % LANES == 0
    ctx_row = bsz
    nrow = -(-(bsz + 1) // SUBLANES) * SUBLANES
    cc = jnp.concatenate([c, c_ctx[None], jnp.zeros((nrow - bsz - 1, d), F32)], axis=0)
    mods = _mods_call(cc, mod_w, mod_b).reshape(depth, nrow, 6, d)
    f = ffn_w_down.shape[1]
    fc = next(t for t in (256, 128) if f % t == 0)
    packed = jax.vmap(functools.partial(_ffn_pack_params, fc=fc))(ffn_w_up, ffn_conv_w, ffn_conv_b, ffn_w_down)
    xl, xc = x, ctx
    for l in range(depth):
        m, j = l % N_MIXERS, l // N_MIXERS
        need_ctx = l < depth - 1
        mods_l = mods[l]
        g1, g2 = norm1_g[l][None], norm2_g[l][None]
        if m == 0:
            a_l, a_c = _rg_mixer(xl, xc, mods_l, ctx_row, g1, rg_w_in[j], rg_conv_w[j], rg_conv_b[j], rg_wa[j],
                                 rg_ba[j], rg_wx[j], rg_bx[j], rg_lam[j], need_ctx)
            w_out = rg_w_out[j]
        elif m == 1:
            a_l, a_c = _na_mixer(xl, xc, mods_l, ctx_row, g1, na_w_in[j], na_rpb[j], need_ctx)
            w_out = na_w_out[j]
        elif m == 2:
            a_l, a_c = _gqa_mixer(xl, xc, mods_l, ctx_row, g1, gqa_w_in[j], gqa_q_norm[j], gqa_k_norm[j], need_ctx)
            w_out = gqa_w_out[j]
        else:
            lambda_init = 0.8 - 0.6 * math.exp(-0.3 * l)
            a_l, a_c = _diff_mixer(xl, xc, mods_l, ctx_row, g1, diff_w_in[j], diff_lq1[j], diff_lk1[j],
                                   diff_lq2[j], diff_lk2[j], diff_subln_g[j], lambda_init, need_ctx)
            w_out = diff_w_out[j]
        w_out = w_out.astype(BF16)
        last = l == depth - 1
        xl, zl = _outproj_call(a_l, xl, mods_l, None, g2, w_out)
        xl = _ffn_call(xl, zl, mods_l, None, packed, l, fc, final_g[None] if last else None)
        if need_ctx:
            xc, zc = _outproj_call(a_c, xc, mods_l, ctx_row, g2, w_out)
            xc = _ffn_call(xc, zc, mods_l, ctx_row, packed, l, fc)
    return xl
```

```python
import functools
import math

import jax
import jax.numpy as jnp
from jax import lax
from jax.experimental import pallas as pl
from jax.experimental.pallas import tpu as pltpu

F32 = jnp.float32
BF16 = jnp.bfloat16
EPS = 1e-6
ROPE_THETA = 10000.0
GRID_W = 64
RG_C = 8.0
RG_CONV_LEFT = 2
N_MIXERS = 4
HEAD_DIM = 64
LOG2E = 1.4426950408889634
Q_SCALE = HEAD_DIM ** -0.5 * LOG2E
LANES = 128
SUBLANES = 8
NEG = -1e30
FFN_HALO = 16
FFN_SUB = 512
FFN_TM = 512
OUTPROJ_TM = 1024
PROJ_SPLIT = 2
PROJ_MIN_ROWS = 256
ATT_SUB = 256
ATT_RB = 32
DIFF_RB = 16
KEY_SLAB = 32
VMEM_LIMIT_BYTES = 56 * 1024 * 1024

_NT = (((1,), (1,)), ((), ()))


def _cparams(sem):
    return pltpu.CompilerParams(dimension_semantics=sem, vmem_limit_bytes=VMEM_LIMIT_BYTES)


def _dot(a, b):
    return jnp.dot(a, b, preferred_element_type=F32)


def _dot_nt(a, b):
    return lax.dot_general(a, b, _NT, preferred_element_type=F32)


def _silu(x):
    return x / (1.0 + jnp.exp(-x))


def _gelu_tanh(x):
    cdf = 0.5 * (1.0 + jnp.tanh(math.sqrt(2.0 / math.pi) * (x + 0.044715 * (x * x * x))))
    return x * cdf


def _rms(x, g):
    y = x * lax.rsqrt(jnp.mean(x * x, axis=-1, keepdims=True) + EPS)
    return y * g


def _normmod(x, g, shift, scale):
    return _rms(x, g) * (1.0 + scale) + shift


def _mods_kernel(cc_ref, w_ref, b_ref, o_ref):
    a = _silu(cc_ref[...]).astype(BF16)
    o_ref[0] = _dot(a, w_ref[0].astype(BF16)) + b_ref[0]


def _mods_call(cc, mod_w, mod_b):
    depth, d, n = mod_w.shape
    r = cc.shape[0]
    tn = n // 4
    return pl.pallas_call(
        _mods_kernel,
        out_shape=jax.ShapeDtypeStruct((depth, r, n), F32),
        grid=(depth, n // tn),
        in_specs=[
            pl.BlockSpec((r, d), lambda l, j: (0, 0)),
            pl.BlockSpec((1, d, tn), lambda l, j: (l, 0, j)),
            pl.BlockSpec((1, 1, tn), lambda l, j: (l, 0, j)),
        ],
        out_specs=pl.BlockSpec((1, r, tn), lambda l, j: (l, 0, j)),
        compiler_params=_cparams(("arbitrary", "arbitrary")),
        name="mods",
    )(cc, mod_w, mod_b.reshape(depth, 1, n))


def _mod_spec(d, row):
    if row is None:
        return pl.BlockSpec((1, 6, d), lambda b, *_: (b, 0, 0))
    return pl.BlockSpec((1, 6, d), lambda b, *_: (row, 0, 0))


def _split_rows(tm):
    sub = tm // PROJ_SPLIT
    return sub if (tm % PROJ_SPLIT == 0 and sub >= PROJ_MIN_ROWS) else tm


def _row_tile(n, pref):
    t = min(n, pref)
    assert n % t == 0
    return t


def _rope_tables(n_tok):
    t = jnp.arange(n_tok)
    row = (t // GRID_W).astype(F32)
    col = (t % GRID_W).astype(F32)
    n = HEAD_DIM // 4
    inv = ROPE_THETA ** (-jnp.arange(n, dtype=F32) / n)
    ang = jnp.concatenate([row[:, None] * inv, col[:, None] * inv], axis=-1)
    cos, sin = jnp.cos(ang), jnp.sin(ang)
    reps = LANES // HEAD_DIM
    cos_t = jnp.tile(jnp.concatenate([cos, cos], axis=-1), (1, reps))
    sin_t = jnp.tile(jnp.concatenate([-sin, sin], axis=-1), (1, reps))
    return cos_t, sin_t


def _swap_halves(x):
    lane = lax.broadcasted_iota(jnp.int32, x.shape, 1)
    lo = (lane & (HEAD_DIM // 2)) == 0
    return jnp.where(lo, pltpu.roll(x, LANES - HEAD_DIM // 2, 1), pltpu.roll(x, HEAD_DIM // 2, 1))


def _rope_cols(u, cos_t, sin_t):
    outs = []
    for c in range(u.shape[1] // LANES):
        xc = u[:, c * LANES:(c + 1) * LANES]
        outs.append(xc * cos_t + _swap_halves(xc) * sin_t)
    return jnp.concatenate(outs, axis=1)


def _rg_proj_kernel(x_ref, mod_ref, g_ref, w_ref, go_ref, xo_ref, *, dr):
    tm = x_ref.shape[1]
    sub = _split_rows(tm)
    for r in range(0, tm, sub):
        rows = slice(r, r + sub)
        z = _normmod(x_ref[0, rows], g_ref[...], mod_ref[0, 0:1], mod_ref[0, 1:2]).astype(BF16)
        u = _dot(z, w_ref[...])
        go_ref[0, rows] = _gelu_tanh(u[:, :dr])
        xo_ref[0, rows] = u[:, dr:]


def _rg_proj_call(x, mods_l, mod_row, g, w):
    b, n, d = x.shape
    dr = w.shape[1] // 2
    tm = _row_tile(n, 512)
    return pl.pallas_call(
        functools.partial(_rg_proj_kernel, dr=dr),
        out_shape=(jax.ShapeDtypeStruct((b, n, dr), F32), jax.ShapeDtypeStruct((b, n, dr), F32)),
        grid=(b, n // tm),
        in_specs=[
            pl.BlockSpec((1, tm, d), lambda bi, i: (bi, i, 0)),
            _mod_spec(d, mod_row),
            pl.BlockSpec((1, d), lambda bi, i: (0, 0)),
            pl.BlockSpec((d, 2 * dr), lambda bi, i: (0, 0)),
        ],
        out_specs=(pl.BlockSpec((1, tm, dr), lambda bi, i: (bi, i, 0)),
                   pl.BlockSpec((1, tm, dr), lambda bi, i: (bi, i, 0))),
        compiler_params=_cparams(("arbitrary", "arbitrary")),
        name="rg_proj",
    )(x, mods_l, g, w)


def _qkv_proj_kernel(x_ref, mod_ref, g_ref, w_ref, *rest, d, rope, qknorm):
    rest = list(rest)
    if rope:
        cos_ref, sin_ref = rest.pop(0), rest.pop(0)
    if qknorm:
        gain_ref, ind_ref, indt_ref = rest.pop(0), rest.pop(0), rest.pop(0)
    q_ref, k_ref, v_ref = rest
    nk = k_ref.shape[-1]
    tm = x_ref.shape[1]
    sub = _split_rows(tm)
    for r in range(0, tm, sub):
        rows = slice(r, r + sub)
        z = _normmod(x_ref[0, rows], g_ref[...], mod_ref[0, 0:1], mod_ref[0, 1:2]).astype(BF16)
        u = _dot(z, w_ref[...])
        qk = u[:, :d + nk]
        if qknorm:
            x2 = qk * qk
            hi = x2.astype(BF16)
            lo = (x2 - hi.astype(F32)).astype(BF16)
            ssq = _dot(hi, ind_ref[...]) + _dot(lo, ind_ref[...])
            rs = lax.rsqrt(ssq * (1.0 / HEAD_DIM) + EPS)
            rhi = rs.astype(BF16)
            rlo = (rs - rhi.astype(F32)).astype(BF16)
            rb = _dot(rhi, indt_ref[...]) + _dot(rlo, indt_ref[...])
            qk = (qk * rb) * gain_ref[...]
        if rope:
            qk = _rope_cols(qk, cos_ref[rows], sin_ref[rows])
        q_ref[0, rows] = (qk[:, :d] * Q_SCALE).astype(q_ref.dtype)
        k_ref[0, rows] = qk[:, d:].astype(k_ref.dtype)
        v_ref[0, rows] = u[:, d + nk:].astype(v_ref.dtype)


def _qkv_proj_call(x, mods_l, mod_row, g, w, nk, *, rope=None, qknorm=None):
    b, n, d = x.shape
    nv = w.shape[1] - d - nk
    tm = _row_tile(n, 512)
    const = lambda bi, i: (0, 0)
    in_specs = [
        pl.BlockSpec((1, tm, d), lambda bi, i: (bi, i, 0)),
        _mod_spec(d, mod_row),
        pl.BlockSpec((1, d), const),
        pl.BlockSpec(w.shape, const),
    ]
    args = [x, mods_l, g, w]
    if rope is not None:
        in_specs += [pl.BlockSpec((tm, LANES), lambda bi, i: (i, 0))] * 2
        args += list(rope)
    if qknorm is not None:
        in_specs += [pl.BlockSpec(a.shape, const) for a in qknorm]
        args += list(qknorm)
    out_spec = lambda w_: pl.BlockSpec((1, tm, w_), lambda bi, i: (bi, i, 0))
    return pl.pallas_call(
        functools.partial(_qkv_proj_kernel, d=d, rope=rope is not None, qknorm=qknorm is not None),
        out_shape=(jax.ShapeDtypeStruct((b, n, d), BF16), jax.ShapeDtypeStruct((b, n, nk), BF16),
                   jax.ShapeDtypeStruct((b, n, nv), BF16)),
        grid=(b, n // tm),
        in_specs=in_specs,
        out_specs=(out_spec(d), out_spec(nk), out_spec(nv)),
        compiler_params=_cparams(("arbitrary", "arbitrary")),
        name="qkv_proj",
    )(*args)


def _rg_conv(xs_scr, x, prev, nxt, cw):
    t = x.shape[0]
    h = SUBLANES
    xs_scr[0:h] = prev
    xs_scr[h:h + t] = x
    xs_scr[h + t:2 * h + t] = nxt
    y = cw[4:5] + cw[0:1] * xs_scr[h - 2:h - 2 + t]
    y = y + cw[1:2] * xs_scr[h - 1:h - 1 + t]
    y = y + cw[2:3] * x
    y = y + cw[3:4] * xs_scr[h + 1:h + 1 + t]
    return y


def _sigmoid(x):
    return 0.5 * jnp.tanh(0.5 * x) + 0.5


def _rg_gates(xc, w, ba, bx, lam):
    gw = xc.shape[1]
    y = _dot(xc.astype(BF16), w)
    r = _sigmoid(y[:, :gw] + ba)
    i = _sigmoid(y[:, gw:] + bx)
    nl = -lam
    softplus = jnp.maximum(nl, 0.0) + jnp.log1p(jnp.exp(-jnp.abs(nl)))
    log_a = r * (-RG_C * softplus)
    a = jnp.exp(log_a)
    one_minus_a2 = -jnp.tanh(log_a) * (a * a + 1.0)
    root = jnp.where(one_minus_a2 > 0.0, one_minus_a2 * lax.rsqrt(one_minus_a2), 0.0)
    return a, root * (i * xc)


def _rg_scan_tile(a, b, carry, reverse):
    t, gw = a.shape
    nck = t // SUBLANES
    a = a.reshape(nck, SUBLANES, gw)
    b = b.reshape(nck, SUBLANES, gw)
    rmod = lax.broadcasted_iota(jnp.int32, (1, SUBLANES, 1), 1)
    for s in (1, 2, 4):
        ok = (rmod < SUBLANES - s) if reverse else (rmod >= s)
        shift = SUBLANES - s if reverse else s
        ash, bsh = pltpu.roll(a, shift, 1), pltpu.roll(b, shift, 1)
        b = a * jnp.where(ok, bsh, 0.0) + b
        a = a * jnp.where(ok, ash, 1.0)
    hs = [None] * nck
    for ck in (range(nck - 1, -1, -1) if reverse else range(nck)):
        hc = b[ck] + a[ck] * carry
        carry = hc[0:1] if reverse else hc[SUBLANES - 1:SUBLANES]
        hs[ck] = hc
    return jnp.concatenate(hs, axis=0), carry


def _rg_scan_kernel(xc_ref, gc_ref, xl_ref, xp_ref, xn_ref, gl_ref, cw_ref, w_ref, gp_ref,
                    mc_ref, ml_ref, hf_scr, xv_scr, car_scr, xs_scr, *, nt):
    s = pl.program_id(2)
    gw = xl_ref.shape[-1]
    tl = xl_ref.shape[1]
    cw = cw_ref[0]
    gp = gp_ref[0]

    def gates(xconv, d):
        w = w_ref[0, :, d * 2 * gw:(d + 1) * 2 * gw]
        return _rg_gates(xconv, w, gp[3 * d:3 * d + 1], gp[3 * d + 1:3 * d + 2], gp[3 * d + 2:3 * d + 3])

    @pl.when(s == 0)
    def _ctx():
        zeros8 = jnp.zeros((SUBLANES, gw), F32)
        xconv = _rg_conv(xs_scr, xc_ref[0], zeros8, zeros8, cw)
        zero = jnp.zeros((1, gw), F32)
        a0, b0 = gates(xconv, 0)
        hf, cf = _rg_scan_tile(a0, b0, zero, False)
        a1, b1 = gates(xconv, 1)
        hb, cb = _rg_scan_tile(a1, b1, zero, True)
        car_scr[0:1] = cf
        car_scr[1:2] = cb
        mc_ref[0] = (gc_ref[0] * (hf + hb)).astype(mc_ref.dtype)

    def lat_conv(t):
        prev = xp_ref[0] * (t > 0).astype(F32)
        nxt = xn_ref[0] * (t < nt - 1).astype(F32)
        return _rg_conv(xs_scr, xl_ref[0], prev, nxt, cw)

    @pl.when((s >= 1) & (s <= nt))
    def _fwd():
        t = s - 1
        xconv = lat_conv(t)
        xv_scr[pl.ds(pl.multiple_of(t * tl, tl), tl), :] = xconv
        a0, b0 = gates(xconv, 0)
        h, c = _rg_scan_tile(a0, b0, car_scr[0:1], False)
        car_scr[0:1] = c
        hf_scr[pl.ds(pl.multiple_of(t * tl, tl), tl), :] = h

    @pl.when(s > nt)
    def _bwd():
        t = 2 * nt - s
        a1, b1 = gates(xv_scr[pl.ds(pl.multiple_of(t * tl, tl), tl), :], 1)
        h, c = _rg_scan_tile(a1, b1, car_scr[1:2], True)
        car_scr[1:2] = c
        hf = hf_scr[pl.ds(pl.multiple_of(t * tl, tl), tl), :]
        ml_ref[0] = (gl_ref[0] * (hf + h)).astype(ml_ref.dtype)


def _rg_scan_call(xr_c, g_c, xr_l, g_l, cw, wbd, gp):
    b, s_len, dr = xr_l.shape
    c_len = xr_c.shape[1]
    ng, gw = wbd.shape[0], wbd.shape[1]
    tl = _row_tile(s_len, 512)
    nt = s_len // tl
    nb8 = tl // SUBLANES

    def tile(s):
        return jnp.clip(s - 1, 0, nt - 1)

    def tile_bwd(s):
        return jnp.where(s <= nt, nt - 1, 2 * nt - s)

    return pl.pallas_call(
        functools.partial(_rg_scan_kernel, nt=nt),
        out_shape=(jax.ShapeDtypeStruct((b, c_len, dr), BF16), jax.ShapeDtypeStruct((b, s_len, dr), BF16)),
        grid=(b, ng, 2 * nt + 1),
        in_specs=[
            pl.BlockSpec((1, c_len, gw), lambda bi, h, s: (bi, 0, h)),
            pl.BlockSpec((1, c_len, gw), lambda bi, h, s: (bi, 0, h)),
            pl.BlockSpec((1, tl, gw), lambda bi, h, s: (bi, tile(s), h)),
            pl.BlockSpec((1, SUBLANES, gw), lambda bi, h, s: (bi, jnp.maximum(tile(s) * nb8 - 1, 0), h)),
            pl.BlockSpec((1, SUBLANES, gw),
                         lambda bi, h, s: (bi, jnp.minimum((tile(s) + 1) * nb8, s_len // SUBLANES - 1), h)),
            pl.BlockSpec((1, tl, gw), lambda bi, h, s: (bi, tile_bwd(s), h)),
            pl.BlockSpec((1, SUBLANES, gw), lambda bi, h, s: (h, 0, 0)),
            pl.BlockSpec((1, gw, 4 * gw), lambda bi, h, s: (h, 0, 0)),
            pl.BlockSpec((1, SUBLANES, gw), lambda bi, h, s: (h, 0, 0)),
        ],
        out_specs=(pl.BlockSpec((1, c_len, gw), lambda bi, h, s: (bi, 0, h)),
                   pl.BlockSpec((1, tl, gw), lambda bi, h, s: (bi, tile_bwd(s), h))),
        scratch_shapes=[pltpu.VMEM((s_len, gw), F32), pltpu.VMEM((s_len, gw), F32), pltpu.VMEM((SUBLANES, gw), F32),
                        pltpu.VMEM((max(tl, c_len) + 2 * SUBLANES, gw), F32)],
        compiler_params=_cparams(("arbitrary", "arbitrary", "arbitrary")),
        name="rg_scan",
    )(xr_c, g_c, xr_l, xr_l, xr_l, g_l, cw, wbd, gp)


def _rg_pack_params(conv_w, conv_b, wa, ba, wx, bx, lam):
    nblk, bw = wa.shape[1], wa.shape[2]
    dr = nblk * bw
    per = next(p for p in range(1, nblk + 1) if nblk % p == 0 and (p * bw) % LANES == 0)
    ng, gw = nblk // per, per * bw

    def dense(w):
        w = w.reshape(ng, per, bw, bw)
        eye = jnp.eye(per, dtype=w.dtype)
        return jnp.einsum('gpjk,pq->gpjqk', w, eye).reshape(ng, gw, gw)

    wbd = jnp.concatenate([dense(wa[0]), dense(wx[0]), dense(wa[1]), dense(wx[1])], axis=-1).astype(BF16)
    cw = jnp.concatenate([conv_w, conv_b[None], jnp.zeros((SUBLANES - 1 - conv_w.shape[0], dr), F32)], axis=0)
    cw = cw.reshape(SUBLANES, ng, gw).transpose(1, 0, 2)
    gp = jnp.stack([ba[0], bx[0], lam[0], ba[1], bx[1], lam[1], jnp.zeros_like(lam[0]), jnp.zeros_like(lam[0])])
    gp = gp.reshape(SUBLANES, ng, gw).transpose(1, 0, 2)
    return cw, wbd, gp


def _pair_masks(shape):
    lane = lax.broadcasted_iota(jnp.int32, shape, 1)
    return lane < HEAD_DIM


def _half_q(q, hh):
    first = _pair_masks(q.shape)
    return jnp.where(first if hh == 0 else jnp.logical_not(first), q, jnp.zeros_like(q))


def _scores_to_scratch(qh, k_refs, s_scr, slot):
    off = 0
    for k_ref in k_refs:
        n = k_ref.shape[1]
        s_scr[slot, :, off:off + n] = _dot_nt(qh, k_ref[0])
        off += n


def _exp_rows(s):
    e = jnp.exp2(s - s.max(axis=-1, keepdims=True))
    return e, e.sum(axis=-1, keepdims=True)


def _softmax_to_scratch(s_scr, p_scr, slot, rows):
    linv = []
    for r in range(0, rows, ATT_RB):
        e, l = _exp_rows(s_scr[slot, r:r + ATT_RB, :])
        p_scr[slot, r:r + ATT_RB, :] = e.astype(BF16)
        linv.append(1.0 / l)
    return jnp.concatenate(linv, axis=0)


def _pv_from_scratch(p_scr, slot, v_refs):
    off, o = 0, None
    for v_ref in v_refs:
        n = v_ref.shape[1]
        t = _dot(p_scr[slot, :, off:off + n], v_ref[0])
        o = t if o is None else o + t
        off += n
    return o


def _plain_attn_kernel(q_ref, *refs, nsrc):
    k_refs, v_refs = refs[0:2 * nsrc:2], refs[1:2 * nsrc:2]
    o_ref, s_scr, p_scr = refs[2 * nsrc:]
    tq, qw = q_ref.shape[1], q_ref.shape[2]
    sub = s_scr.shape[1]
    units = [(r, c, hh) for r in range(0, tq, sub) for c in range(0, qw, LANES) for hh in range(2)]

    def scores(u):
        r, c, hh = units[u]
        _scores_to_scratch(_half_q(q_ref[0, r:r + sub, c:c + LANES], hh), k_refs, s_scr, u % 2)

    scores(0)
    outs = [None, None]
    for u, (r, c, hh) in enumerate(units):
        if u + 1 < len(units):
            scores(u + 1)
        linv = _softmax_to_scratch(s_scr, p_scr, u % 2, sub)
        outs[hh] = _pv_from_scratch(p_scr, u % 2, v_refs) * linv
        if hh == 1:
            o_ref[0, r:r + sub, c:c + LANES] = jnp.where(_pair_masks((sub, LANES)), outs[0], outs[1]).astype(o_ref.dtype)


def _plain_attn_t_kernel(q_ref, *refs, nsrc):
    k_refs, v_refs = refs[0:2 * nsrc:2], refs[1:2 * nsrc:2]
    o_ref = refs[2 * nsrc]
    vt_scrs = refs[2 * nsrc + 1:3 * nsrc + 1]
    s_scr, p_scr = refs[3 * nsrc + 1:]
    tq, qw = q_ref.shape[1], q_ref.shape[2]
    sub = s_scr.shape[2]

    @pl.when(pl.program_id(2) == 0)
    def _transpose_values():
        for v_ref, vt_scr in zip(v_refs, vt_scrs):
            vt_scr[...] = v_ref[0].astype(F32).T.astype(BF16)

    units = [(r, c, hh) for r in range(0, tq, sub) for c in range(0, qw, LANES) for hh in range(2)]

    def scores(u):
        r, c, hh = units[u]
        qh = _half_q(q_ref[0, r:r + sub, c:c + LANES], hh)
        off = 0
        for k_ref in k_refs:
            n = k_ref.shape[1]
            s_scr[u % ns, off:off + n, :] = _dot_nt(k_ref[0], qh)
            off += n

    ns = s_scr.shape[0]
    for u in range(min(ns - 1, len(units))):
        scores(u)
    outs = [None, None]
    for u, (r, c, hh) in enumerate(units):
        if u + ns - 1 < len(units):
            scores(u + ns - 1)
        slabs = [slice(i, i + KEY_SLAB) for i in range(0, s_scr.shape[1], KEY_SLAB)]
        macc = s_scr[u % ns, slabs[0], :]
        for sl in slabs[1:]:
            macc = jnp.maximum(macc, s_scr[u % ns, sl, :])
        m = jnp.max(macc, axis=0, keepdims=True)
        lacc = None
        for sl in slabs:
            e = jnp.exp2(s_scr[u % ns, sl, :] - m)
            p_scr[u % 2, sl, :] = e.astype(BF16)
            lacc = e if lacc is None else lacc + e
        linv = 1.0 / jnp.sum(lacc, axis=0, keepdims=True)
        off, ot = 0, None
        for vt_scr in vt_scrs:
            n = vt_scr.shape[1]
            t = _dot(vt_scr[hh * HEAD_DIM:(hh + 1) * HEAD_DIM, :], p_scr[u % 2, off:off + n, :])
            ot = t if ot is None else ot + t
            off += n
        outs[hh] = ot * linv
        if hh == 1:
            o_ref[0, r:r + sub, c:c + LANES] = jnp.concatenate(outs, axis=0).T.astype(o_ref.dtype)


def _plain_attn_t_call(q, kvs, qw):
    b, lq, d = q.shape
    tq = _row_tile(lq, 4 * ATT_SUB)
    sub = min(tq, ATT_SUB)
    nk = sum(k.shape[1] for k, _ in kvs)
    in_specs = [pl.BlockSpec((1, tq, qw), lambda bi, j, i: (bi, i, j))]
    args = [q]
    for k, v in kvs:
        spec = pl.BlockSpec((1, k.shape[1], LANES), lambda bi, j, i: (bi, 0, j))
        in_specs += [spec, spec]
        args += [k, v]
    return pl.pallas_call(
        functools.partial(_plain_attn_t_kernel, nsrc=len(kvs)),
        out_shape=jax.ShapeDtypeStruct((b, lq, d), BF16),
        grid=(b, d // qw, lq // tq),
        in_specs=in_specs,
        out_specs=pl.BlockSpec((1, tq, qw), lambda bi, j, i: (bi, i, j)),
        scratch_shapes=([pltpu.VMEM((LANES, k.shape[1]), BF16) for k, _ in kvs]
                        + [pltpu.VMEM((3, nk, sub), F32), pltpu.VMEM((2, nk, sub), BF16)]),
        compiler_params=_cparams(("arbitrary", "arbitrary", "arbitrary")),
        name="plain_attn_t",
    )(*args)


def _plain_attn_call(q, kvs, qw):
    b, lq, d = q.shape
    tq = _row_tile(lq, 2 * ATT_SUB)
    sub = min(tq, ATT_SUB)
    nk = sum(k.shape[1] for k, _ in kvs)
    in_specs = [pl.BlockSpec((1, tq, qw), lambda bi, j, i: (bi, i, j))]
    args = [q]
    for k, v in kvs:
        spec = pl.BlockSpec((1, k.shape[1], LANES), lambda bi, j, i: (bi, 0, j))
        in_specs += [spec, spec]
        args += [k, v]
    return pl.pallas_call(
        functools.partial(_plain_attn_kernel, nsrc=len(kvs)),
        out_shape=jax.ShapeDtypeStruct((b, lq, d), BF16),
        grid=(b, d // qw, lq // tq),
        in_specs=in_specs,
        out_specs=pl.BlockSpec((1, tq, qw), lambda bi, j, i: (bi, i, j)),
        scratch_shapes=[pltpu.VMEM((2, sub, nk), F32), pltpu.VMEM((2, sub, nk), BF16)],
        compiler_params=_cparams(("arbitrary", "arbitrary", "arbitrary")),
        name="plain_attn",
    )(*args)


def _diff_attn_kernel(q_ref, lam_ref, g_ref, *refs, nsrc, lambda_init):
    k_refs, v_refs = refs[0:2 * nsrc:2], refs[1:2 * nsrc:2]
    o_ref, s_scr, a_scr = refs[2 * nsrc:]
    lp = lam_ref[...]
    lam = (jnp.exp(jnp.sum(lp[0:1] * lp[1:2], axis=-1, keepdims=True))
           - jnp.exp(jnp.sum(lp[2:3] * lp[3:4], axis=-1, keepdims=True)) + lambda_init)
    tq = q_ref.shape[1]
    sub = s_scr.shape[1]
    ns = s_scr.shape[0] // 2
    units = list(range(0, tq, sub))

    def scores(u):
        q = q_ref[0, units[u]:units[u] + sub]
        for hh in range(2):
            _scores_to_scratch(_half_q(q, hh), k_refs, s_scr, 2 * (u % ns) + hh)

    def combine(u):
        linv = []
        for r in range(0, sub, DIFF_RB):
            e0, l0 = _exp_rows(s_scr[2 * (u % ns), r:r + DIFF_RB, :])
            e1, l1 = _exp_rows(s_scr[2 * (u % ns) + 1, r:r + DIFF_RB, :])
            a_scr[u % 2, r:r + DIFF_RB, :] = (e0 - e1 * (lam * l0 / l1)).astype(BF16)
            linv.append(1.0 / l0)
        return jnp.concatenate(linv, axis=0)

    for u in range(min(ns - 1, len(units))):
        scores(u)
    for u, r in enumerate(units):
        if u + ns - 1 < len(units):
            scores(u + ns - 1)
        linv = combine(u)
        o = _pv_from_scratch(a_scr, u % 2, v_refs) * linv
        o_ref[0, r:r + sub] = (_rms(o, g_ref[...]) * (1.0 - lambda_init)).astype(o_ref.dtype)


def _diff_attn_call(q, kvs, lam_params, subln_g, lambda_init):
    b, lq, d = q.shape
    tq = _row_tile(lq, 8 * ATT_SUB)
    sub = min(tq, ATT_SUB)
    nk = sum(k.shape[1] for k, _ in kvs)
    const = lambda bi, j, i: (0, 0)
    in_specs = [pl.BlockSpec((1, tq, LANES), lambda bi, j, i: (bi, i, j)),
                pl.BlockSpec(lam_params.shape, const), pl.BlockSpec(subln_g.shape, const)]
    args = [q, lam_params, subln_g]
    for k, v in kvs:
        spec = pl.BlockSpec((1, k.shape[1], LANES), lambda bi, j, i: (bi, 0, j))
        in_specs += [spec, spec]
        args += [k, v]
    return pl.pallas_call(
        functools.partial(_diff_attn_kernel, nsrc=len(kvs), lambda_init=lambda_init),
        out_shape=jax.ShapeDtypeStruct((b, lq, d), BF16),
        grid=(b, d // LANES, lq // tq),
        in_specs=in_specs,
        out_specs=pl.BlockSpec((1, tq, LANES), lambda bi, j, i: (bi, i, j)),
        scratch_shapes=[pltpu.VMEM((6, sub, nk), F32), pltpu.VMEM((2, sub, nk), BF16)],
        compiler_params=_cparams(("arbitrary", "arbitrary", "arbitrary")),
        name="diff_attn",
    )(*args)


def _na_attn_t_kernel(q_ref, k_ref, v_ref, kc_ref, vc_ref, rpb_ref, o_ref, tt2_scr, vt_scr, vct_scr, s_scr, p_scr, *,
                      rows, kr, kcw, nq):
    w = GRID_W
    nwin = nq + kr
    ntab = 2 * kr
    nkw = nwin * w
    sub = nq * w
    nk = s_scr.shape[1]
    nsub = q_ref.shape[1] // sub
    i = pl.program_id(2)
    shift = int(math.log2(w))

    @pl.when((i == 0) & (pl.program_id(1) == 0))
    def _build_bias():
        cp_io = lax.broadcasted_iota(jnp.int32, (w, LANES), 0)
        c_io = lax.broadcasted_iota(jnp.int32, (w, LANES), 1)
        cs = jnp.clip(c_io - kcw // 2, 0, w - kcw)
        colok = (cp_io >= cs) & (cp_io < cs + kcw)
        neg = jnp.full((w, LANES), NEG, F32)
        for hh in range(2):
            r = pltpu.roll(rpb_ref[hh] * LOG2E, LANES - (kcw - 1), 1)
            ts = []
            for e in range(ntab - 1):
                t = jnp.broadcast_to(r[e:e + 1], (w, LANES))
                t = pltpu.roll(t, 0, 1, stride=1, stride_axis=0)
                ts.append(jnp.where(colok, t, NEG))
            for x in range(ntab):
                lo = ts[x] if x < ntab - 1 else neg
                hi = ts[x - 1] if x >= 1 else neg
                tt2_scr[hh, x] = jnp.where(c_io < w, lo, pltpu.roll(hi, w, 1))

    @pl.when(i == 0)
    def _transpose_values():
        for c in range(vt_scr.shape[0]):
            vt_scr[c] = v_ref[0, c * sub:(c + 1) * sub, :].astype(F32).T.astype(BF16)
        vct_scr[...] = vc_ref[0].astype(F32).T.astype(BF16)

    units = [(t, hh) for t in range(nsub) for hh in range(2)]
    ns = s_scr.shape[0]

    def window(t):
        r0 = (i * nsub + t) * nq
        return r0, jnp.clip(r0 - kr // 2, 0, rows - nwin)

    def scores(u):
        t, hh = units[u]
        _, ks = window(t)
        qh = _half_q(q_ref[0, t * sub:(t + 1) * sub], hh)
        s_scr[u % ns, :nkw, :] = _dot_nt(k_ref[0, pl.ds(pl.multiple_of(ks * w, sub), nkw), :], qh)
        s_scr[u % ns, nkw:, :] = _dot_nt(kc_ref[0], qh)

    def softmax(u):
        t, hh = units[u]
        r0, ks = window(t)
        q_row = r0 + jnp.right_shift(lax.broadcasted_iota(jnp.int32, (1, sub), 1), shift)
        rs = jnp.clip(q_row - kr // 2, 0, rows - kr)
        macc = None
        for j in range(nwin):
            key_row = ks + j
            ok = (key_row >= rs) & (key_row < rs + kr)
            d0 = key_row - r0 + kr - 1
            bias = jnp.concatenate([tt2_scr[hh, jnp.clip(d0 - 2 * g, 0, ntab - 1)] for g in range(sub // LANES)], axis=1)
            rows_j = slice(j * w, (j + 1) * w)
            s = jnp.where(ok, s_scr[u % ns, rows_j, :] + bias, NEG)
            s_scr[u % ns, rows_j, :] = s
            macc = s if macc is None else jnp.maximum(macc, s)
        for j in range(nkw, nk, w):
            macc = jnp.maximum(macc, s_scr[u % ns, j:j + w, :])
        m = jnp.max(macc, axis=0, keepdims=True)
        lacc = None
        for j in range(0, nk, KEY_SLAB):
            e = jnp.exp2(s_scr[u % ns, j:j + KEY_SLAB, :] - m)
            p_scr[u % 2, j:j + KEY_SLAB, :] = e.astype(BF16)
            lacc = e if lacc is None else lacc + e
        return 1.0 / jnp.sum(lacc, axis=0, keepdims=True)

    for u in range(min(ns - 1, len(units))):
        scores(u)
    outs = [None, None]
    for u, (t, hh) in enumerate(units):
        if u + ns - 1 < len(units):
            scores(u + ns - 1)
        linv = softmax(u)
        _, ks = window(t)
        head = slice(hh * HEAD_DIM, (hh + 1) * HEAD_DIM)
        ot = _dot(vct_scr[head, :], p_scr[u % 2, nkw:, :])
        for c in range(nwin // nq):
            ot = ot + _dot(vt_scr[ks // nq + c, head, :], p_scr[u % 2, c * sub:(c + 1) * sub, :])
        outs[hh] = ot * linv
        if hh == 1:
            o_ref[0, t * sub:(t + 1) * sub] = jnp.concatenate(outs, axis=0).T.astype(o_ref.dtype)


def _na_attn_t_call(q, k, v, kc, vc, rpb):
    b, s_len, d = q.shape
    c_len = kc.shape[1]
    nh, ndr, ndc = rpb.shape
    kr, kcw = (ndr + 1) // 2, (ndc + 1) // 2
    rows = s_len // GRID_W
    nq = 4
    assert GRID_W == HEAD_DIM and rows % nq == 0 and rows >= nq + kr and (nq + kr) % nq == 0 and (kr // 2) % nq == 0
    assert ndc <= LANES and 2 * kr <= 2 * SUBLANES and nq * GRID_W == 2 * LANES
    rpb_p = jnp.full((nh, 2 * SUBLANES, LANES), NEG, F32).at[:, :ndr, :ndc].set(rpb[:, :, ::-1])
    sub = nq * GRID_W
    tq = _row_tile(s_len, 8 * sub)
    nk = (nq + kr) * GRID_W + c_len
    full = lambda n: pl.BlockSpec((1, n, LANES), lambda j, bi, i: (bi, 0, j))
    return pl.pallas_call(
        functools.partial(_na_attn_t_kernel, rows=rows, kr=kr, kcw=kcw, nq=nq),
        out_shape=jax.ShapeDtypeStruct((b, s_len, d), BF16),
        grid=(d // LANES, b, s_len // tq),
        in_specs=[pl.BlockSpec((1, tq, LANES), lambda j, bi, i: (bi, i, j)),
                  full(s_len), full(s_len), full(c_len), full(c_len),
                  pl.BlockSpec((2, 2 * SUBLANES, LANES), lambda j, bi, i: (j, 0, 0))],
        out_specs=pl.BlockSpec((1, tq, LANES), lambda j, bi, i: (bi, i, j)),
        scratch_shapes=[pltpu.VMEM((2, 2 * kr, GRID_W, LANES), F32), pltpu.VMEM((s_len // sub, LANES, sub), BF16),
                        pltpu.VMEM((LANES, c_len), BF16), pltpu.VMEM((3, nk, sub), F32),
                        pltpu.VMEM((2, nk, sub), BF16)],
        compiler_params=_cparams(("arbitrary", "arbitrary", "arbitrary")),
        name="na_attn_t",
    )(q, k, v, kc, vc, rpb_p)


def _outproj_kernel(a_ref, x_ref, mod_ref, g_ref, w_ref, o_ref, z_ref):
    tm = x_ref.shape[1]
    sub = _split_rows(tm)
    for r in range(0, tm, sub):
        rows = slice(r, r + sub)
        x = x_ref[0, rows] + mod_ref[0, 2:3] * _dot(a_ref[0, rows], w_ref[...])
        o_ref[0, rows] = x
        z_ref[0, rows] = _normmod(x, g_ref[...], mod_ref[0, 3:4], mod_ref[0, 4:5]).astype(z_ref.dtype)


def _outproj_call(a, x, mods_l, mod_row, g, w):
    b, n, d = x.shape
    k = a.shape[-1]
    tm = _row_tile(n, OUTPROJ_TM)
    row = lambda w_: pl.BlockSpec((1, tm, w_), lambda bi, i: (bi, i, 0))
    return pl.pallas_call(
        _outproj_kernel,
        out_shape=(jax.ShapeDtypeStruct((b, n, d), F32), jax.ShapeDtypeStruct((b, n, d), BF16)),
        grid=(b, n // tm),
        in_specs=[row(k), row(d), _mod_spec(d, mod_row),
                  pl.BlockSpec((1, d), lambda bi, i: (0, 0)),
                  pl.BlockSpec((k, d), lambda bi, i: (0, 0))],
        out_specs=(row(d), row(d)),
        compiler_params=_cparams(("arbitrary", "arbitrary")),
        name="outproj",
    )(a, x, mods_l, g, w)


def _ffn_kernel(x_ref, z_ref, zp_ref, zn_ref, mod_ref, wu_ref, cw_ref, wd_ref, *rest, fc, final):
    if final:
        fg_ref, o_ref, z_scr, u_scr, h_scr = rest
    else:
        o_ref, z_scr, u_scr, h_scr = rest
    i = pl.program_id(1)
    nt = pl.num_programs(1)
    tm, d = x_ref.shape[1], x_ref.shape[2]
    hb = FFN_HALO
    nch = wd_ref.shape[0] // fc
    z_scr[hb:hb + tm] = z_ref[0]
    z_scr[0:hb] = jnp.where(i > 0, zp_ref[0], jnp.zeros_like(zp_ref[0]))
    z_scr[hb + tm:] = jnp.where(i < nt - 1, zn_ref[0], jnp.zeros_like(zn_ref[0]))

    f = wd_ref.shape[0]
    nu = u_scr.shape[0]
    gate_cols = lambda j: slice(j * fc, (j + 1) * fc)
    value_cols = lambda j: slice(f + j * fc, f + (j + 1) * fc)

    def up_proj(j):
        z = z_scr[...]
        u_scr[j % nu, :, :fc] = _dot(z, wu_ref[:, gate_cols(j)])
        u_scr[j % nu, :, fc:] = _dot(z, wu_ref[:, value_cols(j)])

    def conv(j, cols, lanes, r0, sub):
        cw = cw_ref[:, cols]
        cv = cw[3:4] + cw[0:1] * u_scr[j % nu, r0 - 1:r0 - 1 + sub, lanes]
        cv = cv + cw[1:2] * u_scr[j % nu, r0:r0 + sub, lanes]
        return cv + cw[2:3] * u_scr[j % nu, r0 + 1:r0 + 1 + sub, lanes]

    def conv_act(j):
        sub = min(FFN_SUB, tm)
        for sb in range(tm // sub):
            r0 = hb + sb * sub
            hg = 0.5 * conv(j, gate_cols(j), slice(0, fc), r0, sub)
            h = (hg + hg * jnp.tanh(hg)) * conv(j, value_cols(j), slice(fc, 2 * fc), r0, sub)
            h_scr[sb * sub:(sb + 1) * sub, j * fc:(j + 1) * fc] = h.astype(BF16)

    for j in range(min(nu - 1, nch)):
        up_proj(j)
    for j in range(nch - 1):
        if j + nu - 1 < nch:
            up_proj(j + nu - 1)
        conv_act(j)
    split = (nch - 1) * fc
    acc = _dot(h_scr[:, :split], wd_ref[:split, :])
    conv_act(nch - 1)
    acc = acc + _dot(h_scr[:, split:], wd_ref[split:, :])
    out = x_ref[0] + mod_ref[0, 5:6] * acc
    if final:
        out = _rms(out, fg_ref[...])
    o_ref[0] = out


def _ffn_pack_params(w_up, conv_w, conv_b, w_down, fc):
    f2 = w_up.shape[1]
    assert (f2 // 2) % fc == 0
    cw = jnp.concatenate([conv_w, conv_b[None], jnp.zeros((SUBLANES - 1 - conv_w.shape[0], f2), F32)], axis=0)
    return w_up.astype(BF16), cw, w_down.astype(BF16)


def _ffn_call(x, z, mods_l, mod_row, packed, layer, fc, final_g=None):
    b, n, d = x.shape
    wu, cw, wd = packed
    nch, fc2 = wd.shape[1] // fc, 2 * fc
    tm = _row_tile(n, FFN_TM)
    hb = FFN_HALO
    nbh = tm // hb
    once = dict(pipeline_mode=pl.Buffered(1))
    layer_spec = lambda a, **kw: pl.BlockSpec((None,) + a.shape[1:], lambda bi, i: (layer,) + (0,) * (a.ndim - 1), **kw)
    in_specs = [
        pl.BlockSpec((1, tm, d), lambda bi, i: (bi, i, 0)),
        pl.BlockSpec((1, tm, d), lambda bi, i: (bi, i, 0)),
        pl.BlockSpec((1, hb, d), lambda bi, i: (bi, jnp.maximum(i * nbh - 1, 0), 0)),
        pl.BlockSpec((1, hb, d), lambda bi, i: (bi, jnp.minimum((i + 1) * nbh, n // hb - 1), 0)),
        _mod_spec(d, mod_row),
        layer_spec(wu, **once),
        layer_spec(cw),
        layer_spec(wd, **once),
    ]
    args = [x, z, z, z, mods_l, wu, cw, wd]
    if final_g is not None:
        in_specs.append(pl.BlockSpec((1, d), lambda bi, i: (0, 0)))
        args.append(final_g)
    return pl.pallas_call(
        functools.partial(_ffn_kernel, fc=fc2 // 2, final=final_g is not None),
        out_shape=jax.ShapeDtypeStruct((b, n, d), F32),
        grid=(b, n // tm),
        in_specs=in_specs,
        out_specs=pl.BlockSpec((1, tm, d), lambda bi, i: (bi, i, 0)),
        scratch_shapes=[pltpu.VMEM((tm + 2 * hb, d), BF16), pltpu.VMEM((3, tm + 2 * hb, fc2), F32),
                        pltpu.VMEM((tm, nch * fc2 // 2), BF16)],
        compiler_params=_cparams(("arbitrary", "arbitrary")),
        name="ffn",
    )(*args)


def _rg_mixer(xl, xc, mods_l, ctx_row, g, w_in, conv_w, conv_b, wa, ba, wx, bx, lam, need_ctx):
    w = w_in.astype(BF16)
    g_l, xr_l = _rg_proj_call(xl, mods_l, None, g, w)
    g_c, xr_c = _rg_proj_call(xc, mods_l, ctx_row, g, w)
    cw, wbd, gp = _rg_pack_params(conv_w, conv_b, wa, ba, wx, bx, lam)
    m_c, m_l = _rg_scan_call(xr_c, g_c, xr_l, g_l, cw, wbd, gp)
    return m_l, (m_c if need_ctx else None)


def _na_mixer(xl, xc, mods_l, ctx_row, g, w_in, rpb, need_ctx):
    d = xl.shape[-1]
    w = w_in.astype(BF16)
    ql, kl, vl = _qkv_proj_call(xl, mods_l, None, g, w, d)
    qc, kc, vc = _qkv_proj_call(xc, mods_l, ctx_row, g, w, d)
    o_l = _na_attn_t_call(ql, kl, vl, kc, vc, rpb)
    o_c = _plain_attn_call(qc, [(kc, vc)], LANES) if need_ctx else None
    return o_l, o_c


def _gqa_mixer(xl, xc, mods_l, ctx_row, g, w_in, q_norm, k_norm, need_ctx):
    d = xl.shape[-1]
    nh = d // HEAD_DIM
    nkv = (w_in.shape[1] // HEAD_DIM - nh) // 2
    group = nh // nkv
    assert group % 2 == 0 and q_norm.shape[0] == HEAD_DIM
    dup = lambda wk: jnp.concatenate([wk.reshape(d, nkv, 1, HEAD_DIM)] * 2, axis=2).reshape(d, 2 * nkv * HEAD_DIM)
    nk = 2 * nkv * HEAD_DIM
    w = jnp.concatenate([w_in[:, :d], dup(w_in[:, d:d + nkv * HEAD_DIM]), dup(w_in[:, d + nkv * HEAD_DIM:])],
                        axis=1).astype(BF16)
    nseg = (d + nk) // HEAD_DIM
    assert nseg <= LANES
    seg = jnp.arange(d + nk) // HEAD_DIM
    ind = (seg[:, None] == jnp.arange(LANES)[None, :]).astype(BF16)
    gain = jnp.concatenate([jnp.tile(q_norm, nh), jnp.tile(k_norm, 2 * nkv)])[None]
    qkn = (gain, ind, ind.T)
    ql, kl, vl = _qkv_proj_call(xl, mods_l, None, g, w, nk, rope=_rope_tables(xl.shape[1]), qknorm=qkn)
    qc, kc, vc = _qkv_proj_call(xc, mods_l, ctx_row, g, w, nk, qknorm=qkn)
    qw = group * HEAD_DIM
    o_l = _plain_attn_t_call(ql, [(kl, vl), (kc, vc)], qw)
    o_c = _plain_attn_call(qc, [(kc, vc)], qw) if need_ctx else None
    return o_l, o_c


def _diff_mixer(xl, xc, mods_l, ctx_row, g, w_in, lq1, lk1, lq2, lk2, subln_g, lambda_init, need_ctx):
    d = xl.shape[-1]
    assert lq1.shape[0] == HEAD_DIM and subln_g.shape[0] == LANES
    w = w_in.astype(BF16)
    ql, kl, vl = _qkv_proj_call(xl, mods_l, None, g, w, d, rope=_rope_tables(xl.shape[1]))
    qc, kc, vc = _qkv_proj_call(xc, mods_l, ctx_row, g, w, d)
    lam_params = jnp.concatenate([jnp.stack([lq1, lk1, lq2, lk2]), jnp.zeros((SUBLANES - 4, HEAD_DIM), F32)])
    o_l = _diff_attn_call(ql, [(kl, vl), (kc, vc)], lam_params, subln_g[None], lambda_init)
    o_c = _diff_attn_call(qc, [(kc, vc)], lam_params, subln_g[None], lambda_init) if need_ctx else None
    return o_l, o_c


def kernel(x, c, ctx, c_ctx, mod_w, mod_b, norm1_g, norm2_g, rg_w_in, rg_conv_w, rg_conv_b, rg_wa, rg_ba, rg_wx, rg_bx, rg_lam, rg_w_out, na_w_in, na_rpb, na_w_out, gqa_w_in, gqa_q_norm, gqa_k_norm, gqa_w_out, diff_w_in, diff_lq1, diff_lk1, diff_lq2, diff_lk2, diff_subln_g, diff_w_out, ffn_w_up, ffn_conv_w, ffn_conv_b, ffn_w_down, final_g):
    bsz, _, d = x.shape
    depth = mod_w.shape[0]
    assert d % LANES == 0
    ctx_row = bsz
    nrow = -(-(bsz + 1) // SUBLANES) * SUBLANES
    cc = jnp.concatenate([c, c_ctx[None], jnp.zeros((nrow - bsz - 1, d), F32)], axis=0)
    mods = _mods_call(cc, mod_w, mod_b).reshape(depth, nrow, 6, d)
    f = ffn_w_down.shape[1]
    fc = next(t for t in (256, 128) if f % t == 0)
    packed = jax.vmap(functools.partial(_ffn_pack_params, fc=fc))(ffn_w_up, ffn_conv_w, ffn_conv_b, ffn_w_down)
    xl, xc = x, ctx
    for l in range(depth):
        m, j = l % N_MIXERS, l // N_MIXERS
        need_ctx = l < depth - 1
        mods_l = mods[l]
        g1, g2 = norm1_g[l][None], norm2_g[l][None]
        if m == 0:
            a_l, a_c = _rg_mixer(xl, xc, mods_l, ctx_row, g1, rg_w_in[j], rg_conv_w[j], rg_conv_b[j], rg_wa[j],
                                 rg_ba[j], rg_wx[j], rg_bx[j], rg_lam[j], need_ctx)
            w_out = rg_w_out[j]
        elif m == 1:
            a_l, a_c = _na_mixer(xl, xc, mods_l, ctx_row, g1, na_w_in[j], na_rpb[j], need_ctx)
            w_out = na_w_out[j]
        elif m == 2:
            a_l, a_c = _gqa_mixer(xl, xc, mods_l, ctx_row, g1, gqa_w_in[j], gqa_q_norm[j], gqa_k_norm[j], need_ctx)
            w_out = gqa_w_out[j]
        else:
            lambda_init = 0.8 - 0.6 * math.exp(-0.3 * l)
            a_l, a_c = _diff_mixer(xl, xc, mods_l, ctx_row, g1, diff_w_in[j], diff_lq1[j], diff_lk1[j],
                                   diff_lq2[j], diff_lk2[j], diff_subln_g[j], lambda_init, need_ctx)
            w_out = diff_w_out[j]
        w_out = w_out.astype(BF16)
        last = l == depth - 1
        xl, zl = _outproj_call(a_l, xl, mods_l, None, g2, w_out)
        xl = _ffn_call(xl, zl, mods_l, None, packed, l, fc, final_g[None] if last else None)
        if need_ctx:
            xc, zc = _outproj_call(a_c, xc, mods_l, ctx_row, g2, w_out)
            xc = _ffn_call(xc, zc, mods_l, ctx_row, packed, l, fc)
    return xl
```

```python
import functools
import math

import jax
import jax.numpy as jnp
from jax import lax
from jax.experimental import pallas as pl
from jax.experimental.pallas import tpu as pltpu

F32 = jnp.float32
BF16 = jnp.bfloat16
EPS = 1e-6
ROPE_THETA = 10000.0
GRID_W = 64
RG_C = 8.0
RG_CONV_LEFT = 2
N_MIXERS = 4
HEAD_DIM = 64
LOG2E = 1.4426950408889634
Q_SCALE = HEAD_DIM ** -0.5 * LOG2E
LANES = 128
SUBLANES = 8
NEG = -1e30
FFN_HALO = 16
FFN_SUB = 512
FFN_TM = 512
OUTPROJ_TM = 1024
RG_TL = 1024
PROJ_TM = 1024
PROJ_SPLIT = 4
PROJ_MIN_ROWS = 256
ATT_SUB = 256
ATT_RB = 32
DIFF_RB = 16
KEY_SLAB = 32
VMEM_LIMIT_BYTES = 56 * 1024 * 1024

_NT = (((1,), (1,)), ((), ()))


def _cparams(sem):
    return pltpu.CompilerParams(dimension_semantics=sem, vmem_limit_bytes=VMEM_LIMIT_BYTES)


def _dot(a, b):
    return jnp.dot(a, b, preferred_element_type=F32)


def _dot_nt(a, b):
    return lax.dot_general(a, b, _NT, preferred_element_type=F32)


def _silu(x):
    return x / (1.0 + jnp.exp(-x))


def _gelu_tanh(x):
    cdf = 0.5 * (1.0 + jnp.tanh(math.sqrt(2.0 / math.pi) * (x + 0.044715 * (x * x * x))))
    return x * cdf


def _rms(x, g):
    y = x * lax.rsqrt(jnp.mean(x * x, axis=-1, keepdims=True) + EPS)
    return y * g


def _normmod(x, g, shift, scale):
    return _rms(x, g) * (1.0 + scale) + shift


def _mods_kernel(cc_ref, w_ref, b_ref, o_ref):
    a = _silu(cc_ref[...]).astype(BF16)
    o_ref[0] = _dot(a, w_ref[0].astype(BF16)) + b_ref[0]


def _mods_call(cc, mod_w, mod_b):
    depth, d, n = mod_w.shape
    r = cc.shape[0]
    tn = n // 4
    return pl.pallas_call(
        _mods_kernel,
        out_shape=jax.ShapeDtypeStruct((depth, r, n), F32),
        grid=(depth, n // tn),
        in_specs=[
            pl.BlockSpec((r, d), lambda l, j: (0, 0)),
            pl.BlockSpec((1, d, tn), lambda l, j: (l, 0, j)),
            pl.BlockSpec((1, 1, tn), lambda l, j: (l, 0, j)),
        ],
        out_specs=pl.BlockSpec((1, r, tn), lambda l, j: (l, 0, j)),
        compiler_params=_cparams(("arbitrary", "arbitrary")),
        name="mods",
    )(cc, mod_w, mod_b.reshape(depth, 1, n))


def _mod_spec(d, row):
    if row is None:
        return pl.BlockSpec((1, 6, d), lambda b, *_: (b, 0, 0))
    return pl.BlockSpec((1, 6, d), lambda b, *_: (row, 0, 0))


def _split_rows(tm):
    sub = tm // PROJ_SPLIT
    return sub if (tm % PROJ_SPLIT == 0 and sub >= PROJ_MIN_ROWS) else tm


def _row_tile(n, pref):
    t = min(n, pref)
    assert n % t == 0
    return t


def _rope_tables(n_tok):
    t = jnp.arange(n_tok)
    row = (t // GRID_W).astype(F32)
    col = (t % GRID_W).astype(F32)
    n = HEAD_DIM // 4
    inv = ROPE_THETA ** (-jnp.arange(n, dtype=F32) / n)
    ang = jnp.concatenate([row[:, None] * inv, col[:, None] * inv], axis=-1)
    cos, sin = jnp.cos(ang), jnp.sin(ang)
    reps = LANES // HEAD_DIM
    cos_t = jnp.tile(jnp.concatenate([cos, cos], axis=-1), (1, reps))
    sin_t = jnp.tile(jnp.concatenate([-sin, sin], axis=-1), (1, reps))
    return cos_t, sin_t


def _swap_halves(x):
    lane = lax.broadcasted_iota(jnp.int32, x.shape, 1)
    lo = (lane & (HEAD_DIM // 2)) == 0
    return jnp.where(lo, pltpu.roll(x, LANES - HEAD_DIM // 2, 1), pltpu.roll(x, HEAD_DIM // 2, 1))


def _rope_cols(u, cos_t, sin_t):
    outs = []
    for c in range(u.shape[1] // LANES):
        xc = u[:, c * LANES:(c + 1) * LANES]
        outs.append(xc * cos_t + _swap_halves(xc) * sin_t)
    return jnp.concatenate(outs, axis=1)


def _rg_proj_kernel(x_ref, mod_ref, g_ref, w_ref, go_ref, xo_ref, *, dr):
    tm = x_ref.shape[1]
    sub = _split_rows(tm)
    for r in range(0, tm, sub):
        rows = slice(r, r + sub)
        z = _normmod(x_ref[0, rows], g_ref[...], mod_ref[0, 0:1], mod_ref[0, 1:2]).astype(BF16)
        u = _dot(z, w_ref[...])
        go_ref[0, rows] = _gelu_tanh(u[:, :dr])
        xo_ref[0, rows] = u[:, dr:]


def _rg_proj_call(x, mods_l, mod_row, g, w):
    b, n, d = x.shape
    dr = w.shape[1] // 2
    tm = _row_tile(n, PROJ_TM)
    return pl.pallas_call(
        functools.partial(_rg_proj_kernel, dr=dr),
        out_shape=(jax.ShapeDtypeStruct((b, n, dr), F32), jax.ShapeDtypeStruct((b, n, dr), F32)),
        grid=(b, n // tm),
        in_specs=[
            pl.BlockSpec((1, tm, d), lambda bi, i: (bi, i, 0)),
            _mod_spec(d, mod_row),
            pl.BlockSpec((1, d), lambda bi, i: (0, 0)),
            pl.BlockSpec((d, 2 * dr), lambda bi, i: (0, 0)),
        ],
        out_specs=(pl.BlockSpec((1, tm, dr), lambda bi, i: (bi, i, 0)),
                   pl.BlockSpec((1, tm, dr), lambda bi, i: (bi, i, 0))),
        compiler_params=_cparams(("arbitrary", "arbitrary")),
        name="rg_proj",
    )(x, mods_l, g, w)


def _qkv_proj_kernel(x_ref, mod_ref, g_ref, w_ref, *rest, d, rope, qknorm):
    rest = list(rest)
    if rope:
        cos_ref, sin_ref = rest.pop(0), rest.pop(0)
    if qknorm:
        gain_ref, ind_ref, indt_ref = rest.pop(0), rest.pop(0), rest.pop(0)
    q_ref, k_ref, v_ref = rest
    nk = k_ref.shape[-1]
    tm = x_ref.shape[1]
    sub = _split_rows(tm)
    for r in range(0, tm, sub):
        rows = slice(r, r + sub)
        z = _normmod(x_ref[0, rows], g_ref[...], mod_ref[0, 0:1], mod_ref[0, 1:2]).astype(BF16)
        u = _dot(z, w_ref[...])
        qk = u[:, :d + nk]
        if qknorm:
            x2 = qk * qk
            hi = x2.astype(BF16)
            lo = (x2 - hi.astype(F32)).astype(BF16)
            ssq = _dot(hi, ind_ref[...]) + _dot(lo, ind_ref[...])
            rs = lax.rsqrt(ssq * (1.0 / HEAD_DIM) + EPS)
            rhi = rs.astype(BF16)
            rlo = (rs - rhi.astype(F32)).astype(BF16)
            rb = _dot(rhi, indt_ref[...]) + _dot(rlo, indt_ref[...])
            qk = (qk * rb) * gain_ref[...]
        if rope:
            qk = _rope_cols(qk, cos_ref[rows], sin_ref[rows])
        q_ref[0, rows] = (qk[:, :d] * Q_SCALE).astype(q_ref.dtype)
        k_ref[0, rows] = qk[:, d:].astype(k_ref.dtype)
        v_ref[0, rows] = u[:, d + nk:].astype(v_ref.dtype)


def _qkv_proj_call(x, mods_l, mod_row, g, w, nk, *, rope=None, qknorm=None):
    b, n, d = x.shape
    nv = w.shape[1] - d - nk
    tm = _row_tile(n, PROJ_TM)
    const = lambda bi, i: (0, 0)
    in_specs = [
        pl.BlockSpec((1, tm, d), lambda bi, i: (bi, i, 0)),
        _mod_spec(d, mod_row),
        pl.BlockSpec((1, d), const),
        pl.BlockSpec(w.shape, const),
    ]
    args = [x, mods_l, g, w]
    if rope is not None:
        in_specs += [pl.BlockSpec((tm, LANES), lambda bi, i: (i, 0))] * 2
        args += list(rope)
    if qknorm is not None:
        in_specs += [pl.BlockSpec(a.shape, const) for a in qknorm]
        args += list(qknorm)
    out_spec = lambda w_: pl.BlockSpec((1, tm, w_), lambda bi, i: (bi, i, 0))
    return pl.pallas_call(
        functools.partial(_qkv_proj_kernel, d=d, rope=rope is not None, qknorm=qknorm is not None),
        out_shape=(jax.ShapeDtypeStruct((b, n, d), BF16), jax.ShapeDtypeStruct((b, n, nk), BF16),
                   jax.ShapeDtypeStruct((b, n, nv), BF16)),
        grid=(b, n // tm),
        in_specs=in_specs,
        out_specs=(out_spec(d), out_spec(nk), out_spec(nv)),
        compiler_params=_cparams(("arbitrary", "arbitrary")),
        name="qkv_proj",
    )(*args)


def _rg_conv(xs_scr, x, prev, nxt, cw):
    t = x.shape[0]
    h = SUBLANES
    xs_scr[0:h] = prev
    xs_scr[h:h + t] = x
    xs_scr[h + t:2 * h + t] = nxt
    y = cw[4:5] + cw[0:1] * xs_scr[h - 2:h - 2 + t]
    y = y + cw[1:2] * xs_scr[h - 1:h - 1 + t]
    y = y + cw[2:3] * x
    y = y + cw[3:4] * xs_scr[h + 1:h + 1 + t]
    return y


def _sigmoid(x):
    return 0.5 * jnp.tanh(0.5 * x) + 0.5


def _rg_gates(xc, w, ba, bx, lam):
    gw = xc.shape[1]
    y = _dot(xc.astype(BF16), w)
    r = _sigmoid(y[:, :gw] + ba)
    i = _sigmoid(y[:, gw:] + bx)
    nl = -lam
    softplus = jnp.maximum(nl, 0.0) + jnp.log1p(jnp.exp(-jnp.abs(nl)))
    log_a = r * (-RG_C * softplus)
    a = jnp.exp(log_a)
    one_minus_a2 = -jnp.tanh(log_a) * (a * a + 1.0)
    root = jnp.where(one_minus_a2 > 0.0, one_minus_a2 * lax.rsqrt(one_minus_a2), 0.0)
    return a, root * (i * xc)


def _rg_scan_tile(a, b, carry, reverse):
    t, gw = a.shape
    nck = t // SUBLANES
    a = a.reshape(nck, SUBLANES, gw)
    b = b.reshape(nck, SUBLANES, gw)
    rmod = lax.broadcasted_iota(jnp.int32, (1, SUBLANES, 1), 1)
    for s in (1, 2, 4):
        ok = (rmod < SUBLANES - s) if reverse else (rmod >= s)
        shift = SUBLANES - s if reverse else s
        ash, bsh = pltpu.roll(a, shift, 1), pltpu.roll(b, shift, 1)
        b = a * jnp.where(ok, bsh, 0.0) + b
        a = a * jnp.where(ok, ash, 1.0)
    hs = [None] * nck
    for ck in (range(nck - 1, -1, -1) if reverse else range(nck)):
        hc = b[ck] + a[ck] * carry
        carry = hc[0:1] if reverse else hc[SUBLANES - 1:SUBLANES]
        hs[ck] = hc
    return jnp.concatenate(hs, axis=0), carry


def _rg_scan_kernel(xc_ref, gc_ref, xl_ref, xp_ref, xn_ref, gl_ref, cw_ref, w_ref, gp_ref,
                    mc_ref, ml_ref, hf_scr, xv_scr, car_scr, xs_scr, *, nt):
    s = pl.program_id(2)
    gw = xl_ref.shape[-1]
    tl = xl_ref.shape[1]
    cw = cw_ref[0]
    gp = gp_ref[0]

    def gates(xconv, d):
        w = w_ref[0, :, d * 2 * gw:(d + 1) * 2 * gw]
        return _rg_gates(xconv, w, gp[3 * d:3 * d + 1], gp[3 * d + 1:3 * d + 2], gp[3 * d + 2:3 * d + 3])

    @pl.when(s == 0)
    def _ctx():
        zeros8 = jnp.zeros((SUBLANES, gw), F32)
        xconv = _rg_conv(xs_scr, xc_ref[0], zeros8, zeros8, cw)
        zero = jnp.zeros((1, gw), F32)
        a0, b0 = gates(xconv, 0)
        hf, cf = _rg_scan_tile(a0, b0, zero, False)
        a1, b1 = gates(xconv, 1)
        hb, cb = _rg_scan_tile(a1, b1, zero, True)
        car_scr[0:1] = cf
        car_scr[1:2] = cb
        mc_ref[0] = (gc_ref[0] * (hf + hb)).astype(mc_ref.dtype)

    def lat_conv(t):
        prev = xp_ref[0] * (t > 0).astype(F32)
        nxt = xn_ref[0] * (t < nt - 1).astype(F32)
        return _rg_conv(xs_scr, xl_ref[0], prev, nxt, cw)

    @pl.when((s >= 1) & (s <= nt))
    def _fwd():
        t = s - 1
        xconv = lat_conv(t)
        xv_scr[pl.ds(pl.multiple_of(t * tl, tl), tl), :] = xconv
        a0, b0 = gates(xconv, 0)
        h, c = _rg_scan_tile(a0, b0, car_scr[0:1], False)
        car_scr[0:1] = c
        hf_scr[pl.ds(pl.multiple_of(t * tl, tl), tl), :] = h

    @pl.when(s > nt)
    def _bwd():
        t = 2 * nt - s
        a1, b1 = gates(xv_scr[pl.ds(pl.multiple_of(t * tl, tl), tl), :], 1)
        h, c = _rg_scan_tile(a1, b1, car_scr[1:2], True)
        car_scr[1:2] = c
        hf = hf_scr[pl.ds(pl.multiple_of(t * tl, tl), tl), :]
        ml_ref[0] = (gl_ref[0] * (hf + h)).astype(ml_ref.dtype)


def _rg_scan_call(xr_c, g_c, xr_l, g_l, cw, wbd, gp):
    b, s_len, dr = xr_l.shape
    c_len = xr_c.shape[1]
    ng, gw = wbd.shape[0], wbd.shape[1]
    tl = _row_tile(s_len, RG_TL)
    nt = s_len // tl
    nb8 = tl // SUBLANES

    def tile(s):
        return jnp.clip(s - 1, 0, nt - 1)

    def tile_bwd(s):
        return jnp.where(s <= nt, nt - 1, 2 * nt - s)

    return pl.pallas_call(
        functools.partial(_rg_scan_kernel, nt=nt),
        out_shape=(jax.ShapeDtypeStruct((b, c_len, dr), BF16), jax.ShapeDtypeStruct((b, s_len, dr), BF16)),
        grid=(b, ng, 2 * nt + 1),
        in_specs=[
            pl.BlockSpec((1, c_len, gw), lambda bi, h, s: (bi, 0, h)),
            pl.BlockSpec((1, c_len, gw), lambda bi, h, s: (bi, 0, h)),
            pl.BlockSpec((1, tl, gw), lambda bi, h, s: (bi, tile(s), h)),
            pl.BlockSpec((1, SUBLANES, gw), lambda bi, h, s: (bi, jnp.maximum(tile(s) * nb8 - 1, 0), h)),
            pl.BlockSpec((1, SUBLANES, gw),
                         lambda bi, h, s: (bi, jnp.minimum((tile(s) + 1) * nb8, s_len // SUBLANES - 1), h)),
            pl.BlockSpec((1, tl, gw), lambda bi, h, s: (bi, tile_bwd(s), h)),
            pl.BlockSpec((1, SUBLANES, gw), lambda bi, h, s: (h, 0, 0)),
            pl.BlockSpec((1, gw, 4 * gw), lambda bi, h, s: (h, 0, 0)),
            pl.BlockSpec((1, SUBLANES, gw), lambda bi, h, s: (h, 0, 0)),
        ],
        out_specs=(pl.BlockSpec((1, c_len, gw), lambda bi, h, s: (bi, 0, h)),
                   pl.BlockSpec((1, tl, gw), lambda bi, h, s: (bi, tile_bwd(s), h))),
        scratch_shapes=[pltpu.VMEM((s_len, gw), F32), pltpu.VMEM((s_len, gw), F32), pltpu.VMEM((SUBLANES, gw), F32),
                        pltpu.VMEM((max(tl, c_len) + 2 * SUBLANES, gw), F32)],
        compiler_params=_cparams(("arbitrary", "arbitrary", "arbitrary")),
        name="rg_scan",
    )(xr_c, g_c, xr_l, xr_l, xr_l, g_l, cw, wbd, gp)


def _rg_pack_params(conv_w, conv_b, wa, ba, wx, bx, lam):
    nblk, bw = wa.shape[1], wa.shape[2]
    dr = nblk * bw
    per = next(p for p in range(1, nblk + 1) if nblk % p == 0 and (p * bw) % LANES == 0)
    ng, gw = nblk // per, per * bw

    def dense(w):
        w = w.reshape(ng, per, bw, bw)
        eye = jnp.eye(per, dtype=w.dtype)
        return jnp.einsum('gpjk,pq->gpjqk', w, eye).reshape(ng, gw, gw)

    wbd = jnp.concatenate([dense(wa[0]), dense(wx[0]), dense(wa[1]), dense(wx[1])], axis=-1).astype(BF16)
    cw = jnp.concatenate([conv_w, conv_b[None], jnp.zeros((SUBLANES - 1 - conv_w.shape[0], dr), F32)], axis=0)
    cw = cw.reshape(SUBLANES, ng, gw).transpose(1, 0, 2)
    gp = jnp.stack([ba[0], bx[0], lam[0], ba[1], bx[1], lam[1], jnp.zeros_like(lam[0]), jnp.zeros_like(lam[0])])
    gp = gp.reshape(SUBLANES, ng, gw).transpose(1, 0, 2)
    return cw, wbd, gp


def _pair_masks(shape):
    lane = lax.broadcasted_iota(jnp.int32, shape, 1)
    return lane < HEAD_DIM


def _half_q(q, hh):
    first = _pair_masks(q.shape)
    return jnp.where(first if hh == 0 else jnp.logical_not(first), q, jnp.zeros_like(q))


def _scores_to_scratch(qh, k_refs, s_scr, slot):
    off = 0
    for k_ref in k_refs:
        n = k_ref.shape[1]
        s_scr[slot, :, off:off + n] = _dot_nt(qh, k_ref[0])
        off += n


def _exp_rows(s):
    e = jnp.exp2(s - s.max(axis=-1, keepdims=True))
    return e, e.sum(axis=-1, keepdims=True)


def _softmax_to_scratch(s_scr, p_scr, slot, rows):
    linv = []
    for r in range(0, rows, ATT_RB):
        e, l = _exp_rows(s_scr[slot, r:r + ATT_RB, :])
        p_scr[slot, r:r + ATT_RB, :] = e.astype(BF16)
        linv.append(1.0 / l)
    return jnp.concatenate(linv, axis=0)


def _pv_from_scratch(p_scr, slot, v_refs):
    off, o = 0, None
    for v_ref in v_refs:
        n = v_ref.shape[1]
        t = _dot(p_scr[slot, :, off:off + n], v_ref[0])
        o = t if o is None else o + t
        off += n
    return o


def _plain_attn_kernel(q_ref, *refs, nsrc):
    k_refs, v_refs = refs[0:2 * nsrc:2], refs[1:2 * nsrc:2]
    o_ref, s_scr, p_scr = refs[2 * nsrc:]
    tq, qw = q_ref.shape[1], q_ref.shape[2]
    sub = s_scr.shape[1]
    units = [(r, c, hh) for r in range(0, tq, sub) for c in range(0, qw, LANES) for hh in range(2)]

    def scores(u):
        r, c, hh = units[u]
        _scores_to_scratch(_half_q(q_ref[0, r:r + sub, c:c + LANES], hh), k_refs, s_scr, u % 2)

    scores(0)
    outs = [None, None]
    for u, (r, c, hh) in enumerate(units):
        if u + 1 < len(units):
            scores(u + 1)
        linv = _softmax_to_scratch(s_scr, p_scr, u % 2, sub)
        outs[hh] = _pv_from_scratch(p_scr, u % 2, v_refs) * linv
        if hh == 1:
            o_ref[0, r:r + sub, c:c + LANES] = jnp.where(_pair_masks((sub, LANES)), outs[0], outs[1]).astype(o_ref.dtype)


def _plain_attn_t_kernel(q_ref, *refs, nsrc):
    k_refs, v_refs = refs[0:2 * nsrc:2], refs[1:2 * nsrc:2]
    o_ref = refs[2 * nsrc]
    vt_scrs = refs[2 * nsrc + 1:3 * nsrc + 1]
    s_scr, p_scr = refs[3 * nsrc + 1:]
    tq, qw = q_ref.shape[1], q_ref.shape[2]
    sub = s_scr.shape[2]

    @pl.when(pl.program_id(2) == 0)
    def _transpose_values():
        for v_ref, vt_scr in zip(v_refs, vt_scrs):
            vt_scr[...] = v_ref[0].astype(F32).T.astype(BF16)

    units = [(r, c, hh) for r in range(0, tq, sub) for c in range(0, qw, LANES) for hh in range(2)]

    def scores(u):
        r, c, hh = units[u]
        qh = _half_q(q_ref[0, r:r + sub, c:c + LANES], hh)
        off = 0
        for k_ref in k_refs:
            n = k_ref.shape[1]
            s_scr[u % ns, off:off + n, :] = _dot_nt(k_ref[0], qh)
            off += n

    ns = s_scr.shape[0]
    for u in range(min(ns - 1, len(units))):
        scores(u)
    outs = [None, None]
    for u, (r, c, hh) in enumerate(units):
        if u + ns - 1 < len(units):
            scores(u + ns - 1)
        slabs = [slice(i, i + KEY_SLAB) for i in range(0, s_scr.shape[1], KEY_SLAB)]
        macc = s_scr[u % ns, slabs[0], :]
        for sl in slabs[1:]:
            macc = jnp.maximum(macc, s_scr[u % ns, sl, :])
        m = jnp.max(macc, axis=0, keepdims=True)
        lacc = None
        for sl in slabs:
            e = jnp.exp2(s_scr[u % ns, sl, :] - m)
            p_scr[u % 2, sl, :] = e.astype(BF16)
            lacc = e if lacc is None else lacc + e
        linv = 1.0 / jnp.sum(lacc, axis=0, keepdims=True)
        off, ot = 0, None
        for vt_scr in vt_scrs:
            n = vt_scr.shape[1]
            t = _dot(vt_scr[hh * HEAD_DIM:(hh + 1) * HEAD_DIM, :], p_scr[u % 2, off:off + n, :])
            ot = t if ot is None else ot + t
            off += n
        outs[hh] = ot * linv
        if hh == 1:
            o_ref[0, r:r + sub, c:c + LANES] = jnp.concatenate(outs, axis=0).T.astype(o_ref.dtype)


def _plain_attn_t_call(q, kvs, qw):
    b, lq, d = q.shape
    tq = _row_tile(lq, 4 * ATT_SUB)
    sub = min(tq, ATT_SUB)
    nk = sum(k.shape[1] for k, _ in kvs)
    in_specs = [pl.BlockSpec((1, tq, qw), lambda bi, j, i: (bi, i, j))]
    args = [q]
    for k, v in kvs:
        spec = pl.BlockSpec((1, k.shape[1], LANES), lambda bi, j, i: (bi, 0, j))
        in_specs += [spec, spec]
        args += [k, v]
    return pl.pallas_call(
        functools.partial(_plain_attn_t_kernel, nsrc=len(kvs)),
        out_shape=jax.ShapeDtypeStruct((b, lq, d), BF16),
        grid=(b, d // qw, lq // tq),
        in_specs=in_specs,
        out_specs=pl.BlockSpec((1, tq, qw), lambda bi, j, i: (bi, i, j)),
        scratch_shapes=([pltpu.VMEM((LANES, k.shape[1]), BF16) for k, _ in kvs]
                        + [pltpu.VMEM((3, nk, sub), F32), pltpu.VMEM((2, nk, sub), BF16)]),
        compiler_params=_cparams(("arbitrary", "arbitrary", "arbitrary")),
        name="plain_attn_t",
    )(*args)


def _plain_attn_call(q, kvs, qw):
    b, lq, d = q.shape
    tq = _row_tile(lq, 2 * ATT_SUB)
    sub = min(tq, ATT_SUB)
    nk = sum(k.shape[1] for k, _ in kvs)
    in_specs = [pl.BlockSpec((1, tq, qw), lambda bi, j, i: (bi, i, j))]
    args = [q]
    for k, v in kvs:
        spec = pl.BlockSpec((1, k.shape[1], LANES), lambda bi, j, i: (bi, 0, j))
        in_specs += [spec, spec]
        args += [k, v]
    return pl.pallas_call(
        functools.partial(_plain_attn_kernel, nsrc=len(kvs)),
        out_shape=jax.ShapeDtypeStruct((b, lq, d), BF16),
        grid=(b, d // qw, lq // tq),
        in_specs=in_specs,
        out_specs=pl.BlockSpec((1, tq, qw), lambda bi, j, i: (bi, i, j)),
        scratch_shapes=[pltpu.VMEM((2, sub, nk), F32), pltpu.VMEM((2, sub, nk), BF16)],
        compiler_params=_cparams(("arbitrary", "arbitrary", "arbitrary")),
        name="plain_attn",
    )(*args)


def _diff_attn_kernel(q_ref, lam_ref, g_ref, *refs, nsrc, lambda_init):
    k_refs, v_refs = refs[0:2 * nsrc:2], refs[1:2 * nsrc:2]
    o_ref, s_scr, a_scr = refs[2 * nsrc:]
    lp = lam_ref[...]
    lam = (jnp.exp(jnp.sum(lp[0:1] * lp[1:2], axis=-1, keepdims=True))
           - jnp.exp(jnp.sum(lp[2:3] * lp[3:4], axis=-1, keepdims=True)) + lambda_init)
    tq = q_ref.shape[1]
    sub = s_scr.shape[1]
    ns = s_scr.shape[0] // 2
    units = list(range(0, tq, sub))

    def scores(u):
        q = q_ref[0, units[u]:units[u] + sub]
        for hh in range(2):
            _scores_to_scratch(_half_q(q, hh), k_refs, s_scr, 2 * (u % ns) + hh)

    def combine(u):
        linv = []
        for r in range(0, sub, DIFF_RB):
            e0, l0 = _exp_rows(s_scr[2 * (u % ns), r:r + DIFF_RB, :])
            e1, l1 = _exp_rows(s_scr[2 * (u % ns) + 1, r:r + DIFF_RB, :])
            a_scr[u % 2, r:r + DIFF_RB, :] = (e0 - e1 * (lam * l0 / l1)).astype(BF16)
            linv.append(1.0 / l0)
        return jnp.concatenate(linv, axis=0)

    for u in range(min(ns - 1, len(units))):
        scores(u)
    for u, r in enumerate(units):
        if u + ns - 1 < len(units):
            scores(u + ns - 1)
        linv = combine(u)
        o = _pv_from_scratch(a_scr, u % 2, v_refs) * linv
        o_ref[0, r:r + sub] = (_rms(o, g_ref[...]) * (1.0 - lambda_init)).astype(o_ref.dtype)


def _diff_attn_call(q, kvs, lam_params, subln_g, lambda_init):
    b, lq, d = q.shape
    tq = _row_tile(lq, 8 * ATT_SUB)
    sub = min(tq, ATT_SUB)
    nk = sum(k.shape[1] for k, _ in kvs)
    const = lambda bi, j, i: (0, 0)
    in_specs = [pl.BlockSpec((1, tq, LANES), lambda bi, j, i: (bi, i, j)),
                pl.BlockSpec(lam_params.shape, const), pl.BlockSpec(subln_g.shape, const)]
    args = [q, lam_params, subln_g]
    for k, v in kvs:
        spec = pl.BlockSpec((1, k.shape[1], LANES), lambda bi, j, i: (bi, 0, j))
        in_specs += [spec, spec]
        args += [k, v]
    return pl.pallas_call(
        functools.partial(_diff_attn_kernel, nsrc=len(kvs), lambda_init=lambda_init),
        out_shape=jax.ShapeDtypeStruct((b, lq, d), BF16),
        grid=(b, d // LANES, lq // tq),
        in_specs=in_specs,
        out_specs=pl.BlockSpec((1, tq, LANES), lambda bi, j, i: (bi, i, j)),
        scratch_shapes=[pltpu.VMEM((6, sub, nk), F32), pltpu.VMEM((2, sub, nk), BF16)],
        compiler_params=_cparams(("arbitrary", "arbitrary", "arbitrary")),
        name="diff_attn",
    )(*args)


def _na_attn_t_kernel(q_ref, k_ref, v_ref, kc_ref, vc_ref, rpb_ref, o_ref, tt2_scr, vt_scr, vct_scr, s_scr, p_scr, *,
                      rows, kr, kcw, nq):
    w = GRID_W
    nwin = nq + kr
    ntab = 2 * kr
    nkw = nwin * w
    sub = nq * w
    nk = s_scr.shape[1]
    nsub = q_ref.shape[1] // sub
    i = pl.program_id(2)
    shift = int(math.log2(w))

    @pl.when((i == 0) & (pl.program_id(1) == 0))
    def _build_bias():
        cp_io = lax.broadcasted_iota(jnp.int32, (w, LANES), 0)
        c_io = lax.broadcasted_iota(jnp.int32, (w, LANES), 1)
        cs = jnp.clip(c_io - kcw // 2, 0, w - kcw)
        colok = (cp_io >= cs) & (cp_io < cs + kcw)
        neg = jnp.full((w, LANES), NEG, F32)
        for hh in range(2):
            r = pltpu.roll(rpb_ref[hh] * LOG2E, LANES - (kcw - 1), 1)
            ts = []
            for e in range(ntab - 1):
                t = jnp.broadcast_to(r[e:e + 1], (w, LANES))
                t = pltpu.roll(t, 0, 1, stride=1, stride_axis=0)
                ts.append(jnp.where(colok, t, NEG))
            for x in range(ntab):
                lo = ts[x] if x < ntab - 1 else neg
                hi = ts[x - 1] if x >= 1 else neg
                tt2_scr[hh, x] = jnp.where(c_io < w, lo, pltpu.roll(hi, w, 1))

    @pl.when(i == 0)
    def _transpose_values():
        for c in range(vt_scr.shape[0]):
            vt_scr[c] = v_ref[0, c * sub:(c + 1) * sub, :].astype(F32).T.astype(BF16)
        vct_scr[...] = vc_ref[0].astype(F32).T.astype(BF16)

    units = [(t, hh) for t in range(nsub) for hh in range(2)]
    ns = s_scr.shape[0]

    def window(t):
        r0 = (i * nsub + t) * nq
        return r0, jnp.clip(r0 - kr // 2, 0, rows - nwin)

    def scores(u):
        t, hh = units[u]
        _, ks = window(t)
        qh = _half_q(q_ref[0, t * sub:(t + 1) * sub], hh)
        s_scr[u % ns, :nkw, :] = _dot_nt(k_ref[0, pl.ds(pl.multiple_of(ks * w, sub), nkw), :], qh)
        s_scr[u % ns, nkw:, :] = _dot_nt(kc_ref[0], qh)

    def softmax(u):
        t, hh = units[u]
        r0, ks = window(t)
        q_row = r0 + jnp.right_shift(lax.broadcasted_iota(jnp.int32, (1, sub), 1), shift)
        rs = jnp.clip(q_row - kr // 2, 0, rows - kr)
        macc = None
        for j in range(nwin):
            key_row = ks + j
            ok = (key_row >= rs) & (key_row < rs + kr)
            d0 = key_row - r0 + kr - 1
            bias = jnp.concatenate([tt2_scr[hh, jnp.clip(d0 - 2 * g, 0, ntab - 1)] for g in range(sub // LANES)], axis=1)
            rows_j = slice(j * w, (j + 1) * w)
            s = jnp.where(ok, s_scr[u % ns, rows_j, :] + bias, NEG)
            s_scr[u % ns, rows_j, :] = s
            macc = s if macc is None else jnp.maximum(macc, s)
        for j in range(nkw, nk, w):
            macc = jnp.maximum(macc, s_scr[u % ns, j:j + w, :])
        m = jnp.max(macc, axis=0, keepdims=True)
        lacc = None
        for j in range(0, nk, KEY_SLAB):
            e = jnp.exp2(s_scr[u % ns, j:j + KEY_SLAB, :] - m)
            p_scr[u % 2, j:j + KEY_SLAB, :] = e.astype(BF16)
            lacc = e if lacc is None else lacc + e
        return 1.0 / jnp.sum(lacc, axis=0, keepdims=True)

    for u in range(min(ns - 1, len(units))):
        scores(u)
    outs = [None, None]
    for u, (t, hh) in enumerate(units):
        if u + ns - 1 < len(units):
            scores(u + ns - 1)
        linv = softmax(u)
        _, ks = window(t)
        head = slice(hh * HEAD_DIM, (hh + 1) * HEAD_DIM)
        ot = _dot(vct_scr[head, :], p_scr[u % 2, nkw:, :])
        for c in range(nwin // nq):
            ot = ot + _dot(vt_scr[ks // nq + c, head, :], p_scr[u % 2, c * sub:(c + 1) * sub, :])
        outs[hh] = ot * linv
        if hh == 1:
            o_ref[0, t * sub:(t + 1) * sub] = jnp.concatenate(outs, axis=0).T.astype(o_ref.dtype)


def _na_attn_t_call(q, k, v, kc, vc, rpb):
    b, s_len, d = q.shape
    c_len = kc.shape[1]
    nh, ndr, ndc = rpb.shape
    kr, kcw = (ndr + 1) // 2, (ndc + 1) // 2
    rows = s_len // GRID_W
    nq = 4
    assert GRID_W == HEAD_DIM and rows % nq == 0 and rows >= nq + kr and (nq + kr) % nq == 0 and (kr // 2) % nq == 0
    assert ndc <= LANES and 2 * kr <= 2 * SUBLANES and nq * GRID_W == 2 * LANES
    rpb_p = jnp.full((nh, 2 * SUBLANES, LANES), NEG, F32).at[:, :ndr, :ndc].set(rpb[:, :, ::-1])
    sub = nq * GRID_W
    tq = _row_tile(s_len, 8 * sub)
    nk = (nq + kr) * GRID_W + c_len
    full = lambda n: pl.BlockSpec((1, n, LANES), lambda j, bi, i: (bi, 0, j))
    return pl.pallas_call(
        functools.partial(_na_attn_t_kernel, rows=rows, kr=kr, kcw=kcw, nq=nq),
        out_shape=jax.ShapeDtypeStruct((b, s_len, d), BF16),
        grid=(d // LANES, b, s_len // tq),
        in_specs=[pl.BlockSpec((1, tq, LANES), lambda j, bi, i: (bi, i, j)),
                  full(s_len), full(s_len), full(c_len), full(c_len),
                  pl.BlockSpec((2, 2 * SUBLANES, LANES), lambda j, bi, i: (j, 0, 0))],
        out_specs=pl.BlockSpec((1, tq, LANES), lambda j, bi, i: (bi, i, j)),
        scratch_shapes=[pltpu.VMEM((2, 2 * kr, GRID_W, LANES), F32), pltpu.VMEM((s_len // sub, LANES, sub), BF16),
                        pltpu.VMEM((LANES, c_len), BF16), pltpu.VMEM((3, nk, sub), F32),
                        pltpu.VMEM((2, nk, sub), BF16)],
        compiler_params=_cparams(("arbitrary", "arbitrary", "arbitrary")),
        name="na_attn_t",
    )(q, k, v, kc, vc, rpb_p)


def _outproj_kernel(a_ref, x_ref, mod_ref, g_ref, w_ref, o_ref, z_ref):
    tm = x_ref.shape[1]
    sub = _split_rows(tm)
    for r in range(0, tm, sub):
        rows = slice(r, r + sub)
        x = x_ref[0, rows] + mod_ref[0, 2:3] * _dot(a_ref[0, rows], w_ref[...])
        o_ref[0, rows] = x
        z_ref[0, rows] = _normmod(x, g_ref[...], mod_ref[0, 3:4], mod_ref[0, 4:5]).astype(z_ref.dtype)


def _outproj_call(a, x, mods_l, mod_row, g, w):
    b, n, d = x.shape
    k = a.shape[-1]
    tm = _row_tile(n, OUTPROJ_TM)
    row = lambda w_: pl.BlockSpec((1, tm, w_), lambda bi, i: (bi, i, 0))
    return pl.pallas_call(
        _outproj_kernel,
        out_shape=(jax.ShapeDtypeStruct((b, n, d), F32), jax.ShapeDtypeStruct((b, n, d), BF16)),
        grid=(b, n // tm),
        in_specs=[row(k), row(d), _mod_spec(d, mod_row),
                  pl.BlockSpec((1, d), lambda bi, i: (0, 0)),
                  pl.BlockSpec((k, d), lambda bi, i: (0, 0))],
        out_specs=(row(d), row(d)),
        compiler_params=_cparams(("arbitrary", "arbitrary")),
        name="outproj",
    )(a, x, mods_l, g, w)


def _ffn_kernel(x_ref, z_ref, zp_ref, zn_ref, mod_ref, wu_ref, cw_ref, wd_ref, *rest, fc, final):
    if final:
        fg_ref, o_ref, z_scr, u_scr, h_scr = rest
    else:
        o_ref, z_scr, u_scr, h_scr = rest
    i = pl.program_id(1)
    nt = pl.num_programs(1)
    tm, d = x_ref.shape[1], x_ref.shape[2]
    hb = FFN_HALO
    nch = wd_ref.shape[0] // fc
    z_scr[hb:hb + tm] = z_ref[0]
    z_scr[0:hb] = jnp.where(i > 0, zp_ref[0], jnp.zeros_like(zp_ref[0]))
    z_scr[hb + tm:] = jnp.where(i < nt - 1, zn_ref[0], jnp.zeros_like(zn_ref[0]))

    f = wd_ref.shape[0]
    nu = u_scr.shape[0]
    gate_cols = lambda j: slice(j * fc, (j + 1) * fc)
    value_cols = lambda j: slice(f + j * fc, f + (j + 1) * fc)

    def up_proj(j):
        z = z_scr[...]
        u_scr[j % nu, :, :fc] = _dot(z, wu_ref[:, gate_cols(j)])
        u_scr[j % nu, :, fc:] = _dot(z, wu_ref[:, value_cols(j)])

    def conv(j, cols, lanes, r0, sub):
        cw = cw_ref[:, cols]
        cv = cw[3:4] + cw[0:1] * u_scr[j % nu, r0 - 1:r0 - 1 + sub, lanes]
        cv = cv + cw[1:2] * u_scr[j % nu, r0:r0 + sub, lanes]
        return cv + cw[2:3] * u_scr[j % nu, r0 + 1:r0 + 1 + sub, lanes]

    def conv_act(j):
        sub = min(FFN_SUB, tm)
        for sb in range(tm // sub):
            r0 = hb + sb * sub
            hg = 0.5 * conv(j, gate_cols(j), slice(0, fc), r0, sub)
            h = (hg + hg * jnp.tanh(hg)) * conv(j, value_cols(j), slice(fc, 2 * fc), r0, sub)
            h_scr[sb * sub:(sb + 1) * sub, j * fc:(j + 1) * fc] = h.astype(BF16)

    for j in range(min(nu - 1, nch)):
        up_proj(j)
    for j in range(nch - 1):
        if j + nu - 1 < nch:
            up_proj(j + nu - 1)
        conv_act(j)
    split = (nch - 1) * fc
    acc = _dot(h_scr[:, :split], wd_ref[:split, :])
    conv_act(nch - 1)
    acc = acc + _dot(h_scr[:, split:], wd_ref[split:, :])
    out = x_ref[0] + mod_ref[0, 5:6] * acc
    if final:
        out = _rms(out, fg_ref[...])
    o_ref[0] = out


def _ffn_pack_params(w_up, conv_w, conv_b, w_down, fc):
    f2 = w_up.shape[1]
    assert (f2 // 2) % fc == 0
    cw = jnp.concatenate([conv_w, conv_b[None], jnp.zeros((SUBLANES - 1 - conv_w.shape[0], f2), F32)], axis=0)
    return w_up.astype(BF16), cw, w_down.astype(BF16)


def _ffn_call(x, z, mods_l, mod_row, packed, layer, fc, final_g=None):
    b, n, d = x.shape
    wu, cw, wd = packed
    nch, fc2 = wd.shape[1] // fc, 2 * fc
    tm = _row_tile(n, FFN_TM)
    hb = FFN_HALO
    nbh = tm // hb
    once = dict(pipeline_mode=pl.Buffered(1))
    layer_spec = lambda a, **kw: pl.BlockSpec((None,) + a.shape[1:], lambda bi, i: (layer,) + (0,) * (a.ndim - 1), **kw)
    in_specs = [
        pl.BlockSpec((1, tm, d), lambda bi, i: (bi, i, 0)),
        pl.BlockSpec((1, tm, d), lambda bi, i: (bi, i, 0)),
        pl.BlockSpec((1, hb, d), lambda bi, i: (bi, jnp.maximum(i * nbh - 1, 0), 0)),
        pl.BlockSpec((1, hb, d), lambda bi, i: (bi, jnp.minimum((i + 1) * nbh, n // hb - 1), 0)),
        _mod_spec(d, mod_row),
        layer_spec(wu, **once),
        layer_spec(cw),
        layer_spec(wd, **once),
    ]
    args = [x, z, z, z, mods_l, wu, cw, wd]
    if final_g is not None:
        in_specs.append(pl.BlockSpec((1, d), lambda bi, i: (0, 0)))
        args.append(final_g)
    return pl.pallas_call(
        functools.partial(_ffn_kernel, fc=fc2 // 2, final=final_g is not None),
        out_shape=jax.ShapeDtypeStruct((b, n, d), F32),
        grid=(b, n // tm),
        in_specs=in_specs,
        out_specs=pl.BlockSpec((1, tm, d), lambda bi, i: (bi, i, 0)),
        scratch_shapes=[pltpu.VMEM((tm + 2 * hb, d), BF16), pltpu.VMEM((3, tm + 2 * hb, fc2), F32),
                        pltpu.VMEM((tm, nch * fc2 // 2), BF16)],
        compiler_params=_cparams(("arbitrary", "arbitrary")),
        name="ffn",
    )(*args)


def _rg_mixer(xl, xc, mods_l, ctx_row, g, w_in, conv_w, conv_b, wa, ba, wx, bx, lam, need_ctx):
    w = w_in.astype(BF16)
    g_l, xr_l = _rg_proj_call(xl, mods_l, None, g, w)
    g_c, xr_c = _rg_proj_call(xc, mods_l, ctx_row, g, w)
    cw, wbd, gp = _rg_pack_params(conv_w, conv_b, wa, ba, wx, bx, lam)
    m_c, m_l = _rg_scan_call(xr_c, g_c, xr_l, g_l, cw, wbd, gp)
    return m_l, (m_c if need_ctx else None)


def _na_mixer(xl, xc, mods_l, ctx_row, g, w_in, rpb, need_ctx):
    d = xl.shape[-1]
    w = w_in.astype(BF16)
    ql, kl, vl = _qkv_proj_call(xl, mods_l, None, g, w, d)
    qc, kc, vc = _qkv_proj_call(xc, mods_l, ctx_row, g, w, d)
    o_l = _na_attn_t_call(ql, kl, vl, kc, vc, rpb)
    o_c = _plain_attn_call(qc, [(kc, vc)], LANES) if need_ctx else None
    return o_l, o_c


def _gqa_mixer(xl, xc, mods_l, ctx_row, g, w_in, q_norm, k_norm, need_ctx):
    d = xl.shape[-1]
    nh = d // HEAD_DIM
    nkv = (w_in.shape[1] // HEAD_DIM - nh) // 2
    group = nh // nkv
    assert group % 2 == 0 and q_norm.shape[0] == HEAD_DIM
    dup = lambda wk: jnp.concatenate([wk.reshape(d, nkv, 1, HEAD_DIM)] * 2, axis=2).reshape(d, 2 * nkv * HEAD_DIM)
    nk = 2 * nkv * HEAD_DIM
    w = jnp.concatenate([w_in[:, :d], dup(w_in[:, d:d + nkv * HEAD_DIM]), dup(w_in[:, d + nkv * HEAD_DIM:])],
                        axis=1).astype(BF16)
    nseg = (d + nk) // HEAD_DIM
    assert nseg <= LANES
    seg = jnp.arange(d + nk) // HEAD_DIM
    ind = (seg[:, None] == jnp.arange(LANES)[None, :]).astype(BF16)
    gain = jnp.concatenate([jnp.tile(q_norm, nh), jnp.tile(k_norm, 2 * nkv)])[None]
    qkn = (gain, ind, ind.T)
    ql, kl, vl = _qkv_proj_call(xl, mods_l, None, g, w, nk, rope=_rope_tables(xl.shape[1]), qknorm=qkn)
    qc, kc, vc = _qkv_proj_call(xc, mods_l, ctx_row, g, w, nk, qknorm=qkn)
    qw = group * HEAD_DIM
    o_l = _plain_attn_t_call(ql, [(kl, vl), (kc, vc)], qw)
    o_c = _plain_attn_call(qc, [(kc, vc)], qw) if need_ctx else None
    return o_l, o_c


def _diff_mixer(xl, xc, mods_l, ctx_row, g, w_in, lq1, lk1, lq2, lk2, subln_g, lambda_init, need_ctx):
    d = xl.shape[-1]
    assert lq1.shape[0] == HEAD_DIM and subln_g.shape[0] == LANES
    w = w_in.astype(BF16)
    ql, kl, vl = _qkv_proj_call(xl, mods_l, None, g, w, d, rope=_rope_tables(xl.shape[1]))
    qc, kc, vc = _qkv_proj_call(xc, mods_l, ctx_row, g, w, d)
    lam_params = jnp.concatenate([jnp.stack([lq1, lk1, lq2, lk2]), jnp.zeros((SUBLANES - 4, HEAD_DIM), F32)])
    o_l = _diff_attn_call(ql, [(kl, vl), (kc, vc)], lam_params, subln_g[None], lambda_init)
    o_c = _diff_attn_call(qc, [(kc, vc)], lam_params, subln_g[None], lambda_init) if need_ctx else None
    return o_l, o_c


def kernel(x, c, ctx, c_ctx, mod_w, mod_b, norm1_g, norm2_g, rg_w_in, rg_conv_w, rg_conv_b, rg_wa, rg_ba, rg_wx, rg_bx, rg_lam, rg_w_out, na_w_in, na_rpb, na_w_out, gqa_w_in, gqa_q_norm, gqa_k_norm, gqa_w_out, diff_w_in, diff_lq1, diff_lk1, diff_lq2, diff_lk2, diff_subln_g, diff_w_out, ffn_w_up, ffn_conv_w, ffn_conv_b, ffn_w_down, final_g):
    bsz, _, d = x.shape
    depth = mod_w.shape[0]
    assert d % LANES == 0
    ctx_row = bsz
    nrow = -(-(bsz + 1) // SUBLANES) * SUBLANES
    cc = jnp.concatenate([c, c_ctx[None], jnp.zeros((nrow - bsz - 1, d), F32)], axis=0)
    mods = _mods_call(cc, mod_w, mod_b).reshape(depth, nrow, 6, d)
    f = ffn_w_down.shape[1]
    fc = next(t for t in (256, 128) if f % t == 0)
    packed = jax.vmap(functools.partial(_ffn_pack_params, fc=fc))(ffn_w_up, ffn_conv_w, ffn_conv_b, ffn_w_down)
    xl, xc = x, ctx
    for l in range(depth):
        m, j = l % N_MIXERS, l // N_MIXERS
        need_ctx = l < depth - 1
        mods_l = mods[l]
        g1, g2 = norm1_g[l][None], norm2_g[l][None]
        if m == 0:
            a_l, a_c = _rg_mixer(xl, xc, mods_l, ctx_row, g1, rg_w_in[j], rg_conv_w[j], rg_conv_b[j], rg_wa[j],
                                 rg_ba[j], rg_wx[j], rg_bx[j], rg_lam[j], need_ctx)
            w_out = rg_w_out[j]
        elif m == 1:
            a_l, a_c = _na_mixer(xl, xc, mods_l, ctx_row, g1, na_w_in[j], na_rpb[j], need_ctx)
            w_out = na_w_out[j]
        elif m == 2:
            a_l, a_c = _gqa_mixer(xl, xc, mods_l, ctx_row, g1, gqa_w_in[j], gqa_q_norm[j], gqa_k_norm[j], need_ctx)
            w_out = gqa_w_out[j]
        else:
            lambda_init = 0.8 - 0.6 * math.exp(-0.3 * l)
            a_l, a_c = _diff_mixer(xl, xc, mods_l, ctx_row, g1, diff_w_in[j], diff_lq1[j], diff_lk1[j],
                                   diff_lq2[j], diff_lk2[j], diff_subln_g[j], lambda_init, need_ctx)
            w_out = diff_w_out[j]
        w_out = w_out.astype(BF16)
        last = l == depth - 1
        xl, zl = _outproj_call(a_l, xl, mods_l, None, g2, w_out)
        xl = _ffn_call(xl, zl, mods_l, None, packed, l, fc, final_g[None] if last else None)
        if need_ctx:
            xc, zc = _outproj_call(a_c, xc, mods_l, ctx_row, g2, w_out)
            xc = _ffn_call(xc, zc, mods_l, ctx_row, packed, l, fc)
    return xl
```

```python
import functools
import math

import jax
import jax.numpy as jnp
from jax import lax
from jax.experimental import pallas as pl
from jax.experimental.pallas import tpu as pltpu

F32 = jnp.float32
BF16 = jnp.bfloat16
EPS = 1e-6
ROPE_THETA = 10000.0
GRID_W = 64
RG_C = 8.0
RG_CONV_LEFT = 2
N_MIXERS = 4
HEAD_DIM = 64
LOG2E = 1.4426950408889634
Q_SCALE = HEAD_DIM ** -0.5 * LOG2E
LANES = 128
SUBLANES = 8
NEG = -1e30
FFN_HALO = 16
FFN_SUB = 512
FFN_TM = 512
FFN_SLOTS = 4
OUTPROJ_TM = 1024
RG_TL = 1024
PROJ_TM = 1024
PROJ_SPLIT = 4
PROJ_MIN_ROWS = 256
ATT_SUB = 256
ATT_RB = 32
DIFF_RB = 16
DIFF_UNITS = 3
KEY_SLAB = 32
VMEM_LIMIT_BYTES = 56 * 1024 * 1024

_NT = (((1,), (1,)), ((), ()))


def _cparams(sem):
    return pltpu.CompilerParams(dimension_semantics=sem, vmem_limit_bytes=VMEM_LIMIT_BYTES)


def _dot(a, b):
    return jnp.dot(a, b, preferred_element_type=F32)


def _dot_nt(a, b):
    return lax.dot_general(a, b, _NT, preferred_element_type=F32)


def _silu(x):
    return x / (1.0 + jnp.exp(-x))


def _gelu_tanh(x):
    cdf = 0.5 * (1.0 + jnp.tanh(math.sqrt(2.0 / math.pi) * (x + 0.044715 * (x * x * x))))
    return x * cdf


def _rms(x, g):
    y = x * lax.rsqrt(jnp.mean(x * x, axis=-1, keepdims=True) + EPS)
    return y * g


def _normmod(x, g, shift, scale):
    return _rms(x, g) * (1.0 + scale) + shift


def _mods_kernel(cc_ref, w_ref, b_ref, o_ref):
    a = _silu(cc_ref[...]).astype(BF16)
    o_ref[0] = _dot(a, w_ref[0].astype(BF16)) + b_ref[0]


def _mods_call(cc, mod_w, mod_b):
    depth, d, n = mod_w.shape
    r = cc.shape[0]
    tn = n // 4
    return pl.pallas_call(
        _mods_kernel,
        out_shape=jax.ShapeDtypeStruct((depth, r, n), F32),
        grid=(depth, n // tn),
        in_specs=[
            pl.BlockSpec((r, d), lambda l, j: (0, 0)),
            pl.BlockSpec((1, d, tn), lambda l, j: (l, 0, j)),
            pl.BlockSpec((1, 1, tn), lambda l, j: (l, 0, j)),
        ],
        out_specs=pl.BlockSpec((1, r, tn), lambda l, j: (l, 0, j)),
        compiler_params=_cparams(("arbitrary", "arbitrary")),
        name="mods",
    )(cc, mod_w, mod_b.reshape(depth, 1, n))


def _mod_spec(d, row):
    if row is None:
        return pl.BlockSpec((1, 6, d), lambda b, *_: (b, 0, 0))
    return pl.BlockSpec((1, 6, d), lambda b, *_: (row, 0, 0))


def _split_rows(tm):
    sub = tm // PROJ_SPLIT
    return sub if (tm % PROJ_SPLIT == 0 and sub >= PROJ_MIN_ROWS) else tm


def _row_tile(n, pref):
    t = min(n, pref)
    assert n % t == 0
    return t


def _rope_tables(n_tok):
    t = jnp.arange(n_tok)
    row = (t // GRID_W).astype(F32)
    col = (t % GRID_W).astype(F32)
    n = HEAD_DIM // 4
    inv = ROPE_THETA ** (-jnp.arange(n, dtype=F32) / n)
    ang = jnp.concatenate([row[:, None] * inv, col[:, None] * inv], axis=-1)
    cos, sin = jnp.cos(ang), jnp.sin(ang)
    reps = LANES // HEAD_DIM
    cos_t = jnp.tile(jnp.concatenate([cos, cos], axis=-1), (1, reps))
    sin_t = jnp.tile(jnp.concatenate([-sin, sin], axis=-1), (1, reps))
    return cos_t, sin_t


def _swap_halves(x):
    lane = lax.broadcasted_iota(jnp.int32, x.shape, 1)
    lo = (lane & (HEAD_DIM // 2)) == 0
    return jnp.where(lo, pltpu.roll(x, LANES - HEAD_DIM // 2, 1), pltpu.roll(x, HEAD_DIM // 2, 1))


def _rope_cols(u, cos_t, sin_t):
    outs = []
    for c in range(u.shape[1] // LANES):
        xc = u[:, c * LANES:(c + 1) * LANES]
        outs.append(xc * cos_t + _swap_halves(xc) * sin_t)
    return jnp.concatenate(outs, axis=1)


def _rg_proj_kernel(x_ref, mod_ref, g_ref, w_ref, go_ref, xo_ref, *, dr):
    tm = x_ref.shape[1]
    sub = _split_rows(tm)
    for r in range(0, tm, sub):
        rows = slice(r, r + sub)
        z = _normmod(x_ref[0, rows], g_ref[...], mod_ref[0, 0:1], mod_ref[0, 1:2]).astype(BF16)
        u = _dot(z, w_ref[...])
        go_ref[0, rows] = _gelu_tanh(u[:, :dr])
        xo_ref[0, rows] = u[:, dr:]


def _rg_proj_call(x, mods_l, mod_row, g, w):
    b, n, d = x.shape
    dr = w.shape[1] // 2
    tm = _row_tile(n, PROJ_TM)
    return pl.pallas_call(
        functools.partial(_rg_proj_kernel, dr=dr),
        out_shape=(jax.ShapeDtypeStruct((b, n, dr), F32), jax.ShapeDtypeStruct((b, n, dr), F32)),
        grid=(b, n // tm),
        in_specs=[
            pl.BlockSpec((1, tm, d), lambda bi, i: (bi, i, 0)),
            _mod_spec(d, mod_row),
            pl.BlockSpec((1, d), lambda bi, i: (0, 0)),
            pl.BlockSpec((d, 2 * dr), lambda bi, i: (0, 0)),
        ],
        out_specs=(pl.BlockSpec((1, tm, dr), lambda bi, i: (bi, i, 0)),
                   pl.BlockSpec((1, tm, dr), lambda bi, i: (bi, i, 0))),
        compiler_params=_cparams(("arbitrary", "arbitrary")),
        name="rg_proj",
    )(x, mods_l, g, w)


def _qkv_proj_kernel(x_ref, mod_ref, g_ref, w_ref, *rest, d, rope, qknorm):
    rest = list(rest)
    if rope:
        cos_ref, sin_ref = rest.pop(0), rest.pop(0)
    if qknorm:
        gain_ref, ind_ref, indt_ref = rest.pop(0), rest.pop(0), rest.pop(0)
    q_ref, k_ref, v_ref = rest
    nk = k_ref.shape[-1]
    tm = x_ref.shape[1]
    sub = _split_rows(tm)
    for r in range(0, tm, sub):
        rows = slice(r, r + sub)
        z = _normmod(x_ref[0, rows], g_ref[...], mod_ref[0, 0:1], mod_ref[0, 1:2]).astype(BF16)
        u = _dot(z, w_ref[...])
        qk = u[:, :d + nk]
        if qknorm:
            x2 = qk * qk
            hi = x2.astype(BF16)
            lo = (x2 - hi.astype(F32)).astype(BF16)
            ssq = _dot(hi, ind_ref[...]) + _dot(lo, ind_ref[...])
            rs = lax.rsqrt(ssq * (1.0 / HEAD_DIM) + EPS)
            rhi = rs.astype(BF16)
            rlo = (rs - rhi.astype(F32)).astype(BF16)
            rb = _dot(rhi, indt_ref[...]) + _dot(rlo, indt_ref[...])
            qk = (qk * rb) * gain_ref[...]
        if rope:
            qk = _rope_cols(qk, cos_ref[rows], sin_ref[rows])
        q_ref[0, rows] = (qk[:, :d] * Q_SCALE).astype(q_ref.dtype)
        k_ref[0, rows] = qk[:, d:].astype(k_ref.dtype)
        v_ref[0, rows] = u[:, d + nk:].astype(v_ref.dtype)


def _qkv_proj_call(x, mods_l, mod_row, g, w, nk, *, rope=None, qknorm=None):
    b, n, d = x.shape
    nv = w.shape[1] - d - nk
    tm = _row_tile(n, PROJ_TM)
    const = lambda bi, i: (0, 0)
    in_specs = [
        pl.BlockSpec((1, tm, d), lambda bi, i: (bi, i, 0)),
        _mod_spec(d, mod_row),
        pl.BlockSpec((1, d), const),
        pl.BlockSpec(w.shape, const),
    ]
    args = [x, mods_l, g, w]
    if rope is not None:
        in_specs += [pl.BlockSpec((tm, LANES), lambda bi, i: (i, 0))] * 2
        args += list(rope)
    if qknorm is not None:
        in_specs += [pl.BlockSpec(a.shape, const) for a in qknorm]
        args += list(qknorm)
    out_spec = lambda w_: pl.BlockSpec((1, tm, w_), lambda bi, i: (bi, i, 0))
    return pl.pallas_call(
        functools.partial(_qkv_proj_kernel, d=d, rope=rope is not None, qknorm=qknorm is not None),
        out_shape=(jax.ShapeDtypeStruct((b, n, d), BF16), jax.ShapeDtypeStruct((b, n, nk), BF16),
                   jax.ShapeDtypeStruct((b, n, nv), BF16)),
        grid=(b, n // tm),
        in_specs=in_specs,
        out_specs=(out_spec(d), out_spec(nk), out_spec(nv)),
        compiler_params=_cparams(("arbitrary", "arbitrary")),
        name="qkv_proj",
    )(*args)


def _rg_conv(xs_scr, x, prev, nxt, cw):
    t = x.shape[0]
    h = SUBLANES
    xs_scr[0:h] = prev
    xs_scr[h:h + t] = x
    xs_scr[h + t:2 * h + t] = nxt
    y = cw[4:5] + cw[0:1] * xs_scr[h - 2:h - 2 + t]
    y = y + cw[1:2] * xs_scr[h - 1:h - 1 + t]
    y = y + cw[2:3] * x
    y = y + cw[3:4] * xs_scr[h + 1:h + 1 + t]
    return y


def _sigmoid(x):
    return 0.5 * jnp.tanh(0.5 * x) + 0.5


def _rg_gates(xc, w, ba, bx, lam):
    gw = xc.shape[1]
    y = _dot(xc.astype(BF16), w)
    r = _sigmoid(y[:, :gw] + ba)
    i = _sigmoid(y[:, gw:] + bx)
    nl = -lam
    softplus = jnp.maximum(nl, 0.0) + jnp.log1p(jnp.exp(-jnp.abs(nl)))
    log_a = r * (-RG_C * softplus)
    a = jnp.exp(log_a)
    one_minus_a2 = -jnp.tanh(log_a) * (a * a + 1.0)
    root = jnp.where(one_minus_a2 > 0.0, one_minus_a2 * lax.rsqrt(one_minus_a2), 0.0)
    return a, root * (i * xc)


def _rg_scan_tile(a, b, carry, reverse):
    t, gw = a.shape
    nck = t // SUBLANES
    a = a.reshape(nck, SUBLANES, gw)
    b = b.reshape(nck, SUBLANES, gw)
    rmod = lax.broadcasted_iota(jnp.int32, (1, SUBLANES, 1), 1)
    for s in (1, 2, 4):
        ok = (rmod < SUBLANES - s) if reverse else (rmod >= s)
        shift = SUBLANES - s if reverse else s
        ash, bsh = pltpu.roll(a, shift, 1), pltpu.roll(b, shift, 1)
        b = a * jnp.where(ok, bsh, 0.0) + b
        a = a * jnp.where(ok, ash, 1.0)
    hs = [None] * nck
    for ck in (range(nck - 1, -1, -1) if reverse else range(nck)):
        hc = b[ck] + a[ck] * carry
        carry = hc[0:1] if reverse else hc[SUBLANES - 1:SUBLANES]
        hs[ck] = hc
    return jnp.concatenate(hs, axis=0), carry


def _rg_scan_kernel(xc_ref, gc_ref, xl_ref, xp_ref, xn_ref, gl_ref, cw_ref, w_ref, gp_ref,
                    mc_ref, ml_ref, hf_scr, xv_scr, car_scr, xs_scr, *, nt):
    s = pl.program_id(2)
    gw = xl_ref.shape[-1]
    tl = xl_ref.shape[1]
    cw = cw_ref[0]
    gp = gp_ref[0]

    def gates(xconv, d):
        w = w_ref[0, :, d * 2 * gw:(d + 1) * 2 * gw]
        return _rg_gates(xconv, w, gp[3 * d:3 * d + 1], gp[3 * d + 1:3 * d + 2], gp[3 * d + 2:3 * d + 3])

    @pl.when(s == 0)
    def _ctx():
        zeros8 = jnp.zeros((SUBLANES, gw), F32)
        xconv = _rg_conv(xs_scr, xc_ref[0], zeros8, zeros8, cw)
        zero = jnp.zeros((1, gw), F32)
        a0, b0 = gates(xconv, 0)
        hf, cf = _rg_scan_tile(a0, b0, zero, False)
        a1, b1 = gates(xconv, 1)
        hb, cb = _rg_scan_tile(a1, b1, zero, True)
        car_scr[0:1] = cf
        car_scr[1:2] = cb
        mc_ref[0] = (gc_ref[0] * (hf + hb)).astype(mc_ref.dtype)

    def lat_conv(t):
        prev = xp_ref[0] * (t > 0).astype(F32)
        nxt = xn_ref[0] * (t < nt - 1).astype(F32)
        return _rg_conv(xs_scr, xl_ref[0], prev, nxt, cw)

    @pl.when((s >= 1) & (s <= nt))
    def _fwd():
        t = s - 1
        xconv = lat_conv(t)
        xv_scr[pl.ds(pl.multiple_of(t * tl, tl), tl), :] = xconv
        a0, b0 = gates(xconv, 0)
        h, c = _rg_scan_tile(a0, b0, car_scr[0:1], False)
        car_scr[0:1] = c
        hf_scr[pl.ds(pl.multiple_of(t * tl, tl), tl), :] = h

    @pl.when(s > nt)
    def _bwd():
        t = 2 * nt - s
        a1, b1 = gates(xv_scr[pl.ds(pl.multiple_of(t * tl, tl), tl), :], 1)
        h, c = _rg_scan_tile(a1, b1, car_scr[1:2], True)
        car_scr[1:2] = c
        hf = hf_scr[pl.ds(pl.multiple_of(t * tl, tl), tl), :]
        ml_ref[0] = (gl_ref[0] * (hf + h)).astype(ml_ref.dtype)


def _rg_scan_call(xr_c, g_c, xr_l, g_l, cw, wbd, gp):
    b, s_len, dr = xr_l.shape
    c_len = xr_c.shape[1]
    ng, gw = wbd.shape[0], wbd.shape[1]
    tl = _row_tile(s_len, RG_TL)
    nt = s_len // tl
    nb8 = tl // SUBLANES

    def tile(s):
        return jnp.clip(s - 1, 0, nt - 1)

    def tile_bwd(s):
        return jnp.where(s <= nt, nt - 1, 2 * nt - s)

    return pl.pallas_call(
        functools.partial(_rg_scan_kernel, nt=nt),
        out_shape=(jax.ShapeDtypeStruct((b, c_len, dr), BF16), jax.ShapeDtypeStruct((b, s_len, dr), BF16)),
        grid=(b, ng, 2 * nt + 1),
        in_specs=[
            pl.BlockSpec((1, c_len, gw), lambda bi, h, s: (bi, 0, h)),
            pl.BlockSpec((1, c_len, gw), lambda bi, h, s: (bi, 0, h)),
            pl.BlockSpec((1, tl, gw), lambda bi, h, s: (bi, tile(s), h)),
            pl.BlockSpec((1, SUBLANES, gw), lambda bi, h, s: (bi, jnp.maximum(tile(s) * nb8 - 1, 0), h)),
            pl.BlockSpec((1, SUBLANES, gw),
                         lambda bi, h, s: (bi, jnp.minimum((tile(s) + 1) * nb8, s_len // SUBLANES - 1), h)),
            pl.BlockSpec((1, tl, gw), lambda bi, h, s: (bi, tile_bwd(s), h)),
            pl.BlockSpec((1, SUBLANES, gw), lambda bi, h, s: (h, 0, 0)),
            pl.BlockSpec((1, gw, 4 * gw), lambda bi, h, s: (h, 0, 0)),
            pl.BlockSpec((1, SUBLANES, gw), lambda bi, h, s: (h, 0, 0)),
        ],
        out_specs=(pl.BlockSpec((1, c_len, gw), lambda bi, h, s: (bi, 0, h)),
                   pl.BlockSpec((1, tl, gw), lambda bi, h, s: (bi, tile_bwd(s), h))),
        scratch_shapes=[pltpu.VMEM((s_len, gw), F32), pltpu.VMEM((s_len, gw), F32), pltpu.VMEM((SUBLANES, gw), F32),
                        pltpu.VMEM((max(tl, c_len) + 2 * SUBLANES, gw), F32)],
        compiler_params=_cparams(("arbitrary", "arbitrary", "arbitrary")),
        name="rg_scan",
    )(xr_c, g_c, xr_l, xr_l, xr_l, g_l, cw, wbd, gp)


def _rg_pack_params(conv_w, conv_b, wa, ba, wx, bx, lam):
    nblk, bw = wa.shape[1], wa.shape[2]
    dr = nblk * bw
    per = next(p for p in range(1, nblk + 1) if nblk % p == 0 and (p * bw) % LANES == 0)
    ng, gw = nblk // per, per * bw

    def dense(w):
        w = w.reshape(ng, per, bw, bw)
        eye = jnp.eye(per, dtype=w.dtype)
        return jnp.einsum('gpjk,pq->gpjqk', w, eye).reshape(ng, gw, gw)

    wbd = jnp.concatenate([dense(wa[0]), dense(wx[0]), dense(wa[1]), dense(wx[1])], axis=-1).astype(BF16)
    cw = jnp.concatenate([conv_w, conv_b[None], jnp.zeros((SUBLANES - 1 - conv_w.shape[0], dr), F32)], axis=0)
    cw = cw.reshape(SUBLANES, ng, gw).transpose(1, 0, 2)
    gp = jnp.stack([ba[0], bx[0], lam[0], ba[1], bx[1], lam[1], jnp.zeros_like(lam[0]), jnp.zeros_like(lam[0])])
    gp = gp.reshape(SUBLANES, ng, gw).transpose(1, 0, 2)
    return cw, wbd, gp


def _pair_masks(shape):
    lane = lax.broadcasted_iota(jnp.int32, shape, 1)
    return lane < HEAD_DIM


def _half_q(q, hh):
    first = _pair_masks(q.shape)
    return jnp.where(first if hh == 0 else jnp.logical_not(first), q, jnp.zeros_like(q))


def _scores_to_scratch(qh, k_refs, s_scr, slot):
    off = 0
    for k_ref in k_refs:
        n = k_ref.shape[1]
        s_scr[slot, :, off:off + n] = _dot_nt(qh, k_ref[0])
        off += n


def _exp_rows(s):
    e = jnp.exp2(s - s.max(axis=-1, keepdims=True))
    return e, e.sum(axis=-1, keepdims=True)


def _softmax_to_scratch(s_scr, p_scr, slot, rows):
    linv = []
    for r in range(0, rows, ATT_RB):
        e, l = _exp_rows(s_scr[slot, r:r + ATT_RB, :])
        p_scr[slot, r:r + ATT_RB, :] = e.astype(BF16)
        linv.append(1.0 / l)
    return jnp.concatenate(linv, axis=0)


def _pv_from_scratch(p_scr, slot, v_refs):
    off, o = 0, None
    for v_ref in v_refs:
        n = v_ref.shape[1]
        t = _dot(p_scr[slot, :, off:off + n], v_ref[0])
        o = t if o is None else o + t
        off += n
    return o


def _plain_attn_kernel(q_ref, *refs, nsrc):
    k_refs, v_refs = refs[0:2 * nsrc:2], refs[1:2 * nsrc:2]
    o_ref, s_scr, p_scr = refs[2 * nsrc:]
    tq, qw = q_ref.shape[1], q_ref.shape[2]
    sub = s_scr.shape[1]
    units = [(r, c, hh) for r in range(0, tq, sub) for c in range(0, qw, LANES) for hh in range(2)]

    def scores(u):
        r, c, hh = units[u]
        _scores_to_scratch(_half_q(q_ref[0, r:r + sub, c:c + LANES], hh), k_refs, s_scr, u % 2)

    scores(0)
    outs = [None, None]
    for u, (r, c, hh) in enumerate(units):
        if u + 1 < len(units):
            scores(u + 1)
        linv = _softmax_to_scratch(s_scr, p_scr, u % 2, sub)
        outs[hh] = _pv_from_scratch(p_scr, u % 2, v_refs) * linv
        if hh == 1:
            o_ref[0, r:r + sub, c:c + LANES] = jnp.where(_pair_masks((sub, LANES)), outs[0], outs[1]).astype(o_ref.dtype)


def _plain_attn_t_kernel(q_ref, *refs, nsrc):
    k_refs, v_refs = refs[0:2 * nsrc:2], refs[1:2 * nsrc:2]
    o_ref = refs[2 * nsrc]
    vt_scrs = refs[2 * nsrc + 1:3 * nsrc + 1]
    s_scr, p_scr = refs[3 * nsrc + 1:]
    tq, qw = q_ref.shape[1], q_ref.shape[2]
    sub = s_scr.shape[2]

    @pl.when(pl.program_id(2) == 0)
    def _transpose_values():
        for v_ref, vt_scr in zip(v_refs, vt_scrs):
            vt_scr[...] = v_ref[0].astype(F32).T.astype(BF16)

    units = [(r, c, hh) for r in range(0, tq, sub) for c in range(0, qw, LANES) for hh in range(2)]

    def scores(u):
        r, c, hh = units[u]
        qh = _half_q(q_ref[0, r:r + sub, c:c + LANES], hh)
        off = 0
        for k_ref in k_refs:
            n = k_ref.shape[1]
            s_scr[u % ns, off:off + n, :] = _dot_nt(k_ref[0], qh)
            off += n

    ns = s_scr.shape[0]
    for u in range(min(ns - 1, len(units))):
        scores(u)
    outs = [None, None]
    for u, (r, c, hh) in enumerate(units):
        if u + ns - 1 < len(units):
            scores(u + ns - 1)
        slabs = [slice(i, i + KEY_SLAB) for i in range(0, s_scr.shape[1], KEY_SLAB)]
        macc = s_scr[u % ns, slabs[0], :]
        for sl in slabs[1:]:
            macc = jnp.maximum(macc, s_scr[u % ns, sl, :])
        m = jnp.max(macc, axis=0, keepdims=True)
        lacc = None
        for sl in slabs:
            e = jnp.exp2(s_scr[u % ns, sl, :] - m)
            p_scr[u % 2, sl, :] = e.astype(BF16)
            lacc = e if lacc is None else lacc + e
        linv = 1.0 / jnp.sum(lacc, axis=0, keepdims=True)
        off, ot = 0, None
        for vt_scr in vt_scrs:
            n = vt_scr.shape[1]
            t = _dot(vt_scr[hh * HEAD_DIM:(hh + 1) * HEAD_DIM, :], p_scr[u % 2, off:off + n, :])
            ot = t if ot is None else ot + t
            off += n
        outs[hh] = ot * linv
        if hh == 1:
            o_ref[0, r:r + sub, c:c + LANES] = jnp.concatenate(outs, axis=0).T.astype(o_ref.dtype)


def _plain_attn_t_call(q, kvs, qw):
    b, lq, d = q.shape
    tq = _row_tile(lq, 4 * ATT_SUB)
    sub = min(tq, ATT_SUB)
    nk = sum(k.shape[1] for k, _ in kvs)
    in_specs = [pl.BlockSpec((1, tq, qw), lambda bi, j, i: (bi, i, j))]
    args = [q]
    for k, v in kvs:
        spec = pl.BlockSpec((1, k.shape[1], LANES), lambda bi, j, i: (bi, 0, j))
        in_specs += [spec, spec]
        args += [k, v]
    return pl.pallas_call(
        functools.partial(_plain_attn_t_kernel, nsrc=len(kvs)),
        out_shape=jax.ShapeDtypeStruct((b, lq, d), BF16),
        grid=(b, d // qw, lq // tq),
        in_specs=in_specs,
        out_specs=pl.BlockSpec((1, tq, qw), lambda bi, j, i: (bi, i, j)),
        scratch_shapes=([pltpu.VMEM((LANES, k.shape[1]), BF16) for k, _ in kvs]
                        + [pltpu.VMEM((3, nk, sub), F32), pltpu.VMEM((2, nk, sub), BF16)]),
        compiler_params=_cparams(("arbitrary", "arbitrary", "arbitrary")),
        name="plain_attn_t",
    )(*args)


def _plain_attn_call(q, kvs, qw):
    b, lq, d = q.shape
    tq = _row_tile(lq, 2 * ATT_SUB)
    sub = min(tq, ATT_SUB)
    nk = sum(k.shape[1] for k, _ in kvs)
    in_specs = [pl.BlockSpec((1, tq, qw), lambda bi, j, i: (bi, i, j))]
    args = [q]
    for k, v in kvs:
        spec = pl.BlockSpec((1, k.shape[1], LANES), lambda bi, j, i: (bi, 0, j))
        in_specs += [spec, spec]
        args += [k, v]
    return pl.pallas_call(
        functools.partial(_plain_attn_kernel, nsrc=len(kvs)),
        out_shape=jax.ShapeDtypeStruct((b, lq, d), BF16),
        grid=(b, d // qw, lq // tq),
        in_specs=in_specs,
        out_specs=pl.BlockSpec((1, tq, qw), lambda bi, j, i: (bi, i, j)),
        scratch_shapes=[pltpu.VMEM((2, sub, nk), F32), pltpu.VMEM((2, sub, nk), BF16)],
        compiler_params=_cparams(("arbitrary", "arbitrary", "arbitrary")),
        name="plain_attn",
    )(*args)


def _diff_attn_kernel(q_ref, lam_ref, g_ref, *refs, nsrc, lambda_init):
    k_refs, v_refs = refs[0:2 * nsrc:2], refs[1:2 * nsrc:2]
    o_ref, s_scr, a_scr = refs[2 * nsrc:]
    lp = lam_ref[...]
    lam = (jnp.exp(jnp.sum(lp[0:1] * lp[1:2], axis=-1, keepdims=True))
           - jnp.exp(jnp.sum(lp[2:3] * lp[3:4], axis=-1, keepdims=True)) + lambda_init)
    tq = q_ref.shape[1]
    sub = s_scr.shape[1]
    ns = s_scr.shape[0] // 2
    units = list(range(0, tq, sub))

    def scores(u):
        q = q_ref[0, units[u]:units[u] + sub]
        for hh in range(2):
            _scores_to_scratch(_half_q(q, hh), k_refs, s_scr, 2 * (u % ns) + hh)

    def combine(u):
        linv = []
        for r in range(0, sub, DIFF_RB):
            e0, l0 = _exp_rows(s_scr[2 * (u % ns), r:r + DIFF_RB, :])
            e1, l1 = _exp_rows(s_scr[2 * (u % ns) + 1, r:r + DIFF_RB, :])
            a_scr[u % 2, r:r + DIFF_RB, :] = (e0 - e1 * (lam * l0 / l1)).astype(BF16)
            linv.append(1.0 / l0)
        return jnp.concatenate(linv, axis=0)

    for u in range(min(ns - 1, len(units))):
        scores(u)
    for u, r in enumerate(units):
        if u + ns - 1 < len(units):
            scores(u + ns - 1)
        linv = combine(u)
        o = _pv_from_scratch(a_scr, u % 2, v_refs) * linv
        o_ref[0, r:r + sub] = (_rms(o, g_ref[...]) * (1.0 - lambda_init)).astype(o_ref.dtype)


def _diff_attn_call(q, kvs, lam_params, subln_g, lambda_init):
    b, lq, d = q.shape
    tq = _row_tile(lq, 8 * ATT_SUB)
    sub = min(tq, ATT_SUB)
    nk = sum(k.shape[1] for k, _ in kvs)
    const = lambda bi, j, i: (0, 0)
    in_specs = [pl.BlockSpec((1, tq, LANES), lambda bi, j, i: (bi, i, j)),
                pl.BlockSpec(lam_params.shape, const), pl.BlockSpec(subln_g.shape, const)]
    args = [q, lam_params, subln_g]
    for k, v in kvs:
        spec = pl.BlockSpec((1, k.shape[1], LANES), lambda bi, j, i: (bi, 0, j))
        in_specs += [spec, spec]
        args += [k, v]
    return pl.pallas_call(
        functools.partial(_diff_attn_kernel, nsrc=len(kvs), lambda_init=lambda_init),
        out_shape=jax.ShapeDtypeStruct((b, lq, d), BF16),
        grid=(b, d // LANES, lq // tq),
        in_specs=in_specs,
        out_specs=pl.BlockSpec((1, tq, LANES), lambda bi, j, i: (bi, i, j)),
        scratch_shapes=[pltpu.VMEM((2 * DIFF_UNITS, sub, nk), F32), pltpu.VMEM((2, sub, nk), BF16)],
        compiler_params=_cparams(("arbitrary", "arbitrary", "arbitrary")),
        name="diff_attn",
    )(*args)


def _na_attn_t_kernel(q_ref, k_ref, v_ref, kc_ref, vc_ref, rpb_ref, o_ref, tt2_scr, vt_scr, vct_scr, s_scr, p_scr, *,
                      rows, kr, kcw, nq):
    w = GRID_W
    nwin = nq + kr
    ntab = 2 * kr
    nkw = nwin * w
    sub = nq * w
    nk = s_scr.shape[1]
    nsub = q_ref.shape[1] // sub
    i = pl.program_id(2)
    shift = int(math.log2(w))

    @pl.when((i == 0) & (pl.program_id(1) == 0))
    def _build_bias():
        cp_io = lax.broadcasted_iota(jnp.int32, (w, LANES), 0)
        c_io = lax.broadcasted_iota(jnp.int32, (w, LANES), 1)
        cs = jnp.clip(c_io - kcw // 2, 0, w - kcw)
        colok = (cp_io >= cs) & (cp_io < cs + kcw)
        neg = jnp.full((w, LANES), NEG, F32)
        for hh in range(2):
            r = pltpu.roll(rpb_ref[hh] * LOG2E, LANES - (kcw - 1), 1)
            ts = []
            for e in range(ntab - 1):
                t = jnp.broadcast_to(r[e:e + 1], (w, LANES))
                t = pltpu.roll(t, 0, 1, stride=1, stride_axis=0)
                ts.append(jnp.where(colok, t, NEG))
            for x in range(ntab):
                lo = ts[x] if x < ntab - 1 else neg
                hi = ts[x - 1] if x >= 1 else neg
                tt2_scr[hh, x] = jnp.where(c_io < w, lo, pltpu.roll(hi, w, 1))

    @pl.when(i == 0)
    def _transpose_values():
        for c in range(vt_scr.shape[0]):
            vt_scr[c] = v_ref[0, c * sub:(c + 1) * sub, :].astype(F32).T.astype(BF16)
        vct_scr[...] = vc_ref[0].astype(F32).T.astype(BF16)

    units = [(t, hh) for t in range(nsub) for hh in range(2)]
    ns = s_scr.shape[0]

    def window(t):
        r0 = (i * nsub + t) * nq
        return r0, jnp.clip(r0 - kr // 2, 0, rows - nwin)

    def scores(u):
        t, hh = units[u]
        _, ks = window(t)
        qh = _half_q(q_ref[0, t * sub:(t + 1) * sub], hh)
        s_scr[u % ns, :nkw, :] = _dot_nt(k_ref[0, pl.ds(pl.multiple_of(ks * w, sub), nkw), :], qh)
        s_scr[u % ns, nkw:, :] = _dot_nt(kc_ref[0], qh)

    def softmax(u):
        t, hh = units[u]
        r0, ks = window(t)
        q_row = r0 + jnp.right_shift(lax.broadcasted_iota(jnp.int32, (1, sub), 1), shift)
        rs = jnp.clip(q_row - kr // 2, 0, rows - kr)
        macc = None
        for j in range(nwin):
            key_row = ks + j
            ok = (key_row >= rs) & (key_row < rs + kr)
            d0 = key_row - r0 + kr - 1
            bias = jnp.concatenate([tt2_scr[hh, jnp.clip(d0 - 2 * g, 0, ntab - 1)] for g in range(sub // LANES)], axis=1)
            rows_j = slice(j * w, (j + 1) * w)
            s = jnp.where(ok, s_scr[u % ns, rows_j, :] + bias, NEG)
            s_scr[u % ns, rows_j, :] = s
            macc = s if macc is None else jnp.maximum(macc, s)
        for j in range(nkw, nk, w):
            macc = jnp.maximum(macc, s_scr[u % ns, j:j + w, :])
        m = jnp.max(macc, axis=0, keepdims=True)
        lacc = None
        for j in range(0, nk, KEY_SLAB):
            e = jnp.exp2(s_scr[u % ns, j:j + KEY_SLAB, :] - m)
            p_scr[u % 2, j:j + KEY_SLAB, :] = e.astype(BF16)
            lacc = e if lacc is None else lacc + e
        return 1.0 / jnp.sum(lacc, axis=0, keepdims=True)

    for u in range(min(ns - 1, len(units))):
        scores(u)
    outs = [None, None]
    for u, (t, hh) in enumerate(units):
        if u + ns - 1 < len(units):
            scores(u + ns - 1)
        linv = softmax(u)
        _, ks = window(t)
        head = slice(hh * HEAD_DIM, (hh + 1) * HEAD_DIM)
        ot = _dot(vct_scr[head, :], p_scr[u % 2, nkw:, :])
        for c in range(nwin // nq):
            ot = ot + _dot(vt_scr[ks // nq + c, head, :], p_scr[u % 2, c * sub:(c + 1) * sub, :])
        outs[hh] = ot * linv
        if hh == 1:
            o_ref[0, t * sub:(t + 1) * sub] = jnp.concatenate(outs, axis=0).T.astype(o_ref.dtype)


def _na_attn_t_call(q, k, v, kc, vc, rpb):
    b, s_len, d = q.shape
    c_len = kc.shape[1]
    nh, ndr, ndc = rpb.shape
    kr, kcw = (ndr + 1) // 2, (ndc + 1) // 2
    rows = s_len // GRID_W
    nq = 4
    assert GRID_W == HEAD_DIM and rows % nq == 0 and rows >= nq + kr and (nq + kr) % nq == 0 and (kr // 2) % nq == 0
    assert ndc <= LANES and 2 * kr <= 2 * SUBLANES and nq * GRID_W == 2 * LANES
    rpb_p = jnp.full((nh, 2 * SUBLANES, LANES), NEG, F32).at[:, :ndr, :ndc].set(rpb[:, :, ::-1])
    sub = nq * GRID_W
    tq = _row_tile(s_len, 8 * sub)
    nk = (nq + kr) * GRID_W + c_len
    full = lambda n: pl.BlockSpec((1, n, LANES), lambda j, bi, i: (bi, 0, j))
    return pl.pallas_call(
        functools.partial(_na_attn_t_kernel, rows=rows, kr=kr, kcw=kcw, nq=nq),
        out_shape=jax.ShapeDtypeStruct((b, s_len, d), BF16),
        grid=(d // LANES, b, s_len // tq),
        in_specs=[pl.BlockSpec((1, tq, LANES), lambda j, bi, i: (bi, i, j)),
                  full(s_len), full(s_len), full(c_len), full(c_len),
                  pl.BlockSpec((2, 2 * SUBLANES, LANES), lambda j, bi, i: (j, 0, 0))],
        out_specs=pl.BlockSpec((1, tq, LANES), lambda j, bi, i: (bi, i, j)),
        scratch_shapes=[pltpu.VMEM((2, 2 * kr, GRID_W, LANES), F32), pltpu.VMEM((s_len // sub, LANES, sub), BF16),
                        pltpu.VMEM((LANES, c_len), BF16), pltpu.VMEM((3, nk, sub), F32),
                        pltpu.VMEM((2, nk, sub), BF16)],
        compiler_params=_cparams(("arbitrary", "arbitrary", "arbitrary")),
        name="na_attn_t",
    )(q, k, v, kc, vc, rpb_p)


def _outproj_kernel(a_ref, x_ref, mod_ref, g_ref, w_ref, o_ref, z_ref):
    tm = x_ref.shape[1]
    sub = _split_rows(tm)
    for r in range(0, tm, sub):
        rows = slice(r, r + sub)
        x = x_ref[0, rows] + mod_ref[0, 2:3] * _dot(a_ref[0, rows], w_ref[...])
        o_ref[0, rows] = x
        z_ref[0, rows] = _normmod(x, g_ref[...], mod_ref[0, 3:4], mod_ref[0, 4:5]).astype(z_ref.dtype)


def _outproj_call(a, x, mods_l, mod_row, g, w):
    b, n, d = x.shape
    k = a.shape[-1]
    tm = _row_tile(n, OUTPROJ_TM)
    row = lambda w_: pl.BlockSpec((1, tm, w_), lambda bi, i: (bi, i, 0))
    return pl.pallas_call(
        _outproj_kernel,
        out_shape=(jax.ShapeDtypeStruct((b, n, d), F32), jax.ShapeDtypeStruct((b, n, d), BF16)),
        grid=(b, n // tm),
        in_specs=[row(k), row(d), _mod_spec(d, mod_row),
                  pl.BlockSpec((1, d), lambda bi, i: (0, 0)),
                  pl.BlockSpec((k, d), lambda bi, i: (0, 0))],
        out_specs=(row(d), row(d)),
        compiler_params=_cparams(("arbitrary", "arbitrary")),
        name="outproj",
    )(a, x, mods_l, g, w)


def _ffn_kernel(x_ref, z_ref, zp_ref, zn_ref, mod_ref, wu_ref, cw_ref, wd_ref, *rest, fc, final):
    if final:
        fg_ref, o_ref, z_scr, u_scr, h_scr = rest
    else:
        o_ref, z_scr, u_scr, h_scr = rest
    i = pl.program_id(1)
    nt = pl.num_programs(1)
    tm, d = x_ref.shape[1], x_ref.shape[2]
    hb = FFN_HALO
    nch = wd_ref.shape[0] // fc
    z_scr[hb:hb + tm] = z_ref[0]
    z_scr[0:hb] = jnp.where(i > 0, zp_ref[0], jnp.zeros_like(zp_ref[0]))
    z_scr[hb + tm:] = jnp.where(i < nt - 1, zn_ref[0], jnp.zeros_like(zn_ref[0]))

    f = wd_ref.shape[0]
    nu = u_scr.shape[0]
    gate_cols = lambda j: slice(j * fc, (j + 1) * fc)
    value_cols = lambda j: slice(f + j * fc, f + (j + 1) * fc)

    def up_proj(j):
        z = z_scr[...]
        u_scr[j % nu, :, :fc] = _dot(z, wu_ref[:, gate_cols(j)])
        u_scr[j % nu, :, fc:] = _dot(z, wu_ref[:, value_cols(j)])

    def conv(j, cols, lanes, r0, sub):
        cw = cw_ref[:, cols]
        cv = cw[3:4] + cw[0:1] * u_scr[j % nu, r0 - 1:r0 - 1 + sub, lanes]
        cv = cv + cw[1:2] * u_scr[j % nu, r0:r0 + sub, lanes]
        return cv + cw[2:3] * u_scr[j % nu, r0 + 1:r0 + 1 + sub, lanes]

    def conv_act(j):
        sub = min(FFN_SUB, tm)
        for sb in range(tm // sub):
            r0 = hb + sb * sub
            hg = 0.5 * conv(j, gate_cols(j), slice(0, fc), r0, sub)
            h = (hg + hg * jnp.tanh(hg)) * conv(j, value_cols(j), slice(fc, 2 * fc), r0, sub)
            h_scr[sb * sub:(sb + 1) * sub, j * fc:(j + 1) * fc] = h.astype(BF16)

    for j in range(min(nu - 1, nch)):
        up_proj(j)
    for j in range(nch - 1):
        if j + nu - 1 < nch:
            up_proj(j + nu - 1)
        conv_act(j)
    split = (nch - 1) * fc
    acc = _dot(h_scr[:, :split], wd_ref[:split, :])
    conv_act(nch - 1)
    acc = acc + _dot(h_scr[:, split:], wd_ref[split:, :])
    out = x_ref[0] + mod_ref[0, 5:6] * acc
    if final:
        out = _rms(out, fg_ref[...])
    o_ref[0] = out


def _ffn_pack_params(w_up, conv_w, conv_b, w_down, fc):
    f2 = w_up.shape[1]
    assert (f2 // 2) % fc == 0
    cw = jnp.concatenate([conv_w, conv_b[None], jnp.zeros((SUBLANES - 1 - conv_w.shape[0], f2), F32)], axis=0)
    return w_up.astype(BF16), cw, w_down.astype(BF16)


def _ffn_call(x, z, mods_l, mod_row, packed, layer, fc, final_g=None):
    b, n, d = x.shape
    wu, cw, wd = packed
    nch, fc2 = wd.shape[1] // fc, 2 * fc
    tm = _row_tile(n, FFN_TM)
    hb = FFN_HALO
    nbh = tm // hb
    once = dict(pipeline_mode=pl.Buffered(1))
    layer_spec = lambda a, **kw: pl.BlockSpec((None,) + a.shape[1:], lambda bi, i: (layer,) + (0,) * (a.ndim - 1), **kw)
    in_specs = [
        pl.BlockSpec((1, tm, d), lambda bi, i: (bi, i, 0)),
        pl.BlockSpec((1, tm, d), lambda bi, i: (bi, i, 0)),
        pl.BlockSpec((1, hb, d), lambda bi, i: (bi, jnp.maximum(i * nbh - 1, 0), 0)),
        pl.BlockSpec((1, hb, d), lambda bi, i: (bi, jnp.minimum((i + 1) * nbh, n // hb - 1), 0)),
        _mod_spec(d, mod_row),
        layer_spec(wu, **once),
        layer_spec(cw),
        layer_spec(wd, **once),
    ]
    args = [x, z, z, z, mods_l, wu, cw, wd]
    if final_g is not None:
        in_specs.append(pl.BlockSpec((1, d), lambda bi, i: (0, 0)))
        args.append(final_g)
    return pl.pallas_call(
        functools.partial(_ffn_kernel, fc=fc2 // 2, final=final_g is not None),
        out_shape=jax.ShapeDtypeStruct((b, n, d), F32),
        grid=(b, n // tm),
        in_specs=in_specs,
        out_specs=pl.BlockSpec((1, tm, d), lambda bi, i: (bi, i, 0)),
        scratch_shapes=[pltpu.VMEM((tm + 2 * hb, d), BF16), pltpu.VMEM((FFN_SLOTS, tm + 2 * hb, fc2), F32),
                        pltpu.VMEM((tm, nch * fc2 // 2), BF16)],
        compiler_params=_cparams(("arbitrary", "arbitrary")),
        name="ffn",
    )(*args)


def _rg_mixer(xl, xc, mods_l, ctx_row, g, w_in, conv_w, conv_b, wa, ba, wx, bx, lam, need_ctx):
    w = w_in.astype(BF16)
    g_l, xr_l = _rg_proj_call(xl, mods_l, None, g, w)
    g_c, xr_c = _rg_proj_call(xc, mods_l, ctx_row, g, w)
    cw, wbd, gp = _rg_pack_params(conv_w, conv_b, wa, ba, wx, bx, lam)
    m_c, m_l = _rg_scan_call(xr_c, g_c, xr_l, g_l, cw, wbd, gp)
    return m_l, (m_c if need_ctx else None)


def _na_mixer(xl, xc, mods_l, ctx_row, g, w_in, rpb, need_ctx):
    d = xl.shape[-1]
    w = w_in.astype(BF16)
    ql, kl, vl = _qkv_proj_call(xl, mods_l, None, g, w, d)
    qc, kc, vc = _qkv_proj_call(xc, mods_l, ctx_row, g, w, d)
    o_l = _na_attn_t_call(ql, kl, vl, kc, vc, rpb)
    o_c = _plain_attn_call(qc, [(kc, vc)], LANES) if need_ctx else None
    return o_l, o_c


def _gqa_mixer(xl, xc, mods_l, ctx_row, g, w_in, q_norm, k_norm, need_ctx):
    d = xl.shape[-1]
    nh = d // HEAD_DIM
    nkv = (w_in.shape[1] // HEAD_DIM - nh) // 2
    group = nh // nkv
    assert group % 2 == 0 and q_norm.shape[0] == HEAD_DIM
    dup = lambda wk: jnp.concatenate([wk.reshape(d, nkv, 1, HEAD_DIM)] * 2, axis=2).reshape(d, 2 * nkv * HEAD_DIM)
    nk = 2 * nkv * HEAD_DIM
    w = jnp.concatenate([w_in[:, :d], dup(w_in[:, d:d + nkv * HEAD_DIM]), dup(w_in[:, d + nkv * HEAD_DIM:])],
                        axis=1).astype(BF16)
    nseg = (d + nk) // HEAD_DIM
    assert nseg <= LANES
    seg = jnp.arange(d + nk) // HEAD_DIM
    ind = (seg[:, None] == jnp.arange(LANES)[None, :]).astype(BF16)
    gain = jnp.concatenate([jnp.tile(q_norm, nh), jnp.tile(k_norm, 2 * nkv)])[None]
    qkn = (gain, ind, ind.T)
    ql, kl, vl = _qkv_proj_call(xl, mods_l, None, g, w, nk, rope=_rope_tables(xl.shape[1]), qknorm=qkn)
    qc, kc, vc = _qkv_proj_call(xc, mods_l, ctx_row, g, w, nk, qknorm=qkn)
    qw = group * HEAD_DIM
    o_l = _plain_attn_t_call(ql, [(kl, vl), (kc, vc)], qw)
    o_c = _plain_attn_call(qc, [(kc, vc)], qw) if need_ctx else None
    return o_l, o_c


def _diff_mixer(xl, xc, mods_l, ctx_row, g, w_in, lq1, lk1, lq2, lk2, subln_g, lambda_init, need_ctx):
    d = xl.shape[-1]
    assert lq1.shape[0] == HEAD_DIM and subln_g.shape[0] == LANES
    w = w_in.astype(BF16)
    ql, kl, vl = _qkv_proj_call(xl, mods_l, None, g, w, d, rope=_rope_tables(xl.shape[1]))
    qc, kc, vc = _qkv_proj_call(xc, mods_l, ctx_row, g, w, d)
    lam_params = jnp.concatenate([jnp.stack([lq1, lk1, lq2, lk2]), jnp.zeros((SUBLANES - 4, HEAD_DIM), F32)])
    o_l = _diff_attn_call(ql, [(kl, vl), (kc, vc)], lam_params, subln_g[None], lambda_init)
    o_c = _diff_attn_call(qc, [(kc, vc)], lam_params, subln_g[None], lambda_init) if need_ctx else None
    return o_l, o_c


def kernel(x, c, ctx, c_ctx, mod_w, mod_b, norm1_g, norm2_g, rg_w_in, rg_conv_w, rg_conv_b, rg_wa, rg_ba, rg_wx, rg_bx, rg_lam, rg_w_out, na_w_in, na_rpb, na_w_out, gqa_w_in, gqa_q_norm, gqa_k_norm, gqa_w_out, diff_w_in, diff_lq1, diff_lk1, diff_lq2, diff_lk2, diff_subln_g, diff_w_out, ffn_w_up, ffn_conv_w, ffn_conv_b, ffn_w_down, final_g):
    bsz, _, d = x.shape
    depth = mod_w.shape[0]
    assert d % LANES == 0
    ctx_row = bsz
    nrow = -(-(bsz + 1) // SUBLANES) * SUBLANES
    cc = jnp.concatenate([c, c_ctx[None], jnp.zeros((nrow - bsz - 1, d), F32)], axis=0)
    mods = _mods_call(cc, mod_w, mod_b).reshape(depth, nrow, 6, d)
    f = ffn_w_down.shape[1]
    fc = next(t for t in (256, 128) if f % t == 0)
    packed = jax.vmap(functools.partial(_ffn_pack_params, fc=fc))(ffn_w_up, ffn_conv_w, ffn_conv_b, ffn_w_down)
    xl, xc = x, ctx
    for l in range(depth):
        m, j = l % N_MIXERS, l // N_MIXERS
        need_ctx = l < depth - 1
        mods_l = mods[l]
        g1, g2 = norm1_g[l][None], norm2_g[l][None]
        if m == 0:
            a_l, a_c = _rg_mixer(xl, xc, mods_l, ctx_row, g1, rg_w_in[j], rg_conv_w[j], rg_conv_b[j], rg_wa[j],
                                 rg_ba[j], rg_wx[j], rg_bx[j], rg_lam[j], need_ctx)
            w_out = rg_w_out[j]
        elif m == 1:
            a_l, a_c = _na_mixer(xl, xc, mods_l, ctx_row, g1, na_w_in[j], na_rpb[j], need_ctx)
            w_out = na_w_out[j]
        elif m == 2:
            a_l, a_c = _gqa_mixer(xl, xc, mods_l, ctx_row, g1, gqa_w_in[j], gqa_q_norm[j], gqa_k_norm[j], need_ctx)
            w_out = gqa_w_out[j]
        else:
            lambda_init = 0.8 - 0.6 * math.exp(-0.3 * l)
            a_l, a_c = _diff_mixer(xl, xc, mods_l, ctx_row, g1, diff_w_in[j], diff_lq1[j], diff_lk1[j],
                                   diff_lq2[j], diff_lk2[j], diff_subln_g[j], lambda_init, need_ctx)
            w_out = diff_w_out[j]
        w_out = w_out.astype(BF16)
        last = l == depth - 1
        xl, zl = _outproj_call(a_l, xl, mods_l, None, g2, w_out)
        xl = _ffn_call(xl, zl, mods_l, None, packed, l, fc, final_g[None] if last else None)
        if need_ctx:
            xc, zc = _outproj_call(a_c, xc, mods_l, ctx_row, g2, w_out)
            xc = _ffn_call(xc, zc, mods_l, ctx_row, packed, l, fc)
    return xl
```

```python
import functools
import math

import jax
import jax.numpy as jnp
from jax import lax
from jax.experimental import pallas as pl
from jax.experimental.pallas import tpu as pltpu

F32 = jnp.float32
BF16 = jnp.bfloat16
EPS = 1e-6
ROPE_THETA = 10000.0
GRID_W = 64
RG_C = 8.0
RG_CONV_LEFT = 2
N_MIXERS = 4
HEAD_DIM = 64
LOG2E = 1.4426950408889634
Q_SCALE = HEAD_DIM ** -0.5 * LOG2E
LANES = 128
SUBLANES = 8
NEG = -1e30
FFN_HALO = 16
FFN_SUB = 512
FFN_TM = 512
FFN_SLOTS = 4
FFN_DOWN_GROUP = 10
OUTPROJ_TM = 1024
RG_TL = 1024
PROJ_TM = 1024
PROJ_SPLIT = 4
PROJ_MIN_ROWS = 256
ATT_SUB = 256
ATT_RB = 32
DIFF_RB = 16
DIFF_UNITS = 3
KEY_SLAB = 32
VMEM_LIMIT_BYTES = 56 * 1024 * 1024

_NT = (((1,), (1,)), ((), ()))


def _cparams(sem):
    return pltpu.CompilerParams(dimension_semantics=sem, vmem_limit_bytes=VMEM_LIMIT_BYTES)


def _dot(a, b):
    return jnp.dot(a, b, preferred_element_type=F32)


def _dot_nt(a, b):
    return lax.dot_general(a, b, _NT, preferred_element_type=F32)


def _silu(x):
    return x / (1.0 + jnp.exp(-x))


def _gelu_tanh(x):
    cdf = 0.5 * (1.0 + jnp.tanh(math.sqrt(2.0 / math.pi) * (x + 0.044715 * (x * x * x))))
    return x * cdf


def _rms(x, g):
    y = x * lax.rsqrt(jnp.mean(x * x, axis=-1, keepdims=True) + EPS)
    return y * g


def _normmod(x, g, shift, scale):
    return _rms(x, g) * (1.0 + scale) + shift


def _mods_kernel(cc_ref, w_ref, b_ref, o_ref):
    a = _silu(cc_ref[...]).astype(BF16)
    o_ref[0] = _dot(a, w_ref[0].astype(BF16)) + b_ref[0]


def _mods_call(cc, mod_w, mod_b):
    depth, d, n = mod_w.shape
    r = cc.shape[0]
    tn = n // 4
    return pl.pallas_call(
        _mods_kernel,
        out_shape=jax.ShapeDtypeStruct((depth, r, n), F32),
        grid=(depth, n // tn),
        in_specs=[
            pl.BlockSpec((r, d), lambda l, j: (0, 0)),
            pl.BlockSpec((1, d, tn), lambda l, j: (l, 0, j)),
            pl.BlockSpec((1, 1, tn), lambda l, j: (l, 0, j)),
        ],
        out_specs=pl.BlockSpec((1, r, tn), lambda l, j: (l, 0, j)),
        compiler_params=_cparams(("arbitrary", "arbitrary")),
        name="mods",
    )(cc, mod_w, mod_b.reshape(depth, 1, n))


def _mod_spec(d, row):
    if row is None:
        return pl.BlockSpec((1, 6, d), lambda b, *_: (b, 0, 0))
    return pl.BlockSpec((1, 6, d), lambda b, *_: (row, 0, 0))


def _split_rows(tm):
    sub = tm // PROJ_SPLIT
    return sub if (tm % PROJ_SPLIT == 0 and sub >= PROJ_MIN_ROWS) else tm


def _row_tile(n, pref):
    t = min(n, pref)
    assert n % t == 0
    return t


def _rope_tables(n_tok):
    t = jnp.arange(n_tok)
    row = (t // GRID_W).astype(F32)
    col = (t % GRID_W).astype(F32)
    n = HEAD_DIM // 4
    inv = ROPE_THETA ** (-jnp.arange(n, dtype=F32) / n)
    ang = jnp.concatenate([row[:, None] * inv, col[:, None] * inv], axis=-1)
    cos, sin = jnp.cos(ang), jnp.sin(ang)
    reps = LANES // HEAD_DIM
    cos_t = jnp.tile(jnp.concatenate([cos, cos], axis=-1), (1, reps))
    sin_t = jnp.tile(jnp.concatenate([-sin, sin], axis=-1), (1, reps))
    return cos_t, sin_t


def _swap_halves(x):
    lane = lax.broadcasted_iota(jnp.int32, x.shape, 1)
    lo = (lane & (HEAD_DIM // 2)) == 0
    return jnp.where(lo, pltpu.roll(x, LANES - HEAD_DIM // 2, 1), pltpu.roll(x, HEAD_DIM // 2, 1))


def _rope_cols(u, cos_t, sin_t):
    outs = []
    for c in range(u.shape[1] // LANES):
        xc = u[:, c * LANES:(c + 1) * LANES]
        outs.append(xc * cos_t + _swap_halves(xc) * sin_t)
    return jnp.concatenate(outs, axis=1)


def _rg_proj_kernel(x_ref, mod_ref, g_ref, w_ref, go_ref, xo_ref, *, dr):
    tm = x_ref.shape[1]
    sub = _split_rows(tm)
    for r in range(0, tm, sub):
        rows = slice(r, r + sub)
        z = _normmod(x_ref[0, rows], g_ref[...], mod_ref[0, 0:1], mod_ref[0, 1:2]).astype(BF16)
        u = _dot(z, w_ref[...])
        go_ref[0, rows] = _gelu_tanh(u[:, :dr])
        xo_ref[0, rows] = u[:, dr:]


def _rg_proj_call(x, mods_l, mod_row, g, w):
    b, n, d = x.shape
    dr = w.shape[1] // 2
    tm = _row_tile(n, PROJ_TM)
    return pl.pallas_call(
        functools.partial(_rg_proj_kernel, dr=dr),
        out_shape=(jax.ShapeDtypeStruct((b, n, dr), F32), jax.ShapeDtypeStruct((b, n, dr), F32)),
        grid=(b, n // tm),
        in_specs=[
            pl.BlockSpec((1, tm, d), lambda bi, i: (bi, i, 0)),
            _mod_spec(d, mod_row),
            pl.BlockSpec((1, d), lambda bi, i: (0, 0)),
            pl.BlockSpec((d, 2 * dr), lambda bi, i: (0, 0)),
        ],
        out_specs=(pl.BlockSpec((1, tm, dr), lambda bi, i: (bi, i, 0)),
                   pl.BlockSpec((1, tm, dr), lambda bi, i: (bi, i, 0))),
        compiler_params=_cparams(("arbitrary", "arbitrary")),
        name="rg_proj",
    )(x, mods_l, g, w)


def _qkv_proj_kernel(x_ref, mod_ref, g_ref, w_ref, *rest, d, rope, qknorm):
    rest = list(rest)
    if rope:
        cos_ref, sin_ref = rest.pop(0), rest.pop(0)
    if qknorm:
        gain_ref, ind_ref, indt_ref = rest.pop(0), rest.pop(0), rest.pop(0)
    q_ref, k_ref, v_ref = rest
    nk = k_ref.shape[-1]
    tm = x_ref.shape[1]
    sub = _split_rows(tm)
    for r in range(0, tm, sub):
        rows = slice(r, r + sub)
        z = _normmod(x_ref[0, rows], g_ref[...], mod_ref[0, 0:1], mod_ref[0, 1:2]).astype(BF16)
        u = _dot(z, w_ref[...])
        qk = u[:, :d + nk]
        if qknorm:
            x2 = qk * qk
            hi = x2.astype(BF16)
            lo = (x2 - hi.astype(F32)).astype(BF16)
            ssq = _dot(hi, ind_ref[...]) + _dot(lo, ind_ref[...])
            rs = lax.rsqrt(ssq * (1.0 / HEAD_DIM) + EPS)
            rhi = rs.astype(BF16)
            rlo = (rs - rhi.astype(F32)).astype(BF16)
            rb = _dot(rhi, indt_ref[...]) + _dot(rlo, indt_ref[...])
            qk = (qk * rb) * gain_ref[...]
        if rope:
            qk = _rope_cols(qk, cos_ref[rows], sin_ref[rows])
        q_ref[0, rows] = (qk[:, :d] * Q_SCALE).astype(q_ref.dtype)
        k_ref[0, rows] = qk[:, d:].astype(k_ref.dtype)
        v_ref[0, rows] = u[:, d + nk:].astype(v_ref.dtype)


def _qkv_proj_call(x, mods_l, mod_row, g, w, nk, *, rope=None, qknorm=None):
    b, n, d = x.shape
    nv = w.shape[1] - d - nk
    tm = _row_tile(n, PROJ_TM)
    const = lambda bi, i: (0, 0)
    in_specs = [
        pl.BlockSpec((1, tm, d), lambda bi, i: (bi, i, 0)),
        _mod_spec(d, mod_row),
        pl.BlockSpec((1, d), const),
        pl.BlockSpec(w.shape, const),
    ]
    args = [x, mods_l, g, w]
    if rope is not None:
        in_specs += [pl.BlockSpec((tm, LANES), lambda bi, i: (i, 0))] * 2
        args += list(rope)
    if qknorm is not None:
        in_specs += [pl.BlockSpec(a.shape, const) for a in qknorm]
        args += list(qknorm)
    out_spec = lambda w_: pl.BlockSpec((1, tm, w_), lambda bi, i: (bi, i, 0))
    return pl.pallas_call(
        functools.partial(_qkv_proj_kernel, d=d, rope=rope is not None, qknorm=qknorm is not None),
        out_shape=(jax.ShapeDtypeStruct((b, n, d), BF16), jax.ShapeDtypeStruct((b, n, nk), BF16),
                   jax.ShapeDtypeStruct((b, n, nv), BF16)),
        grid=(b, n // tm),
        in_specs=in_specs,
        out_specs=(out_spec(d), out_spec(nk), out_spec(nv)),
        compiler_params=_cparams(("arbitrary", "arbitrary")),
        name="qkv_proj",
    )(*args)


def _rg_conv(xs_scr, x, prev, nxt, cw):
    t = x.shape[0]
    h = SUBLANES
    xs_scr[0:h] = prev
    xs_scr[h:h + t] = x
    xs_scr[h + t:2 * h + t] = nxt
    y = cw[4:5] + cw[0:1] * xs_scr[h - 2:h - 2 + t]
    y = y + cw[1:2] * xs_scr[h - 1:h - 1 + t]
    y = y + cw[2:3] * x
    y = y + cw[3:4] * xs_scr[h + 1:h + 1 + t]
    return y


def _sigmoid(x):
    return 0.5 * jnp.tanh(0.5 * x) + 0.5


def _rg_gates(xc, w, ba, bx, lam):
    t, gw = xc.shape
    nl = -lam
    softplus = jnp.maximum(nl, 0.0) + jnp.log1p(jnp.exp(-jnp.abs(nl)))
    sub = _split_rows(t)
    a_parts, b_parts = [], []
    for r0 in range(0, t, sub):
        x = xc[r0:r0 + sub]
        y = _dot(x.astype(BF16), w)
        r = _sigmoid(y[:, :gw] + ba)
        i = _sigmoid(y[:, gw:] + bx)
        log_a = r * (-RG_C * softplus)
        a = jnp.exp(log_a)
        one_minus_a2 = -jnp.tanh(log_a) * (a * a + 1.0)
        root = jnp.where(one_minus_a2 > 0.0, one_minus_a2 * lax.rsqrt(one_minus_a2), 0.0)
        a_parts.append(a)
        b_parts.append(root * (i * x))
    return jnp.concatenate(a_parts, axis=0), jnp.concatenate(b_parts, axis=0)


def _rg_scan_tile(a, b, carry, reverse):
    t, gw = a.shape
    nck = t // SUBLANES
    a = a.reshape(nck, SUBLANES, gw)
    b = b.reshape(nck, SUBLANES, gw)
    rmod = lax.broadcasted_iota(jnp.int32, (1, SUBLANES, 1), 1)
    for s in (1, 2, 4):
        ok = (rmod < SUBLANES - s) if reverse else (rmod >= s)
        shift = SUBLANES - s if reverse else s
        ash, bsh = pltpu.roll(a, shift, 1), pltpu.roll(b, shift, 1)
        b = a * jnp.where(ok, bsh, 0.0) + b
        a = a * jnp.where(ok, ash, 1.0)
    hs = [None] * nck
    for ck in (range(nck - 1, -1, -1) if reverse else range(nck)):
        hc = b[ck] + a[ck] * carry
        carry = hc[0:1] if reverse else hc[SUBLANES - 1:SUBLANES]
        hs[ck] = hc
    return jnp.concatenate(hs, axis=0), carry


def _rg_scan_kernel(xc_ref, gc_ref, xl_ref, xp_ref, xn_ref, gl_ref, cw_ref, w_ref, gp_ref,
                    mc_ref, ml_ref, hf_scr, xv_scr, car_scr, xs_scr, *, nt):
    s = pl.program_id(2)
    gw = xl_ref.shape[-1]
    tl = xl_ref.shape[1]
    cw = cw_ref[0]
    gp = gp_ref[0]

    def gates(xconv, d):
        w = w_ref[0, :, d * 2 * gw:(d + 1) * 2 * gw]
        return _rg_gates(xconv, w, gp[3 * d:3 * d + 1], gp[3 * d + 1:3 * d + 2], gp[3 * d + 2:3 * d + 3])

    @pl.when(s == 0)
    def _ctx():
        zeros8 = jnp.zeros((SUBLANES, gw), F32)
        xconv = _rg_conv(xs_scr, xc_ref[0], zeros8, zeros8, cw)
        zero = jnp.zeros((1, gw), F32)
        a0, b0 = gates(xconv, 0)
        hf, cf = _rg_scan_tile(a0, b0, zero, False)
        a1, b1 = gates(xconv, 1)
        hb, cb = _rg_scan_tile(a1, b1, zero, True)
        car_scr[0:1] = cf
        car_scr[1:2] = cb
        mc_ref[0] = (gc_ref[0] * (hf + hb)).astype(mc_ref.dtype)

    def lat_conv(t):
        prev = xp_ref[0] * (t > 0).astype(F32)
        nxt = xn_ref[0] * (t < nt - 1).astype(F32)
        return _rg_conv(xs_scr, xl_ref[0], prev, nxt, cw)

    @pl.when((s >= 1) & (s <= nt))
    def _fwd():
        t = s - 1
        xconv = lat_conv(t)
        xv_scr[pl.ds(pl.multiple_of(t * tl, tl), tl), :] = xconv
        a0, b0 = gates(xconv, 0)
        h, c = _rg_scan_tile(a0, b0, car_scr[0:1], False)
        car_scr[0:1] = c
        hf_scr[pl.ds(pl.multiple_of(t * tl, tl), tl), :] = h

    @pl.when(s > nt)
    def _bwd():
        t = 2 * nt - s
        a1, b1 = gates(xv_scr[pl.ds(pl.multiple_of(t * tl, tl), tl), :], 1)
        h, c = _rg_scan_tile(a1, b1, car_scr[1:2], True)
        car_scr[1:2] = c
        hf = hf_scr[pl.ds(pl.multiple_of(t * tl, tl), tl), :]
        ml_ref[0] = (gl_ref[0] * (hf + h)).astype(ml_ref.dtype)


def _rg_scan_call(xr_c, g_c, xr_l, g_l, cw, wbd, gp):
    b, s_len, dr = xr_l.shape
    c_len = xr_c.shape[1]
    ng, gw = wbd.shape[0], wbd.shape[1]
    tl = _row_tile(s_len, RG_TL)
    nt = s_len // tl
    nb8 = tl // SUBLANES

    def tile(s):
        return jnp.clip(s - 1, 0, nt - 1)

    def tile_bwd(s):
        return jnp.where(s <= nt, nt - 1, 2 * nt - s)

    return pl.pallas_call(
        functools.partial(_rg_scan_kernel, nt=nt),
        out_shape=(jax.ShapeDtypeStruct((b, c_len, dr), BF16), jax.ShapeDtypeStruct((b, s_len, dr), BF16)),
        grid=(b, ng, 2 * nt + 1),
        in_specs=[
            pl.BlockSpec((1, c_len, gw), lambda bi, h, s: (bi, 0, h)),
            pl.BlockSpec((1, c_len, gw), lambda bi, h, s: (bi, 0, h)),
            pl.BlockSpec((1, tl, gw), lambda bi, h, s: (bi, tile(s), h)),
            pl.BlockSpec((1, SUBLANES, gw), lambda bi, h, s: (bi, jnp.maximum(tile(s) * nb8 - 1, 0), h)),
            pl.BlockSpec((1, SUBLANES, gw),
                         lambda bi, h, s: (bi, jnp.minimum((tile(s) + 1) * nb8, s_len // SUBLANES - 1), h)),
            pl.BlockSpec((1, tl, gw), lambda bi, h, s: (bi, tile_bwd(s), h)),
            pl.BlockSpec((1, SUBLANES, gw), lambda bi, h, s: (h, 0, 0)),
            pl.BlockSpec((1, gw, 4 * gw), lambda bi, h, s: (h, 0, 0)),
            pl.BlockSpec((1, SUBLANES, gw), lambda bi, h, s: (h, 0, 0)),
        ],
        out_specs=(pl.BlockSpec((1, c_len, gw), lambda bi, h, s: (bi, 0, h)),
                   pl.BlockSpec((1, tl, gw), lambda bi, h, s: (bi, tile_bwd(s), h))),
        scratch_shapes=[pltpu.VMEM((s_len, gw), F32), pltpu.VMEM((s_len, gw), F32), pltpu.VMEM((SUBLANES, gw), F32),
                        pltpu.VMEM((max(tl, c_len) + 2 * SUBLANES, gw), F32)],
        compiler_params=_cparams(("arbitrary", "arbitrary", "arbitrary")),
        name="rg_scan",
    )(xr_c, g_c, xr_l, xr_l, xr_l, g_l, cw, wbd, gp)


def _rg_pack_params(conv_w, conv_b, wa, ba, wx, bx, lam):
    nblk, bw = wa.shape[1], wa.shape[2]
    dr = nblk * bw
    per = next(p for p in range(1, nblk + 1) if nblk % p == 0 and (p * bw) % LANES == 0)
    ng, gw = nblk // per, per * bw

    def dense(w):
        w = w.reshape(ng, per, bw, bw)
        eye = jnp.eye(per, dtype=w.dtype)
        return jnp.einsum('gpjk,pq->gpjqk', w, eye).reshape(ng, gw, gw)

    wbd = jnp.concatenate([dense(wa[0]), dense(wx[0]), dense(wa[1]), dense(wx[1])], axis=-1).astype(BF16)
    cw = jnp.concatenate([conv_w, conv_b[None], jnp.zeros((SUBLANES - 1 - conv_w.shape[0], dr), F32)], axis=0)
    cw = cw.reshape(SUBLANES, ng, gw).transpose(1, 0, 2)
    gp = jnp.stack([ba[0], bx[0], lam[0], ba[1], bx[1], lam[1], jnp.zeros_like(lam[0]), jnp.zeros_like(lam[0])])
    gp = gp.reshape(SUBLANES, ng, gw).transpose(1, 0, 2)
    return cw, wbd, gp


def _pair_masks(shape):
    lane = lax.broadcasted_iota(jnp.int32, shape, 1)
    return lane < HEAD_DIM


def _half_q(q, hh):
    first = _pair_masks(q.shape)
    return jnp.where(first if hh == 0 else jnp.logical_not(first), q, jnp.zeros_like(q))


def _scores_to_scratch(qh, k_refs, s_scr, slot):
    off = 0
    for k_ref in k_refs:
        n = k_ref.shape[1]
        s_scr[slot, :, off:off + n] = _dot_nt(qh, k_ref[0])
        off += n


def _exp_rows(s):
    e = jnp.exp2(s - s.max(axis=-1, keepdims=True))
    return e, e.sum(axis=-1, keepdims=True)


def _softmax_to_scratch(s_scr, p_scr, slot, rows):
    linv = []
    for r in range(0, rows, ATT_RB):
        e, l = _exp_rows(s_scr[slot, r:r + ATT_RB, :])
        p_scr[slot, r:r + ATT_RB, :] = e.astype(BF16)
        linv.append(1.0 / l)
    return jnp.concatenate(linv, axis=0)


def _pv_from_scratch(p_scr, slot, v_refs):
    off, o = 0, None
    for v_ref in v_refs:
        n = v_ref.shape[1]
        t = _dot(p_scr[slot, :, off:off + n], v_ref[0])
        o = t if o is None else o + t
        off += n
    return o


def _plain_attn_kernel(q_ref, *refs, nsrc):
    k_refs, v_refs = refs[0:2 * nsrc:2], refs[1:2 * nsrc:2]
    o_ref, s_scr, p_scr = refs[2 * nsrc:]
    tq, qw = q_ref.shape[1], q_ref.shape[2]
    sub = s_scr.shape[1]
    units = [(r, c, hh) for r in range(0, tq, sub) for c in range(0, qw, LANES) for hh in range(2)]

    def scores(u):
        r, c, hh = units[u]
        _scores_to_scratch(_half_q(q_ref[0, r:r + sub, c:c + LANES], hh), k_refs, s_scr, u % 2)

    scores(0)
    outs = [None, None]
    for u, (r, c, hh) in enumerate(units):
        if u + 1 < len(units):
            scores(u + 1)
        linv = _softmax_to_scratch(s_scr, p_scr, u % 2, sub)
        outs[hh] = _pv_from_scratch(p_scr, u % 2, v_refs) * linv
        if hh == 1:
            o_ref[0, r:r + sub, c:c + LANES] = jnp.where(_pair_masks((sub, LANES)), outs[0], outs[1]).astype(o_ref.dtype)


def _plain_attn_t_kernel(q_ref, *refs, nsrc):
    k_refs, v_refs = refs[0:2 * nsrc:2], refs[1:2 * nsrc:2]
    o_ref = refs[2 * nsrc]
    vt_scrs = refs[2 * nsrc + 1:3 * nsrc + 1]
    s_scr, p_scr = refs[3 * nsrc + 1:]
    tq, qw = q_ref.shape[1], q_ref.shape[2]
    sub = s_scr.shape[2]

    @pl.when(pl.program_id(2) == 0)
    def _transpose_values():
        for v_ref, vt_scr in zip(v_refs, vt_scrs):
            vt_scr[...] = v_ref[0].astype(F32).T.astype(BF16)

    units = [(r, c, hh) for r in range(0, tq, sub) for c in range(0, qw, LANES) for hh in range(2)]

    def scores(u):
        r, c, hh = units[u]
        qh = _half_q(q_ref[0, r:r + sub, c:c + LANES], hh)
        off = 0
        for k_ref in k_refs:
            n = k_ref.shape[1]
            s_scr[u % ns, off:off + n, :] = _dot_nt(k_ref[0], qh)
            off += n

    ns = s_scr.shape[0]
    for u in range(min(ns - 1, len(units))):
        scores(u)
    outs = [None, None]
    for u, (r, c, hh) in enumerate(units):
        if u + ns - 1 < len(units):
            scores(u + ns - 1)
        slabs = [slice(i, i + KEY_SLAB) for i in range(0, s_scr.shape[1], KEY_SLAB)]
        macc = s_scr[u % ns, slabs[0], :]
        for sl in slabs[1:]:
            macc = jnp.maximum(macc, s_scr[u % ns, sl, :])
        m = jnp.max(macc, axis=0, keepdims=True)
        lacc = None
        for sl in slabs:
            e = jnp.exp2(s_scr[u % ns, sl, :] - m)
            p_scr[u % 2, sl, :] = e.astype(BF16)
            lacc = e if lacc is None else lacc + e
        linv = 1.0 / jnp.sum(lacc, axis=0, keepdims=True)
        off, ot = 0, None
        for vt_scr in vt_scrs:
            n = vt_scr.shape[1]
            t = _dot(vt_scr[hh * HEAD_DIM:(hh + 1) * HEAD_DIM, :], p_scr[u % 2, off:off + n, :])
            ot = t if ot is None else ot + t
            off += n
        outs[hh] = ot * linv
        if hh == 1:
            o_ref[0, r:r + sub, c:c + LANES] = jnp.concatenate(outs, axis=0).T.astype(o_ref.dtype)


def _plain_attn_t_call(q, kvs, qw):
    b, lq, d = q.shape
    tq = _row_tile(lq, 4 * ATT_SUB)
    sub = min(tq, ATT_SUB)
    nk = sum(k.shape[1] for k, _ in kvs)
    in_specs = [pl.BlockSpec((1, tq, qw), lambda bi, j, i: (bi, i, j))]
    args = [q]
    for k, v in kvs:
        spec = pl.BlockSpec((1, k.shape[1], LANES), lambda bi, j, i: (bi, 0, j))
        in_specs += [spec, spec]
        args += [k, v]
    return pl.pallas_call(
        functools.partial(_plain_attn_t_kernel, nsrc=len(kvs)),
        out_shape=jax.ShapeDtypeStruct((b, lq, d), BF16),
        grid=(b, d // qw, lq // tq),
        in_specs=in_specs,
        out_specs=pl.BlockSpec((1, tq, qw), lambda bi, j, i: (bi, i, j)),
        scratch_shapes=([pltpu.VMEM((LANES, k.shape[1]), BF16) for k, _ in kvs]
                        + [pltpu.VMEM((3, nk, sub), F32), pltpu.VMEM((2, nk, sub), BF16)]),
        compiler_params=_cparams(("arbitrary", "arbitrary", "arbitrary")),
        name="plain_attn_t",
    )(*args)


def _plain_attn_call(q, kvs, qw):
    b, lq, d = q.shape
    tq = _row_tile(lq, 2 * ATT_SUB)
    sub = min(tq, ATT_SUB)
    nk = sum(k.shape[1] for k, _ in kvs)
    in_specs = [pl.BlockSpec((1, tq, qw), lambda bi, j, i: (bi, i, j))]
    args = [q]
    for k, v in kvs:
        spec = pl.BlockSpec((1, k.shape[1], LANES), lambda bi, j, i: (bi, 0, j))
        in_specs += [spec, spec]
        args += [k, v]
    return pl.pallas_call(
        functools.partial(_plain_attn_kernel, nsrc=len(kvs)),
        out_shape=jax.ShapeDtypeStruct((b, lq, d), BF16),
        grid=(b, d // qw, lq // tq),
        in_specs=in_specs,
        out_specs=pl.BlockSpec((1, tq, qw), lambda bi, j, i: (bi, i, j)),
        scratch_shapes=[pltpu.VMEM((2, sub, nk), F32), pltpu.VMEM((2, sub, nk), BF16)],
        compiler_params=_cparams(("arbitrary", "arbitrary", "arbitrary")),
        name="plain_attn",
    )(*args)


def _diff_attn_kernel(q_ref, lam_ref, g_ref, *refs, nsrc, lambda_init):
    k_refs, v_refs = refs[0:2 * nsrc:2], refs[1:2 * nsrc:2]
    o_ref, s_scr, a_scr = refs[2 * nsrc:]
    lp = lam_ref[...]
    lam = (jnp.exp(jnp.sum(lp[0:1] * lp[1:2], axis=-1, keepdims=True))
           - jnp.exp(jnp.sum(lp[2:3] * lp[3:4], axis=-1, keepdims=True)) + lambda_init)
    tq = q_ref.shape[1]
    sub = s_scr.shape[1]
    ns = s_scr.shape[0] // 2
    units = list(range(0, tq, sub))

    def scores(u):
        q = q_ref[0, units[u]:units[u] + sub]
        for hh in range(2):
            _scores_to_scratch(_half_q(q, hh), k_refs, s_scr, 2 * (u % ns) + hh)

    def combine(u):
        linv = []
        for r in range(0, sub, DIFF_RB):
            e0, l0 = _exp_rows(s_scr[2 * (u % ns), r:r + DIFF_RB, :])
            e1, l1 = _exp_rows(s_scr[2 * (u % ns) + 1, r:r + DIFF_RB, :])
            a_scr[u % 2, r:r + DIFF_RB, :] = (e0 - e1 * (lam * l0 / l1)).astype(BF16)
            linv.append(1.0 / l0)
        return jnp.concatenate(linv, axis=0)

    for u in range(min(ns - 1, len(units))):
        scores(u)
    for u, r in enumerate(units):
        if u + ns - 1 < len(units):
            scores(u + ns - 1)
        linv = combine(u)
        o = _pv_from_scratch(a_scr, u % 2, v_refs) * linv
        o_ref[0, r:r + sub] = (_rms(o, g_ref[...]) * (1.0 - lambda_init)).astype(o_ref.dtype)


def _diff_attn_call(q, kvs, lam_params, subln_g, lambda_init):
    b, lq, d = q.shape
    tq = _row_tile(lq, 8 * ATT_SUB)
    sub = min(tq, ATT_SUB)
    nk = sum(k.shape[1] for k, _ in kvs)
    const = lambda bi, j, i: (0, 0)
    in_specs = [pl.BlockSpec((1, tq, LANES), lambda bi, j, i: (bi, i, j)),
                pl.BlockSpec(lam_params.shape, const), pl.BlockSpec(subln_g.shape, const)]
    args = [q, lam_params, subln_g]
    for k, v in kvs:
        spec = pl.BlockSpec((1, k.shape[1], LANES), lambda bi, j, i: (bi, 0, j))
        in_specs += [spec, spec]
        args += [k, v]
    return pl.pallas_call(
        functools.partial(_diff_attn_kernel, nsrc=len(kvs), lambda_init=lambda_init),
        out_shape=jax.ShapeDtypeStruct((b, lq, d), BF16),
        grid=(b, d // LANES, lq // tq),
        in_specs=in_specs,
        out_specs=pl.BlockSpec((1, tq, LANES), lambda bi, j, i: (bi, i, j)),
        scratch_shapes=[pltpu.VMEM((2 * DIFF_UNITS, sub, nk), F32), pltpu.VMEM((2, sub, nk), BF16)],
        compiler_params=_cparams(("arbitrary", "arbitrary", "arbitrary")),
        name="diff_attn",
    )(*args)


def _na_attn_t_kernel(q_ref, k_ref, v_ref, kc_ref, vc_ref, rpb_ref, o_ref, tt2_scr, vt_scr, vct_scr, s_scr, p_scr, *,
                      rows, kr, kcw, nq):
    w = GRID_W
    nwin = nq + kr
    ntab = 2 * kr
    nkw = nwin * w
    sub = nq * w
    nk = s_scr.shape[1]
    nsub = q_ref.shape[1] // sub
    i = pl.program_id(2)
    shift = int(math.log2(w))

    @pl.when((i == 0) & (pl.program_id(1) == 0))
    def _build_bias():
        cp_io = lax.broadcasted_iota(jnp.int32, (w, LANES), 0)
        c_io = lax.broadcasted_iota(jnp.int32, (w, LANES), 1)
        cs = jnp.clip(c_io - kcw // 2, 0, w - kcw)
        colok = (cp_io >= cs) & (cp_io < cs + kcw)
        neg = jnp.full((w, LANES), NEG, F32)
        for hh in range(2):
            r = pltpu.roll(rpb_ref[hh] * LOG2E, LANES - (kcw - 1), 1)
            ts = []
            for e in range(ntab - 1):
                t = jnp.broadcast_to(r[e:e + 1], (w, LANES))
                t = pltpu.roll(t, 0, 1, stride=1, stride_axis=0)
                ts.append(jnp.where(colok, t, NEG))
            for x in range(ntab):
                lo = ts[x] if x < ntab - 1 else neg
                hi = ts[x - 1] if x >= 1 else neg
                tt2_scr[hh, x] = jnp.where(c_io < w, lo, pltpu.roll(hi, w, 1))

    @pl.when(i == 0)
    def _transpose_values():
        for c in range(vt_scr.shape[0]):
            vt_scr[c] = v_ref[0, c * sub:(c + 1) * sub, :].astype(F32).T.astype(BF16)
        vct_scr[...] = vc_ref[0].astype(F32).T.astype(BF16)

    units = [(t, hh) for t in range(nsub) for hh in range(2)]
    ns = s_scr.shape[0]

    def window(t):
        r0 = (i * nsub + t) * nq
        return r0, jnp.clip(r0 - kr // 2, 0, rows - nwin)

    def scores(u):
        t, hh = units[u]
        _, ks = window(t)
        qh = _half_q(q_ref[0, t * sub:(t + 1) * sub], hh)
        s_scr[u % ns, :nkw, :] = _dot_nt(k_ref[0, pl.ds(pl.multiple_of(ks * w, sub), nkw), :], qh)
        s_scr[u % ns, nkw:, :] = _dot_nt(kc_ref[0], qh)

    def softmax(u):
        t, hh = units[u]
        r0, ks = window(t)
        q_row = r0 + jnp.right_shift(lax.broadcasted_iota(jnp.int32, (1, sub), 1), shift)
        rs = jnp.clip(q_row - kr // 2, 0, rows - kr)
        macc = None
        for j in range(nwin):
            key_row = ks + j
            ok = (key_row >= rs) & (key_row < rs + kr)
            d0 = key_row - r0 + kr - 1
            bias = jnp.concatenate([tt2_scr[hh, jnp.clip(d0 - 2 * g, 0, ntab - 1)] for g in range(sub // LANES)], axis=1)
            rows_j = slice(j * w, (j + 1) * w)
            s = jnp.where(ok, s_scr[u % ns, rows_j, :] + bias, NEG)
            s_scr[u % ns, rows_j, :] = s
            macc = s if macc is None else jnp.maximum(macc, s)
        for j in range(nkw, nk, w):
            macc = jnp.maximum(macc, s_scr[u % ns, j:j + w, :])
        m = jnp.max(macc, axis=0, keepdims=True)
        lacc = None
        for j in range(0, nk, KEY_SLAB):
            e = jnp.exp2(s_scr[u % ns, j:j + KEY_SLAB, :] - m)
            p_scr[u % 2, j:j + KEY_SLAB, :] = e.astype(BF16)
            lacc = e if lacc is None else lacc + e
        return 1.0 / jnp.sum(lacc, axis=0, keepdims=True)

    for u in range(min(ns - 1, len(units))):
        scores(u)
    outs = [None, None]
    for u, (t, hh) in enumerate(units):
        if u + ns - 1 < len(units):
            scores(u + ns - 1)
        linv = softmax(u)
        _, ks = window(t)
        head = slice(hh * HEAD_DIM, (hh + 1) * HEAD_DIM)
        ot = _dot(vct_scr[head, :], p_scr[u % 2, nkw:, :])
        for c in range(nwin // nq):
            ot = ot + _dot(vt_scr[ks // nq + c, head, :], p_scr[u % 2, c * sub:(c + 1) * sub, :])
        outs[hh] = ot * linv
        if hh == 1:
            o_ref[0, t * sub:(t + 1) * sub] = jnp.concatenate(outs, axis=0).T.astype(o_ref.dtype)


def _na_attn_t_call(q, k, v, kc, vc, rpb):
    b, s_len, d = q.shape
    c_len = kc.shape[1]
    nh, ndr, ndc = rpb.shape
    kr, kcw = (ndr + 1) // 2, (ndc + 1) // 2
    rows = s_len // GRID_W
    nq = 4
    assert GRID_W == HEAD_DIM and rows % nq == 0 and rows >= nq + kr and (nq + kr) % nq == 0 and (kr // 2) % nq == 0
    assert ndc <= LANES and 2 * kr <= 2 * SUBLANES and nq * GRID_W == 2 * LANES
    rpb_p = jnp.full((nh, 2 * SUBLANES, LANES), NEG, F32).at[:, :ndr, :ndc].set(rpb[:, :, ::-1])
    sub = nq * GRID_W
    tq = _row_tile(s_len, 8 * sub)
    nk = (nq + kr) * GRID_W + c_len
    full = lambda n: pl.BlockSpec((1, n, LANES), lambda j, bi, i: (bi, 0, j))
    return pl.pallas_call(
        functools.partial(_na_attn_t_kernel, rows=rows, kr=kr, kcw=kcw, nq=nq),
        out_shape=jax.ShapeDtypeStruct((b, s_len, d), BF16),
        grid=(d // LANES, b, s_len // tq),
        in_specs=[pl.BlockSpec((1, tq, LANES), lambda j, bi, i: (bi, i, j)),
                  full(s_len), full(s_len), full(c_len), full(c_len),
                  pl.BlockSpec((2, 2 * SUBLANES, LANES), lambda j, bi, i: (j, 0, 0))],
        out_specs=pl.BlockSpec((1, tq, LANES), lambda j, bi, i: (bi, i, j)),
        scratch_shapes=[pltpu.VMEM((2, 2 * kr, GRID_W, LANES), F32), pltpu.VMEM((s_len // sub, LANES, sub), BF16),
                        pltpu.VMEM((LANES, c_len), BF16), pltpu.VMEM((3, nk, sub), F32),
                        pltpu.VMEM((2, nk, sub), BF16)],
        compiler_params=_cparams(("arbitrary", "arbitrary", "arbitrary")),
        name="na_attn_t",
    )(q, k, v, kc, vc, rpb_p)


def _outproj_kernel(a_ref, x_ref, mod_ref, g_ref, w_ref, o_ref, z_ref):
    tm = x_ref.shape[1]
    sub = _split_rows(tm)
    for r in range(0, tm, sub):
        rows = slice(r, r + sub)
        x = x_ref[0, rows] + mod_ref[0, 2:3] * _dot(a_ref[0, rows], w_ref[...])
        o_ref[0, rows] = x
        z_ref[0, rows] = _normmod(x, g_ref[...], mod_ref[0, 3:4], mod_ref[0, 4:5]).astype(z_ref.dtype)


def _outproj_call(a, x, mods_l, mod_row, g, w):
    b, n, d = x.shape
    k = a.shape[-1]
    tm = _row_tile(n, OUTPROJ_TM)
    row = lambda w_: pl.BlockSpec((1, tm, w_), lambda bi, i: (bi, i, 0))
    return pl.pallas_call(
        _outproj_kernel,
        out_shape=(jax.ShapeDtypeStruct((b, n, d), F32), jax.ShapeDtypeStruct((b, n, d), BF16)),
        grid=(b, n // tm),
        in_specs=[row(k), row(d), _mod_spec(d, mod_row),
                  pl.BlockSpec((1, d), lambda bi, i: (0, 0)),
                  pl.BlockSpec((k, d), lambda bi, i: (0, 0))],
        out_specs=(row(d), row(d)),
        compiler_params=_cparams(("arbitrary", "arbitrary")),
        name="outproj",
    )(a, x, mods_l, g, w)


def _ffn_kernel(x_ref, z_ref, zp_ref, zn_ref, mod_ref, wu_ref, cw_ref, wd_ref, *rest, fc, final):
    if final:
        fg_ref, o_ref, z_scr, u_scr, h_scr = rest
    else:
        o_ref, z_scr, u_scr, h_scr = rest
    i = pl.program_id(1)
    nt = pl.num_programs(1)
    tm, d = x_ref.shape[1], x_ref.shape[2]
    hb = FFN_HALO
    nch = wd_ref.shape[0] // fc
    z_scr[hb:hb + tm] = z_ref[0]
    z_scr[0:hb] = jnp.where(i > 0, zp_ref[0], jnp.zeros_like(zp_ref[0]))
    z_scr[hb + tm:] = jnp.where(i < nt - 1, zn_ref[0], jnp.zeros_like(zn_ref[0]))

    f = wd_ref.shape[0]
    nu = u_scr.shape[0]
    gate_cols = lambda j: slice(j * fc, (j + 1) * fc)
    value_cols = lambda j: slice(f + j * fc, f + (j + 1) * fc)

    def up_proj(j):
        z = z_scr[...]
        u_scr[j % nu, :, :fc] = _dot(z, wu_ref[:, gate_cols(j)])
        u_scr[j % nu, :, fc:] = _dot(z, wu_ref[:, value_cols(j)])

    def conv(j, cols, lanes, r0, sub):
        cw = cw_ref[:, cols]
        cv = cw[3:4] + cw[0:1] * u_scr[j % nu, r0 - 1:r0 - 1 + sub, lanes]
        cv = cv + cw[1:2] * u_scr[j % nu, r0:r0 + sub, lanes]
        return cv + cw[2:3] * u_scr[j % nu, r0 + 1:r0 + 1 + sub, lanes]

    def conv_act(j):
        sub = min(FFN_SUB, tm)
        for sb in range(tm // sub):
            r0 = hb + sb * sub
            hg = 0.5 * conv(j, gate_cols(j), slice(0, fc), r0, sub)
            h = (hg + hg * jnp.tanh(hg)) * conv(j, value_cols(j), slice(fc, 2 * fc), r0, sub)
            h_scr[sb * sub:(sb + 1) * sub, j * fc:(j + 1) * fc] = h.astype(BF16)

    for j in range(min(nu - 1, nch)):
        up_proj(j)
    acc, done = None, 0
    for j in range(nch):
        if j + nu - 1 < nch:
            up_proj(j + nu - 1)
        conv_act(j)
        if (j + 1) % FFN_DOWN_GROUP == 0 or j == nch - 1:
            cols = slice(done * fc, (j + 1) * fc)
            part = _dot(h_scr[:, cols], wd_ref[cols, :])
            acc = part if acc is None else acc + part
            done = j + 1
    out = x_ref[0] + mod_ref[0, 5:6] * acc
    if final:
        out = _rms(out, fg_ref[...])
    o_ref[0] = out


def _ffn_pack_params(w_up, conv_w, conv_b, w_down, fc):
    f2 = w_up.shape[1]
    assert (f2 // 2) % fc == 0
    cw = jnp.concatenate([conv_w, conv_b[None], jnp.zeros((SUBLANES - 1 - conv_w.shape[0], f2), F32)], axis=0)
    return w_up.astype(BF16), cw, w_down.astype(BF16)


def _ffn_call(x, z, mods_l, mod_row, packed, layer, fc, final_g=None):
    b, n, d = x.shape
    wu, cw, wd = packed
    nch, fc2 = wd.shape[1] // fc, 2 * fc
    tm = _row_tile(n, FFN_TM)
    hb = FFN_HALO
    nbh = tm // hb
    once = dict(pipeline_mode=pl.Buffered(1))
    layer_spec = lambda a, **kw: pl.BlockSpec((None,) + a.shape[1:], lambda bi, i: (layer,) + (0,) * (a.ndim - 1), **kw)
    in_specs = [
        pl.BlockSpec((1, tm, d), lambda bi, i: (bi, i, 0)),
        pl.BlockSpec((1, tm, d), lambda bi, i: (bi, i, 0)),
        pl.BlockSpec((1, hb, d), lambda bi, i: (bi, jnp.maximum(i * nbh - 1, 0), 0)),
        pl.BlockSpec((1, hb, d), lambda bi, i: (bi, jnp.minimum((i + 1) * nbh, n // hb - 1), 0)),
        _mod_spec(d, mod_row),
        layer_spec(wu, **once),
        layer_spec(cw),
        layer_spec(wd, **once),
    ]
    args = [x, z, z, z, mods_l, wu, cw, wd]
    if final_g is not None:
        in_specs.append(pl.BlockSpec((1, d), lambda bi, i: (0, 0)))
        args.append(final_g)
    return pl.pallas_call(
        functools.partial(_ffn_kernel, fc=fc2 // 2, final=final_g is not None),
        out_shape=jax.ShapeDtypeStruct((b, n, d), F32),
        grid=(b, n // tm),
        in_specs=in_specs,
        out_specs=pl.BlockSpec((1, tm, d), lambda bi, i: (bi, i, 0)),
        scratch_shapes=[pltpu.VMEM((tm + 2 * hb, d), BF16), pltpu.VMEM((FFN_SLOTS, tm + 2 * hb, fc2), F32),
                        pltpu.VMEM((tm, nch * fc2 // 2), BF16)],
        compiler_params=_cparams(("arbitrary", "arbitrary")),
        name="ffn",
    )(*args)


def _rg_mixer(xl, xc, mods_l, ctx_row, g, w_in, conv_w, conv_b, wa, ba, wx, bx, lam, need_ctx):
    w = w_in.astype(BF16)
    g_l, xr_l = _rg_proj_call(xl, mods_l, None, g, w)
    g_c, xr_c = _rg_proj_call(xc, mods_l, ctx_row, g, w)
    cw, wbd, gp = _rg_pack_params(conv_w, conv_b, wa, ba, wx, bx, lam)
    m_c, m_l = _rg_scan_call(xr_c, g_c, xr_l, g_l, cw, wbd, gp)
    return m_l, (m_c if need_ctx else None)


def _na_mixer(xl, xc, mods_l, ctx_row, g, w_in, rpb, need_ctx):
    d = xl.shape[-1]
    w = w_in.astype(BF16)
    ql, kl, vl = _qkv_proj_call(xl, mods_l, None, g, w, d)
    qc, kc, vc = _qkv_proj_call(xc, mods_l, ctx_row, g, w, d)
    o_l = _na_attn_t_call(ql, kl, vl, kc, vc, rpb)
    o_c = _plain_attn_call(qc, [(kc, vc)], LANES) if need_ctx else None
    return o_l, o_c


def _gqa_mixer(xl, xc, mods_l, ctx_row, g, w_in, q_norm, k_norm, need_ctx):
    d = xl.shape[-1]
    nh = d // HEAD_DIM
    nkv = (w_in.shape[1] // HEAD_DIM - nh) // 2
    group = nh // nkv
    assert group % 2 == 0 and q_norm.shape[0] == HEAD_DIM
    dup = lambda wk: jnp.concatenate([wk.reshape(d, nkv, 1, HEAD_DIM)] * 2, axis=2).reshape(d, 2 * nkv * HEAD_DIM)
    nk = 2 * nkv * HEAD_DIM
    w = jnp.concatenate([w_in[:, :d], dup(w_in[:, d:d + nkv * HEAD_DIM]), dup(w_in[:, d + nkv * HEAD_DIM:])],
                        axis=1).astype(BF16)
    nseg = (d + nk) // HEAD_DIM
    assert nseg <= LANES
    seg = jnp.arange(d + nk) // HEAD_DIM
    ind = (seg[:, None] == jnp.arange(LANES)[None, :]).astype(BF16)
    gain = jnp.concatenate([jnp.tile(q_norm, nh), jnp.tile(k_norm, 2 * nkv)])[None]
    qkn = (gain, ind, ind.T)
    ql, kl, vl = _qkv_proj_call(xl, mods_l, None, g, w, nk, rope=_rope_tables(xl.shape[1]), qknorm=qkn)
    qc, kc, vc = _qkv_proj_call(xc, mods_l, ctx_row, g, w, nk, qknorm=qkn)
    qw = group * HEAD_DIM
    o_l = _plain_attn_t_call(ql, [(kl, vl), (kc, vc)], qw)
    o_c = _plain_attn_call(qc, [(kc, vc)], qw) if need_ctx else None
    return o_l, o_c


def _diff_mixer(xl, xc, mods_l, ctx_row, g, w_in, lq1, lk1, lq2, lk2, subln_g, lambda_init, need_ctx):
    d = xl.shape[-1]
    assert lq1.shape[0] == HEAD_DIM and subln_g.shape[0] == LANES
    w = w_in.astype(BF16)
    ql, kl, vl = _qkv_proj_call(xl, mods_l, None, g, w, d, rope=_rope_tables(xl.shape[1]))
    qc, kc, vc = _qkv_proj_call(xc, mods_l, ctx_row, g, w, d)
    lam_params = jnp.concatenate([jnp.stack([lq1, lk1, lq2, lk2]), jnp.zeros((SUBLANES - 4, HEAD_DIM), F32)])
    o_l = _diff_attn_call(ql, [(kl, vl), (kc, vc)], lam_params, subln_g[None], lambda_init)
    o_c = _diff_attn_call(qc, [(kc, vc)], lam_params, subln_g[None], lambda_init) if need_ctx else None
    return o_l, o_c


def kernel(x, c, ctx, c_ctx, mod_w, mod_b, norm1_g, norm2_g, rg_w_in, rg_conv_w, rg_conv_b, rg_wa, rg_ba, rg_wx, rg_bx, rg_lam, rg_w_out, na_w_in, na_rpb, na_w_out, gqa_w_in, gqa_q_norm, gqa_k_norm, gqa_w_out, diff_w_in, diff_lq1, diff_lk1, diff_lq2, diff_lk2, diff_subln_g, diff_w_out, ffn_w_up, ffn_conv_w, ffn_conv_b, ffn_w_down, final_g):
    bsz, _, d = x.shape
    depth = mod_w.shape[0]
    assert d % LANES == 0
    ctx_row = bsz
    nrow = -(-(bsz + 1) // SUBLANES) * SUBLANES
    cc = jnp.concatenate([c, c_ctx[None], jnp.zeros((nrow - bsz - 1, d), F32)], axis=0)
    mods = _mods_call(cc, mod_w, mod_b).reshape(depth, nrow, 6, d)
    f = ffn_w_down.shape[1]
    fc = next(t for t in (256, 128) if f % t == 0)
    packed = jax.vmap(functools.partial(_ffn_pack_params, fc=fc))(ffn_w_up, ffn_conv_w, ffn_conv_b, ffn_w_down)
    xl, xc = x, ctx
    for l in range(depth):
        m, j = l % N_MIXERS, l // N_MIXERS
        need_ctx = l < depth - 1
        mods_l = mods[l]
        g1, g2 = norm1_g[l][None], norm2_g[l][None]
        if m == 0:
            a_l, a_c = _rg_mixer(xl, xc, mods_l, ctx_row, g1, rg_w_in[j], rg_conv_w[j], rg_conv_b[j], rg_wa[j],
                                 rg_ba[j], rg_wx[j], rg_bx[j], rg_lam[j], need_ctx)
            w_out = rg_w_out[j]
        elif m == 1:
            a_l, a_c = _na_mixer(xl, xc, mods_l, ctx_row, g1, na_w_in[j], na_rpb[j], need_ctx)
            w_out = na_w_out[j]
        elif m == 2:
            a_l, a_c = _gqa_mixer(xl, xc, mods_l, ctx_row, g1, gqa_w_in[j], gqa_q_norm[j], gqa_k_norm[j], need_ctx)
            w_out = gqa_w_out[j]
        else:
            lambda_init = 0.8 - 0.6 * math.exp(-0.3 * l)
            a_l, a_c = _diff_mixer(xl, xc, mods_l, ctx_row, g1, diff_w_in[j], diff_lq1[j], diff_lk1[j],
                                   diff_lq2[j], diff_lk2[j], diff_subln_g[j], lambda_init, need_ctx)
            w_out = diff_w_out[j]
        w_out = w_out.astype(BF16)
        last = l == depth - 1
        xl, zl = _outproj_call(a_l, xl, mods_l, None, g2, w_out)
        xl = _ffn_call(xl, zl, mods_l, None, packed, l, fc, final_g[None] if last else None)
        if need_ctx:
            xc, zc = _outproj_call(a_c, xc, mods_l, ctx_row, g2, w_out)
            xc = _ffn_call(xc, zc, mods_l, ctx_row, packed, l, fc)
    return xl
```
